```python
import jax, jax.numpy as jnp
from jax import lax
import numpy as np

D_MODEL = 1024
BATCH = 32
SEQ = 2048
DEPTH = 2

N_META = 16
HEAD_DIM = 64
CONV_WIDTH = D_MODEL // 4
CONV_HEADS = CONV_WIDTH // HEAD_DIM
CONV_K = 3
POOL_WIDTH = D_MODEL // 4
POOL_WINDOWS = (2, 4, 8, 16)
N_POOL_GROUPS = len(POOL_WINDOWS)
PG = POOL_WIDTH // N_POOL_GROUPS
ATTN_WIDTH = D_MODEL // 2
ATTN_HEADS = ATTN_WIDTH // HEAD_DIM
MIX_WIDTH = CONV_WIDTH + POOL_WIDTH + ATTN_WIDTH
IN_WIDTH = 3 * CONV_WIDTH + POOL_WIDTH + 3 * ATTN_WIDTH
D_FF = 4 * D_MODEL
Q_BLOCK = 128
EPS = 1e-6
SPLITS = (CONV_WIDTH, 2 * CONV_WIDTH, 3 * CONV_WIDTH,
          3 * CONV_WIDTH + POOL_WIDTH,
          3 * CONV_WIDTH + POOL_WIDTH + ATTN_WIDTH,
          3 * CONV_WIDTH + POOL_WIDTH + 2 * ATTN_WIDTH)

kernel_name = "hybrid_conv_pool_stickbreak_trunk"


def rms_norm(x, g):
    xf = x.astype(jnp.float32)
    y = xf * lax.rsqrt(jnp.mean(xf * xf, axis=-1, keepdims=True) + EPS)
    return (y * g.astype(jnp.float32)).astype(x.dtype)


def causal_dwconv(u, w):
    c = u.shape[-1]
    return lax.conv_general_dilated(
        u, w[:, None, :].astype(u.dtype), window_strides=(1,),
        padding=[(CONV_K - 1, 0)], dimension_numbers=("NWC", "WIO", "NWC"),
        feature_group_count=c)


def multiscale_pool(u, w_grp, scale):
    b, l, _ = u.shape
    ug = u.reshape(b, l, N_POOL_GROUPS, PG).astype(jnp.float32)
    cs = jnp.concatenate([jnp.zeros((b, 1, N_POOL_GROUPS, PG), jnp.float32),
                          lax.cumsum(ug, axis=1)], axis=1)
    t = jnp.arange(l)
    means = []
    for g, w in enumerate(POOL_WINDOWS):
        lo = jnp.maximum(t + 1 - w, 0)
        s = cs[:, 1:, g] - cs[:, lo, g]
        cnt = (t + 1 - lo).astype(jnp.float32)
        means.append(s / cnt[None, :, None])
    pooled = (jnp.stack(means, axis=2) - ug).astype(u.dtype)
    y = jnp.einsum("blgc,gcd->blgd", pooled, w_grp)
    return y.reshape(b, l, POOL_WIDTH) * scale


def stick_breaking_attention(q, k, v):
    l = q.shape[2]
    starts = [0] + list(range(N_META, l, Q_BLOCK))
    ends = starts[1:] + [l]
    scale = HEAD_DIM ** -0.5
    outs = []
    for start, end in zip(starts, ends):
        qb = q[:, :, start:end].astype(jnp.float32)
        kb = k[:, :, :end].astype(jnp.float32)
        z = jnp.einsum("bhqd,bhkd->bhqk", qb, kb) * scale
        mask = jnp.arange(end)[None, :] < jnp.arange(start, end)[:, None]
        log_keep = jnp.where(mask, jax.nn.log_sigmoid(-z), 0.0)
        between = lax.cumsum(log_keep, axis=3, reverse=True) - log_keep
        a = jnp.where(mask, jnp.exp(jax.nn.log_sigmoid(z) + between), 0.0)
        outs.append(jnp.einsum("bhqk,bhkd->bhqd", a, v[:, :, :end].astype(jnp.float32)))
    return jnp.concatenate(outs, axis=2).astype(v.dtype)


def _fwd_setup_inputs(seed: int = 0) -> dict:
    key = jax.random.key(seed)
    ks = jax.random.split(key, 12)
    f32 = jnp.float32
    nrm = lambda k, s: jax.random.normal(k, s, f32)
    return {
        "x": nrm(ks[0], (BATCH, SEQ, D_MODEL)),
        "meta_tokens": nrm(ks[1], (N_META, D_MODEL)),
        "g_mix": 1.0 + 0.02 * nrm(ks[2], (DEPTH, D_MODEL)),
        "w_in": nrm(ks[3], (DEPTH, D_MODEL, IN_WIDTH)) * D_MODEL ** -0.5,
        "w_conv": nrm(ks[4], (DEPTH, CONV_K, CONV_WIDTH)) * CONV_K ** -0.5,
        "w_pool": nrm(ks[5], (DEPTH, N_POOL_GROUPS, PG, PG)) * PG ** -0.5,
        "pool_scale": 1.0 + 0.02 * nrm(ks[6], (DEPTH, POOL_WIDTH)),
        "w_out": nrm(ks[7], (DEPTH, MIX_WIDTH, D_MODEL)) * MIX_WIDTH ** -0.5,
        "g_mlp": 1.0 + 0.02 * nrm(ks[8], (DEPTH, D_MODEL)),
        "w_up": nrm(ks[9], (DEPTH, D_MODEL, D_FF)) * D_MODEL ** -0.5,
        "w_down": nrm(ks[10], (DEPTH, D_FF, D_MODEL)) * D_FF ** -0.5,
        "g_final": 1.0 + 0.02 * nrm(ks[11], (D_MODEL,)),
    }


def _fwd_reference(x, meta_tokens, g_mix, w_in, w_conv, w_pool, pool_scale, w_out,
              g_mlp, w_up, w_down, g_final):
    b = x.shape[0]
    meta = jnp.broadcast_to(meta_tokens[None].astype(x.dtype), (b, N_META, D_MODEL))
    h = jnp.concatenate([meta, x], axis=1)
    l = h.shape[1]
    to_heads = lambda t: t.reshape(b, l, ATTN_HEADS, HEAD_DIM).transpose(0, 2, 1, 3)
    for i in range(DEPTH):
        u = rms_norm(h, g_mix[i]) @ w_in[i]
        c_b, c_c, c_x, p_in, q, k, v = jnp.split(u, SPLITS, axis=-1)
        y_conv = c_b * causal_dwconv(c_c * c_x, w_conv[i])
        y_pool = multiscale_pool(p_in, w_pool[i], pool_scale[i])
        y_attn = stick_breaking_attention(to_heads(q), to_heads(k), to_heads(v))
        y_attn = y_attn.transpose(0, 2, 1, 3).reshape(b, l, ATTN_WIDTH)
        h = h + jnp.concatenate([y_conv, y_pool, y_attn], axis=-1) @ w_out[i]
        m = rms_norm(h, g_mlp[i]) @ w_up[i]
        h = h + jnp.square(jax.nn.relu(m)) @ w_down[i]
    return rms_norm(h, g_final)[:, N_META:]


import jax as _jax
import jax.numpy as _jnp

TWIN_FORMAT = 'train_step'
FWD_PARAMS = ['x', 'meta_tokens', 'g_mix', 'w_in', 'w_conv', 'w_pool', 'pool_scale', 'w_out', 'g_mlp', 'w_up', 'w_down', 'g_final']
TWIN_WEIGHTS = ['meta_tokens', 'g_mix', 'w_in', 'w_conv', 'w_pool', 'pool_scale', 'w_out', 'g_mlp', 'w_up', 'w_down', 'g_final']
TWIN_DIFF_INPUT = 'x'
TWIN_INPUTS = ['x', 'meta_tokens', 'g_mix', 'w_in', 'w_conv', 'w_pool', 'pool_scale', 'w_out', 'g_mlp', 'w_up', 'w_down', 'g_final', 'loss_target', 'm_meta_tokens', 'm_g_mix', 'm_w_in', 'm_w_conv', 'm_w_pool', 'm_pool_scale', 'm_w_out', 'm_g_mlp', 'm_w_up', 'm_w_down', 'm_g_final', 'v_meta_tokens', 'v_g_mix', 'v_w_in', 'v_w_conv', 'v_w_pool', 'v_pool_scale', 'v_w_out', 'v_g_mlp', 'v_w_up', 'v_w_down', 'v_g_final']
TWIN_OUTPUTS = ['loss', 'grad_x', 'grad_meta_tokens', 'grad_g_mix', 'grad_w_in', 'grad_w_conv', 'grad_w_pool', 'grad_pool_scale', 'grad_w_out', 'grad_g_mlp', 'grad_w_up', 'grad_w_down', 'grad_g_final', 'delta_meta_tokens', 'delta_g_mix', 'delta_w_in', 'delta_w_conv', 'delta_w_pool', 'delta_pool_scale', 'delta_w_out', 'delta_g_mlp', 'delta_w_up', 'delta_w_down', 'delta_g_final', 'new_m_meta_tokens', 'new_m_g_mix', 'new_m_w_in', 'new_m_w_conv', 'new_m_w_pool', 'new_m_pool_scale', 'new_m_w_out', 'new_m_g_mlp', 'new_m_w_up', 'new_m_w_down', 'new_m_g_final', 'new_v_meta_tokens', 'new_v_g_mix', 'new_v_w_in', 'new_v_w_conv', 'new_v_w_pool', 'new_v_pool_scale', 'new_v_w_out', 'new_v_g_mlp', 'new_v_w_up', 'new_v_w_down', 'new_v_g_final']
TWIN_LEAF_KINDS = {'loss': 'loss', 'grad_x': 'grad_x', 'grad_meta_tokens': 'grad_w', 'grad_g_mix': 'grad_w', 'grad_w_in': 'grad_w', 'grad_w_conv': 'grad_w', 'grad_w_pool': 'grad_w', 'grad_pool_scale': 'grad_w', 'grad_w_out': 'grad_w', 'grad_g_mlp': 'grad_w', 'grad_w_up': 'grad_w', 'grad_w_down': 'grad_w', 'grad_g_final': 'grad_w', 'delta_meta_tokens': 'delta_w', 'delta_g_mix': 'delta_w', 'delta_w_in': 'delta_w', 'delta_w_conv': 'delta_w', 'delta_w_pool': 'delta_w', 'delta_pool_scale': 'delta_w', 'delta_w_out': 'delta_w', 'delta_g_mlp': 'delta_w', 'delta_w_up': 'delta_w', 'delta_w_down': 'delta_w', 'delta_g_final': 'delta_w', 'new_m_meta_tokens': 'new_m', 'new_m_g_mix': 'new_m', 'new_m_w_in': 'new_m', 'new_m_w_conv': 'new_m', 'new_m_w_pool': 'new_m', 'new_m_pool_scale': 'new_m', 'new_m_w_out': 'new_m', 'new_m_g_mlp': 'new_m', 'new_m_w_up': 'new_m', 'new_m_w_down': 'new_m', 'new_m_g_final': 'new_m', 'new_v_meta_tokens': 'new_v', 'new_v_g_mix': 'new_v', 'new_v_w_in': 'new_v', 'new_v_w_conv': 'new_v', 'new_v_w_pool': 'new_v', 'new_v_pool_scale': 'new_v', 'new_v_w_out': 'new_v', 'new_v_g_mlp': 'new_v', 'new_v_w_up': 'new_v', 'new_v_w_down': 'new_v', 'new_v_g_final': 'new_v'}


def _forward(args):
    return _fwd_reference(*[args[k] for k in FWD_PARAMS])


def _output_shape():
    out = _jax.eval_shape(lambda: _forward(_fwd_setup_inputs(0)))
    return out.shape, out.dtype

N_MICROBATCH = 1
ADAM_LR = 0.001
ADAM_B1 = 0.9
ADAM_B2 = 0.999
ADAM_EPS = 1e-08
ADAM_WD = 0.01
ADAM_STEP = 10
PER_EXAMPLE_BATCH_AXIS = {'x': 0, 'loss_target': 0}
SHARED_INPUTS = []
_WEIGHT_DTYPES = {'meta_tokens': _jnp.float32, 'g_mix': _jnp.float32, 'w_in': _jnp.float32, 'w_conv': _jnp.float32, 'w_pool': _jnp.float32, 'pool_scale': _jnp.float32, 'w_out': _jnp.float32, 'g_mlp': _jnp.float32, 'w_up': _jnp.float32, 'w_down': _jnp.float32, 'g_final': _jnp.float32}
MOMENT_SCALE = {'meta_tokens': 5.076577e-03, 'g_mix': 2.367089e-01, 'w_in': 1.362471e-01, 'w_conv': 1.898682e-01, 'w_pool': 1.668404e-01, 'pool_scale': 1.768455e-01, 'w_out': 1.576649e-01, 'g_mlp': 1.935789e-01, 'w_up': 9.753796e-02, 'w_down': 1.884321e-01, 'g_final': 6.473265e+01}


def _to_microbatches(a, axis):
    t = _jnp.moveaxis(a, axis, 0)
    t = t.reshape((N_MICROBATCH, t.shape[0] // N_MICROBATCH) + t.shape[1:])
    return _jnp.moveaxis(t, 1, axis + 1)


def setup_inputs(seed: int = 0) -> dict:
    inp = _fwd_setup_inputs(seed)
    key = _jax.random.fold_in(_jax.random.key(seed), 7919)
    shape, _ = _output_shape()
    out = dict(inp)
    out["loss_target"] = _jax.random.normal(_jax.random.fold_in(key, 0), shape, _jnp.float32)
    for i, name in enumerate(TWIN_WEIGHTS):
        w = inp[name].astype(_jnp.float32)
        if MOMENT_SCALE is None:
            s = _jnp.sqrt(_jnp.mean(_jnp.square(w)) + 1e-30)
        else:
            s = MOMENT_SCALE[name]
        km, kv = _jax.random.split(_jax.random.fold_in(key, i + 1))
        out[name] = w
        out["m_" + name] = s * _jax.random.normal(km, w.shape, _jnp.float32)
        out["v_" + name] = (s * s) * _jax.random.uniform(kv, w.shape, _jnp.float32, 0.5, 1.5)
    if N_MICROBATCH > 1:
        for name, axis in PER_EXAMPLE_BATCH_AXIS.items():
            out[name] = _to_microbatches(out[name], axis)
    return {'x': out['x'], 'meta_tokens': out['meta_tokens'], 'g_mix': out['g_mix'], 'w_in': out['w_in'], 'w_conv': out['w_conv'], 'w_pool': out['w_pool'], 'pool_scale': out['pool_scale'], 'w_out': out['w_out'], 'g_mlp': out['g_mlp'], 'w_up': out['w_up'], 'w_down': out['w_down'], 'g_final': out['g_final'], 'loss_target': out['loss_target'], 'm_meta_tokens': out['m_meta_tokens'], 'm_g_mix': out['m_g_mix'], 'm_w_in': out['m_w_in'], 'm_w_conv': out['m_w_conv'], 'm_w_pool': out['m_w_pool'], 'm_pool_scale': out['m_pool_scale'], 'm_w_out': out['m_w_out'], 'm_g_mlp': out['m_g_mlp'], 'm_w_up': out['m_w_up'], 'm_w_down': out['m_w_down'], 'm_g_final': out['m_g_final'], 'v_meta_tokens': out['v_meta_tokens'], 'v_g_mix': out['v_g_mix'], 'v_w_in': out['v_w_in'], 'v_w_conv': out['v_w_conv'], 'v_w_pool': out['v_w_pool'], 'v_pool_scale': out['v_pool_scale'], 'v_w_out': out['v_w_out'], 'v_g_mlp': out['v_g_mlp'], 'v_w_up': out['v_w_up'], 'v_w_down': out['v_w_down'], 'v_g_final': out['v_g_final']}


def _loss(weights, diff, rest, loss_target):
    with _jax.named_scope("forward"):
        args = {**rest, TWIN_DIFF_INPUT: diff, **{k: w.astype(_WEIGHT_DTYPES[k]) for k, w in weights.items()}}
        y = _forward(args)
    with _jax.named_scope("loss_head"):
        err = _jnp.square(y.astype(_jnp.float32) - loss_target)
        return 0.5 * _jnp.sum(_jnp.mean(err, axis=-1)) if err.ndim else 0.5 * err


def _adamw(w, g, m, v):
    m = ADAM_B1 * m + (1.0 - ADAM_B1) * g
    v = ADAM_B2 * v + (1.0 - ADAM_B2) * _jnp.square(g)
    m_hat = m / (1.0 - ADAM_B1 ** ADAM_STEP)
    v_hat = v / (1.0 - ADAM_B2 ** ADAM_STEP)
    delta = -ADAM_LR * (m_hat / (_jnp.sqrt(v_hat) + ADAM_EPS) + ADAM_WD * w)
    return delta, m, v


def reference(x, meta_tokens, g_mix, w_in, w_conv, w_pool, pool_scale, w_out, g_mlp, w_up, w_down, g_final, loss_target, m_meta_tokens, m_g_mix, m_w_in, m_w_conv, m_w_pool, m_pool_scale, m_w_out, m_g_mlp, m_w_up, m_w_down, m_g_final, v_meta_tokens, v_g_mix, v_w_in, v_w_conv, v_w_pool, v_pool_scale, v_w_out, v_g_mlp, v_w_up, v_w_down, v_g_final):
    given = dict(x=x, meta_tokens=meta_tokens, g_mix=g_mix, w_in=w_in, w_conv=w_conv, w_pool=w_pool, pool_scale=pool_scale, w_out=w_out, g_mlp=g_mlp, w_up=w_up, w_down=w_down, g_final=g_final, loss_target=loss_target, m_meta_tokens=m_meta_tokens, m_g_mix=m_g_mix, m_w_in=m_w_in, m_w_conv=m_w_conv, m_w_pool=m_w_pool, m_pool_scale=m_pool_scale, m_w_out=m_w_out, m_g_mlp=m_g_mlp, m_w_up=m_w_up, m_w_down=m_w_down, m_g_final=m_g_final, v_meta_tokens=v_meta_tokens, v_g_mix=v_g_mix, v_w_in=v_w_in, v_w_conv=v_w_conv, v_w_pool=v_w_pool, v_pool_scale=v_pool_scale, v_w_out=v_w_out, v_g_mlp=v_g_mlp, v_w_up=v_w_up, v_w_down=v_w_down, v_g_final=v_g_final)
    weights = {n: given[n] for n in TWIN_WEIGHTS}
    shared = {n: given[n] for n in SHARED_INPUTS}
    per_example = {n: given[n] for n in ['x']}
    grad_fn = _jax.value_and_grad(_loss, argnums=(0, 1))

    def one_microbatch(ex, loss_target):
        ex = dict(ex)
        diff = ex.pop(TWIN_DIFF_INPUT)
        return grad_fn(weights, diff, {**shared, **ex}, loss_target)

    if N_MICROBATCH == 1:
        loss, (grad_w, grad_x) = one_microbatch(per_example, given["loss_target"])
    else:
        def body(carry, xs):
            loss_sum, grad_sum = carry
            l_k, (gw_k, gx_k) = one_microbatch(xs[0], xs[1])
            with _jax.named_scope("update"):
                return (loss_sum + l_k, _jax.tree.map(_jnp.add, grad_sum, gw_k)), gx_k

        init = (_jnp.zeros((), _jnp.float32), _jax.tree.map(_jnp.zeros_like, weights))
        (loss, grad_w), grad_x = _jax.lax.scan(body, init, (per_example, given["loss_target"]))
    with _jax.named_scope("update"):
        delta_w, new_m, new_v = {}, {}, {}
        for n in TWIN_WEIGHTS:
            delta_w[n], new_m[n], new_v[n] = _adamw(weights[n], grad_w[n], given["m_" + n], given["v_" + n])
    return (loss, grad_x, *[grad_w[n] for n in TWIN_WEIGHTS], *[delta_w[n] for n in TWIN_WEIGHTS],
            *[new_m[n] for n in TWIN_WEIGHTS], *[new_v[n] for n in TWIN_WEIGHTS])
```

```python
import functools

import jax
import jax.numpy as jnp
from jax import lax
from jax.experimental import pallas as pl
from jax.experimental.pallas import tpu as pltpu

F32 = jnp.float32
BF16 = jnp.bfloat16

N_DEV = 8
N_META = 16
HEAD_DIM = 64
BLK = 128
PAD = BLK - N_META
N_POOL_GROUPS = 4
EPS = 1e-6
PACK_W = 1024

ADAM_LR = 0.001
ADAM_B1 = 0.9
ADAM_B2 = 0.999
ADAM_EPS = 1e-08
ADAM_WD = 0.01
ADAM_STEP = 10

V7X_VMEM_LIMIT = 56 * 1024 * 1024


def _params(*sem):
    return pltpu.CompilerParams(dimension_semantics=sem, vmem_limit_bytes=V7X_VMEM_LIMIT)


def _row_tile(t):
    for tm in (512, 256, 128):
        if t % tm == 0:
            return tm
    raise ValueError(f"row count {t} is not a multiple of 128")


def _full(a):
    nd = a.ndim
    return pl.BlockSpec(a.shape, lambda *_: (0,) * nd)


def _dot(a, b):
    return jnp.dot(a, b, preferred_element_type=F32)


def _dot_nt(a, b):
    return lax.dot_general(a, b, (((1,), (1,)), ((), ())), preferred_element_type=F32)


def _dot_tn(a, b):
    return lax.dot_general(a, b, (((0,), (0,)), ((), ())), preferred_element_type=F32)


def _rms_fwd(x, g):
    rstd = lax.rsqrt(jnp.mean(x * x, axis=-1, keepdims=True) + EPS)
    return x * rstd * g


def _rms_bwd(dy, x, g):
    rstd = lax.rsqrt(jnp.mean(x * x, axis=-1, keepdims=True) + EPS)
    xhat = x * rstd
    dxhat = dy * g
    dx = rstd * (dxhat - xhat * jnp.mean(dxhat * xhat, axis=-1, keepdims=True))
    return dx, jnp.sum(dy * xhat, axis=0, keepdims=True)


def _inproj_fwd(h, g, wa, wq, wk, wv):
    t, d = h.shape
    tm = _row_tile(t)
    ua_w, aw = wa.shape[1], wq.shape[1]

    def body(h_ref, g_ref, wa_ref, wq_ref, wk_ref, wv_ref, ua_ref, q_ref, k_ref, v_ref, hn_ref):
        hn = _rms_fwd(h_ref[...], g_ref[...]).astype(BF16)
        hn_ref[...] = hn
        ua_ref[...] = _dot(hn, wa_ref[...])
        q_ref[...] = _dot(hn, wq_ref[...]).astype(BF16)
        k_ref[...] = _dot(hn, wk_ref[...]).astype(BF16)
        v_ref[...] = _dot(hn, wv_ref[...]).astype(BF16)

    row = lambda w: pl.BlockSpec((tm, w), lambda i: (i, 0))
    return pl.pallas_call(
        body, name="inproj_fwd", grid=(t // tm,),
        in_specs=[row(d), _full(g), _full(wa), _full(wq), _full(wk), _full(wv)],
        out_specs=[row(ua_w), row(aw), row(aw), row(aw), row(d)],
        out_shape=[jax.ShapeDtypeStruct((t, ua_w), F32)] + [jax.ShapeDtypeStruct((t, aw), BF16)] * 3
        + [jax.ShapeDtypeStruct((t, d), BF16)],
        compiler_params=_params("arbitrary"),
    )(h, g, wa, wq, wk, wv)


def _pool_geometry(lp, pw):
    pg = pw // N_POOL_GROUPS
    row = lax.broadcasted_iota(jnp.int32, (lp, pw), 0)
    lane = lax.broadcasted_iota(jnp.int32, (lp, pw), 1)
    grp = [(lane >= g * pg) & (lane < (g + 1) * pg) for g in range(N_POOL_GROUPS)]
    wlen = jnp.where(grp[0], 2, jnp.where(grp[1], 4, jnp.where(grp[2], 8, 16)))
    cnt = jnp.clip(row - (PAD - 1), 1, wlen).astype(F32)
    return grp, cnt, row >= PAD


def _by_group(grp, vals):
    return jnp.where(grp[0], vals[0], jnp.where(grp[1], vals[1], jnp.where(grp[2], vals[2], vals[3])))


def _pooled(p, grp, cnt, real):
    s2 = p + pltpu.roll(p, 1, 0)
    s4 = s2 + pltpu.roll(s2, 2, 0)
    s8 = s4 + pltpu.roll(s4, 4, 0)
    s16 = s8 + pltpu.roll(s8, 8, 0)
    return jnp.where(real, _by_group(grp, (s2, s4, s8, s16)) / cnt - p, 0.0)


def _conv(uu, wc_ref):
    return wc_ref[2:3, :] * uu + wc_ref[1:2, :] * pltpu.roll(uu, 1, 0) + wc_ref[0:1, :] * pltpu.roll(uu, 2, 0)


def _mixer_fwd(ua, wconv, wbd, pscale, nb):
    t = ua.shape[0]
    lp = t // nb
    cw, pw = wconv.shape[1], wbd.shape[0]

    def body(ua_ref, wc_ref, wbd_ref, ps_ref, y_ref):
        cb = ua_ref[:, 0:cw]
        uu = ua_ref[:, cw:2 * cw] * ua_ref[:, 2 * cw:3 * cw]
        y_ref[:, 0:cw] = (cb * _conv(uu, wc_ref)).astype(BF16)
        grp, cnt, real = _pool_geometry(lp, pw)
        pooled = _pooled(ua_ref[:, 3 * cw:3 * cw + pw], grp, cnt, real)
        y_ref[:, cw:cw + pw] = (_dot(pooled.astype(BF16), wbd_ref[...]) * ps_ref[...]).astype(BF16)

    return pl.pallas_call(
        body, name="mixer_fwd", grid=(nb,),
        in_specs=[pl.BlockSpec((lp, ua.shape[1]), lambda s: (s, 0)), _full(wconv), _full(wbd), _full(pscale)],
        out_specs=pl.BlockSpec((lp, cw + pw), lambda s: (s, 0)),
        out_shape=jax.ShapeDtypeStruct((t, cw + pw), BF16),
        compiler_params=_params("arbitrary"),
    )(ua, wconv, wbd, pscale)


def _split_bf16(x, parts):
    out = []
    for _ in range(parts):
        hi = x.astype(BF16)
        out.append(hi)
        x = x - hi.astype(F32)
    return out


def _attn_block(qh, kj, valid, carry, mstrict):
    z = _dot_nt(qh, kj) * (HEAD_DIM ** -0.5)
    ls = jnp.minimum(z, 0.0) - jnp.log1p(jnp.exp(-jnp.abs(z)))
    lk = jnp.where(valid, ls - z, 0.0)
    hi, lo = _split_bf16(lk, 2)
    between = _dot(hi, mstrict) + _dot(lo, mstrict) + carry
    a = jnp.where(valid, jnp.exp(ls + between), 0.0)
    return ls, lk, a


def _attn_consts():
    r = lax.broadcasted_iota(jnp.int32, (BLK, BLK), 0)
    c = lax.broadcasted_iota(jnp.int32, (BLK, BLK), 1)
    lane = lax.broadcasted_iota(jnp.int32, (1, 2 * HEAD_DIM), 1)
    heads = (lane < HEAD_DIM, lane >= HEAD_DIM)
    return r, c, heads


def _rows(i):
    return pl.ds(pl.multiple_of(i * BLK, BLK), BLK)


def _attn_fwd(q, k, v, nb):
    t, aw = q.shape
    lp = t // nb
    nblk = lp // BLK

    def body(q_ref, k_ref, v_ref, o_ref):
        r, c, heads = _attn_consts()
        mstrict = jnp.where(r > c, 1.0, 0.0).astype(BF16)

        def qblock(i, _):
            qi = q_ref[_rows(i), :]
            out = jnp.zeros((BLK, 2 * HEAD_DIM), F32)
            for hm in heads:
                qh = jnp.where(hm, qi, jnp.zeros_like(qi))

                def kblock(jj, st, qh=qh, hm=hm):
                    acc, carry = st
                    j = i - jj
                    valid = (c - r) < (i - j) * BLK
                    vj = v_ref[_rows(j), :]
                    _, lk, a = _attn_block(qh, k_ref[_rows(j), :], valid, carry, mstrict)
                    acc = acc + _dot(a.astype(BF16), jnp.where(hm, vj, jnp.zeros_like(vj)))
                    return acc, carry + jnp.sum(lk, axis=1, keepdims=True)

                acc, _ = lax.fori_loop(0, i + 1, kblock, (jnp.zeros((BLK, 2 * HEAD_DIM), F32), jnp.zeros((BLK, 1), F32)))
                out = out + acc
            o_ref[_rows(i), :] = out.astype(BF16)
            return 0

        lax.fori_loop(0, nblk, qblock, 0)

    spec = pl.BlockSpec((lp, 2 * HEAD_DIM), lambda s, p: (s, p))
    return pl.pallas_call(
        body, name="attn_fwd", grid=(nb, aw // (2 * HEAD_DIM)),
        in_specs=[spec, spec, spec], out_specs=spec,
        out_shape=jax.ShapeDtypeStruct((t, aw), BF16),
        compiler_params=_params("arbitrary", "arbitrary"),
    )(q, k, v)


def _outproj(h, ya, yb, wa, wb):
    t, d = h.shape
    tm = _row_tile(t)

    def body(h_ref, ya_ref, yb_ref, wa_ref, wb_ref, o_ref):
        o_ref[...] = h_ref[...] + _dot(ya_ref[...], wa_ref[...]) + _dot(yb_ref[...], wb_ref[...])

    row = lambda w: pl.BlockSpec((tm, w), lambda i: (i, 0))
    return pl.pallas_call(
        body, name="outproj", grid=(t // tm,),
        in_specs=[row(d), row(ya.shape[1]), row(yb.shape[1]), _full(wa), _full(wb)],
        out_specs=row(d), out_shape=jax.ShapeDtypeStruct((t, d), F32),
        compiler_params=_params("arbitrary"),
    )(h, ya, yb, wa, wb)


def _mlp_fwd(h, g, wup, wdown):
    t, d = h.shape
    tm = _row_tile(t)
    nff, _, fc = wup.shape

    def body(h_ref, g_ref, wu_ref, wd_ref, o_ref, hn_ref, acc):
        kk = pl.program_id(1)

        @pl.when(kk == 0)
        def _():
            x = h_ref[...]
            hn_ref[...] = _rms_fwd(x, g_ref[...]).astype(BF16)
            acc[...] = x

        m = _dot(hn_ref[...], wu_ref[...])
        a = jnp.square(jnp.maximum(m, 0.0)).astype(BF16)
        acc[...] += _dot(a, wd_ref[...])

        @pl.when(kk == nff - 1)
        def _():
            o_ref[...] = acc[...]

    row = pl.BlockSpec((tm, d), lambda i, kk: (i, 0))
    return pl.pallas_call(
        body, name="mlp_fwd", grid=(t // tm, nff),
        in_specs=[row, _full(g), pl.BlockSpec((None, d, fc), lambda i, kk: (kk, 0, 0)),
                  pl.BlockSpec((fc, d), lambda i, kk: (kk, 0))],
        out_specs=[row, row],
        out_shape=[jax.ShapeDtypeStruct((t, d), F32), jax.ShapeDtypeStruct((t, d), BF16)],
        scratch_shapes=[pltpu.VMEM((tm, d), F32)],
        compiler_params=_params("arbitrary", "arbitrary"),
    )(h, g, wup, wdown)


def _loss_head(h, g, target, nb):
    t, d = h.shape
    nblk = t // nb // BLK
    nx = nblk - 1

    def body(h_ref, g_ref, t_ref, loss_ref, dh_ref, dg_ref):
        s, i = pl.program_id(0), pl.program_id(1)

        @pl.when((s == 0) & (i == 0))
        def _():
            loss_ref[...] = jnp.zeros_like(loss_ref)
            dg_ref[...] = jnp.zeros_like(dg_ref)

        @pl.when(i == 0)
        def _():
            dh_ref[...] = jnp.zeros_like(dh_ref)

        @pl.when(i > 0)
        def _():
            x, gg = h_ref[...], g_ref[...]
            err = _rms_fwd(x, gg) - t_ref[...]
            loss_ref[...] += jnp.sum(err * err) * (0.5 / d)
            dx, dg = _rms_bwd(err * (1.0 / d), x, gg)
            dh_ref[...] = dx
            dg_ref[...] += dg

    hspec = pl.BlockSpec((BLK, d), lambda s, i: (s * nblk + i, 0))
    return pl.pallas_call(
        body, name="loss_head", grid=(nb, nblk),
        in_specs=[hspec, _full(g), pl.BlockSpec((BLK, d), lambda s, i: (s * nx + jnp.maximum(i - 1, 0), 0))],
        out_specs=[pl.BlockSpec((1, BLK), lambda s, i: (0, 0)), hspec, pl.BlockSpec((1, d), lambda s, i: (0, 0))],
        out_shape=[jax.ShapeDtypeStruct((1, BLK), F32), jax.ShapeDtypeStruct((t, d), F32), jax.ShapeDtypeStruct((1, d), F32)],
        compiler_params=_params("arbitrary", "arbitrary"),
    )(h, g, target)


def _mlp_bwd(dh, h, hn, g, wup, wdown):
    t, d = h.shape
    tm = _row_tile(t)
    nff, _, fc = wup.shape

    def body(dh_ref, h_ref, hn_ref, g_ref, wu_ref, wd_ref, dhm_ref, a_ref, dm_ref, dg_ref, dhn, dhb):
        i, kk = pl.program_id(0), pl.program_id(1)

        @pl.when((i == 0) & (kk == 0))
        def _():
            dg_ref[...] = jnp.zeros_like(dg_ref)

        @pl.when(kk == 0)
        def _():
            dhb[...] = dh_ref[...].astype(BF16)
            dhn[...] = jnp.zeros_like(dhn)

        r = jnp.maximum(_dot(hn_ref[...], wu_ref[...]), 0.0)
        a_ref[...] = (r * r).astype(BF16)
        dm = (_dot_nt(dhb[...], wd_ref[...]) * (2.0 * r)).astype(BF16)
        dm_ref[...] = dm
        dhn[...] += _dot_nt(dm, wu_ref[...])

        @pl.when(kk == nff - 1)
        def _():
            dx, dg = _rms_bwd(dhn[...], h_ref[...], g_ref[...])
            dhm_ref[...] = dh_ref[...] + dx
            dg_ref[...] += dg

    row = pl.BlockSpec((tm, d), lambda i, kk: (i, 0))
    ff = pl.BlockSpec((tm, fc), lambda i, kk: (i, kk))
    return pl.pallas_call(
        body, name="mlp_bwd", grid=(t // tm, nff),
        in_specs=[row, row, row, _full(g), pl.BlockSpec((None, d, fc), lambda i, kk: (kk, 0, 0)),
                  pl.BlockSpec((fc, d), lambda i, kk: (kk, 0))],
        out_specs=[row, ff, ff, pl.BlockSpec((1, d), lambda i, kk: (0, 0))],
        out_shape=[jax.ShapeDtypeStruct((t, d), F32), jax.ShapeDtypeStruct((t, nff * fc), BF16),
                   jax.ShapeDtypeStruct((t, nff * fc), BF16), jax.ShapeDtypeStruct((1, d), F32)],
        scratch_shapes=[pltpu.VMEM((tm, d), F32), pltpu.VMEM((tm, d), BF16)],
        compiler_params=_params("arbitrary", "arbitrary"),
    )(dh, h, hn, g, wup, wdown)


def _matmul_tn(x, y, name):
    t, k1 = x.shape
    n = y.shape[1]
    tt = _row_tile(t)
    tn = min(n, 512)

    def body(x_ref, y_ref, o_ref):
        @pl.when(pl.program_id(1) == 0)
        def _():
            o_ref[...] = jnp.zeros_like(o_ref)

        o_ref[...] += _dot_tn(x_ref[...].astype(BF16), y_ref[...].astype(BF16))

    return pl.pallas_call(
        body, name=name, grid=(n // tn, t // tt),
        in_specs=[pl.BlockSpec((tt, k1), lambda j, r: (r, 0)), pl.BlockSpec((tt, tn), lambda j, r: (r, j))],
        out_specs=pl.BlockSpec((k1, tn), lambda j, r: (0, j)),
        out_shape=jax.ShapeDtypeStruct((k1, n), F32),
        compiler_params=_params("arbitrary", "arbitrary"),
    )(x, y)


def _outproj_bwd(dh, wa, wb):
    t, d = dh.shape
    tm = _row_tile(t)
    na, nbw = wa.shape[0], wb.shape[0]

    def body(dh_ref, wa_ref, wb_ref, da_ref, db_ref):
        x = dh_ref[...].astype(BF16)
        da_ref[...] = _dot_nt(x, wa_ref[...])
        db_ref[...] = _dot_nt(x, wb_ref[...]).astype(BF16)

    row = lambda w: pl.BlockSpec((tm, w), lambda i: (i, 0))
    return pl.pallas_call(
        body, name="outproj_bwd", grid=(t // tm,),
        in_specs=[row(d), _full(wa), _full(wb)], out_specs=[row(na), row(nbw)],
        out_shape=[jax.ShapeDtypeStruct((t, na), F32), jax.ShapeDtypeStruct((t, nbw), BF16)],
        compiler_params=_params("arbitrary"),
    )(dh, wa, wb)


def _attn_bwd(q, k, v, do, nb):
    t, aw = q.shape
    lp = t // nb
    nblk = lp // BLK
    hw = 2 * HEAD_DIM

    def body(q_ref, k_ref, v_ref, do_ref, dq_ref, dk_ref, dv_ref, dk_acc, dv_acc, g_s, sig_s):
        r, c, heads = _attn_consts()
        mstrict = jnp.where(r > c, 1.0, 0.0).astype(BF16)
        mprefix = jnp.where(r < c, 1.0, 0.0).astype(BF16)
        dk_acc[...] = jnp.zeros_like(dk_acc)
        dv_acc[...] = jnp.zeros_like(dv_acc)

        def qblock(i, _):
            qi = q_ref[_rows(i), :]
            doi = do_ref[_rows(i), :]
            dq = jnp.zeros((BLK, hw), F32)
            for hm in heads:
                qh = jnp.where(hm, qi, jnp.zeros_like(qi))
                doh = jnp.where(hm, doi, jnp.zeros_like(doi))

                def down(jj, carry, qh=qh, doh=doh):
                    j = i - jj
                    valid = (c - r) < (i - j) * BLK
                    ls, lk, a = _attn_block(qh, k_ref[_rows(j), :], valid, carry, mstrict)
                    g_s[j] = _dot_nt(doh, v_ref[_rows(j), :]) * a
                    sig_s[j] = jnp.exp(ls)
                    dv_acc[_rows(j), :] += _dot_tn(a.astype(BF16), doh)
                    return carry + jnp.sum(lk, axis=1, keepdims=True)

                lax.fori_loop(0, i + 1, down, jnp.zeros((BLK, 1), F32))

                def up(j, st, qh=qh, hm=hm):
                    acc, carry = st
                    valid = (c - r) < (i - j) * BLK
                    g, sig = g_s[j], sig_s[j]
                    p0, p1, p2 = _split_bf16(g, 3)
                    before = _dot(p0, mprefix) + _dot(p1, mprefix) + _dot(p2, mprefix) + carry
                    dz = jnp.where(valid, g * (1.0 - sig) - before * sig, 0.0) * (HEAD_DIM ** -0.5)
                    dzb = dz.astype(BF16)
                    kj = k_ref[_rows(j), :]
                    acc = acc + _dot(dzb, jnp.where(hm, kj, jnp.zeros_like(kj)))
                    dk_acc[_rows(j), :] += _dot_tn(dzb, qh)
                    return acc, carry + jnp.sum(g, axis=1, keepdims=True)

                acc, _ = lax.fori_loop(0, i + 1, up, (jnp.zeros((BLK, hw), F32), jnp.zeros((BLK, 1), F32)))
                dq = dq + acc
            dq_ref[_rows(i), :] = dq.astype(BF16)
            return 0

        lax.fori_loop(0, nblk, qblock, 0)
        dk_ref[...] = dk_acc[...].astype(BF16)
        dv_ref[...] = dv_acc[...].astype(BF16)

    spec = pl.BlockSpec((lp, hw), lambda s, p: (s, p))
    return pl.pallas_call(
        body, name="attn_bwd", grid=(nb, aw // hw),
        in_specs=[spec] * 4, out_specs=[spec] * 3,
        out_shape=[jax.ShapeDtypeStruct((t, aw), BF16)] * 3,
        scratch_shapes=[pltpu.VMEM((lp, hw), F32), pltpu.VMEM((lp, hw), F32),
                        pltpu.VMEM((nblk, BLK, BLK), F32), pltpu.VMEM((nblk, BLK, BLK), F32)],
        compiler_params=_params("arbitrary", "arbitrary"),
    )(q, k, v, do)


def _mixer_bwd(ua, dy, wconv, wbd, pscale, nb):
    t = ua.shape[0]
    lp = t // nb
    cw, pw = wconv.shape[1], wbd.shape[0]

    def body(ua_ref, dy_ref, wc_ref, wbd_ref, ps_ref, du_ref, dwc_ref, dps_ref, dwb_ref):
        @pl.when(pl.program_id(0) == 0)
        def _():
            dwc_ref[...] = jnp.zeros_like(dwc_ref)
            dps_ref[...] = jnp.zeros_like(dps_ref)
            dwb_ref[...] = jnp.zeros_like(dwb_ref)

        up = lambda x, n: pltpu.roll(x, lp - n, 0)
        cb, cc, cx = ua_ref[:, 0:cw], ua_ref[:, cw:2 * cw], ua_ref[:, 2 * cw:3 * cw]
        uu = cc * cx
        dyc = dy_ref[:, 0:cw]
        du_ref[:, 0:cw] = (dyc * _conv(uu, wc_ref)).astype(BF16)
        dcv = dyc * cb
        duu = wc_ref[2:3, :] * dcv + wc_ref[1:2, :] * up(dcv, 1) + wc_ref[0:1, :] * up(dcv, 2)
        du_ref[:, cw:2 * cw] = (duu * cx).astype(BF16)
        du_ref[:, 2 * cw:3 * cw] = (duu * cc).astype(BF16)
        dwc_ref[0:1, :] += jnp.sum(dcv * pltpu.roll(uu, 2, 0), axis=0, keepdims=True)
        dwc_ref[1:2, :] += jnp.sum(dcv * pltpu.roll(uu, 1, 0), axis=0, keepdims=True)
        dwc_ref[2:3, :] += jnp.sum(dcv * uu, axis=0, keepdims=True)
        grp, cnt, real = _pool_geometry(lp, pw)
        p = ua_ref[:, 3 * cw:3 * cw + pw]
        pooled = _pooled(p, grp, cnt, real).astype(BF16)
        dyp = dy_ref[:, cw:cw + pw]
        dps_ref[...] += jnp.sum(dyp * _dot(pooled, wbd_ref[...]), axis=0, keepdims=True)
        dpre = (dyp * ps_ref[...]).astype(BF16)
        dwb_ref[...] += _dot_tn(pooled, dpre)
        dpooled = jnp.where(real, _dot_nt(dpre, wbd_ref[...]), 0.0)
        xm = dpooled / cnt
        l2 = xm + up(xm, 1)
        l4 = l2 + up(l2, 2)
        l8 = l4 + up(l4, 4)
        l16 = l8 + up(l8, 8)
        du_ref[:, 3 * cw:3 * cw + pw] = jnp.where(real, _by_group(grp, (l2, l4, l8, l16)) - dpooled, 0.0).astype(BF16)

    seq = lambda w: pl.BlockSpec((lp, w), lambda s: (s, 0))
    return pl.pallas_call(
        body, name="mixer_bwd", grid=(nb,),
        in_specs=[seq(ua.shape[1]), seq(cw + pw), _full(wconv), _full(wbd), _full(pscale)],
        out_specs=[seq(ua.shape[1]), pl.BlockSpec((3, cw), lambda s: (0, 0)), pl.BlockSpec((1, pw), lambda s: (0, 0)),
                   pl.BlockSpec((pw, pw), lambda s: (0, 0))],
        out_shape=[jax.ShapeDtypeStruct(ua.shape, BF16), jax.ShapeDtypeStruct((3, cw), F32),
                   jax.ShapeDtypeStruct((1, pw), F32), jax.ShapeDtypeStruct((pw, pw), F32)],
        compiler_params=_params("arbitrary"),
    )(ua, dy, wconv, wbd, pscale)


def _inproj_bwd(dh, h, g, dua, dq, dk, dv, wa, wq, wk, wv):
    t, d = h.shape
    tm = _row_tile(t)

    def body(dh_ref, h_ref, g_ref, dua_ref, dq_ref, dk_ref, dv_ref, wa_ref, wq_ref, wk_ref, wv_ref, o_ref, dg_ref):
        @pl.when(pl.program_id(0) == 0)
        def _():
            dg_ref[...] = jnp.zeros_like(dg_ref)

        dhn = (_dot_nt(dua_ref[...], wa_ref[...]) + _dot_nt(dq_ref[...], wq_ref[...])
               + _dot_nt(dk_ref[...], wk_ref[...]) + _dot_nt(dv_ref[...], wv_ref[...]))
        dx, dg = _rms_bwd(dhn, h_ref[...], g_ref[...])
        o_ref[...] = dh_ref[...] + dx
        dg_ref[...] += dg

    row = lambda w: pl.BlockSpec((tm, w), lambda i: (i, 0))
    return pl.pallas_call(
        body, name="inproj_bwd", grid=(t // tm,),
        in_specs=[row(d), row(d), _full(g), row(dua.shape[1]), row(dq.shape[1]), row(dk.shape[1]), row(dv.shape[1]),
                  _full(wa), _full(wq), _full(wk), _full(wv)],
        out_specs=[row(d), pl.BlockSpec((1, d), lambda i: (0, 0))],
        out_shape=[jax.ShapeDtypeStruct((t, d), F32), jax.ShapeDtypeStruct((1, d), F32)],
        compiler_params=_params("arbitrary"),
    )(dh, h, g, dua, dq, dk, dv, wa, wq, wk, wv)


def _part_tile(r):
    for tr in (256, 128, 64, 32, 16, 8):
        if r % tr == 0:
            return tr
    return r


def _sum_parts(x, name):
    n, r, c = x.shape
    tr = _part_tile(r)

    def body(x_ref, o_ref):
        acc = x_ref[0]
        for p in range(1, n):
            acc = acc + x_ref[p]
        o_ref[...] = acc

    return pl.pallas_call(
        body, name=name, grid=(r // tr,),
        in_specs=[pl.BlockSpec((n, tr, c), lambda i: (0, i, 0))], out_specs=pl.BlockSpec((tr, c), lambda i: (i, 0)),
        out_shape=jax.ShapeDtypeStruct((r, c), F32), compiler_params=_params("arbitrary"),
    )(x)


def _adamw(w, gparts, m, v, name):
    n, r, c = gparts.shape
    tr = _part_tile(r)

    def body(w_ref, g_ref, m_ref, v_ref, go_ref, d_ref, mo_ref, vo_ref):
        g = g_ref[0]
        for p in range(1, n):
            g = g + g_ref[p]
        go_ref[...] = g
        mm = ADAM_B1 * m_ref[...] + (1.0 - ADAM_B1) * g
        vv = ADAM_B2 * v_ref[...] + (1.0 - ADAM_B2) * jnp.square(g)
        mo_ref[...] = mm
        vo_ref[...] = vv
        m_hat = mm / (1.0 - ADAM_B1 ** ADAM_STEP)
        v_hat = vv / (1.0 - ADAM_B2 ** ADAM_STEP)
        d_ref[...] = -ADAM_LR * (m_hat / (jnp.sqrt(v_hat) + ADAM_EPS) + ADAM_WD * w_ref[...])

    row = pl.BlockSpec((tr, c), lambda i: (i, 0))
    return pl.pallas_call(
        body, name=name, grid=(r // tr,),
        in_specs=[row, pl.BlockSpec((n, tr, c), lambda i: (0, i, 0)), row, row], out_specs=[row] * 4,
        out_shape=[jax.ShapeDtypeStruct((r, c), F32)] * 4, compiler_params=_params("arbitrary"),
    )(w, gparts, m, v)


MESH = pl.DeviceIdType.MESH
ANY = pl.BlockSpec(memory_space=pl.ANY)


def _all_gather(x, name):
    r, c = x.shape

    def body(x_ref, out_ref, send_sems, recv_sems, local_sem):
        mx, my, mc = lax.axis_index("x"), lax.axis_index("y"), lax.axis_index("c")
        me, sibling = (mx, my, mc), (mx, my, 1 - mc)
        chips = [(1 - mx, my), (mx, 1 - my), (1 - mx, 1 - my)]

        def slot(px, py, pc):
            return out_ref.at[4 * px + 2 * py + pc]

        def copy(kk, block, to, src=None):
            return pltpu.make_async_remote_copy(
                src_ref=slot(*block) if src is None else src, dst_ref=slot(*block),
                send_sem=send_sems.at[kk], recv_sem=recv_sems.at[kk], device_id=to, device_id_type=MESH)

        mine = pltpu.make_async_copy(x_ref, slot(*me), local_sem)
        mine.start()
        first = [copy(0, me, sibling, src=x_ref)]
        first += [copy(1 + j, me, (*chip, mc), src=x_ref) for j, chip in enumerate(chips)]
        for cp in first:
            cp.start()
        passed = [copy(4 + j, (*chip, mc), sibling) for j, chip in enumerate(chips)]
        for j, chip in enumerate(chips):
            copy(1 + j, (*chip, mc), me).wait_recv()
            passed[j].start()
        copy(0, sibling, me).wait_recv()
        for j, chip in enumerate(chips):
            copy(4 + j, (*chip, 1 - mc), me).wait_recv()
        for cp in first + passed:
            cp.wait_send()
        mine.wait()

    return pl.pallas_call(
        body, name=name, in_specs=[ANY], out_specs=ANY,
        out_shape=jax.ShapeDtypeStruct((N_DEV, r, c), x.dtype),
        scratch_shapes=[pltpu.SemaphoreType.DMA((7,)), pltpu.SemaphoreType.DMA((7,)), pltpu.SemaphoreType.DMA(())],
    )(x)


def _all_to_all(x, name):
    def body(x_ref, out_ref, send_sems, recv_sems, local_sem):
        mx, my, mc = lax.axis_index("x"), lax.axis_index("y"), lax.axis_index("c")
        me = 4 * mx + 2 * my + mc
        mine = pltpu.make_async_copy(x_ref.at[me], out_ref.at[me], local_sem)
        mine.start()
        copies = []
        for kk in range(1, N_DEV):
            px, py, pc = mx ^ (kk >> 2), my ^ ((kk >> 1) & 1), mc ^ (kk & 1)
            peer = 4 * px + 2 * py + pc
            copies.append((peer, pltpu.make_async_remote_copy(
                src_ref=x_ref.at[peer], dst_ref=out_ref.at[me], send_sem=send_sems.at[kk - 1],
                recv_sem=recv_sems.at[kk - 1], device_id=(px, py, pc), device_id_type=MESH)))
        for _, cp in copies:
            cp.start()
        for kk, (peer, cp) in enumerate(copies):
            pltpu.make_async_remote_copy(
                src_ref=x_ref.at[peer], dst_ref=out_ref.at[peer], send_sem=send_sems.at[kk],
                recv_sem=recv_sems.at[kk], device_id=(mx, my, mc), device_id_type=MESH).wait_recv()
        for _, cp in copies:
            cp.wait_send()
        mine.wait()

    return pl.pallas_call(
        body, name=name, in_specs=[ANY], out_specs=ANY,
        out_shape=jax.ShapeDtypeStruct(x.shape, x.dtype),
        scratch_shapes=[pltpu.SemaphoreType.DMA((7,)), pltpu.SemaphoreType.DMA((7,)), pltpu.SemaphoreType.DMA(())],
    )(x)


def _rows_of(a):
    return a.reshape(-1, PACK_W)


def _pack_big(w_in, w_out, w_up, w_down):
    return jnp.concatenate([_rows_of(w_in), _rows_of(w_out), _rows_of(w_up), _rows_of(w_down)], axis=0)


def _unpack_big(buf, shapes):
    out, r0 = [], 0
    for shp in shapes:
        n = 1
        for s in shp:
            n *= s
        out.append(buf[r0:r0 + n // PACK_W].reshape(shp))
        r0 += n // PACK_W
    return out


def _shards_last(a):
    lead = a.shape[:-1]
    return jnp.moveaxis(a.reshape(*lead, N_DEV, a.shape[-1] // N_DEV), -2, 0)


def _shards_rows(a):
    l, n, c = a.shape
    return jnp.moveaxis(a.reshape(l, N_DEV, n // N_DEV, c), 1, 0)


def _flat_pad(parts, mult):
    flat = jnp.concatenate([p.reshape(-1) for p in parts])
    return jnp.pad(flat, (0, (-flat.shape[0]) % mult))


def _unflatten(flat, shapes):
    out, o = [], 0
    for shp in shapes:
        n = 1
        for s in shp:
            n *= s
        out.append(flat[o:o + n].reshape(shp))
        o += n
    return out


def kernel(x, meta_tokens, g_mix, w_in, w_conv, w_pool, pool_scale, w_out, g_mlp, w_up, w_down, g_final, loss_target, m_meta_tokens, m_g_mix, m_w_in, m_w_conv, m_w_pool, m_pool_scale, m_w_out, m_g_mlp, m_w_up, m_w_down, m_g_final, v_meta_tokens, v_g_mix, v_w_in, v_w_conv, v_w_pool, v_pool_scale, v_w_out, v_g_mlp, v_w_up, v_w_down, v_g_final):
    nb, seq, d = x.shape
    depth = g_mix.shape[0]
    lp = BLK + seq
    t = nb * lp
    cw = w_conv.shape[2] * N_DEV
    pw = pool_scale.shape[1]
    pg = pw // N_POOL_GROUPS
    aw = (w_in.shape[2] * N_DEV - 3 * cw - pw) // 3
    ua_w = 3 * cw + pw
    d_ff = w_up.shape[2] * N_DEV
    me = 4 * lax.axis_index("x") + 2 * lax.axis_index("y") + lax.axis_index("c")
    big_shapes = [w_in.shape, w_out.shape, w_up.shape, w_down.shape]

    gathered = _all_gather(_pack_big(w_in, w_out, w_up, w_down).astype(BF16), "gather_weights")
    per_dev = [_unpack_big(gathered[p], big_shapes) for p in range(N_DEV)]
    w_in_f = jnp.concatenate([pd[0] for pd in per_dev], axis=2)
    w_out_f = jnp.concatenate([pd[1] for pd in per_dev], axis=1)
    w_up_f = jnp.stack([pd[2] for pd in per_dev], axis=1)
    w_down_f = jnp.concatenate([pd[3] for pd in per_dev], axis=1)
    small_in = _all_gather(_flat_pad([meta_tokens, w_conv], 8 * BLK).reshape(-1, BLK), "gather_small_weights").reshape(N_DEV, -1)
    meta_f = jnp.moveaxis(small_in[:, :meta_tokens.size].reshape(N_DEV, N_META, d // N_DEV), 0, 1).reshape(N_META, d)
    w_conv_f = small_in[:, meta_tokens.size:meta_tokens.size + w_conv.size].reshape((N_DEV,) + w_conv.shape)
    w_conv_f = jnp.moveaxis(w_conv_f, 0, 2).reshape(depth, w_conv.shape[1], cw)

    wa = w_in_f[:, :, :ua_w]
    wq = w_in_f[:, :, ua_w:ua_w + aw]
    wk = w_in_f[:, :, ua_w + aw:ua_w + 2 * aw]
    wv = w_in_f[:, :, ua_w + 2 * aw:]
    wo_a, wo_b = w_out_f[:, :cw + pw], w_out_f[:, cw + pw:]
    wbd = jnp.zeros((depth, pw, pw), F32)
    for gi in range(N_POOL_GROUPS):
        wbd = wbd.at[:, gi * pg:(gi + 1) * pg, gi * pg:(gi + 1) * pg].set(w_pool[:, gi])
    wbd = wbd.astype(BF16)

    h = jnp.concatenate([jnp.zeros((nb, PAD, d), F32), jnp.broadcast_to(meta_f[None], (nb, N_META, d)), x], axis=1).reshape(t, d)
    saved = []
    for l in range(depth):
        ua, q, k, v, hn1 = _inproj_fwd(h, g_mix[l][None], wa[l], wq[l], wk[l], wv[l])
        ycp = _mixer_fwd(ua, w_conv_f[l], wbd[l], pool_scale[l][None], nb)
        yat = _attn_fwd(q, k, v, nb)
        h_mid = _outproj(h, ycp, yat, wo_a[l], wo_b[l])
        h_next, hn2 = _mlp_fwd(h_mid, g_mlp[l][None], w_up_f[l], w_down_f[l])
        saved.append((h, ua, q, k, v, hn1, ycp, yat, h_mid, hn2))
        h = h_next

    loss_part, dh, dg_final = _loss_head(h, g_final[None], loss_target.reshape(nb * seq, d), nb)

    g_gmix, g_gmlp, g_wconv, g_pscale, g_wpool = [None] * depth, [None] * depth, [None] * depth, [None] * depth, [None] * depth
    g_win, g_wout, g_wup, g_wdown = [None] * depth, [None] * depth, [None] * depth, [None] * depth
    for l in reversed(range(depth)):
        h_in, ua, q, k, v, hn1, ycp, yat, h_mid, hn2 = saved[l]
        dh_mid, act, dm, g_gmlp[l] = _mlp_bwd(dh, h_mid, hn2, g_mlp[l][None], w_up_f[l], w_down_f[l])
        g_wup[l] = _matmul_tn(hn2, dm, "grad_w_up")
        g_wdown[l] = _matmul_tn(act, dh, "grad_w_down")
        dycp, do = _outproj_bwd(dh_mid, wo_a[l], wo_b[l])
        g_wout[l] = jnp.concatenate([_matmul_tn(ycp, dh_mid, "grad_w_out_a"), _matmul_tn(yat, dh_mid, "grad_w_out_b")], axis=0)
        dq, dk, dv = _attn_bwd(q, k, v, do, nb)
        dua, g_wconv[l], g_pscale[l], dwb = _mixer_bwd(ua, dycp, w_conv_f[l], wbd[l], pool_scale[l][None], nb)
        g_wpool[l] = jnp.stack([dwb[gi * pg:(gi + 1) * pg, gi * pg:(gi + 1) * pg] for gi in range(N_POOL_GROUPS)])
        dh, g_gmix[l] = _inproj_bwd(dh_mid, h_in, g_mix[l][None], dua, dq, dk, dv, wa[l], wq[l], wk[l], wv[l])
        g_win[l] = jnp.concatenate([_matmul_tn(hn1, dua, "grad_w_in_a"), _matmul_tn(hn1, dq, "grad_w_in_q"),
                                    _matmul_tn(hn1, dk, "grad_w_in_k"), _matmul_tn(hn1, dv, "grad_w_in_v")], axis=1)
    dh3 = dh.reshape(nb, lp, d)
    grad_x = dh3[:, BLK:]
    g_meta = _sum_parts(dh3[:, PAD:BLK], "sum_meta_grad")

    send = jnp.concatenate([
        _shards_last(jnp.stack(g_win)).reshape(N_DEV, -1, PACK_W), _shards_rows(jnp.stack(g_wout)).reshape(N_DEV, -1, PACK_W),
        _shards_last(jnp.stack(g_wup)).reshape(N_DEV, -1, PACK_W), _shards_rows(jnp.stack(g_wdown)).reshape(N_DEV, -1, PACK_W)], axis=1)
    parts = _all_to_all(send, "exchange_big_grads")
    big = _adamw(_pack_big(w_in, w_out, w_up, w_down), parts, _pack_big(m_w_in, m_w_out, m_w_up, m_w_down),
                 _pack_big(v_w_in, v_w_out, v_w_up, v_w_down), "adamw_big")
    (gr_win, gr_wout, gr_wup, gr_wdown), (de_win, de_wout, de_wup, de_wdown), (nm_win, nm_wout, nm_wup, nm_wdown), \
        (nv_win, nv_wout, nv_wup, nv_wdown) = [_unpack_big(b, big_shapes) for b in big]

    small_full = [g_meta, jnp.concatenate(g_gmix), jnp.stack(g_wconv), jnp.stack(g_wpool), jnp.concatenate(g_pscale),
                  jnp.concatenate(g_gmlp), dg_final.reshape(-1)]
    small_shapes = [a.shape for a in small_full]
    small_sum = _sum_parts(_all_gather(_flat_pad(small_full, 8 * BLK).reshape(-1, BLK), "gather_small_grads"), "sum_small_grads")
    gr_meta, gr_gmix, gr_wconv, gr_wpool, gr_pscale, gr_gmlp, gr_gfinal = _unflatten(small_sum.reshape(-1), small_shapes)
    gr_meta = lax.dynamic_slice_in_dim(gr_meta, me * (d // N_DEV), d // N_DEV, axis=1)
    gr_wconv = lax.dynamic_slice_in_dim(gr_wconv, me * (cw // N_DEV), cw // N_DEV, axis=2)
    small_g = [gr_meta, gr_gmix, gr_wconv, gr_wpool, gr_pscale, gr_gmlp, gr_gfinal]
    local_shapes = [a.shape for a in small_g]
    pack_small = lambda parts_: _flat_pad(parts_, 8 * BLK).reshape(-1, BLK)
    small = _adamw(pack_small([meta_tokens, g_mix, w_conv, w_pool, pool_scale, g_mlp, g_final]), pack_small(small_g)[None],
                   pack_small([m_meta_tokens, m_g_mix, m_w_conv, m_w_pool, m_pool_scale, m_g_mlp, m_g_final]),
                   pack_small([v_meta_tokens, v_g_mix, v_w_conv, v_w_pool, v_pool_scale, v_g_mlp, v_g_final]), "adamw_small")
    _, de_s, nm_s, nv_s = [_unflatten(b.reshape(-1), local_shapes) for b in small]

    loss = lax.psum(loss_part[0, 0], ("x", "y", "c"))

    def ordered(meta, gmix, wconv, wpool, pscale, gmlp, gfinal, win, wout, wup, wdown):
        return [meta, gmix, win, wconv, wpool, pscale, wout, gmlp, wup, wdown, gfinal]

    grads = ordered(*small_g, gr_win, gr_wout, gr_wup, gr_wdown)
    deltas = ordered(*de_s, de_win, de_wout, de_wup, de_wdown)
    new_m = ordered(*nm_s, nm_win, nm_wout, nm_wup, nm_wdown)
    new_v = ordered(*nv_s, nv_win, nv_wout, nv_wup, nv_wdown)
    return (loss, grad_x, *grads, *deltas, *new_m, *new_v)
```

```python
import functools

import jax
import jax.numpy as jnp
from jax import lax
from jax.experimental import pallas as pl
from jax.experimental.pallas import tpu as pltpu

F32 = jnp.float32
BF16 = jnp.bfloat16

N_DEV = 8
N_META = 16
HEAD_DIM = 64
BLK = 128
PAD = BLK - N_META
N_POOL_GROUPS = 4
EPS = 1e-6
PACK_W = 1024

ADAM_LR = 0.001
ADAM_B1 = 0.9
ADAM_B2 = 0.999
ADAM_EPS = 1e-08
ADAM_WD = 0.01
ADAM_STEP = 10

V7X_VMEM_LIMIT = 56 * 1024 * 1024


def _params(*sem):
    return pltpu.CompilerParams(dimension_semantics=sem, vmem_limit_bytes=V7X_VMEM_LIMIT)


def _row_tile(t):
    for tm in (512, 256, 128):
        if t % tm == 0:
            return tm
    raise ValueError(f"row count {t} is not a multiple of 128")


def _full(a):
    nd = a.ndim
    return pl.BlockSpec(a.shape, lambda *_: (0,) * nd)


def _dot(a, b):
    return jnp.dot(a, b, preferred_element_type=F32)


def _dot_nt(a, b):
    return lax.dot_general(a, b, (((1,), (1,)), ((), ())), preferred_element_type=F32)


def _dot_tn(a, b):
    return lax.dot_general(a, b, (((0,), (0,)), ((), ())), preferred_element_type=F32)


def _rms_fwd(x, g):
    rstd = lax.rsqrt(jnp.mean(x * x, axis=-1, keepdims=True) + EPS)
    return x * rstd * g


def _rms_bwd(dy, x, g):
    rstd = lax.rsqrt(jnp.mean(x * x, axis=-1, keepdims=True) + EPS)
    xhat = x * rstd
    dxhat = dy * g
    dx = rstd * (dxhat - xhat * jnp.mean(dxhat * xhat, axis=-1, keepdims=True))
    return dx, jnp.sum(dy * xhat, axis=0, keepdims=True)


def _inproj_fwd(h, g, wa, wq, wk, wv):
    t, d = h.shape
    tm = _row_tile(t)
    ua_w, aw = wa.shape[1], wq.shape[1]

    def body(h_ref, g_ref, wa_ref, wq_ref, wk_ref, wv_ref, ua_ref, q_ref, k_ref, v_ref, hn_ref):
        hn = _rms_fwd(h_ref[...], g_ref[...]).astype(BF16)
        hn_ref[...] = hn
        ua_ref[...] = _dot(hn, wa_ref[...])
        q_ref[...] = _dot(hn, wq_ref[...]).astype(BF16)
        k_ref[...] = _dot(hn, wk_ref[...]).astype(BF16)
        v_ref[...] = _dot(hn, wv_ref[...]).astype(BF16)

    row = lambda w: pl.BlockSpec((tm, w), lambda i: (i, 0))
    return pl.pallas_call(
        body, name="inproj_fwd", grid=(t // tm,),
        in_specs=[row(d), _full(g), _full(wa), _full(wq), _full(wk), _full(wv)],
        out_specs=[row(ua_w), row(aw), row(aw), row(aw), row(d)],
        out_shape=[jax.ShapeDtypeStruct((t, ua_w), F32)] + [jax.ShapeDtypeStruct((t, aw), BF16)] * 3
        + [jax.ShapeDtypeStruct((t, d), BF16)],
        compiler_params=_params("arbitrary"),
    )(h, g, wa, wq, wk, wv)


def _pool_geometry(lp, pw):
    pg = pw // N_POOL_GROUPS
    row = lax.broadcasted_iota(jnp.int32, (lp, pw), 0)
    lane = lax.broadcasted_iota(jnp.int32, (lp, pw), 1)
    grp = [(lane >= g * pg) & (lane < (g + 1) * pg) for g in range(N_POOL_GROUPS)]
    wlen = jnp.where(grp[0], 2, jnp.where(grp[1], 4, jnp.where(grp[2], 8, 16)))
    cnt = jnp.clip(row - (PAD - 1), 1, wlen).astype(F32)
    return grp, cnt, row >= PAD


def _by_group(grp, vals):
    return jnp.where(grp[0], vals[0], jnp.where(grp[1], vals[1], jnp.where(grp[2], vals[2], vals[3])))


def _pooled(p, grp, cnt, real):
    s2 = p + pltpu.roll(p, 1, 0)
    s4 = s2 + pltpu.roll(s2, 2, 0)
    s8 = s4 + pltpu.roll(s4, 4, 0)
    s16 = s8 + pltpu.roll(s8, 8, 0)
    return jnp.where(real, _by_group(grp, (s2, s4, s8, s16)) / cnt - p, 0.0)


def _conv(uu, wc_ref):
    return wc_ref[2:3, :] * uu + wc_ref[1:2, :] * pltpu.roll(uu, 1, 0) + wc_ref[0:1, :] * pltpu.roll(uu, 2, 0)


def _mixer_fwd(ua, wconv, wbd, pscale, nb):
    t = ua.shape[0]
    lp = t // nb
    cw, pw = wconv.shape[1], wbd.shape[0]

    def body(ua_ref, wc_ref, wbd_ref, ps_ref, y_ref):
        cb = ua_ref[:, 0:cw]
        uu = ua_ref[:, cw:2 * cw] * ua_ref[:, 2 * cw:3 * cw]
        y_ref[:, 0:cw] = (cb * _conv(uu, wc_ref)).astype(BF16)
        grp, cnt, real = _pool_geometry(lp, pw)
        pooled = _pooled(ua_ref[:, 3 * cw:3 * cw + pw], grp, cnt, real)
        y_ref[:, cw:cw + pw] = (_dot(pooled.astype(BF16), wbd_ref[...]) * ps_ref[...]).astype(BF16)

    return pl.pallas_call(
        body, name="mixer_fwd", grid=(nb,),
        in_specs=[pl.BlockSpec((lp, ua.shape[1]), lambda s: (s, 0)), _full(wconv), _full(wbd), _full(pscale)],
        out_specs=pl.BlockSpec((lp, cw + pw), lambda s: (s, 0)),
        out_shape=jax.ShapeDtypeStruct((t, cw + pw), BF16),
        compiler_params=_params("arbitrary"),
    )(ua, wconv, wbd, pscale)


def _split_bf16(x, parts):
    out = []
    for _ in range(parts):
        hi = x.astype(BF16)
        out.append(hi)
        x = x - hi.astype(F32)
    return out


ATT_NW = 2
ATT_W = ATT_NW * BLK
ATT_PAIRS = 2
HW = 2 * HEAD_DIM
ATT_SCALE = HEAD_DIM ** -0.5


def _attn_block(qs, kw, valid, carry, mstrict):
    z = _dot_nt(qs, kw)
    ls = jnp.minimum(z, 0.0) - jnp.log1p(jnp.exp(-jnp.abs(z)))
    lk = jnp.where(valid, ls - z, 0.0)
    hi, lo = _split_bf16(lk, 2)
    between = _dot(hi, mstrict) + _dot(lo, mstrict) + carry
    a = jnp.where(valid, jnp.exp(ls + between), 0.0)
    return ls, lk, a


def _attn_consts():
    r = lax.broadcasted_iota(jnp.int32, (2 * BLK, ATT_W), 0)
    c = lax.broadcasted_iota(jnp.int32, (2 * BLK, ATT_W), 1)
    cmr = c - (r & (BLK - 1))
    kr = lax.broadcasted_iota(jnp.int32, (ATT_W, ATT_W), 0)
    kc = lax.broadcasted_iota(jnp.int32, (ATT_W, ATT_W), 1)
    lane = lax.broadcasted_iota(jnp.int32, (1, HW), 1)
    heads = (lane < HEAD_DIM, lane >= HEAD_DIM)
    return c, cmr, kr, kc, heads


def _rows(i):
    return pl.ds(pl.multiple_of(i * BLK, BLK), BLK)


def _stack_heads(x, heads):
    zero = jnp.zeros_like(x)
    return jnp.concatenate([jnp.where(heads[0], x, zero), jnp.where(heads[1], x, zero)], axis=0)


def _unstack_heads(x, heads):
    return jnp.where(heads[0], x[:BLK], x[BLK:])


def _window(i, s):
    start = jnp.maximum(i + 1 - ATT_NW * (s + 1), 0) * BLK
    return start, (i + 1 - ATT_NW * s) * BLK


def _attn_fwd(q, k, v, nb):
    t, aw = q.shape
    lp = t // nb
    nblk = lp // BLK
    pp = min(ATT_PAIRS, aw // HW)
    cols = [slice(p * HW, (p + 1) * HW) for p in range(pp)]

    def body(q_ref, k_ref, v_ref, o_ref):
        c, cmr, kr, kc, heads = _attn_consts()
        mstrict = jnp.where(kr > kc, 1.0, 0.0).astype(BF16)

        def qblock(i, _):
            qs = [_stack_heads(q_ref[_rows(i), cs] * ATT_SCALE, heads) for cs in cols]

            def window(s, st):
                accs, carries = st
                start, end = _window(i, s)
                valid = (cmr < i * BLK - start) & (c < end - start)
                keys = pl.ds(pl.multiple_of(start, BLK), ATT_W)
                new_accs, new_carries = [], []
                for p, cs in enumerate(cols):
                    _, lk, a = _attn_block(qs[p], k_ref[keys, cs], valid, carries[p], mstrict)
                    new_accs.append(accs[p] + _dot(a.astype(BF16), v_ref[keys, cs]))
                    new_carries.append(carries[p] + jnp.sum(lk, axis=1, keepdims=True))
                return tuple(new_accs), tuple(new_carries)

            init = (tuple(jnp.zeros((2 * BLK, HW), F32) for _ in cols), tuple(jnp.zeros((2 * BLK, 1), F32) for _ in cols))
            accs, _ = lax.fori_loop(0, (i + ATT_NW) // ATT_NW, window, init)
            for p, cs in enumerate(cols):
                o_ref[_rows(i), cs] = _unstack_heads(accs[p], heads).astype(BF16)
            return 0

        lax.fori_loop(0, nblk, qblock, 0)

    spec = pl.BlockSpec((lp, pp * HW), lambda s, p: (s, p))
    return pl.pallas_call(
        body, name="attn_fwd", grid=(nb, aw // (pp * HW)),
        in_specs=[spec, spec, spec], out_specs=spec,
        out_shape=jax.ShapeDtypeStruct((t, aw), BF16),
        compiler_params=_params("arbitrary", "arbitrary"),
    )(q, k, v)


def _outproj(h, ya, yb, wa, wb):
    t, d = h.shape
    tm = _row_tile(t)

    def body(h_ref, ya_ref, yb_ref, wa_ref, wb_ref, o_ref):
        o_ref[...] = h_ref[...] + _dot(ya_ref[...], wa_ref[...]) + _dot(yb_ref[...], wb_ref[...])

    row = lambda w: pl.BlockSpec((tm, w), lambda i: (i, 0))
    return pl.pallas_call(
        body, name="outproj", grid=(t // tm,),
        in_specs=[row(d), row(ya.shape[1]), row(yb.shape[1]), _full(wa), _full(wb)],
        out_specs=row(d), out_shape=jax.ShapeDtypeStruct((t, d), F32),
        compiler_params=_params("arbitrary"),
    )(h, ya, yb, wa, wb)


def _mlp_fwd(h, g, wup, wdown):
    t, d = h.shape
    tm = _row_tile(t)
    nff, _, fc = wup.shape

    def body(h_ref, g_ref, wu_ref, wd_ref, o_ref, hn_ref, acc):
        kk = pl.program_id(1)

        @pl.when(kk == 0)
        def _():
            x = h_ref[...]
            hn_ref[...] = _rms_fwd(x, g_ref[...]).astype(BF16)
            acc[...] = x

        m = _dot(hn_ref[...], wu_ref[...])
        a = jnp.square(jnp.maximum(m, 0.0)).astype(BF16)
        acc[...] += _dot(a, wd_ref[...])

        @pl.when(kk == nff - 1)
        def _():
            o_ref[...] = acc[...]

    row = pl.BlockSpec((tm, d), lambda i, kk: (i, 0))
    return pl.pallas_call(
        body, name="mlp_fwd", grid=(t // tm, nff),
        in_specs=[row, _full(g), pl.BlockSpec((None, d, fc), lambda i, kk: (kk, 0, 0)),
                  pl.BlockSpec((fc, d), lambda i, kk: (kk, 0))],
        out_specs=[row, row],
        out_shape=[jax.ShapeDtypeStruct((t, d), F32), jax.ShapeDtypeStruct((t, d), BF16)],
        scratch_shapes=[pltpu.VMEM((tm, d), F32)],
        compiler_params=_params("arbitrary", "arbitrary"),
    )(h, g, wup, wdown)


def _loss_head(h, g, target, nb):
    t, d = h.shape
    nblk = t // nb // BLK
    nx = nblk - 1

    def body(h_ref, g_ref, t_ref, loss_ref, dh_ref, dg_ref):
        s, i = pl.program_id(0), pl.program_id(1)

        @pl.when((s == 0) & (i == 0))
        def _():
            loss_ref[...] = jnp.zeros_like(loss_ref)
            dg_ref[...] = jnp.zeros_like(dg_ref)

        @pl.when(i == 0)
        def _():
            dh_ref[...] = jnp.zeros_like(dh_ref)

        @pl.when(i > 0)
        def _():
            x, gg = h_ref[...], g_ref[...]
            err = _rms_fwd(x, gg) - t_ref[...]
            loss_ref[...] += jnp.sum(err * err) * (0.5 / d)
            dx, dg = _rms_bwd(err * (1.0 / d), x, gg)
            dh_ref[...] = dx
            dg_ref[...] += dg

    hspec = pl.BlockSpec((BLK, d), lambda s, i: (s * nblk + i, 0))
    return pl.pallas_call(
        body, name="loss_head", grid=(nb, nblk),
        in_specs=[hspec, _full(g), pl.BlockSpec((BLK, d), lambda s, i: (s * nx + jnp.maximum(i - 1, 0), 0))],
        out_specs=[pl.BlockSpec((1, BLK), lambda s, i: (0, 0)), hspec, pl.BlockSpec((1, d), lambda s, i: (0, 0))],
        out_shape=[jax.ShapeDtypeStruct((1, BLK), F32), jax.ShapeDtypeStruct((t, d), F32), jax.ShapeDtypeStruct((1, d), F32)],
        compiler_params=_params("arbitrary", "arbitrary"),
    )(h, g, target)


def _mlp_bwd(dh, h, hn, g, wup, wdown):
    t, d = h.shape
    tm = _row_tile(t)
    nff, _, fc = wup.shape

    def body(dh_ref, h_ref, hn_ref, g_ref, wu_ref, wd_ref, dhm_ref, a_ref, dm_ref, dg_ref, dhn, dhb):
        i, kk = pl.program_id(0), pl.program_id(1)

        @pl.when((i == 0) & (kk == 0))
        def _():
            dg_ref[...] = jnp.zeros_like(dg_ref)

        @pl.when(kk == 0)
        def _():
            dhb[...] = dh_ref[...].astype(BF16)
            dhn[...] = jnp.zeros_like(dhn)

        r = jnp.maximum(_dot(hn_ref[...], wu_ref[...]), 0.0)
        a_ref[...] = (r * r).astype(BF16)
        dm = (_dot_nt(dhb[...], wd_ref[...]) * (2.0 * r)).astype(BF16)
        dm_ref[...] = dm
        dhn[...] += _dot_nt(dm, wu_ref[...])

        @pl.when(kk == nff - 1)
        def _():
            dx, dg = _rms_bwd(dhn[...], h_ref[...], g_ref[...])
            dhm_ref[...] = dh_ref[...] + dx
            dg_ref[...] += dg

    row = pl.BlockSpec((tm, d), lambda i, kk: (i, 0))
    ff = pl.BlockSpec((tm, fc), lambda i, kk: (i, kk))
    return pl.pallas_call(
        body, name="mlp_bwd", grid=(t // tm, nff),
        in_specs=[row, row, row, _full(g), pl.BlockSpec((None, d, fc), lambda i, kk: (kk, 0, 0)),
                  pl.BlockSpec((fc, d), lambda i, kk: (kk, 0))],
        out_specs=[row, ff, ff, pl.BlockSpec((1, d), lambda i, kk: (0, 0))],
        out_shape=[jax.ShapeDtypeStruct((t, d), F32), jax.ShapeDtypeStruct((t, nff * fc), BF16),
                   jax.ShapeDtypeStruct((t, nff * fc), BF16), jax.ShapeDtypeStruct((1, d), F32)],
        scratch_shapes=[pltpu.VMEM((tm, d), F32), pltpu.VMEM((tm, d), BF16)],
        compiler_params=_params("arbitrary", "arbitrary"),
    )(dh, h, hn, g, wup, wdown)


def _matmul_tn(x, y, name):
    t, k1 = x.shape
    n = y.shape[1]
    tt = _row_tile(t)
    tn = min(n, 512)

    def body(x_ref, y_ref, o_ref):
        @pl.when(pl.program_id(1) == 0)
        def _():
            o_ref[...] = jnp.zeros_like(o_ref)

        o_ref[...] += _dot_tn(x_ref[...].astype(BF16), y_ref[...].astype(BF16))

    return pl.pallas_call(
        body, name=name, grid=(n // tn, t // tt),
        in_specs=[pl.BlockSpec((tt, k1), lambda j, r: (r, 0)), pl.BlockSpec((tt, tn), lambda j, r: (r, j))],
        out_specs=pl.BlockSpec((k1, tn), lambda j, r: (0, j)),
        out_shape=jax.ShapeDtypeStruct((k1, n), F32),
        compiler_params=_params("arbitrary", "arbitrary"),
    )(x, y)


def _outproj_bwd(dh, wa, wb):
    t, d = dh.shape
    tm = _row_tile(t)
    na, nbw = wa.shape[0], wb.shape[0]

    def body(dh_ref, wa_ref, wb_ref, da_ref, db_ref):
        x = dh_ref[...].astype(BF16)
        da_ref[...] = _dot_nt(x, wa_ref[...])
        db_ref[...] = _dot_nt(x, wb_ref[...]).astype(BF16)

    row = lambda w: pl.BlockSpec((tm, w), lambda i: (i, 0))
    return pl.pallas_call(
        body, name="outproj_bwd", grid=(t // tm,),
        in_specs=[row(d), _full(wa), _full(wb)], out_specs=[row(na), row(nbw)],
        out_shape=[jax.ShapeDtypeStruct((t, na), F32), jax.ShapeDtypeStruct((t, nbw), BF16)],
        compiler_params=_params("arbitrary"),
    )(dh, wa, wb)


def _attn_bwd(q, k, v, do, nb):
    t, aw = q.shape
    lp = t // nb
    nblk = lp // BLK
    pp = min(ATT_PAIRS, aw // HW)
    cols = [slice(p * HW, (p + 1) * HW) for p in range(pp)]
    max_windows = (nblk + ATT_NW - 1) // ATT_NW

    def body(q_ref, k_ref, v_ref, do_ref, dq_ref, dk_ref, dv_ref, dk_acc, dv_acc, g_s, sig_s):
        c, cmr, kr, kc, heads = _attn_consts()
        mstrict = jnp.where(kr > kc, 1.0, 0.0).astype(BF16)
        mprefix = jnp.where(kr < kc, 1.0, 0.0).astype(BF16)
        dk_acc[...] = jnp.zeros_like(dk_acc)
        dv_acc[...] = jnp.zeros_like(dv_acc)

        def qblock(i, _):
            qs = [_stack_heads(q_ref[_rows(i), cs] * ATT_SCALE, heads) for cs in cols]
            dos = [_stack_heads(do_ref[_rows(i), cs], heads) for cs in cols]
            nwin = (i + ATT_NW) // ATT_NW

            def geometry(s):
                start, end = _window(i, s)
                valid = (cmr < i * BLK - start) & (c < end - start)
                return valid, pl.ds(pl.multiple_of(start, BLK), ATT_W)

            def down(s, carries):
                valid, keys = geometry(s)
                new_carries = []
                for p, cs in enumerate(cols):
                    ls, lk, a = _attn_block(qs[p], k_ref[keys, cs], valid, carries[p], mstrict)
                    g_s[p, s] = _dot_nt(dos[p], v_ref[keys, cs]) * a
                    sig_s[p, s] = jnp.exp(ls)
                    dv_acc[keys, cs] += _dot_tn(a.astype(BF16), dos[p])
                    new_carries.append(carries[p] + jnp.sum(lk, axis=1, keepdims=True))
                return tuple(new_carries)

            lax.fori_loop(0, nwin, down, tuple(jnp.zeros((2 * BLK, 1), F32) for _ in cols))

            def up(ss, st):
                accs, carries = st
                s = nwin - 1 - ss
                valid, keys = geometry(s)
                new_accs, new_carries = [], []
                for p, cs in enumerate(cols):
                    g, sig = g_s[p, s], sig_s[p, s]
                    p0, p1, p2 = _split_bf16(g, 3)
                    before = _dot(p0, mprefix) + _dot(p1, mprefix) + _dot(p2, mprefix) + carries[p]
                    dz = jnp.where(valid, g * (1.0 - sig) - before * sig, 0.0).astype(BF16)
                    new_accs.append(accs[p] + _dot(dz, k_ref[keys, cs]))
                    dk_acc[keys, cs] += _dot_tn(dz, qs[p])
                    new_carries.append(carries[p] + jnp.sum(g, axis=1, keepdims=True))
                return tuple(new_accs), tuple(new_carries)

            init = (tuple(jnp.zeros((2 * BLK, HW), F32) for _ in cols), tuple(jnp.zeros((2 * BLK, 1), F32) for _ in cols))
            accs, _ = lax.fori_loop(0, nwin, up, init)
            for p, cs in enumerate(cols):
                dq_ref[_rows(i), cs] = (_unstack_heads(accs[p], heads) * ATT_SCALE).astype(BF16)
            return 0

        lax.fori_loop(0, nblk, qblock, 0)
        dk_ref[...] = dk_acc[...].astype(BF16)
        dv_ref[...] = dv_acc[...].astype(BF16)

    spec = pl.BlockSpec((lp, pp * HW), lambda s, p: (s, p))
    return pl.pallas_call(
        body, name="attn_bwd", grid=(nb, aw // (pp * HW)),
        in_specs=[spec] * 4, out_specs=[spec] * 3,
        out_shape=[jax.ShapeDtypeStruct((t, aw), BF16)] * 3,
        scratch_shapes=[pltpu.VMEM((lp, pp * HW), F32), pltpu.VMEM((lp, pp * HW), F32),
                        pltpu.VMEM((pp, max_windows, 2 * BLK, ATT_W), F32), pltpu.VMEM((pp, max_windows, 2 * BLK, ATT_W), F32)],
        compiler_params=_params("arbitrary", "arbitrary"),
    )(q, k, v, do)


def _mixer_bwd(ua, dy, wconv, wbd, pscale, nb):
    t = ua.shape[0]
    lp = t // nb
    cw, pw = wconv.shape[1], wbd.shape[0]

    def body(ua_ref, dy_ref, wc_ref, wbd_ref, ps_ref, du_ref, dwc_ref, dps_ref, dwb_ref):
        @pl.when(pl.program_id(0) == 0)
        def _():
            dwc_ref[...] = jnp.zeros_like(dwc_ref)
            dps_ref[...] = jnp.zeros_like(dps_ref)
            dwb_ref[...] = jnp.zeros_like(dwb_ref)

        up = lambda x, n: pltpu.roll(x, lp - n, 0)
        cb, cc, cx = ua_ref[:, 0:cw], ua_ref[:, cw:2 * cw], ua_ref[:, 2 * cw:3 * cw]
        uu = cc * cx
        dyc = dy_ref[:, 0:cw]
        du_ref[:, 0:cw] = (dyc * _conv(uu, wc_ref)).astype(BF16)
        dcv = dyc * cb
        duu = wc_ref[2:3, :] * dcv + wc_ref[1:2, :] * up(dcv, 1) + wc_ref[0:1, :] * up(dcv, 2)
        du_ref[:, cw:2 * cw] = (duu * cx).astype(BF16)
        du_ref[:, 2 * cw:3 * cw] = (duu * cc).astype(BF16)
        dwc_ref[0:1, :] += jnp.sum(dcv * pltpu.roll(uu, 2, 0), axis=0, keepdims=True)
        dwc_ref[1:2, :] += jnp.sum(dcv * pltpu.roll(uu, 1, 0), axis=0, keepdims=True)
        dwc_ref[2:3, :] += jnp.sum(dcv * uu, axis=0, keepdims=True)
        grp, cnt, real = _pool_geometry(lp, pw)
        p = ua_ref[:, 3 * cw:3 * cw + pw]
        pooled = _pooled(p, grp, cnt, real).astype(BF16)
        dyp = dy_ref[:, cw:cw + pw]
        dps_ref[...] += jnp.sum(dyp * _dot(pooled, wbd_ref[...]), axis=0, keepdims=True)
        dpre = (dyp * ps_ref[...]).astype(BF16)
        dwb_ref[...] += _dot_tn(pooled, dpre)
        dpooled = jnp.where(real, _dot_nt(dpre, wbd_ref[...]), 0.0)
        xm = dpooled / cnt
        l2 = xm + up(xm, 1)
        l4 = l2 + up(l2, 2)
        l8 = l4 + up(l4, 4)
        l16 = l8 + up(l8, 8)
        du_ref[:, 3 * cw:3 * cw + pw] = jnp.where(real, _by_group(grp, (l2, l4, l8, l16)) - dpooled, 0.0).astype(BF16)

    seq = lambda w: pl.BlockSpec((lp, w), lambda s: (s, 0))
    return pl.pallas_call(
        body, name="mixer_bwd", grid=(nb,),
        in_specs=[seq(ua.shape[1]), seq(cw + pw), _full(wconv), _full(wbd), _full(pscale)],
        out_specs=[seq(ua.shape[1]), pl.BlockSpec((3, cw), lambda s: (0, 0)), pl.BlockSpec((1, pw), lambda s: (0, 0)),
                   pl.BlockSpec((pw, pw), lambda s: (0, 0))],
        out_shape=[jax.ShapeDtypeStruct(ua.shape, BF16), jax.ShapeDtypeStruct((3, cw), F32),
                   jax.ShapeDtypeStruct((1, pw), F32), jax.ShapeDtypeStruct((pw, pw), F32)],
        compiler_params=_params("arbitrary"),
    )(ua, dy, wconv, wbd, pscale)


def _inproj_bwd(dh, h, g, dua, dq, dk, dv, wa, wq, wk, wv):
    t, d = h.shape
    tm = _row_tile(t)

    def body(dh_ref, h_ref, g_ref, dua_ref, dq_ref, dk_ref, dv_ref, wa_ref, wq_ref, wk_ref, wv_ref, o_ref, dg_ref):
        @pl.when(pl.program_id(0) == 0)
        def _():
            dg_ref[...] = jnp.zeros_like(dg_ref)

        dhn = (_dot_nt(dua_ref[...], wa_ref[...]) + _dot_nt(dq_ref[...], wq_ref[...])
               + _dot_nt(dk_ref[...], wk_ref[...]) + _dot_nt(dv_ref[...], wv_ref[...]))
        dx, dg = _rms_bwd(dhn, h_ref[...], g_ref[...])
        o_ref[...] = dh_ref[...] + dx
        dg_ref[...] += dg

    row = lambda w: pl.BlockSpec((tm, w), lambda i: (i, 0))
    return pl.pallas_call(
        body, name="inproj_bwd", grid=(t // tm,),
        in_specs=[row(d), row(d), _full(g), row(dua.shape[1]), row(dq.shape[1]), row(dk.shape[1]), row(dv.shape[1]),
                  _full(wa), _full(wq), _full(wk), _full(wv)],
        out_specs=[row(d), pl.BlockSpec((1, d), lambda i: (0, 0))],
        out_shape=[jax.ShapeDtypeStruct((t, d), F32), jax.ShapeDtypeStruct((1, d), F32)],
        compiler_params=_params("arbitrary"),
    )(dh, h, g, dua, dq, dk, dv, wa, wq, wk, wv)


def _part_tile(r):
    for tr in (256, 128, 64, 32, 16, 8):
        if r % tr == 0:
            return tr
    return r


def _sum_parts(x, name):
    n, r, c = x.shape
    tr = _part_tile(r)

    def body(x_ref, o_ref):
        acc = x_ref[0]
        for p in range(1, n):
            acc = acc + x_ref[p]
        o_ref[...] = acc

    return pl.pallas_call(
        body, name=name, grid=(r // tr,),
        in_specs=[pl.BlockSpec((n, tr, c), lambda i: (0, i, 0))], out_specs=pl.BlockSpec((tr, c), lambda i: (i, 0)),
        out_shape=jax.ShapeDtypeStruct((r, c), F32), compiler_params=_params("arbitrary"),
    )(x)


def _adamw(w, gparts, m, v, name):
    n, r, c = gparts.shape
    tr = _part_tile(r)

    def body(w_ref, g_ref, m_ref, v_ref, go_ref, d_ref, mo_ref, vo_ref):
        g = g_ref[0]
        for p in range(1, n):
            g = g + g_ref[p]
        go_ref[...] = g
        mm = ADAM_B1 * m_ref[...] + (1.0 - ADAM_B1) * g
        vv = ADAM_B2 * v_ref[...] + (1.0 - ADAM_B2) * jnp.square(g)
        mo_ref[...] = mm
        vo_ref[...] = vv
        m_hat = mm / (1.0 - ADAM_B1 ** ADAM_STEP)
        v_hat = vv / (1.0 - ADAM_B2 ** ADAM_STEP)
        d_ref[...] = -ADAM_LR * (m_hat / (jnp.sqrt(v_hat) + ADAM_EPS) + ADAM_WD * w_ref[...])

    row = pl.BlockSpec((tr, c), lambda i: (i, 0))
    return pl.pallas_call(
        body, name=name, grid=(r // tr,),
        in_specs=[row, pl.BlockSpec((n, tr, c), lambda i: (0, i, 0)), row, row], out_specs=[row] * 4,
        out_shape=[jax.ShapeDtypeStruct((r, c), F32)] * 4, compiler_params=_params("arbitrary"),
    )(w, gparts, m, v)


MESH = pl.DeviceIdType.MESH
ANY = pl.BlockSpec(memory_space=pl.ANY)


def _all_gather(x, name):
    r, c = x.shape

    def body(x_ref, out_ref, send_sems, recv_sems, local_sem):
        mx, my, mc = lax.axis_index("x"), lax.axis_index("y"), lax.axis_index("c")
        me, sibling = (mx, my, mc), (mx, my, 1 - mc)
        chips = [(1 - mx, my), (mx, 1 - my), (1 - mx, 1 - my)]

        def slot(px, py, pc):
            return out_ref.at[4 * px + 2 * py + pc]

        def copy(kk, block, to, src=None):
            return pltpu.make_async_remote_copy(
                src_ref=slot(*block) if src is None else src, dst_ref=slot(*block),
                send_sem=send_sems.at[kk], recv_sem=recv_sems.at[kk], device_id=to, device_id_type=MESH)

        mine = pltpu.make_async_copy(x_ref, slot(*me), local_sem)
        mine.start()
        first = [copy(0, me, sibling, src=x_ref)]
        first += [copy(1 + j, me, (*chip, mc), src=x_ref) for j, chip in enumerate(chips)]
        for cp in first:
            cp.start()
        passed = [copy(4 + j, (*chip, mc), sibling) for j, chip in enumerate(chips)]
        for j, chip in enumerate(chips):
            copy(1 + j, (*chip, mc), me).wait_recv()
            passed[j].start()
        copy(0, sibling, me).wait_recv()
        for j, chip in enumerate(chips):
            copy(4 + j, (*chip, 1 - mc), me).wait_recv()
        for cp in first + passed:
            cp.wait_send()
        mine.wait()

    return pl.pallas_call(
        body, name=name, in_specs=[ANY], out_specs=ANY,
        out_shape=jax.ShapeDtypeStruct((N_DEV, r, c), x.dtype),
        scratch_shapes=[pltpu.SemaphoreType.DMA((7,)), pltpu.SemaphoreType.DMA((7,)), pltpu.SemaphoreType.DMA(())],
    )(x)


def _all_to_all(x, name):
    def body(x_ref, out_ref, send_sems, recv_sems, local_sem):
        mx, my, mc = lax.axis_index("x"), lax.axis_index("y"), lax.axis_index("c")
        me = 4 * mx + 2 * my + mc
        mine = pltpu.make_async_copy(x_ref.at[me], out_ref.at[me], local_sem)
        mine.start()
        copies = []
        for kk in range(1, N_DEV):
            px, py, pc = mx ^ (kk >> 2), my ^ ((kk >> 1) & 1), mc ^ (kk & 1)
            peer = 4 * px + 2 * py + pc
            copies.append((peer, pltpu.make_async_remote_copy(
                src_ref=x_ref.at[peer], dst_ref=out_ref.at[me], send_sem=send_sems.at[kk - 1],
                recv_sem=recv_sems.at[kk - 1], device_id=(px, py, pc), device_id_type=MESH)))
        for _, cp in copies:
            cp.start()
        for kk, (peer, cp) in enumerate(copies):
            pltpu.make_async_remote_copy(
                src_ref=x_ref.at[peer], dst_ref=out_ref.at[peer], send_sem=send_sems.at[kk],
                recv_sem=recv_sems.at[kk], device_id=(mx, my, mc), device_id_type=MESH).wait_recv()
        for _, cp in copies:
            cp.wait_send()
        mine.wait()

    return pl.pallas_call(
        body, name=name, in_specs=[ANY], out_specs=ANY,
        out_shape=jax.ShapeDtypeStruct(x.shape, x.dtype),
        scratch_shapes=[pltpu.SemaphoreType.DMA((7,)), pltpu.SemaphoreType.DMA((7,)), pltpu.SemaphoreType.DMA(())],
    )(x)


def _rows_of(a):
    return a.reshape(-1, PACK_W)


def _pack_big(w_in, w_out, w_up, w_down):
    return jnp.concatenate([_rows_of(w_in), _rows_of(w_out), _rows_of(w_up), _rows_of(w_down)], axis=0)


def _unpack_big(buf, shapes):
    out, r0 = [], 0
    for shp in shapes:
        n = 1
        for s in shp:
            n *= s
        out.append(buf[r0:r0 + n // PACK_W].reshape(shp))
        r0 += n // PACK_W
    return out


def _shards_last(a):
    lead = a.shape[:-1]
    return jnp.moveaxis(a.reshape(*lead, N_DEV, a.shape[-1] // N_DEV), -2, 0)


def _shards_rows(a):
    l, n, c = a.shape
    return jnp.moveaxis(a.reshape(l, N_DEV, n // N_DEV, c), 1, 0)


def _flat_pad(parts, mult):
    flat = jnp.concatenate([p.reshape(-1) for p in parts])
    return jnp.pad(flat, (0, (-flat.shape[0]) % mult))


def _unflatten(flat, shapes):
    out, o = [], 0
    for shp in shapes:
        n = 1
        for s in shp:
            n *= s
        out.append(flat[o:o + n].reshape(shp))
        o += n
    return out


def kernel(x, meta_tokens, g_mix, w_in, w_conv, w_pool, pool_scale, w_out, g_mlp, w_up, w_down, g_final, loss_target, m_meta_tokens, m_g_mix, m_w_in, m_w_conv, m_w_pool, m_pool_scale, m_w_out, m_g_mlp, m_w_up, m_w_down, m_g_final, v_meta_tokens, v_g_mix, v_w_in, v_w_conv, v_w_pool, v_pool_scale, v_w_out, v_g_mlp, v_w_up, v_w_down, v_g_final):
    nb, seq, d = x.shape
    depth = g_mix.shape[0]
    lp = BLK + seq
    t = nb * lp
    cw = w_conv.shape[2] * N_DEV
    pw = pool_scale.shape[1]
    pg = pw // N_POOL_GROUPS
    aw = (w_in.shape[2] * N_DEV - 3 * cw - pw) // 3
    ua_w = 3 * cw + pw
    d_ff = w_up.shape[2] * N_DEV
    me = 4 * lax.axis_index("x") + 2 * lax.axis_index("y") + lax.axis_index("c")
    big_shapes = [w_in.shape, w_out.shape, w_up.shape, w_down.shape]

    gathered = _all_gather(_pack_big(w_in, w_out, w_up, w_down).astype(BF16), "gather_weights")
    per_dev = [_unpack_big(gathered[p], big_shapes) for p in range(N_DEV)]
    w_in_f = jnp.concatenate([pd[0] for pd in per_dev], axis=2)
    w_out_f = jnp.concatenate([pd[1] for pd in per_dev], axis=1)
    w_up_f = jnp.stack([pd[2] for pd in per_dev], axis=1)
    w_down_f = jnp.concatenate([pd[3] for pd in per_dev], axis=1)
    small_in = _all_gather(_flat_pad([meta_tokens, w_conv], 8 * BLK).reshape(-1, BLK), "gather_small_weights").reshape(N_DEV, -1)
    meta_f = jnp.moveaxis(small_in[:, :meta_tokens.size].reshape(N_DEV, N_META, d // N_DEV), 0, 1).reshape(N_META, d)
    w_conv_f = small_in[:, meta_tokens.size:meta_tokens.size + w_conv.size].reshape((N_DEV,) + w_conv.shape)
    w_conv_f = jnp.moveaxis(w_conv_f, 0, 2).reshape(depth, w_conv.shape[1], cw)

    wa = w_in_f[:, :, :ua_w]
    wq = w_in_f[:, :, ua_w:ua_w + aw]
    wk = w_in_f[:, :, ua_w + aw:ua_w + 2 * aw]
    wv = w_in_f[:, :, ua_w + 2 * aw:]
    wo_a, wo_b = w_out_f[:, :cw + pw], w_out_f[:, cw + pw:]
    wbd = jnp.zeros((depth, pw, pw), F32)
    for gi in range(N_POOL_GROUPS):
        wbd = wbd.at[:, gi * pg:(gi + 1) * pg, gi * pg:(gi + 1) * pg].set(w_pool[:, gi])
    wbd = wbd.astype(BF16)

    h = jnp.concatenate([jnp.zeros((nb, PAD, d), F32), jnp.broadcast_to(meta_f[None], (nb, N_META, d)), x], axis=1).reshape(t, d)
    saved = []
    for l in range(depth):
        ua, q, k, v, hn1 = _inproj_fwd(h, g_mix[l][None], wa[l], wq[l], wk[l], wv[l])
        ycp = _mixer_fwd(ua, w_conv_f[l], wbd[l], pool_scale[l][None], nb)
        yat = _attn_fwd(q, k, v, nb)
        h_mid = _outproj(h, ycp, yat, wo_a[l], wo_b[l])
        h_next, hn2 = _mlp_fwd(h_mid, g_mlp[l][None], w_up_f[l], w_down_f[l])
        saved.append((h, ua, q, k, v, hn1, ycp, yat, h_mid, hn2))
        h = h_next

    loss_part, dh, dg_final = _loss_head(h, g_final[None], loss_target.reshape(nb * seq, d), nb)

    g_gmix, g_gmlp, g_wconv, g_pscale, g_wpool = [None] * depth, [None] * depth, [None] * depth, [None] * depth, [None] * depth
    g_win, g_wout, g_wup, g_wdown = [None] * depth, [None] * depth, [None] * depth, [None] * depth
    for l in reversed(range(depth)):
        h_in, ua, q, k, v, hn1, ycp, yat, h_mid, hn2 = saved[l]
        dh_mid, act, dm, g_gmlp[l] = _mlp_bwd(dh, h_mid, hn2, g_mlp[l][None], w_up_f[l], w_down_f[l])
        g_wup[l] = _matmul_tn(hn2, dm, "grad_w_up")
        g_wdown[l] = _matmul_tn(act, dh, "grad_w_down")
        dycp, do = _outproj_bwd(dh_mid, wo_a[l], wo_b[l])
        g_wout[l] = jnp.concatenate([_matmul_tn(ycp, dh_mid, "grad_w_out_a"), _matmul_tn(yat, dh_mid, "grad_w_out_b")], axis=0)
        dq, dk, dv = _attn_bwd(q, k, v, do, nb)
        dua, g_wconv[l], g_pscale[l], dwb = _mixer_bwd(ua, dycp, w_conv_f[l], wbd[l], pool_scale[l][None], nb)
        g_wpool[l] = jnp.stack([dwb[gi * pg:(gi + 1) * pg, gi * pg:(gi + 1) * pg] for gi in range(N_POOL_GROUPS)])
        dh, g_gmix[l] = _inproj_bwd(dh_mid, h_in, g_mix[l][None], dua, dq, dk, dv, wa[l], wq[l], wk[l], wv[l])
        g_win[l] = jnp.concatenate([_matmul_tn(hn1, dua, "grad_w_in_a"), _matmul_tn(hn1, dq, "grad_w_in_q"),
                                    _matmul_tn(hn1, dk, "grad_w_in_k"), _matmul_tn(hn1, dv, "grad_w_in_v")], axis=1)
    dh3 = dh.reshape(nb, lp, d)
    grad_x = dh3[:, BLK:]
    g_meta = _sum_parts(dh3[:, PAD:BLK], "sum_meta_grad")

    send = jnp.concatenate([
        _shards_last(jnp.stack(g_win)).reshape(N_DEV, -1, PACK_W), _shards_rows(jnp.stack(g_wout)).reshape(N_DEV, -1, PACK_W),
        _shards_last(jnp.stack(g_wup)).reshape(N_DEV, -1, PACK_W), _shards_rows(jnp.stack(g_wdown)).reshape(N_DEV, -1, PACK_W)], axis=1)
    parts = _all_to_all(send, "exchange_big_grads")
    big = _adamw(_pack_big(w_in, w_out, w_up, w_down), parts, _pack_big(m_w_in, m_w_out, m_w_up, m_w_down),
                 _pack_big(v_w_in, v_w_out, v_w_up, v_w_down), "adamw_big")
    (gr_win, gr_wout, gr_wup, gr_wdown), (de_win, de_wout, de_wup, de_wdown), (nm_win, nm_wout, nm_wup, nm_wdown), \
        (nv_win, nv_wout, nv_wup, nv_wdown) = [_unpack_big(b, big_shapes) for b in big]

    small_full = [g_meta, jnp.concatenate(g_gmix), jnp.stack(g_wconv), jnp.stack(g_wpool), jnp.concatenate(g_pscale),
                  jnp.concatenate(g_gmlp), dg_final.reshape(-1)]
    small_shapes = [a.shape for a in small_full]
    small_sum = _sum_parts(_all_gather(_flat_pad(small_full, 8 * BLK).reshape(-1, BLK), "gather_small_grads"), "sum_small_grads")
    gr_meta, gr_gmix, gr_wconv, gr_wpool, gr_pscale, gr_gmlp, gr_gfinal = _unflatten(small_sum.reshape(-1), small_shapes)
    gr_meta = lax.dynamic_slice_in_dim(gr_meta, me * (d // N_DEV), d // N_DEV, axis=1)
    gr_wconv = lax.dynamic_slice_in_dim(gr_wconv, me * (cw // N_DEV), cw // N_DEV, axis=2)
    small_g = [gr_meta, gr_gmix, gr_wconv, gr_wpool, gr_pscale, gr_gmlp, gr_gfinal]
    local_shapes = [a.shape for a in small_g]
    pack_small = lambda parts_: _flat_pad(parts_, 8 * BLK).reshape(-1, BLK)
    small = _adamw(pack_small([meta_tokens, g_mix, w_conv, w_pool, pool_scale, g_mlp, g_final]), pack_small(small_g)[None],
                   pack_small([m_meta_tokens, m_g_mix, m_w_conv, m_w_pool, m_pool_scale, m_g_mlp, m_g_final]),
                   pack_small([v_meta_tokens, v_g_mix, v_w_conv, v_w_pool, v_pool_scale, v_g_mlp, v_g_final]), "adamw_small")
    _, de_s, nm_s, nv_s = [_unflatten(b.reshape(-1), local_shapes) for b in small]

    loss = lax.psum(loss_part[0, 0], ("x", "y", "c"))

    def ordered(meta, gmix, wconv, wpool, pscale, gmlp, gfinal, win, wout, wup, wdown):
        return [meta, gmix, win, wconv, wpool, pscale, wout, gmlp, wup, wdown, gfinal]

    grads = ordered(*small_g, gr_win, gr_wout, gr_wup, gr_wdown)
    deltas = ordered(*de_s, de_win, de_wout, de_wup, de_wdown)
    new_m = ordered(*nm_s, nm_win, nm_wout, nm_wup, nm_wdown)
    new_v = ordered(*nv_s, nv_win, nv_wout, nv_wup, nv_wdown)
    return (loss, grad_x, *grads, *deltas, *new_m, *new_v)
```

```python
import functools

import jax
import jax.numpy as jnp
from jax import lax
from jax.experimental import pallas as pl
from jax.experimental.pallas import tpu as pltpu

F32 = jnp.float32
BF16 = jnp.bfloat16

N_DEV = 8
N_META = 16
HEAD_DIM = 64
BLK = 128
PAD = BLK - N_META
N_POOL_GROUPS = 4
EPS = 1e-6
PACK_W = 1024

ADAM_LR = 0.001
ADAM_B1 = 0.9
ADAM_B2 = 0.999
ADAM_EPS = 1e-08
ADAM_WD = 0.01
ADAM_STEP = 10

V7X_VMEM_LIMIT = 56 * 1024 * 1024


def _params(*sem):
    return pltpu.CompilerParams(dimension_semantics=sem, vmem_limit_bytes=V7X_VMEM_LIMIT)


def _row_tile(t):
    for tm in (512, 256, 128):
        if t % tm == 0:
            return tm
    raise ValueError(f"row count {t} is not a multiple of 128")


def _full(a):
    nd = a.ndim
    return pl.BlockSpec(a.shape, lambda *_: (0,) * nd)


def _dot(a, b):
    return jnp.dot(a, b, preferred_element_type=F32)


def _dot_nt(a, b):
    return lax.dot_general(a, b, (((1,), (1,)), ((), ())), preferred_element_type=F32)


def _dot_tn(a, b):
    return lax.dot_general(a, b, (((0,), (0,)), ((), ())), preferred_element_type=F32)


def _rms_fwd(x, g):
    rstd = lax.rsqrt(jnp.mean(x * x, axis=-1, keepdims=True) + EPS)
    return x * rstd * g


def _rms_bwd(dy, x, g):
    rstd = lax.rsqrt(jnp.mean(x * x, axis=-1, keepdims=True) + EPS)
    xhat = x * rstd
    dxhat = dy * g
    dx = rstd * (dxhat - xhat * jnp.mean(dxhat * xhat, axis=-1, keepdims=True))
    return dx, jnp.sum(dy * xhat, axis=0, keepdims=True)


def _inproj_fwd(h, g, wa, wq, wk, wv):
    t, d = h.shape
    tm = _row_tile(t)
    ua_w, aw = wa.shape[1], wq.shape[1]

    def body(h_ref, g_ref, wa_ref, wq_ref, wk_ref, wv_ref, ua_ref, q_ref, k_ref, v_ref, hn_ref):
        hn = _rms_fwd(h_ref[...], g_ref[...]).astype(BF16)
        hn_ref[...] = hn
        ua_ref[...] = _dot(hn, wa_ref[...])
        q_ref[...] = _dot(hn, wq_ref[...]).astype(BF16)
        k_ref[...] = _dot(hn, wk_ref[...]).astype(BF16)
        v_ref[...] = _dot(hn, wv_ref[...]).astype(BF16)

    row = lambda w: pl.BlockSpec((tm, w), lambda i: (i, 0))
    return pl.pallas_call(
        body, name="inproj_fwd", grid=(t // tm,),
        in_specs=[row(d), _full(g), _full(wa), _full(wq), _full(wk), _full(wv)],
        out_specs=[row(ua_w), row(aw), row(aw), row(aw), row(d)],
        out_shape=[jax.ShapeDtypeStruct((t, ua_w), F32)] + [jax.ShapeDtypeStruct((t, aw), BF16)] * 3
        + [jax.ShapeDtypeStruct((t, d), BF16)],
        compiler_params=_params("arbitrary"),
    )(h, g, wa, wq, wk, wv)


def _pool_geometry(lp, pw):
    pg = pw // N_POOL_GROUPS
    row = lax.broadcasted_iota(jnp.int32, (lp, pw), 0)
    lane = lax.broadcasted_iota(jnp.int32, (lp, pw), 1)
    grp = [(lane >= g * pg) & (lane < (g + 1) * pg) for g in range(N_POOL_GROUPS)]
    wlen = jnp.where(grp[0], 2, jnp.where(grp[1], 4, jnp.where(grp[2], 8, 16)))
    cnt = jnp.clip(row - (PAD - 1), 1, wlen).astype(F32)
    return grp, cnt, row >= PAD


def _by_group(grp, vals):
    return jnp.where(grp[0], vals[0], jnp.where(grp[1], vals[1], jnp.where(grp[2], vals[2], vals[3])))


def _pooled(p, grp, cnt, real):
    s2 = p + pltpu.roll(p, 1, 0)
    s4 = s2 + pltpu.roll(s2, 2, 0)
    s8 = s4 + pltpu.roll(s4, 4, 0)
    s16 = s8 + pltpu.roll(s8, 8, 0)
    return jnp.where(real, _by_group(grp, (s2, s4, s8, s16)) / cnt - p, 0.0)


def _conv(uu, wc_ref):
    return wc_ref[2:3, :] * uu + wc_ref[1:2, :] * pltpu.roll(uu, 1, 0) + wc_ref[0:1, :] * pltpu.roll(uu, 2, 0)


def _mixer_fwd(ua, wconv, wbd, pscale, nb):
    t = ua.shape[0]
    lp = t // nb
    cw, pw = wconv.shape[1], wbd.shape[0]

    def body(ua_ref, wc_ref, wbd_ref, ps_ref, y_ref):
        cb = ua_ref[:, 0:cw]
        uu = ua_ref[:, cw:2 * cw] * ua_ref[:, 2 * cw:3 * cw]
        y_ref[:, 0:cw] = (cb * _conv(uu, wc_ref)).astype(BF16)
        grp, cnt, real = _pool_geometry(lp, pw)
        pooled = _pooled(ua_ref[:, 3 * cw:3 * cw + pw], grp, cnt, real)
        y_ref[:, cw:cw + pw] = (_dot(pooled.astype(BF16), wbd_ref[...]) * ps_ref[...]).astype(BF16)

    return pl.pallas_call(
        body, name="mixer_fwd", grid=(nb,),
        in_specs=[pl.BlockSpec((lp, ua.shape[1]), lambda s: (s, 0)), _full(wconv), _full(wbd), _full(pscale)],
        out_specs=pl.BlockSpec((lp, cw + pw), lambda s: (s, 0)),
        out_shape=jax.ShapeDtypeStruct((t, cw + pw), BF16),
        compiler_params=_params("arbitrary"),
    )(ua, wconv, wbd, pscale)


def _split_bf16(x, parts):
    out = []
    for _ in range(parts):
        hi = x.astype(BF16)
        out.append(hi)
        x = x - hi.astype(F32)
    return out


ATT_NW = 2
ATT_W = ATT_NW * BLK
ATT_PAIRS_FWD = 4
ATT_PAIRS_BWD = 4
HW = 2 * HEAD_DIM
ATT_SCALE = HEAD_DIM ** -0.5


def _attn_scores(z, valid):
    ls = jnp.minimum(z, 0.0) - jnp.log(1.0 + jnp.exp(-jnp.abs(z)))
    lk = jnp.where(valid, ls - z, 0.0)
    return ls, lk, _split_bf16(lk, 2)


def _attn_between(split, mstrict):
    return _dot(split[0], mstrict) + _dot(split[1], mstrict)


def _attn_weights(ls, between, carry, valid):
    return jnp.where(valid, jnp.exp(ls + between + carry), 0.0)


def _attn_consts():
    r = lax.broadcasted_iota(jnp.int32, (2 * BLK, ATT_W), 0)
    c = lax.broadcasted_iota(jnp.int32, (2 * BLK, ATT_W), 1)
    cmr = c - (r & (BLK - 1))
    kr = lax.broadcasted_iota(jnp.int32, (ATT_W, ATT_W), 0)
    kc = lax.broadcasted_iota(jnp.int32, (ATT_W, ATT_W), 1)
    lane = lax.broadcasted_iota(jnp.int32, (1, HW), 1)
    heads = (lane < HEAD_DIM, lane >= HEAD_DIM)
    return c, cmr, kr, kc, heads


def _rows(i):
    return pl.ds(pl.multiple_of(i * BLK, BLK), BLK)


def _stack_heads(x, heads):
    zero = jnp.zeros_like(x)
    return jnp.concatenate([jnp.where(heads[0], x, zero), jnp.where(heads[1], x, zero)], axis=0)


def _unstack_heads(x, heads):
    return jnp.where(heads[0], x[:BLK], x[BLK:])


def _window(i, s):
    start = jnp.maximum(i + 1 - ATT_NW * (s + 1), 0) * BLK
    return start, (i + 1 - ATT_NW * s) * BLK


def _attn_fwd(q, k, v, nb):
    t, aw = q.shape
    lp = t // nb
    nblk = lp // BLK
    pp = min(ATT_PAIRS_FWD, aw // HW)
    cols = [slice(p * HW, (p + 1) * HW) for p in range(pp)]

    def body(q_ref, k_ref, v_ref, o_ref):
        c, cmr, kr, kc, heads = _attn_consts()
        mstrict = jnp.where(kr > kc, 1.0, 0.0).astype(BF16)

        def qblock(i, _):
            qs = [_stack_heads(q_ref[_rows(i), cs] * ATT_SCALE, heads) for cs in cols]

            def window(s, st):
                accs, carries = st
                start, end = _window(i, s)
                valid = (cmr < i * BLK - start) & (c < end - start)
                keys = pl.ds(pl.multiple_of(start, BLK), ATT_W)
                zs = [_dot_nt(qs[p], k_ref[keys, cs]) for p, cs in enumerate(cols)]
                ls, lk, between = [], [], []
                for p in range(pp):
                    ls_p, lk_p, split = _attn_scores(zs[p], valid)
                    ls.append(ls_p)
                    lk.append(lk_p)
                    between.append(_attn_between(split, mstrict))
                new_accs, new_carries = [], []
                for p, cs in enumerate(cols):
                    a = _attn_weights(ls[p], between[p], carries[p], valid)
                    new_accs.append(accs[p] + _dot(a.astype(BF16), v_ref[keys, cs]))
                    new_carries.append(carries[p] + jnp.sum(lk[p], axis=1, keepdims=True))
                return tuple(new_accs), tuple(new_carries)

            init = (tuple(jnp.zeros((2 * BLK, HW), F32) for _ in cols), tuple(jnp.zeros((2 * BLK, 1), F32) for _ in cols))
            accs, _ = lax.fori_loop(0, (i + ATT_NW) // ATT_NW, window, init)
            for p, cs in enumerate(cols):
                o_ref[_rows(i), cs] = _unstack_heads(accs[p], heads).astype(BF16)
            return 0

        lax.fori_loop(0, nblk, qblock, 0)

    spec = pl.BlockSpec((lp, pp * HW), lambda s, p: (s, p))
    return pl.pallas_call(
        body, name="attn_fwd", grid=(nb, aw // (pp * HW)),
        in_specs=[spec, spec, spec], out_specs=spec,
        out_shape=jax.ShapeDtypeStruct((t, aw), BF16),
        compiler_params=_params("arbitrary", "arbitrary"),
    )(q, k, v)


def _outproj(h, ya, yb, wa, wb):
    t, d = h.shape
    tm = _row_tile(t)

    def body(h_ref, ya_ref, yb_ref, wa_ref, wb_ref, o_ref):
        o_ref[...] = h_ref[...] + _dot(ya_ref[...], wa_ref[...]) + _dot(yb_ref[...], wb_ref[...])

    row = lambda w: pl.BlockSpec((tm, w), lambda i: (i, 0))
    return pl.pallas_call(
        body, name="outproj", grid=(t // tm,),
        in_specs=[row(d), row(ya.shape[1]), row(yb.shape[1]), _full(wa), _full(wb)],
        out_specs=row(d), out_shape=jax.ShapeDtypeStruct((t, d), F32),
        compiler_params=_params("arbitrary"),
    )(h, ya, yb, wa, wb)


def _mlp_fwd(h, g, wup, wdown):
    t, d = h.shape
    tm = _row_tile(t)
    nff, _, fc = wup.shape

    def body(h_ref, g_ref, wu_ref, wd_ref, o_ref, hn_ref, acc):
        kk = pl.program_id(1)

        @pl.when(kk == 0)
        def _():
            x = h_ref[...]
            hn_ref[...] = _rms_fwd(x, g_ref[...]).astype(BF16)
            acc[...] = x

        m = _dot(hn_ref[...], wu_ref[...])
        a = jnp.square(jnp.maximum(m, 0.0)).astype(BF16)
        acc[...] += _dot(a, wd_ref[...])

        @pl.when(kk == nff - 1)
        def _():
            o_ref[...] = acc[...]

    row = pl.BlockSpec((tm, d), lambda i, kk: (i, 0))
    return pl.pallas_call(
        body, name="mlp_fwd", grid=(t // tm, nff),
        in_specs=[row, _full(g), pl.BlockSpec((None, d, fc), lambda i, kk: (kk, 0, 0)),
                  pl.BlockSpec((fc, d), lambda i, kk: (kk, 0))],
        out_specs=[row, row],
        out_shape=[jax.ShapeDtypeStruct((t, d), F32), jax.ShapeDtypeStruct((t, d), BF16)],
        scratch_shapes=[pltpu.VMEM((tm, d), F32)],
        compiler_params=_params("arbitrary", "arbitrary"),
    )(h, g, wup, wdown)


def _loss_head(h, g, target, nb):
    t, d = h.shape
    nblk = t // nb // BLK
    nx = nblk - 1

    def body(h_ref, g_ref, t_ref, loss_ref, dh_ref, dg_ref):
        s, i = pl.program_id(0), pl.program_id(1)

        @pl.when((s == 0) & (i == 0))
        def _():
            loss_ref[...] = jnp.zeros_like(loss_ref)
            dg_ref[...] = jnp.zeros_like(dg_ref)

        @pl.when(i == 0)
        def _():
            dh_ref[...] = jnp.zeros_like(dh_ref)

        @pl.when(i > 0)
        def _():
            x, gg = h_ref[...], g_ref[...]
            err = _rms_fwd(x, gg) - t_ref[...]
            loss_ref[...] += jnp.sum(err * err) * (0.5 / d)
            dx, dg = _rms_bwd(err * (1.0 / d), x, gg)
            dh_ref[...] = dx
            dg_ref[...] += dg

    hspec = pl.BlockSpec((BLK, d), lambda s, i: (s * nblk + i, 0))
    return pl.pallas_call(
        body, name="loss_head", grid=(nb, nblk),
        in_specs=[hspec, _full(g), pl.BlockSpec((BLK, d), lambda s, i: (s * nx + jnp.maximum(i - 1, 0), 0))],
        out_specs=[pl.BlockSpec((1, BLK), lambda s, i: (0, 0)), hspec, pl.BlockSpec((1, d), lambda s, i: (0, 0))],
        out_shape=[jax.ShapeDtypeStruct((1, BLK), F32), jax.ShapeDtypeStruct((t, d), F32), jax.ShapeDtypeStruct((1, d), F32)],
        compiler_params=_params("arbitrary", "arbitrary"),
    )(h, g, target)


def _mlp_bwd(dh, h, hn, g, wup, wdown):
    t, d = h.shape
    tm = _row_tile(t)
    nff, _, fc = wup.shape

    def body(dh_ref, h_ref, hn_ref, g_ref, wu_ref, wd_ref, dhm_ref, a_ref, dm_ref, dg_ref, dhn, dhb):
        i, kk = pl.program_id(0), pl.program_id(1)

        @pl.when((i == 0) & (kk == 0))
        def _():
            dg_ref[...] = jnp.zeros_like(dg_ref)

        @pl.when(kk == 0)
        def _():
            dhb[...] = dh_ref[...].astype(BF16)
            dhn[...] = jnp.zeros_like(dhn)

        r = jnp.maximum(_dot(hn_ref[...], wu_ref[...]), 0.0)
        a_ref[...] = (r * r).astype(BF16)
        dm = (_dot_nt(dhb[...], wd_ref[...]) * (2.0 * r)).astype(BF16)
        dm_ref[...] = dm
        dhn[...] += _dot_nt(dm, wu_ref[...])

        @pl.when(kk == nff - 1)
        def _():
            dx, dg = _rms_bwd(dhn[...], h_ref[...], g_ref[...])
            dhm_ref[...] = dh_ref[...] + dx
            dg_ref[...] += dg

    row = pl.BlockSpec((tm, d), lambda i, kk: (i, 0))
    ff = pl.BlockSpec((tm, fc), lambda i, kk: (i, kk))
    return pl.pallas_call(
        body, name="mlp_bwd", grid=(t // tm, nff),
        in_specs=[row, row, row, _full(g), pl.BlockSpec((None, d, fc), lambda i, kk: (kk, 0, 0)),
                  pl.BlockSpec((fc, d), lambda i, kk: (kk, 0))],
        out_specs=[row, ff, ff, pl.BlockSpec((1, d), lambda i, kk: (0, 0))],
        out_shape=[jax.ShapeDtypeStruct((t, d), F32), jax.ShapeDtypeStruct((t, nff * fc), BF16),
                   jax.ShapeDtypeStruct((t, nff * fc), BF16), jax.ShapeDtypeStruct((1, d), F32)],
        scratch_shapes=[pltpu.VMEM((tm, d), F32), pltpu.VMEM((tm, d), BF16)],
        compiler_params=_params("arbitrary", "arbitrary"),
    )(dh, h, hn, g, wup, wdown)


def _matmul_tn(x, y, name):
    t, k1 = x.shape
    n = y.shape[1]
    tt = _row_tile(t)
    tn = min(n, 512)
    steps = t // tt

    def body(x_ref, y_ref, o_ref, acc):
        @pl.when(pl.program_id(1) == 0)
        def _():
            acc[...] = jnp.zeros_like(acc)

        acc[...] += _dot_tn(x_ref[...].astype(BF16), y_ref[...].astype(BF16))

        @pl.when(pl.program_id(1) == steps - 1)
        def _():
            o_ref[...] = acc[...].astype(BF16)

    return pl.pallas_call(
        body, name=name, grid=(n // tn, steps),
        in_specs=[pl.BlockSpec((tt, k1), lambda j, r: (r, 0)), pl.BlockSpec((tt, tn), lambda j, r: (r, j))],
        out_specs=pl.BlockSpec((k1, tn), lambda j, r: (0, j)),
        out_shape=jax.ShapeDtypeStruct((k1, n), BF16),
        scratch_shapes=[pltpu.VMEM((k1, tn), F32)],
        compiler_params=_params("arbitrary", "arbitrary"),
    )(x, y)


def _outproj_bwd(dh, wa, wb):
    t, d = dh.shape
    tm = _row_tile(t)
    na, nbw = wa.shape[0], wb.shape[0]

    def body(dh_ref, wa_ref, wb_ref, da_ref, db_ref):
        x = dh_ref[...].astype(BF16)
        da_ref[...] = _dot_nt(x, wa_ref[...])
        db_ref[...] = _dot_nt(x, wb_ref[...]).astype(BF16)

    row = lambda w: pl.BlockSpec((tm, w), lambda i: (i, 0))
    return pl.pallas_call(
        body, name="outproj_bwd", grid=(t // tm,),
        in_specs=[row(d), _full(wa), _full(wb)], out_specs=[row(na), row(nbw)],
        out_shape=[jax.ShapeDtypeStruct((t, na), F32), jax.ShapeDtypeStruct((t, nbw), BF16)],
        compiler_params=_params("arbitrary"),
    )(dh, wa, wb)


def _attn_bwd(q, k, v, do, nb):
    t, aw = q.shape
    lp = t // nb
    nblk = lp // BLK
    pp = min(ATT_PAIRS_BWD, aw // HW)
    cols = [slice(p * HW, (p + 1) * HW) for p in range(pp)]
    max_windows = (nblk + ATT_NW - 1) // ATT_NW

    def body(q_ref, k_ref, v_ref, do_ref, dq_ref, dk_ref, dv_ref, dk_acc, dv_acc, g_s, sig_s):
        c, cmr, kr, kc, heads = _attn_consts()
        mstrict = jnp.where(kr > kc, 1.0, 0.0).astype(BF16)
        mprefix = jnp.where(kr < kc, 1.0, 0.0).astype(BF16)
        dk_acc[...] = jnp.zeros_like(dk_acc)
        dv_acc[...] = jnp.zeros_like(dv_acc)

        def qblock(i, _):
            qs = [_stack_heads(q_ref[_rows(i), cs] * ATT_SCALE, heads) for cs in cols]
            dos = [_stack_heads(do_ref[_rows(i), cs], heads) for cs in cols]
            nwin = (i + ATT_NW) // ATT_NW

            def geometry(s):
                start, end = _window(i, s)
                valid = (cmr < i * BLK - start) & (c < end - start)
                return valid, pl.ds(pl.multiple_of(start, BLK), ATT_W)

            def down(s, carries):
                valid, keys = geometry(s)
                zs = [_dot_nt(qs[p], k_ref[keys, cs]) for p, cs in enumerate(cols)]
                das = [_dot_nt(dos[p], v_ref[keys, cs]) for p, cs in enumerate(cols)]
                ls, lk, between = [], [], []
                for p in range(pp):
                    ls_p, lk_p, split = _attn_scores(zs[p], valid)
                    ls.append(ls_p)
                    lk.append(lk_p)
                    between.append(_attn_between(split, mstrict))
                new_carries = []
                for p, cs in enumerate(cols):
                    a = _attn_weights(ls[p], between[p], carries[p], valid)
                    g_s[p, s] = das[p] * a
                    sig_s[p, s] = jnp.exp(ls[p])
                    dv_acc[keys, cs] += _dot_tn(a.astype(BF16), dos[p])
                    new_carries.append(carries[p] + jnp.sum(lk[p], axis=1, keepdims=True))
                return tuple(new_carries)

            lax.fori_loop(0, nwin, down, tuple(jnp.zeros((2 * BLK, 1), F32) for _ in cols))

            def up(ss, st):
                accs, carries = st
                s = nwin - 1 - ss
                valid, keys = geometry(s)
                gs, before = [], []
                for p in range(pp):
                    g = g_s[p, s]
                    p0, p1, p2 = _split_bf16(g, 3)
                    gs.append(g)
                    before.append(_dot(p0, mprefix) + _dot(p1, mprefix) + _dot(p2, mprefix))
                new_accs, new_carries = [], []
                for p, cs in enumerate(cols):
                    g, sig = gs[p], sig_s[p, s]
                    dz = jnp.where(valid, g * (1.0 - sig) - (before[p] + carries[p]) * sig, 0.0).astype(BF16)
                    new_accs.append(accs[p] + _dot(dz, k_ref[keys, cs]))
                    dk_acc[keys, cs] += _dot_tn(dz, qs[p])
                    new_carries.append(carries[p] + jnp.sum(g, axis=1, keepdims=True))
                return tuple(new_accs), tuple(new_carries)

            init = (tuple(jnp.zeros((2 * BLK, HW), F32) for _ in cols), tuple(jnp.zeros((2 * BLK, 1), F32) for _ in cols))
            accs, _ = lax.fori_loop(0, nwin, up, init)
            for p, cs in enumerate(cols):
                dq_ref[_rows(i), cs] = (_unstack_heads(accs[p], heads) * ATT_SCALE).astype(BF16)
            return 0

        lax.fori_loop(0, nblk, qblock, 0)
        dk_ref[...] = dk_acc[...].astype(BF16)
        dv_ref[...] = dv_acc[...].astype(BF16)

    spec = pl.BlockSpec((lp, pp * HW), lambda s, p: (s, p), pipeline_mode=pl.Buffered(1))
    return pl.pallas_call(
        body, name="attn_bwd", grid=(nb, aw // (pp * HW)),
        in_specs=[spec] * 4, out_specs=[spec] * 3,
        out_shape=[jax.ShapeDtypeStruct((t, aw), BF16)] * 3,
        scratch_shapes=[pltpu.VMEM((lp, pp * HW), F32), pltpu.VMEM((lp, pp * HW), F32),
                        pltpu.VMEM((pp, max_windows, 2 * BLK, ATT_W), F32), pltpu.VMEM((pp, max_windows, 2 * BLK, ATT_W), F32)],
        compiler_params=_params("arbitrary", "arbitrary"),
    )(q, k, v, do)


def _mixer_bwd(ua, dy, wconv, wbd, pscale, nb):
    t = ua.shape[0]
    lp = t // nb
    cw, pw = wconv.shape[1], wbd.shape[0]

    def body(ua_ref, dy_ref, wc_ref, wbd_ref, ps_ref, du_ref, dwc_ref, dps_ref, dwb_ref):
        @pl.when(pl.program_id(0) == 0)
        def _():
            dwc_ref[...] = jnp.zeros_like(dwc_ref)
            dps_ref[...] = jnp.zeros_like(dps_ref)
            dwb_ref[...] = jnp.zeros_like(dwb_ref)

        up = lambda x, n: pltpu.roll(x, lp - n, 0)
        cb, cc, cx = ua_ref[:, 0:cw], ua_ref[:, cw:2 * cw], ua_ref[:, 2 * cw:3 * cw]
        uu = cc * cx
        dyc = dy_ref[:, 0:cw]
        du_ref[:, 0:cw] = (dyc * _conv(uu, wc_ref)).astype(BF16)
        dcv = dyc * cb
        duu = wc_ref[2:3, :] * dcv + wc_ref[1:2, :] * up(dcv, 1) + wc_ref[0:1, :] * up(dcv, 2)
        du_ref[:, cw:2 * cw] = (duu * cx).astype(BF16)
        du_ref[:, 2 * cw:3 * cw] = (duu * cc).astype(BF16)
        dwc_ref[0:1, :] += jnp.sum(dcv * pltpu.roll(uu, 2, 0), axis=0, keepdims=True)
        dwc_ref[1:2, :] += jnp.sum(dcv * pltpu.roll(uu, 1, 0), axis=0, keepdims=True)
        dwc_ref[2:3, :] += jnp.sum(dcv * uu, axis=0, keepdims=True)
        grp, cnt, real = _pool_geometry(lp, pw)
        p = ua_ref[:, 3 * cw:3 * cw + pw]
        pooled = _pooled(p, grp, cnt, real).astype(BF16)
        dyp = dy_ref[:, cw:cw + pw]
        dps_ref[...] += jnp.sum(dyp * _dot(pooled, wbd_ref[...]), axis=0, keepdims=True)
        dpre = (dyp * ps_ref[...]).astype(BF16)
        dwb_ref[...] += _dot_tn(pooled, dpre)
        dpooled = jnp.where(real, _dot_nt(dpre, wbd_ref[...]), 0.0)
        xm = dpooled / cnt
        l2 = xm + up(xm, 1)
        l4 = l2 + up(l2, 2)
        l8 = l4 + up(l4, 4)
        l16 = l8 + up(l8, 8)
        du_ref[:, 3 * cw:3 * cw + pw] = jnp.where(real, _by_group(grp, (l2, l4, l8, l16)) - dpooled, 0.0).astype(BF16)

    seq = lambda w: pl.BlockSpec((lp, w), lambda s: (s, 0))
    return pl.pallas_call(
        body, name="mixer_bwd", grid=(nb,),
        in_specs=[seq(ua.shape[1]), seq(cw + pw), _full(wconv), _full(wbd), _full(pscale)],
        out_specs=[seq(ua.shape[1]), pl.BlockSpec((3, cw), lambda s: (0, 0)), pl.BlockSpec((1, pw), lambda s: (0, 0)),
                   pl.BlockSpec((pw, pw), lambda s: (0, 0))],
        out_shape=[jax.ShapeDtypeStruct(ua.shape, BF16), jax.ShapeDtypeStruct((3, cw), F32),
                   jax.ShapeDtypeStruct((1, pw), F32), jax.ShapeDtypeStruct((pw, pw), F32)],
        compiler_params=_params("arbitrary"),
    )(ua, dy, wconv, wbd, pscale)


def _inproj_bwd(dh, h, g, dua, dq, dk, dv, wa, wq, wk, wv):
    t, d = h.shape
    tm = _row_tile(t)

    def body(dh_ref, h_ref, g_ref, dua_ref, dq_ref, dk_ref, dv_ref, wa_ref, wq_ref, wk_ref, wv_ref, o_ref, dg_ref):
        @pl.when(pl.program_id(0) == 0)
        def _():
            dg_ref[...] = jnp.zeros_like(dg_ref)

        dhn = (_dot_nt(dua_ref[...], wa_ref[...]) + _dot_nt(dq_ref[...], wq_ref[...])
               + _dot_nt(dk_ref[...], wk_ref[...]) + _dot_nt(dv_ref[...], wv_ref[...]))
        dx, dg = _rms_bwd(dhn, h_ref[...], g_ref[...])
        o_ref[...] = dh_ref[...] + dx
        dg_ref[...] += dg

    row = lambda w: pl.BlockSpec((tm, w), lambda i: (i, 0))
    return pl.pallas_call(
        body, name="inproj_bwd", grid=(t // tm,),
        in_specs=[row(d), row(d), _full(g), row(dua.shape[1]), row(dq.shape[1]), row(dk.shape[1]), row(dv.shape[1]),
                  _full(wa), _full(wq), _full(wk), _full(wv)],
        out_specs=[row(d), pl.BlockSpec((1, d), lambda i: (0, 0))],
        out_shape=[jax.ShapeDtypeStruct((t, d), F32), jax.ShapeDtypeStruct((1, d), F32)],
        compiler_params=_params("arbitrary"),
    )(dh, h, g, dua, dq, dk, dv, wa, wq, wk, wv)


def _part_tile(r):
    for tr in (256, 128, 64, 32, 16, 8):
        if r % tr == 0:
            return tr
    return r


def _sum_parts(x, name):
    n, r, c = x.shape
    tr = _part_tile(r)

    def body(x_ref, o_ref):
        acc = x_ref[0]
        for p in range(1, n):
            acc = acc + x_ref[p]
        o_ref[...] = acc

    return pl.pallas_call(
        body, name=name, grid=(r // tr,),
        in_specs=[pl.BlockSpec((n, tr, c), lambda i: (0, i, 0))], out_specs=pl.BlockSpec((tr, c), lambda i: (i, 0)),
        out_shape=jax.ShapeDtypeStruct((r, c), F32), compiler_params=_params("arbitrary"),
    )(x)


def _adamw(w, gparts, m, v, name):
    n, r, c = gparts.shape
    tr = _part_tile(r)

    def body(w_ref, g_ref, m_ref, v_ref, go_ref, d_ref, mo_ref, vo_ref):
        g = g_ref[0].astype(F32)
        for p in range(1, n):
            g = g + g_ref[p].astype(F32)
        go_ref[...] = g
        mm = ADAM_B1 * m_ref[...] + (1.0 - ADAM_B1) * g
        vv = ADAM_B2 * v_ref[...] + (1.0 - ADAM_B2) * jnp.square(g)
        mo_ref[...] = mm
        vo_ref[...] = vv
        m_hat = mm / (1.0 - ADAM_B1 ** ADAM_STEP)
        v_hat = vv / (1.0 - ADAM_B2 ** ADAM_STEP)
        d_ref[...] = -ADAM_LR * (m_hat / (jnp.sqrt(v_hat) + ADAM_EPS) + ADAM_WD * w_ref[...])

    row = pl.BlockSpec((tr, c), lambda i: (i, 0))
    return pl.pallas_call(
        body, name=name, grid=(r // tr,),
        in_specs=[row, pl.BlockSpec((n, tr, c), lambda i: (0, i, 0)), row, row], out_specs=[row] * 4,
        out_shape=[jax.ShapeDtypeStruct((r, c), F32)] * 4, compiler_params=_params("arbitrary"),
    )(w, gparts, m, v)


MESH = pl.DeviceIdType.MESH
ANY = pl.BlockSpec(memory_space=pl.ANY)


def _all_gather(x, name):
    r, c = x.shape

    def body(x_ref, out_ref, send_sems, recv_sems, local_sem):
        mx, my, mc = lax.axis_index("x"), lax.axis_index("y"), lax.axis_index("c")
        me, sibling = (mx, my, mc), (mx, my, 1 - mc)
        chips = [(1 - mx, my), (mx, 1 - my), (1 - mx, 1 - my)]

        def slot(px, py, pc):
            return out_ref.at[4 * px + 2 * py + pc]

        def copy(kk, block, to, src=None):
            return pltpu.make_async_remote_copy(
                src_ref=slot(*block) if src is None else src, dst_ref=slot(*block),
                send_sem=send_sems.at[kk], recv_sem=recv_sems.at[kk], device_id=to, device_id_type=MESH)

        mine = pltpu.make_async_copy(x_ref, slot(*me), local_sem)
        mine.start()
        first = [copy(0, me, sibling, src=x_ref)]
        first += [copy(1 + j, me, (*chip, mc), src=x_ref) for j, chip in enumerate(chips)]
        for cp in first:
            cp.start()
        passed = [copy(4 + j, (*chip, mc), sibling) for j, chip in enumerate(chips)]
        for j, chip in enumerate(chips):
            copy(1 + j, (*chip, mc), me).wait_recv()
            passed[j].start()
        copy(0, sibling, me).wait_recv()
        for j, chip in enumerate(chips):
            copy(4 + j, (*chip, 1 - mc), me).wait_recv()
        for cp in first + passed:
            cp.wait_send()
        mine.wait()

    return pl.pallas_call(
        body, name=name, in_specs=[ANY], out_specs=ANY,
        out_shape=jax.ShapeDtypeStruct((N_DEV, r, c), x.dtype),
        scratch_shapes=[pltpu.SemaphoreType.DMA((7,)), pltpu.SemaphoreType.DMA((7,)), pltpu.SemaphoreType.DMA(())],
    )(x)


def _all_to_all(x, name):
    def body(x_ref, out_ref, send_sems, recv_sems, local_sem):
        mx, my, mc = lax.axis_index("x"), lax.axis_index("y"), lax.axis_index("c")
        me = 4 * mx + 2 * my + mc
        mine = pltpu.make_async_copy(x_ref.at[me], out_ref.at[me], local_sem)
        mine.start()
        copies = []
        for kk in range(1, N_DEV):
            px, py, pc = mx ^ (kk >> 2), my ^ ((kk >> 1) & 1), mc ^ (kk & 1)
            peer = 4 * px + 2 * py + pc
            copies.append((peer, pltpu.make_async_remote_copy(
                src_ref=x_ref.at[peer], dst_ref=out_ref.at[me], send_sem=send_sems.at[kk - 1],
                recv_sem=recv_sems.at[kk - 1], device_id=(px, py, pc), device_id_type=MESH)))
        for _, cp in copies:
            cp.start()
        for kk, (peer, cp) in enumerate(copies):
            pltpu.make_async_remote_copy(
                src_ref=x_ref.at[peer], dst_ref=out_ref.at[peer], send_sem=send_sems.at[kk],
                recv_sem=recv_sems.at[kk], device_id=(mx, my, mc), device_id_type=MESH).wait_recv()
        for _, cp in copies:
            cp.wait_send()
        mine.wait()

    return pl.pallas_call(
        body, name=name, in_specs=[ANY], out_specs=ANY,
        out_shape=jax.ShapeDtypeStruct(x.shape, x.dtype),
        scratch_shapes=[pltpu.SemaphoreType.DMA((7,)), pltpu.SemaphoreType.DMA((7,)), pltpu.SemaphoreType.DMA(())],
    )(x)


def _rows_of(a):
    return a.reshape(-1, PACK_W)


def _pack_big(w_in, w_out, w_up, w_down):
    return jnp.concatenate([_rows_of(w_in), _rows_of(w_out), _rows_of(w_up), _rows_of(w_down)], axis=0)


def _unpack_big(buf, shapes):
    out, r0 = [], 0
    for shp in shapes:
        n = 1
        for s in shp:
            n *= s
        out.append(buf[r0:r0 + n // PACK_W].reshape(shp))
        r0 += n // PACK_W
    return out


def _shards_last(a):
    lead = a.shape[:-1]
    return jnp.moveaxis(a.reshape(*lead, N_DEV, a.shape[-1] // N_DEV), -2, 0)


def _shards_rows(a):
    l, n, c = a.shape
    return jnp.moveaxis(a.reshape(l, N_DEV, n // N_DEV, c), 1, 0)


def _flat_pad(parts, mult):
    flat = jnp.concatenate([p.reshape(-1) for p in parts])
    return jnp.pad(flat, (0, (-flat.shape[0]) % mult))


def _unflatten(flat, shapes):
    out, o = [], 0
    for shp in shapes:
        n = 1
        for s in shp:
            n *= s
        out.append(flat[o:o + n].reshape(shp))
        o += n
    return out


def kernel(x, meta_tokens, g_mix, w_in, w_conv, w_pool, pool_scale, w_out, g_mlp, w_up, w_down, g_final, loss_target, m_meta_tokens, m_g_mix, m_w_in, m_w_conv, m_w_pool, m_pool_scale, m_w_out, m_g_mlp, m_w_up, m_w_down, m_g_final, v_meta_tokens, v_g_mix, v_w_in, v_w_conv, v_w_pool, v_pool_scale, v_w_out, v_g_mlp, v_w_up, v_w_down, v_g_final):
    nb, seq, d = x.shape
    depth = g_mix.shape[0]
    lp = BLK + seq
    t = nb * lp
    cw = w_conv.shape[2] * N_DEV
    pw = pool_scale.shape[1]
    pg = pw // N_POOL_GROUPS
    aw = (w_in.shape[2] * N_DEV - 3 * cw - pw) // 3
    ua_w = 3 * cw + pw
    d_ff = w_up.shape[2] * N_DEV
    me = 4 * lax.axis_index("x") + 2 * lax.axis_index("y") + lax.axis_index("c")
    big_shapes = [w_in.shape, w_out.shape, w_up.shape, w_down.shape]

    gathered = _all_gather(_pack_big(w_in, w_out, w_up, w_down).astype(BF16), "gather_weights")
    per_dev = [_unpack_big(gathered[p], big_shapes) for p in range(N_DEV)]
    w_in_f = jnp.concatenate([pd[0] for pd in per_dev], axis=2)
    w_out_f = jnp.concatenate([pd[1] for pd in per_dev], axis=1)
    w_up_f = jnp.stack([pd[2] for pd in per_dev], axis=1)
    w_down_f = jnp.concatenate([pd[3] for pd in per_dev], axis=1)
    small_in = _all_gather(_flat_pad([meta_tokens, w_conv], 8 * BLK).reshape(-1, BLK), "gather_small_weights").reshape(N_DEV, -1)
    meta_f = jnp.moveaxis(small_in[:, :meta_tokens.size].reshape(N_DEV, N_META, d // N_DEV), 0, 1).reshape(N_META, d)
    w_conv_f = small_in[:, meta_tokens.size:meta_tokens.size + w_conv.size].reshape((N_DEV,) + w_conv.shape)
    w_conv_f = jnp.moveaxis(w_conv_f, 0, 2).reshape(depth, w_conv.shape[1], cw)

    wa = w_in_f[:, :, :ua_w]
    wq = w_in_f[:, :, ua_w:ua_w + aw]
    wk = w_in_f[:, :, ua_w + aw:ua_w + 2 * aw]
    wv = w_in_f[:, :, ua_w + 2 * aw:]
    wo_a, wo_b = w_out_f[:, :cw + pw], w_out_f[:, cw + pw:]
    wbd = jnp.zeros((depth, pw, pw), F32)
    for gi in range(N_POOL_GROUPS):
        wbd = wbd.at[:, gi * pg:(gi + 1) * pg, gi * pg:(gi + 1) * pg].set(w_pool[:, gi])
    wbd = wbd.astype(BF16)

    h = jnp.concatenate([jnp.zeros((nb, PAD, d), F32), jnp.broadcast_to(meta_f[None], (nb, N_META, d)), x], axis=1).reshape(t, d)
    saved = []
    for l in range(depth):
        ua, q, k, v, hn1 = _inproj_fwd(h, g_mix[l][None], wa[l], wq[l], wk[l], wv[l])
        ycp = _mixer_fwd(ua, w_conv_f[l], wbd[l], pool_scale[l][None], nb)
        yat = _attn_fwd(q, k, v, nb)
        h_mid = _outproj(h, ycp, yat, wo_a[l], wo_b[l])
        h_next, hn2 = _mlp_fwd(h_mid, g_mlp[l][None], w_up_f[l], w_down_f[l])
        saved.append((h, ua, q, k, v, hn1, ycp, yat, h_mid, hn2))
        h = h_next

    loss_part, dh, dg_final = _loss_head(h, g_final[None], loss_target.reshape(nb * seq, d), nb)

    g_gmix, g_gmlp, g_wconv, g_pscale, g_wpool = [None] * depth, [None] * depth, [None] * depth, [None] * depth, [None] * depth
    g_win, g_wout, g_wup, g_wdown = [None] * depth, [None] * depth, [None] * depth, [None] * depth
    for l in reversed(range(depth)):
        h_in, ua, q, k, v, hn1, ycp, yat, h_mid, hn2 = saved[l]
        dh_mid, act, dm, g_gmlp[l] = _mlp_bwd(dh, h_mid, hn2, g_mlp[l][None], w_up_f[l], w_down_f[l])
        g_wup[l] = _matmul_tn(hn2, dm, "grad_w_up")
        g_wdown[l] = _matmul_tn(act, dh, "grad_w_down")
        dycp, do = _outproj_bwd(dh_mid, wo_a[l], wo_b[l])
        g_wout[l] = jnp.concatenate([_matmul_tn(ycp, dh_mid, "grad_w_out_a"), _matmul_tn(yat, dh_mid, "grad_w_out_b")], axis=0)
        dq, dk, dv = _attn_bwd(q, k, v, do, nb)
        dua, g_wconv[l], g_pscale[l], dwb = _mixer_bwd(ua, dycp, w_conv_f[l], wbd[l], pool_scale[l][None], nb)
        g_wpool[l] = jnp.stack([dwb[gi * pg:(gi + 1) * pg, gi * pg:(gi + 1) * pg] for gi in range(N_POOL_GROUPS)])
        dh, g_gmix[l] = _inproj_bwd(dh_mid, h_in, g_mix[l][None], dua, dq, dk, dv, wa[l], wq[l], wk[l], wv[l])
        g_win[l] = jnp.concatenate([_matmul_tn(hn1, dua, "grad_w_in_a"), _matmul_tn(hn1, dq, "grad_w_in_q"),
                                    _matmul_tn(hn1, dk, "grad_w_in_k"), _matmul_tn(hn1, dv, "grad_w_in_v")], axis=1)
    dh3 = dh.reshape(nb, lp, d)
    grad_x = dh3[:, BLK:]
    g_meta = _sum_parts(dh3[:, PAD:BLK], "sum_meta_grad")

    send = jnp.concatenate([
        _shards_last(jnp.stack(g_win)).reshape(N_DEV, -1, PACK_W), _shards_rows(jnp.stack(g_wout)).reshape(N_DEV, -1, PACK_W),
        _shards_last(jnp.stack(g_wup)).reshape(N_DEV, -1, PACK_W), _shards_rows(jnp.stack(g_wdown)).reshape(N_DEV, -1, PACK_W)], axis=1)
    parts = _all_to_all(send, "exchange_big_grads")
    big = _adamw(_pack_big(w_in, w_out, w_up, w_down), parts, _pack_big(m_w_in, m_w_out, m_w_up, m_w_down),
                 _pack_big(v_w_in, v_w_out, v_w_up, v_w_down), "adamw_big")
    (gr_win, gr_wout, gr_wup, gr_wdown), (de_win, de_wout, de_wup, de_wdown), (nm_win, nm_wout, nm_wup, nm_wdown), \
        (nv_win, nv_wout, nv_wup, nv_wdown) = [_unpack_big(b, big_shapes) for b in big]

    small_full = [g_meta, jnp.concatenate(g_gmix), jnp.stack(g_wconv), jnp.stack(g_wpool), jnp.concatenate(g_pscale),
                  jnp.concatenate(g_gmlp), dg_final.reshape(-1)]
    small_shapes = [a.shape for a in small_full]
    small_sum = _sum_parts(_all_gather(_flat_pad(small_full, 8 * BLK).reshape(-1, BLK), "gather_small_grads"), "sum_small_grads")
    gr_meta, gr_gmix, gr_wconv, gr_wpool, gr_pscale, gr_gmlp, gr_gfinal = _unflatten(small_sum.reshape(-1), small_shapes)
    gr_meta = lax.dynamic_slice_in_dim(gr_meta, me * (d // N_DEV), d // N_DEV, axis=1)
    gr_wconv = lax.dynamic_slice_in_dim(gr_wconv, me * (cw // N_DEV), cw // N_DEV, axis=2)
    small_g = [gr_meta, gr_gmix, gr_wconv, gr_wpool, gr_pscale, gr_gmlp, gr_gfinal]
    local_shapes = [a.shape for a in small_g]
    pack_small = lambda parts_: _flat_pad(parts_, 8 * BLK).reshape(-1, BLK)
    small = _adamw(pack_small([meta_tokens, g_mix, w_conv, w_pool, pool_scale, g_mlp, g_final]), pack_small(small_g)[None],
                   pack_small([m_meta_tokens, m_g_mix, m_w_conv, m_w_pool, m_pool_scale, m_g_mlp, m_g_final]),
                   pack_small([v_meta_tokens, v_g_mix, v_w_conv, v_w_pool, v_pool_scale, v_g_mlp, v_g_final]), "adamw_small")
    _, de_s, nm_s, nv_s = [_unflatten(b.reshape(-1), local_shapes) for b in small]

    loss = lax.psum(loss_part[0, 0], ("x", "y", "c"))

    def ordered(meta, gmix, wconv, wpool, pscale, gmlp, gfinal, win, wout, wup, wdown):
        return [meta, gmix, win, wconv, wpool, pscale, wout, gmlp, wup, wdown, gfinal]

    grads = ordered(*small_g, gr_win, gr_wout, gr_wup, gr_wdown)
    deltas = ordered(*de_s, de_win, de_wout, de_wup, de_wdown)
    new_m = ordered(*nm_s, nm_win, nm_wout, nm_wup, nm_wdown)
    new_v = ordered(*nv_s, nv_win, nv_wout, nv_wup, nv_wdown)
    return (loss, grad_x, *grads, *deltas, *new_m, *new_v)
```

```python
import functools

import jax
import jax.numpy as jnp
from jax import lax
from jax.experimental import pallas as pl
from jax.experimental.pallas import tpu as pltpu

F32 = jnp.float32
BF16 = jnp.bfloat16

N_DEV = 8
N_META = 16
HEAD_DIM = 64
BLK = 128
PAD = BLK - N_META
N_POOL_GROUPS = 4
EPS = 1e-6
PACK_W = 1024

ADAM_LR = 0.001
ADAM_B1 = 0.9
ADAM_B2 = 0.999
ADAM_EPS = 1e-08
ADAM_WD = 0.01
ADAM_STEP = 10

V7X_VMEM_LIMIT = 56 * 1024 * 1024


def _params(*sem):
    return pltpu.CompilerParams(dimension_semantics=sem, vmem_limit_bytes=V7X_VMEM_LIMIT)


def _row_tile(t):
    for tm in (512, 256, 128):
        if t % tm == 0:
            return tm
    raise ValueError(f"row count {t} is not a multiple of 128")


def _full(a):
    nd = a.ndim
    return pl.BlockSpec(a.shape, lambda *_: (0,) * nd)


def _dot(a, b):
    return jnp.dot(a, b, preferred_element_type=F32)


def _dot_nt(a, b):
    return lax.dot_general(a, b, (((1,), (1,)), ((), ())), preferred_element_type=F32)


def _dot_tn(a, b):
    return lax.dot_general(a, b, (((0,), (0,)), ((), ())), preferred_element_type=F32)


def _rms_fwd(x, g):
    rstd = lax.rsqrt(jnp.mean(x * x, axis=-1, keepdims=True) + EPS)
    return x * rstd * g


def _rms_bwd(dy, x, g):
    rstd = lax.rsqrt(jnp.mean(x * x, axis=-1, keepdims=True) + EPS)
    xhat = x * rstd
    dxhat = dy * g
    dx = rstd * (dxhat - xhat * jnp.mean(dxhat * xhat, axis=-1, keepdims=True))
    return dx, jnp.sum(dy * xhat, axis=0, keepdims=True)


def _inproj_fwd(h, g, wa, wq, wk, wv):
    t, d = h.shape
    tm = _row_tile(t)
    ua_w, aw = wa.shape[1], wq.shape[1]

    def body(h_ref, g_ref, wa_ref, wq_ref, wk_ref, wv_ref, ua_ref, q_ref, k_ref, v_ref, hn_ref):
        hn = _rms_fwd(h_ref[...], g_ref[...]).astype(BF16)
        hn_ref[...] = hn
        ua_ref[...] = _dot(hn, wa_ref[...])
        q_ref[...] = _dot(hn, wq_ref[...]).astype(BF16)
        k_ref[...] = _dot(hn, wk_ref[...]).astype(BF16)
        v_ref[...] = _dot(hn, wv_ref[...]).astype(BF16)

    row = lambda w: pl.BlockSpec((tm, w), lambda i: (i, 0))
    return pl.pallas_call(
        body, name="inproj_fwd", grid=(t // tm,),
        in_specs=[row(d), _full(g), _full(wa), _full(wq), _full(wk), _full(wv)],
        out_specs=[row(ua_w), row(aw), row(aw), row(aw), row(d)],
        out_shape=[jax.ShapeDtypeStruct((t, ua_w), F32)] + [jax.ShapeDtypeStruct((t, aw), BF16)] * 3
        + [jax.ShapeDtypeStruct((t, d), BF16)],
        compiler_params=_params("arbitrary"),
    )(h, g, wa, wq, wk, wv)


def _pool_geometry(lp, pw):
    pg = pw // N_POOL_GROUPS
    row = lax.broadcasted_iota(jnp.int32, (lp, pw), 0)
    lane = lax.broadcasted_iota(jnp.int32, (lp, pw), 1)
    grp = [(lane >= g * pg) & (lane < (g + 1) * pg) for g in range(N_POOL_GROUPS)]
    wlen = jnp.where(grp[0], 2, jnp.where(grp[1], 4, jnp.where(grp[2], 8, 16)))
    cnt = jnp.clip(row - (PAD - 1), 1, wlen).astype(F32)
    return grp, cnt, row >= PAD


def _by_group(grp, vals):
    return jnp.where(grp[0], vals[0], jnp.where(grp[1], vals[1], jnp.where(grp[2], vals[2], vals[3])))


def _pooled(p, grp, cnt, real):
    s2 = p + pltpu.roll(p, 1, 0)
    s4 = s2 + pltpu.roll(s2, 2, 0)
    s8 = s4 + pltpu.roll(s4, 4, 0)
    s16 = s8 + pltpu.roll(s8, 8, 0)
    return jnp.where(real, _by_group(grp, (s2, s4, s8, s16)) / cnt - p, 0.0)


def _conv(uu, wc_ref):
    return wc_ref[2:3, :] * uu + wc_ref[1:2, :] * pltpu.roll(uu, 1, 0) + wc_ref[0:1, :] * pltpu.roll(uu, 2, 0)


def _mixer_fwd(ua, wconv, wbd, pscale, nb):
    t = ua.shape[0]
    lp = t // nb
    cw, pw = wconv.shape[1], wbd.shape[0]

    def body(ua_ref, wc_ref, wbd_ref, ps_ref, y_ref):
        cb = ua_ref[:, 0:cw]
        uu = ua_ref[:, cw:2 * cw] * ua_ref[:, 2 * cw:3 * cw]
        y_ref[:, 0:cw] = (cb * _conv(uu, wc_ref)).astype(BF16)
        grp, cnt, real = _pool_geometry(lp, pw)
        pooled = _pooled(ua_ref[:, 3 * cw:3 * cw + pw], grp, cnt, real)
        y_ref[:, cw:cw + pw] = (_dot(pooled.astype(BF16), wbd_ref[...]) * ps_ref[...]).astype(BF16)

    return pl.pallas_call(
        body, name="mixer_fwd", grid=(nb,),
        in_specs=[pl.BlockSpec((lp, ua.shape[1]), lambda s: (s, 0)), _full(wconv), _full(wbd), _full(pscale)],
        out_specs=pl.BlockSpec((lp, cw + pw), lambda s: (s, 0)),
        out_shape=jax.ShapeDtypeStruct((t, cw + pw), BF16),
        compiler_params=_params("arbitrary"),
    )(ua, wconv, wbd, pscale)


def _split_bf16(x, parts):
    out = []
    for _ in range(parts):
        hi = x.astype(BF16)
        out.append(hi)
        x = x - hi.astype(F32)
    return out


ATT_NW = 2
ATT_W = ATT_NW * BLK
ATT_PAIRS_FWD = 4
ATT_PAIRS_BWD = 4
HW = 2 * HEAD_DIM
ATT_SCALE = HEAD_DIM ** -0.5


def _attn_scores(z, valid):
    ls = jnp.minimum(z, 0.0) - jnp.log(1.0 + jnp.exp(-jnp.abs(z)))
    lk = jnp.where(valid, ls - z, 0.0)
    return ls, lk, _split_bf16(lk, 2)


def _attn_between(split, mstrict):
    return _dot(split[0], mstrict) + _dot(split[1], mstrict)


def _attn_weights(ls, between, carry, valid):
    return jnp.where(valid, jnp.exp(ls + between + carry), 0.0)


def _attn_consts():
    r = lax.broadcasted_iota(jnp.int32, (2 * BLK, ATT_W), 0)
    c = lax.broadcasted_iota(jnp.int32, (2 * BLK, ATT_W), 1)
    cmr = c - (r & (BLK - 1))
    kr = lax.broadcasted_iota(jnp.int32, (ATT_W, ATT_W), 0)
    kc = lax.broadcasted_iota(jnp.int32, (ATT_W, ATT_W), 1)
    lane = lax.broadcasted_iota(jnp.int32, (1, HW), 1)
    heads = (lane < HEAD_DIM, lane >= HEAD_DIM)
    return c, cmr, kr, kc, heads


def _rows(i):
    return pl.ds(pl.multiple_of(i * BLK, BLK), BLK)


def _stack_heads(x, heads):
    zero = jnp.zeros_like(x)
    return jnp.concatenate([jnp.where(heads[0], x, zero), jnp.where(heads[1], x, zero)], axis=0)


def _unstack_heads(x, heads):
    return jnp.where(heads[0], x[:BLK], x[BLK:])


def _window(i, s):
    start = jnp.maximum(i + 1 - ATT_NW * (s + 1), 0) * BLK
    return start, (i + 1 - ATT_NW * s) * BLK


def _attn_fwd(q, k, v, nb, gather=None):
    t, aw = q.shape
    lp = t // nb
    nblk = lp // BLK
    pp = min(ATT_PAIRS_FWD, aw // HW)
    cols = [slice(p * HW, (p + 1) * HW) for p in range(pp)]
    ncol = aw // (pp * HW)
    nsteps = nb * ncol

    def body(q_ref, k_ref, v_ref, *rest):
        if gather is None:
            (o_ref,) = rest
        else:
            x_ref, o_ref, g_ref, send_sems, recv_sems, local_sem = rest
            step = pl.program_id(0) * ncol + pl.program_id(1)
            start, forward, finish = _gather_phases(x_ref, g_ref, send_sems, recv_sems, local_sem)
            pl.when(step == 0)(start)
            pl.when(step == nsteps // 2)(forward)
        c, cmr, kr, kc, heads = _attn_consts()
        mstrict = jnp.where(kr > kc, 1.0, 0.0).astype(BF16)

        def qblock(i, _):
            qs = [_stack_heads(q_ref[_rows(i), cs] * ATT_SCALE, heads) for cs in cols]

            def window(s, st):
                accs, carries = st
                start, end = _window(i, s)
                valid = (cmr < i * BLK - start) & (c < end - start)
                keys = pl.ds(pl.multiple_of(start, BLK), ATT_W)
                zs = [_dot_nt(qs[p], k_ref[keys, cs]) for p, cs in enumerate(cols)]
                ls, lk, between = [], [], []
                for p in range(pp):
                    ls_p, lk_p, split = _attn_scores(zs[p], valid)
                    ls.append(ls_p)
                    lk.append(lk_p)
                    between.append(_attn_between(split, mstrict))
                new_accs, new_carries = [], []
                for p, cs in enumerate(cols):
                    a = _attn_weights(ls[p], between[p], carries[p], valid)
                    new_accs.append(accs[p] + _dot(a.astype(BF16), v_ref[keys, cs]))
                    new_carries.append(carries[p] + jnp.sum(lk[p], axis=1, keepdims=True))
                return tuple(new_accs), tuple(new_carries)

            init = (tuple(jnp.zeros((2 * BLK, HW), F32) for _ in cols), tuple(jnp.zeros((2 * BLK, 1), F32) for _ in cols))
            accs, _ = lax.fori_loop(0, (i + ATT_NW) // ATT_NW, window, init)
            for p, cs in enumerate(cols):
                o_ref[_rows(i), cs] = _unstack_heads(accs[p], heads).astype(BF16)
            return 0

        lax.fori_loop(0, nblk, qblock, 0)
        if gather is not None:
            pl.when(step == nsteps - 1)(finish)

    spec = pl.BlockSpec((lp, pp * HW), lambda s, p: (s, p))
    y_shape = jax.ShapeDtypeStruct((t, aw), BF16)
    if gather is None:
        return pl.pallas_call(
            body, name="attn_fwd", grid=(nb, ncol), in_specs=[spec, spec, spec], out_specs=spec, out_shape=y_shape,
            compiler_params=_params("arbitrary", "arbitrary"),
        )(q, k, v)
    return pl.pallas_call(
        body, name="attn_fwd_gather", grid=(nb, ncol), in_specs=[spec, spec, spec, ANY], out_specs=[spec, ANY],
        out_shape=[y_shape, jax.ShapeDtypeStruct((N_DEV,) + gather.shape, gather.dtype)], scratch_shapes=COMM_SEMS,
        compiler_params=_params("arbitrary", "arbitrary"),
    )(q, k, v, gather)


def _outproj(h, ya, yb, wa, wb):
    t, d = h.shape
    tm = _row_tile(t)

    def body(h_ref, ya_ref, yb_ref, wa_ref, wb_ref, o_ref):
        o_ref[...] = h_ref[...] + _dot(ya_ref[...], wa_ref[...]) + _dot(yb_ref[...], wb_ref[...])

    row = lambda w: pl.BlockSpec((tm, w), lambda i: (i, 0))
    return pl.pallas_call(
        body, name="outproj", grid=(t // tm,),
        in_specs=[row(d), row(ya.shape[1]), row(yb.shape[1]), _full(wa), _full(wb)],
        out_specs=row(d), out_shape=jax.ShapeDtypeStruct((t, d), F32),
        compiler_params=_params("arbitrary"),
    )(h, ya, yb, wa, wb)


def _mlp_fwd(h, g, wup, wdown):
    t, d = h.shape
    tm = _row_tile(t)
    nff, _, fc = wup.shape

    def body(h_ref, g_ref, wu_ref, wd_ref, o_ref, hn_ref, acc):
        kk = pl.program_id(1)

        @pl.when(kk == 0)
        def _():
            x = h_ref[...]
            hn_ref[...] = _rms_fwd(x, g_ref[...]).astype(BF16)
            acc[...] = x

        m = _dot(hn_ref[...], wu_ref[...])
        a = jnp.square(jnp.maximum(m, 0.0)).astype(BF16)
        acc[...] += _dot(a, wd_ref[...])

        @pl.when(kk == nff - 1)
        def _():
            o_ref[...] = acc[...]

    row = pl.BlockSpec((tm, d), lambda i, kk: (i, 0))
    return pl.pallas_call(
        body, name="mlp_fwd", grid=(t // tm, nff),
        in_specs=[row, _full(g), pl.BlockSpec((None, d, fc), lambda i, kk: (kk, 0, 0)),
                  pl.BlockSpec((fc, d), lambda i, kk: (kk, 0))],
        out_specs=[row, row],
        out_shape=[jax.ShapeDtypeStruct((t, d), F32), jax.ShapeDtypeStruct((t, d), BF16)],
        scratch_shapes=[pltpu.VMEM((tm, d), F32)],
        compiler_params=_params("arbitrary", "arbitrary"),
    )(h, g, wup, wdown)


def _loss_head(h, g, target, nb):
    t, d = h.shape
    nblk = t // nb // BLK
    nx = nblk - 1

    def body(h_ref, g_ref, t_ref, loss_ref, dh_ref, dg_ref):
        s, i = pl.program_id(0), pl.program_id(1)

        @pl.when((s == 0) & (i == 0))
        def _():
            loss_ref[...] = jnp.zeros_like(loss_ref)
            dg_ref[...] = jnp.zeros_like(dg_ref)

        @pl.when(i == 0)
        def _():
            dh_ref[...] = jnp.zeros_like(dh_ref)

        @pl.when(i > 0)
        def _():
            x, gg = h_ref[...], g_ref[...]
            err = _rms_fwd(x, gg) - t_ref[...]
            loss_ref[...] += jnp.sum(err * err) * (0.5 / d)
            dx, dg = _rms_bwd(err * (1.0 / d), x, gg)
            dh_ref[...] = dx
            dg_ref[...] += dg

    hspec = pl.BlockSpec((BLK, d), lambda s, i: (s * nblk + i, 0))
    return pl.pallas_call(
        body, name="loss_head", grid=(nb, nblk),
        in_specs=[hspec, _full(g), pl.BlockSpec((BLK, d), lambda s, i: (s * nx + jnp.maximum(i - 1, 0), 0))],
        out_specs=[pl.BlockSpec((1, BLK), lambda s, i: (0, 0)), hspec, pl.BlockSpec((1, d), lambda s, i: (0, 0))],
        out_shape=[jax.ShapeDtypeStruct((1, BLK), F32), jax.ShapeDtypeStruct((t, d), F32), jax.ShapeDtypeStruct((1, d), F32)],
        compiler_params=_params("arbitrary", "arbitrary"),
    )(h, g, target)


def _mlp_bwd(dh, h, hn, g, wup, wdown):
    t, d = h.shape
    tm = _row_tile(t)
    nff, _, fc = wup.shape

    def body(dh_ref, h_ref, hn_ref, g_ref, wu_ref, wd_ref, dhm_ref, a_ref, dm_ref, dg_ref, dhn, dhb):
        i, kk = pl.program_id(0), pl.program_id(1)

        @pl.when((i == 0) & (kk == 0))
        def _():
            dg_ref[...] = jnp.zeros_like(dg_ref)

        @pl.when(kk == 0)
        def _():
            dhb[...] = dh_ref[...].astype(BF16)
            dhn[...] = jnp.zeros_like(dhn)

        r = jnp.maximum(_dot(hn_ref[...], wu_ref[...]), 0.0)
        a_ref[...] = (r * r).astype(BF16)
        dm = (_dot_nt(dhb[...], wd_ref[...]) * (2.0 * r)).astype(BF16)
        dm_ref[...] = dm
        dhn[...] += _dot_nt(dm, wu_ref[...])

        @pl.when(kk == nff - 1)
        def _():
            dx, dg = _rms_bwd(dhn[...], h_ref[...], g_ref[...])
            dhm_ref[...] = dh_ref[...] + dx
            dg_ref[...] += dg

    row = pl.BlockSpec((tm, d), lambda i, kk: (i, 0))
    ff = pl.BlockSpec((tm, fc), lambda i, kk: (i, kk))
    return pl.pallas_call(
        body, name="mlp_bwd", grid=(t // tm, nff),
        in_specs=[row, row, row, _full(g), pl.BlockSpec((None, d, fc), lambda i, kk: (kk, 0, 0)),
                  pl.BlockSpec((fc, d), lambda i, kk: (kk, 0))],
        out_specs=[row, ff, ff, pl.BlockSpec((1, d), lambda i, kk: (0, 0))],
        out_shape=[jax.ShapeDtypeStruct((t, d), F32), jax.ShapeDtypeStruct((t, nff * fc), BF16),
                   jax.ShapeDtypeStruct((t, nff * fc), BF16), jax.ShapeDtypeStruct((1, d), F32)],
        scratch_shapes=[pltpu.VMEM((tm, d), F32), pltpu.VMEM((tm, d), BF16)],
        compiler_params=_params("arbitrary", "arbitrary"),
    )(dh, h, hn, g, wup, wdown)


def _matmul_tn(x, y, name):
    t, k1 = x.shape
    n = y.shape[1]
    tt = _row_tile(t)
    tn = min(n, 512)
    steps = t // tt

    def body(x_ref, y_ref, o_ref, acc):
        @pl.when(pl.program_id(1) == 0)
        def _():
            acc[...] = jnp.zeros_like(acc)

        acc[...] += _dot_tn(x_ref[...].astype(BF16), y_ref[...].astype(BF16))

        @pl.when(pl.program_id(1) == steps - 1)
        def _():
            o_ref[...] = acc[...].astype(BF16)

    return pl.pallas_call(
        body, name=name, grid=(n // tn, steps),
        in_specs=[pl.BlockSpec((tt, k1), lambda j, r: (r, 0)), pl.BlockSpec((tt, tn), lambda j, r: (r, j))],
        out_specs=pl.BlockSpec((k1, tn), lambda j, r: (0, j)),
        out_shape=jax.ShapeDtypeStruct((k1, n), BF16),
        scratch_shapes=[pltpu.VMEM((k1, tn), F32)],
        compiler_params=_params("arbitrary", "arbitrary"),
    )(x, y)


def _outproj_bwd(dh, wa, wb):
    t, d = dh.shape
    tm = _row_tile(t)
    na, nbw = wa.shape[0], wb.shape[0]

    def body(dh_ref, wa_ref, wb_ref, da_ref, db_ref):
        x = dh_ref[...].astype(BF16)
        da_ref[...] = _dot_nt(x, wa_ref[...])
        db_ref[...] = _dot_nt(x, wb_ref[...]).astype(BF16)

    row = lambda w: pl.BlockSpec((tm, w), lambda i: (i, 0))
    return pl.pallas_call(
        body, name="outproj_bwd", grid=(t // tm,),
        in_specs=[row(d), _full(wa), _full(wb)], out_specs=[row(na), row(nbw)],
        out_shape=[jax.ShapeDtypeStruct((t, na), F32), jax.ShapeDtypeStruct((t, nbw), BF16)],
        compiler_params=_params("arbitrary"),
    )(dh, wa, wb)


def _attn_bwd(q, k, v, do, nb, exchange=None):
    t, aw = q.shape
    lp = t // nb
    nblk = lp // BLK
    pp = min(ATT_PAIRS_BWD, aw // HW)
    cols = [slice(p * HW, (p + 1) * HW) for p in range(pp)]
    ncol = aw // (pp * HW)
    nsteps = nb * ncol
    max_windows = (nblk + ATT_NW - 1) // ATT_NW

    def body(q_ref, k_ref, v_ref, do_ref, *rest):
        if exchange is None:
            dq_ref, dk_ref, dv_ref, dk_acc, dv_acc, g_s, sig_s = rest
        else:
            x_ref, dq_ref, dk_ref, dv_ref, e_ref, dk_acc, dv_acc, g_s, sig_s, send_sems, recv_sems, local_sem = rest
            step = pl.program_id(0) * ncol + pl.program_id(1)
            start, finish = _exchange_phases(x_ref, e_ref, send_sems, recv_sems, local_sem)
            pl.when(step == 0)(start)
        c, cmr, kr, kc, heads = _attn_consts()
        mstrict = jnp.where(kr > kc, 1.0, 0.0).astype(BF16)
        mprefix = jnp.where(kr < kc, 1.0, 0.0).astype(BF16)
        dk_acc[...] = jnp.zeros_like(dk_acc)
        dv_acc[...] = jnp.zeros_like(dv_acc)

        def qblock(i, _):
            qs = [_stack_heads(q_ref[_rows(i), cs] * ATT_SCALE, heads) for cs in cols]
            dos = [_stack_heads(do_ref[_rows(i), cs], heads) for cs in cols]
            nwin = (i + ATT_NW) // ATT_NW

            def geometry(s):
                start, end = _window(i, s)
                valid = (cmr < i * BLK - start) & (c < end - start)
                return valid, pl.ds(pl.multiple_of(start, BLK), ATT_W)

            def down(s, carries):
                valid, keys = geometry(s)
                zs = [_dot_nt(qs[p], k_ref[keys, cs]) for p, cs in enumerate(cols)]
                das = [_dot_nt(dos[p], v_ref[keys, cs]) for p, cs in enumerate(cols)]
                ls, lk, between = [], [], []
                for p in range(pp):
                    ls_p, lk_p, split = _attn_scores(zs[p], valid)
                    ls.append(ls_p)
                    lk.append(lk_p)
                    between.append(_attn_between(split, mstrict))
                new_carries = []
                for p, cs in enumerate(cols):
                    a = _attn_weights(ls[p], between[p], carries[p], valid)
                    g_s[p, s] = das[p] * a
                    sig_s[p, s] = jnp.exp(ls[p])
                    dv_acc[keys, cs] += _dot_tn(a.astype(BF16), dos[p])
                    new_carries.append(carries[p] + jnp.sum(lk[p], axis=1, keepdims=True))
                return tuple(new_carries)

            lax.fori_loop(0, nwin, down, tuple(jnp.zeros((2 * BLK, 1), F32) for _ in cols))

            def up(ss, st):
                accs, carries = st
                s = nwin - 1 - ss
                valid, keys = geometry(s)
                gs, before = [], []
                for p in range(pp):
                    g = g_s[p, s]
                    p0, p1, p2 = _split_bf16(g, 3)
                    gs.append(g)
                    before.append(_dot(p0, mprefix) + _dot(p1, mprefix) + _dot(p2, mprefix))
                new_accs, new_carries = [], []
                for p, cs in enumerate(cols):
                    g, sig = gs[p], sig_s[p, s]
                    dz = jnp.where(valid, g * (1.0 - sig) - (before[p] + carries[p]) * sig, 0.0).astype(BF16)
                    new_accs.append(accs[p] + _dot(dz, k_ref[keys, cs]))
                    dk_acc[keys, cs] += _dot_tn(dz, qs[p])
                    new_carries.append(carries[p] + jnp.sum(g, axis=1, keepdims=True))
                return tuple(new_accs), tuple(new_carries)

            init = (tuple(jnp.zeros((2 * BLK, HW), F32) for _ in cols), tuple(jnp.zeros((2 * BLK, 1), F32) for _ in cols))
            accs, _ = lax.fori_loop(0, nwin, up, init)
            for p, cs in enumerate(cols):
                dq_ref[_rows(i), cs] = (_unstack_heads(accs[p], heads) * ATT_SCALE).astype(BF16)
            return 0

        lax.fori_loop(0, nblk, qblock, 0)
        dk_ref[...] = dk_acc[...].astype(BF16)
        dv_ref[...] = dv_acc[...].astype(BF16)
        if exchange is not None:
            pl.when(step == nsteps - 1)(finish)

    spec = pl.BlockSpec((lp, pp * HW), lambda s, p: (s, p), pipeline_mode=pl.Buffered(1))
    d_shape = jax.ShapeDtypeStruct((t, aw), BF16)
    scratch = [pltpu.VMEM((lp, pp * HW), F32), pltpu.VMEM((lp, pp * HW), F32),
               pltpu.VMEM((pp, max_windows, 2 * BLK, ATT_W), F32), pltpu.VMEM((pp, max_windows, 2 * BLK, ATT_W), F32)]
    if exchange is None:
        return pl.pallas_call(
            body, name="attn_bwd", grid=(nb, ncol), in_specs=[spec] * 4, out_specs=[spec] * 3, out_shape=[d_shape] * 3,
            scratch_shapes=scratch, compiler_params=_params("arbitrary", "arbitrary"),
        )(q, k, v, do)
    return pl.pallas_call(
        body, name="attn_bwd_exchange", grid=(nb, ncol), in_specs=[spec] * 4 + [ANY], out_specs=[spec] * 3 + [ANY],
        out_shape=[d_shape] * 3 + [jax.ShapeDtypeStruct(exchange.shape, exchange.dtype)],
        scratch_shapes=scratch + COMM_SEMS, compiler_params=_params("arbitrary", "arbitrary"),
    )(q, k, v, do, exchange)


def _mixer_bwd(ua, dy, wconv, wbd, pscale, nb):
    t = ua.shape[0]
    lp = t // nb
    cw, pw = wconv.shape[1], wbd.shape[0]

    def body(ua_ref, dy_ref, wc_ref, wbd_ref, ps_ref, du_ref, dwc_ref, dps_ref, dwb_ref):
        @pl.when(pl.program_id(0) == 0)
        def _():
            dwc_ref[...] = jnp.zeros_like(dwc_ref)
            dps_ref[...] = jnp.zeros_like(dps_ref)
            dwb_ref[...] = jnp.zeros_like(dwb_ref)

        up = lambda x, n: pltpu.roll(x, lp - n, 0)
        cb, cc, cx = ua_ref[:, 0:cw], ua_ref[:, cw:2 * cw], ua_ref[:, 2 * cw:3 * cw]
        uu = cc * cx
        dyc = dy_ref[:, 0:cw]
        du_ref[:, 0:cw] = (dyc * _conv(uu, wc_ref)).astype(BF16)
        dcv = dyc * cb
        duu = wc_ref[2:3, :] * dcv + wc_ref[1:2, :] * up(dcv, 1) + wc_ref[0:1, :] * up(dcv, 2)
        du_ref[:, cw:2 * cw] = (duu * cx).astype(BF16)
        du_ref[:, 2 * cw:3 * cw] = (duu * cc).astype(BF16)
        dwc_ref[0:1, :] += jnp.sum(dcv * pltpu.roll(uu, 2, 0), axis=0, keepdims=True)
        dwc_ref[1:2, :] += jnp.sum(dcv * pltpu.roll(uu, 1, 0), axis=0, keepdims=True)
        dwc_ref[2:3, :] += jnp.sum(dcv * uu, axis=0, keepdims=True)
        grp, cnt, real = _pool_geometry(lp, pw)
        p = ua_ref[:, 3 * cw:3 * cw + pw]
        pooled = _pooled(p, grp, cnt, real).astype(BF16)
        dyp = dy_ref[:, cw:cw + pw]
        dps_ref[...] += jnp.sum(dyp * _dot(pooled, wbd_ref[...]), axis=0, keepdims=True)
        dpre = (dyp * ps_ref[...]).astype(BF16)
        dwb_ref[...] += _dot_tn(pooled, dpre)
        dpooled = jnp.where(real, _dot_nt(dpre, wbd_ref[...]), 0.0)
        xm = dpooled / cnt
        l2 = xm + up(xm, 1)
        l4 = l2 + up(l2, 2)
        l8 = l4 + up(l4, 4)
        l16 = l8 + up(l8, 8)
        du_ref[:, 3 * cw:3 * cw + pw] = jnp.where(real, _by_group(grp, (l2, l4, l8, l16)) - dpooled, 0.0).astype(BF16)

    seq = lambda w: pl.BlockSpec((lp, w), lambda s: (s, 0))
    return pl.pallas_call(
        body, name="mixer_bwd", grid=(nb,),
        in_specs=[seq(ua.shape[1]), seq(cw + pw), _full(wconv), _full(wbd), _full(pscale)],
        out_specs=[seq(ua.shape[1]), pl.BlockSpec((3, cw), lambda s: (0, 0)), pl.BlockSpec((1, pw), lambda s: (0, 0)),
                   pl.BlockSpec((pw, pw), lambda s: (0, 0))],
        out_shape=[jax.ShapeDtypeStruct(ua.shape, BF16), jax.ShapeDtypeStruct((3, cw), F32),
                   jax.ShapeDtypeStruct((1, pw), F32), jax.ShapeDtypeStruct((pw, pw), F32)],
        compiler_params=_params("arbitrary"),
    )(ua, dy, wconv, wbd, pscale)


def _inproj_bwd(dh, h, g, dua, dq, dk, dv, wa, wq, wk, wv):
    t, d = h.shape
    tm = _row_tile(t)

    def body(dh_ref, h_ref, g_ref, dua_ref, dq_ref, dk_ref, dv_ref, wa_ref, wq_ref, wk_ref, wv_ref, o_ref, dg_ref):
        @pl.when(pl.program_id(0) == 0)
        def _():
            dg_ref[...] = jnp.zeros_like(dg_ref)

        dhn = (_dot_nt(dua_ref[...], wa_ref[...]) + _dot_nt(dq_ref[...], wq_ref[...])
               + _dot_nt(dk_ref[...], wk_ref[...]) + _dot_nt(dv_ref[...], wv_ref[...]))
        dx, dg = _rms_bwd(dhn, h_ref[...], g_ref[...])
        o_ref[...] = dh_ref[...] + dx
        dg_ref[...] += dg

    row = lambda w: pl.BlockSpec((tm, w), lambda i: (i, 0))
    return pl.pallas_call(
        body, name="inproj_bwd", grid=(t // tm,),
        in_specs=[row(d), row(d), _full(g), row(dua.shape[1]), row(dq.shape[1]), row(dk.shape[1]), row(dv.shape[1]),
                  _full(wa), _full(wq), _full(wk), _full(wv)],
        out_specs=[row(d), pl.BlockSpec((1, d), lambda i: (0, 0))],
        out_shape=[jax.ShapeDtypeStruct((t, d), F32), jax.ShapeDtypeStruct((1, d), F32)],
        compiler_params=_params("arbitrary"),
    )(dh, h, g, dua, dq, dk, dv, wa, wq, wk, wv)


def _part_tile(r):
    for tr in (256, 128, 64, 32, 16, 8):
        if r % tr == 0:
            return tr
    return r


def _sum_parts(x, name):
    n, r, c = x.shape
    tr = _part_tile(r)

    def body(x_ref, o_ref):
        acc = x_ref[0]
        for p in range(1, n):
            acc = acc + x_ref[p]
        o_ref[...] = acc

    return pl.pallas_call(
        body, name=name, grid=(r // tr,),
        in_specs=[pl.BlockSpec((n, tr, c), lambda i: (0, i, 0))], out_specs=pl.BlockSpec((tr, c), lambda i: (i, 0)),
        out_shape=jax.ShapeDtypeStruct((r, c), F32), compiler_params=_params("arbitrary"),
    )(x)


def _adamw(w, gparts, m, v, name):
    n, r, c = gparts.shape
    tr = _part_tile(r)

    def body(w_ref, g_ref, m_ref, v_ref, go_ref, d_ref, mo_ref, vo_ref):
        g = g_ref[0].astype(F32)
        for p in range(1, n):
            g = g + g_ref[p].astype(F32)
        go_ref[...] = g
        mm = ADAM_B1 * m_ref[...] + (1.0 - ADAM_B1) * g
        vv = ADAM_B2 * v_ref[...] + (1.0 - ADAM_B2) * jnp.square(g)
        mo_ref[...] = mm
        vo_ref[...] = vv
        m_hat = mm / (1.0 - ADAM_B1 ** ADAM_STEP)
        v_hat = vv / (1.0 - ADAM_B2 ** ADAM_STEP)
        d_ref[...] = -ADAM_LR * (m_hat / (jnp.sqrt(v_hat) + ADAM_EPS) + ADAM_WD * w_ref[...])

    row = pl.BlockSpec((tr, c), lambda i: (i, 0))
    return pl.pallas_call(
        body, name=name, grid=(r // tr,),
        in_specs=[row, pl.BlockSpec((n, tr, c), lambda i: (0, i, 0)), row, row], out_specs=[row] * 4,
        out_shape=[jax.ShapeDtypeStruct((r, c), F32)] * 4, compiler_params=_params("arbitrary"),
    )(w, gparts, m, v)


MESH = pl.DeviceIdType.MESH
ANY = pl.BlockSpec(memory_space=pl.ANY)


def _all_gather(x, name):
    r, c = x.shape

    def body(x_ref, out_ref, send_sems, recv_sems, local_sem):
        start, forward, finish = _gather_phases(x_ref, out_ref, send_sems, recv_sems, local_sem)
        start()
        forward()
        finish()

    return pl.pallas_call(
        body, name=name, in_specs=[ANY], out_specs=ANY,
        out_shape=jax.ShapeDtypeStruct((N_DEV, r, c), x.dtype), scratch_shapes=COMM_SEMS,
    )(x)


COMM_SEMS = [pltpu.SemaphoreType.DMA((7,)), pltpu.SemaphoreType.DMA((7,)), pltpu.SemaphoreType.DMA(())]


def _gather_phases(x_ref, out_ref, send_sems, recv_sems, local_sem):
    mx, my, mc = lax.axis_index("x"), lax.axis_index("y"), lax.axis_index("c")
    me, sibling = (mx, my, mc), (mx, my, 1 - mc)
    chips = [(1 - mx, my), (mx, 1 - my), (1 - mx, 1 - my)]

    def slot(px, py, pc):
        return out_ref.at[4 * px + 2 * py + pc]

    def copy(kk, block, to, src=None):
        return pltpu.make_async_remote_copy(
            src_ref=slot(*block) if src is None else src, dst_ref=slot(*block),
            send_sem=send_sems.at[kk], recv_sem=recv_sems.at[kk], device_id=to, device_id_type=MESH)

    mine = pltpu.make_async_copy(x_ref, slot(*me), local_sem)
    first = [copy(0, me, sibling, src=x_ref)] + [copy(1 + j, me, (*chip, mc), src=x_ref) for j, chip in enumerate(chips)]
    passed = [copy(4 + j, (*chip, mc), sibling) for j, chip in enumerate(chips)]

    def start():
        mine.start()
        for cp in first:
            cp.start()

    def forward():
        for j, chip in enumerate(chips):
            copy(1 + j, (*chip, mc), me).wait_recv()
            passed[j].start()

    def finish():
        copy(0, sibling, me).wait_recv()
        for j, chip in enumerate(chips):
            copy(4 + j, (*chip, 1 - mc), me).wait_recv()
        for cp in first + passed:
            cp.wait_send()
        mine.wait()

    return start, forward, finish


def _exchange_phases(x_ref, out_ref, send_sems, recv_sems, local_sem):
    mx, my, mc = lax.axis_index("x"), lax.axis_index("y"), lax.axis_index("c")
    me = 4 * mx + 2 * my + mc
    mine = pltpu.make_async_copy(x_ref.at[me], out_ref.at[me], local_sem)
    copies = []
    for kk in range(1, N_DEV):
        px, py, pc = mx ^ (kk >> 2), my ^ ((kk >> 1) & 1), mc ^ (kk & 1)
        peer = 4 * px + 2 * py + pc
        copies.append((peer, pltpu.make_async_remote_copy(
            src_ref=x_ref.at[peer], dst_ref=out_ref.at[me], send_sem=send_sems.at[kk - 1],
            recv_sem=recv_sems.at[kk - 1], device_id=(px, py, pc), device_id_type=MESH)))

    def start():
        mine.start()
        for _, cp in copies:
            cp.start()

    def finish():
        for kk, (peer, _) in enumerate(copies):
            pltpu.make_async_remote_copy(
                src_ref=x_ref.at[peer], dst_ref=out_ref.at[peer], send_sem=send_sems.at[kk],
                recv_sem=recv_sems.at[kk], device_id=(mx, my, mc), device_id_type=MESH).wait_recv()
        for _, cp in copies:
            cp.wait_send()
        mine.wait()

    return start, finish


def _all_to_all(x, name):
    def body(x_ref, out_ref, send_sems, recv_sems, local_sem):
        start, finish = _exchange_phases(x_ref, out_ref, send_sems, recv_sems, local_sem)
        start()
        finish()

    return pl.pallas_call(
        body, name=name, in_specs=[ANY], out_specs=ANY,
        out_shape=jax.ShapeDtypeStruct(x.shape, x.dtype), scratch_shapes=COMM_SEMS,
    )(x)


def _rows_of(a):
    return a.reshape(-1, PACK_W)


def _unpack_big(buf, shapes):
    out, r0 = [], 0
    for shp in shapes:
        n = 1
        for s in shp:
            n *= s
        out.append(buf[r0:r0 + n // PACK_W].reshape(shp))
        r0 += n // PACK_W
    return out


def _shards_last(a):
    lead = a.shape[:-1]
    return jnp.moveaxis(a.reshape(*lead, N_DEV, a.shape[-1] // N_DEV), -2, 0)


def _shards_rows(a):
    l, n, c = a.shape
    return jnp.moveaxis(a.reshape(l, N_DEV, n // N_DEV, c), 1, 0)


def _flat_pad(parts, mult):
    flat = jnp.concatenate([p.reshape(-1) for p in parts])
    return jnp.pad(flat, (0, (-flat.shape[0]) % mult))


def _unflatten(flat, shapes):
    out, o = [], 0
    for shp in shapes:
        n = 1
        for s in shp:
            n *= s
        out.append(flat[o:o + n].reshape(shp))
        o += n
    return out


def kernel(x, meta_tokens, g_mix, w_in, w_conv, w_pool, pool_scale, w_out, g_mlp, w_up, w_down, g_final, loss_target, m_meta_tokens, m_g_mix, m_w_in, m_w_conv, m_w_pool, m_pool_scale, m_w_out, m_g_mlp, m_w_up, m_w_down, m_g_final, v_meta_tokens, v_g_mix, v_w_in, v_w_conv, v_w_pool, v_pool_scale, v_w_out, v_g_mlp, v_w_up, v_w_down, v_g_final):
    nb, seq, d = x.shape
    depth = g_mix.shape[0]
    lp = BLK + seq
    t = nb * lp
    cw = w_conv.shape[2] * N_DEV
    pw = pool_scale.shape[1]
    pg = pw // N_POOL_GROUPS
    aw = (w_in.shape[2] * N_DEV - 3 * cw - pw) // 3
    ua_w = 3 * cw + pw
    d_ff = w_up.shape[2] * N_DEV
    me = 4 * lax.axis_index("x") + 2 * lax.axis_index("y") + lax.axis_index("c")
    first_shapes = [w_in[:1].shape]
    rest_shapes = [w_in[1:].shape, w_out.shape, w_up.shape, w_down.shape]

    def pack_first(a_in):
        return _rows_of(a_in[:1])

    def pack_rest(a_in, a_out, a_up, a_down):
        return jnp.concatenate([_rows_of(a_in[1:]), _rows_of(a_out), _rows_of(a_up), _rows_of(a_down)], axis=0)

    g_first = _all_gather(pack_first(w_in).astype(BF16), "gather_w_in0")
    w_in0_f = jnp.concatenate([g_first[p].reshape(first_shapes[0]) for p in range(N_DEV)], axis=2)
    small_in = _all_gather(_flat_pad([meta_tokens, w_conv], 8 * BLK).reshape(-1, BLK), "gather_small_weights").reshape(N_DEV, -1)
    meta_f = jnp.moveaxis(small_in[:, :meta_tokens.size].reshape(N_DEV, N_META, d // N_DEV), 0, 1).reshape(N_META, d)
    w_conv_f = small_in[:, meta_tokens.size:meta_tokens.size + w_conv.size].reshape((N_DEV,) + w_conv.shape)
    w_conv_f = jnp.moveaxis(w_conv_f, 0, 2).reshape(depth, w_conv.shape[1], cw)

    def split_w_in(w):
        return w[:, :ua_w], w[:, ua_w:ua_w + aw], w[:, ua_w + aw:ua_w + 2 * aw], w[:, ua_w + 2 * aw:]

    wbd = jnp.zeros((depth, pw, pw), F32)
    for gi in range(N_POOL_GROUPS):
        wbd = wbd.at[:, gi * pg:(gi + 1) * pg, gi * pg:(gi + 1) * pg].set(w_pool[:, gi])
    wbd = wbd.astype(BF16)

    h = jnp.concatenate([jnp.zeros((nb, PAD, d), F32), jnp.broadcast_to(meta_f[None], (nb, N_META, d)), x], axis=1).reshape(t, d)
    saved, w_in_parts = [], [None] * depth
    w_in_parts[0] = split_w_in(w_in0_f[0])
    for l in range(depth):
        wa, wq, wk, wv = w_in_parts[l]
        ua, q, k, v, hn1 = _inproj_fwd(h, g_mix[l][None], wa, wq, wk, wv)
        ycp = _mixer_fwd(ua, w_conv_f[l], wbd[l], pool_scale[l][None], nb)
        if l == 0:
            yat, g_rest = _attn_fwd(q, k, v, nb, gather=pack_rest(w_in, w_out, w_up, w_down).astype(BF16))
            per_dev = [_unpack_big(g_rest[p], rest_shapes) for p in range(N_DEV)]
            w_in_rest = jnp.concatenate([pd[0] for pd in per_dev], axis=2)
            for ll in range(1, depth):
                w_in_parts[ll] = split_w_in(w_in_rest[ll - 1])
            w_out_f = jnp.concatenate([pd[1] for pd in per_dev], axis=1)
            w_up_f = jnp.stack([pd[2] for pd in per_dev], axis=1)
            w_down_f = jnp.concatenate([pd[3] for pd in per_dev], axis=1)
            wo_a, wo_b = w_out_f[:, :cw + pw], w_out_f[:, cw + pw:]
        else:
            yat = _attn_fwd(q, k, v, nb)
        h_mid =_outproj(h, ycp, yat, wo_a[l], wo_b[l])
        h_next, hn2 = _mlp_fwd(h_mid, g_mlp[l][None], w_up_f[l], w_down_f[l])
        saved.append((h, ua, q, k, v, hn1, ycp, yat, h_mid, hn2))
        h = h_next

    loss_part, dh, dg_final = _loss_head(h, g_final[None], loss_target.reshape(nb * seq, d), nb)

    g_gmix, g_gmlp, g_wconv, g_pscale, g_wpool = [None] * depth, [None] * depth, [None] * depth, [None] * depth, [None] * depth
    g_win, g_wout, g_wup, g_wdown = [None] * depth, [None] * depth, [None] * depth, [None] * depth
    for l in reversed(range(depth)):
        h_in, ua, q, k, v, hn1, ycp, yat, h_mid, hn2 = saved[l]
        wa, wq, wk, wv = w_in_parts[l]
        dh_mid, act, dm, g_gmlp[l] = _mlp_bwd(dh, h_mid, hn2, g_mlp[l][None], w_up_f[l], w_down_f[l])
        g_wup[l] = _matmul_tn(hn2, dm, "grad_w_up")
        g_wdown[l] = _matmul_tn(act, dh, "grad_w_down")
        dycp, do = _outproj_bwd(dh_mid, wo_a[l], wo_b[l])
        g_wout[l] = jnp.concatenate([_matmul_tn(ycp, dh_mid, "grad_w_out_a"), _matmul_tn(yat, dh_mid, "grad_w_out_b")], axis=0)
        if l == 0:
            send_rest = jnp.concatenate([
                _shards_last(jnp.stack(g_win[1:])).reshape(N_DEV, -1, PACK_W), _shards_rows(jnp.stack(g_wout)).reshape(N_DEV, -1, PACK_W),
                _shards_last(jnp.stack(g_wup)).reshape(N_DEV, -1, PACK_W), _shards_rows(jnp.stack(g_wdown)).reshape(N_DEV, -1, PACK_W)], axis=1)
            dq, dk, dv, parts_rest = _attn_bwd(q, k, v, do, nb, exchange=send_rest)
        else:
            dq, dk, dv = _attn_bwd(q, k, v, do, nb)
        dua, g_wconv[l], g_pscale[l], dwb = _mixer_bwd(ua, dycp, w_conv_f[l], wbd[l], pool_scale[l][None], nb)
        g_wpool[l] = jnp.stack([dwb[gi * pg:(gi + 1) * pg, gi * pg:(gi + 1) * pg] for gi in range(N_POOL_GROUPS)])
        dh, g_gmix[l] = _inproj_bwd(dh_mid, h_in, g_mix[l][None], dua, dq, dk, dv, wa, wq, wk, wv)
        g_win[l] = jnp.concatenate([_matmul_tn(hn1, dua, "grad_w_in_a"), _matmul_tn(hn1, dq, "grad_w_in_q"),
                                    _matmul_tn(hn1, dk, "grad_w_in_k"), _matmul_tn(hn1, dv, "grad_w_in_v")], axis=1)
    dh3 = dh.reshape(nb, lp, d)
    grad_x = dh3[:, BLK:]
    g_meta = _sum_parts(dh3[:, PAD:BLK], "sum_meta_grad")

    parts_first = _all_to_all(_shards_last(g_win[0][None]).reshape(N_DEV, -1, PACK_W), "exchange_w_in0_grads")
    rest = _adamw(pack_rest(w_in, w_out, w_up, w_down), parts_rest, pack_rest(m_w_in, m_w_out, m_w_up, m_w_down),
                  pack_rest(v_w_in, v_w_out, v_w_up, v_w_down), "adamw_rest")
    first = _adamw(pack_first(w_in), parts_first, pack_first(m_w_in), pack_first(v_w_in), "adamw_w_in0")
    rest = [_unpack_big(b, rest_shapes) for b in rest]
    first = [_unpack_big(b, first_shapes) for b in first]
    (gr_win, gr_wout, gr_wup, gr_wdown), (de_win, de_wout, de_wup, de_wdown), (nm_win, nm_wout, nm_wup, nm_wdown), \
        (nv_win, nv_wout, nv_wup, nv_wdown) = [[jnp.concatenate([f[0], r[0]], axis=0), r[1], r[2], r[3]] for f, r in zip(first, rest)]

    small_full = [g_meta, jnp.concatenate(g_gmix), jnp.stack(g_wconv), jnp.stack(g_wpool), jnp.concatenate(g_pscale),
                  jnp.concatenate(g_gmlp), dg_final.reshape(-1)]
    small_shapes = [a.shape for a in small_full]
    small_sum = _sum_parts(_all_gather(_flat_pad(small_full, 8 * BLK).reshape(-1, BLK), "gather_small_grads"), "sum_small_grads")
    gr_meta, gr_gmix, gr_wconv, gr_wpool, gr_pscale, gr_gmlp, gr_gfinal = _unflatten(small_sum.reshape(-1), small_shapes)
    gr_meta = lax.dynamic_slice_in_dim(gr_meta, me * (d // N_DEV), d // N_DEV, axis=1)
    gr_wconv = lax.dynamic_slice_in_dim(gr_wconv, me * (cw // N_DEV), cw // N_DEV, axis=2)
    small_g = [gr_meta, gr_gmix, gr_wconv, gr_wpool, gr_pscale, gr_gmlp, gr_gfinal]
    local_shapes = [a.shape for a in small_g]
    pack_small = lambda parts_: _flat_pad(parts_, 8 * BLK).reshape(-1, BLK)
    small = _adamw(pack_small([meta_tokens, g_mix, w_conv, w_pool, pool_scale, g_mlp, g_final]), pack_small(small_g)[None],
                   pack_small([m_meta_tokens, m_g_mix, m_w_conv, m_w_pool, m_pool_scale, m_g_mlp, m_g_final]),
                   pack_small([v_meta_tokens, v_g_mix, v_w_conv, v_w_pool, v_pool_scale, v_g_mlp, v_g_final]), "adamw_small")
    _, de_s, nm_s, nv_s = [_unflatten(b.reshape(-1), local_shapes) for b in small]

    loss = lax.psum(loss_part[0, 0], ("x", "y", "c"))

    def ordered(meta, gmix, wconv, wpool, pscale, gmlp, gfinal, win, wout, wup, wdown):
        return [meta, gmix, win, wconv, wpool, pscale, wout, gmlp, wup, wdown, gfinal]

    grads = ordered(*small_g, gr_win, gr_wout, gr_wup, gr_wdown)
    deltas = ordered(*de_s, de_win, de_wout, de_wup, de_wdown)
    new_m = ordered(*nm_s, nm_win, nm_wout, nm_wup, nm_wdown)
    new_v = ordered(*nv_s, nv_win, nv_wout, nv_wup, nv_wdown)
    return (loss, grad_x, *grads, *deltas, *new_m, *new_v)
```

```python
import functools

import jax
import jax.numpy as jnp
from jax import lax
from jax.experimental import pallas as pl
from jax.experimental.pallas import tpu as pltpu

F32 = jnp.float32
BF16 = jnp.bfloat16

N_DEV = 8
N_META = 16
HEAD_DIM = 64
BLK = 128
PAD = BLK - N_META
N_POOL_GROUPS = 4
EPS = 1e-6
PACK_W = 1024

ADAM_LR = 0.001
ADAM_B1 = 0.9
ADAM_B2 = 0.999
ADAM_EPS = 1e-08
ADAM_WD = 0.01
ADAM_STEP = 10

V7X_VMEM_LIMIT = 56 * 1024 * 1024


def _params(*sem):
    return pltpu.CompilerParams(dimension_semantics=sem, vmem_limit_bytes=V7X_VMEM_LIMIT)


SUBLANES = 8
ROW_TILE = 512
ROW_TILE_MLP = 1152
TN_ACC_BYTES = 8 * 1024 * 1024


def _row_tile(t, cap=ROW_TILE):
    for tm in range(min(cap, t) // SUBLANES * SUBLANES, 0, -SUBLANES):
        if t % tm == 0:
            return tm
    raise ValueError(f"row count {t} is not a multiple of 8")


def _full(a):
    nd = a.ndim
    return pl.BlockSpec(a.shape, lambda *_: (0,) * nd)


def _dot(a, b):
    return jnp.dot(a, b, preferred_element_type=F32)


def _dot_nt(a, b):
    return lax.dot_general(a, b, (((1,), (1,)), ((), ())), preferred_element_type=F32)


def _dot_tn(a, b):
    return lax.dot_general(a, b, (((0,), (0,)), ((), ())), preferred_element_type=F32)


def _rms_fwd(x, g):
    rstd = lax.rsqrt(jnp.mean(x * x, axis=-1, keepdims=True) + EPS)
    return x * rstd * g


def _rms_bwd(dy, x, g):
    rstd = lax.rsqrt(jnp.mean(x * x, axis=-1, keepdims=True) + EPS)
    xhat = x * rstd
    dxhat = dy * g
    dx = rstd * (dxhat - xhat * jnp.mean(dxhat * xhat, axis=-1, keepdims=True))
    return dx, jnp.sum(dy * xhat, axis=0, keepdims=True)


def _inproj_fwd(h, g, wa, wq, wk, wv):
    t, d = h.shape
    tm = _row_tile(t)
    ua_w, aw = wa.shape[1], wq.shape[1]

    def body(h_ref, g_ref, wa_ref, wq_ref, wk_ref, wv_ref, ua_ref, q_ref, k_ref, v_ref, hn_ref):
        hn = _rms_fwd(h_ref[...], g_ref[...]).astype(BF16)
        hn_ref[...] = hn
        ua_ref[...] = _dot(hn, wa_ref[...])
        q_ref[...] = _dot(hn, wq_ref[...]).astype(BF16)
        k_ref[...] = _dot(hn, wk_ref[...]).astype(BF16)
        v_ref[...] = _dot(hn, wv_ref[...]).astype(BF16)

    row = lambda w: pl.BlockSpec((tm, w), lambda i: (i, 0))
    return pl.pallas_call(
        body, name="inproj_fwd", grid=(t // tm,),
        in_specs=[row(d), _full(g), _full(wa), _full(wq), _full(wk), _full(wv)],
        out_specs=[row(ua_w), row(aw), row(aw), row(aw), row(d)],
        out_shape=[jax.ShapeDtypeStruct((t, ua_w), F32)] + [jax.ShapeDtypeStruct((t, aw), BF16)] * 3
        + [jax.ShapeDtypeStruct((t, d), BF16)],
        compiler_params=_params("arbitrary"),
    )(h, g, wa, wq, wk, wv)


def _pool_geometry(lp, pw):
    pg = pw // N_POOL_GROUPS
    row = lax.broadcasted_iota(jnp.int32, (lp, pw), 0)
    lane = lax.broadcasted_iota(jnp.int32, (lp, pw), 1)
    grp = [(lane >= g * pg) & (lane < (g + 1) * pg) for g in range(N_POOL_GROUPS)]
    wlen = jnp.where(grp[0], 2, jnp.where(grp[1], 4, jnp.where(grp[2], 8, 16)))
    cnt = jnp.clip(row - (PAD - 1), 1, wlen).astype(F32)
    return grp, cnt, row >= PAD


def _by_group(grp, vals):
    return jnp.where(grp[0], vals[0], jnp.where(grp[1], vals[1], jnp.where(grp[2], vals[2], vals[3])))


def _pooled(p, grp, cnt, real):
    s2 = p + pltpu.roll(p, 1, 0)
    s4 = s2 + pltpu.roll(s2, 2, 0)
    s8 = s4 + pltpu.roll(s4, 4, 0)
    s16 = s8 + pltpu.roll(s8, 8, 0)
    return jnp.where(real, _by_group(grp, (s2, s4, s8, s16)) / cnt - p, 0.0)


def _conv(uu, wc_ref):
    return wc_ref[2:3, :] * uu + wc_ref[1:2, :] * pltpu.roll(uu, 1, 0) + wc_ref[0:1, :] * pltpu.roll(uu, 2, 0)


def _mixer_fwd(ua, wconv, wbd, pscale, nb):
    t = ua.shape[0]
    lp = t // nb
    cw, pw = wconv.shape[1], wbd.shape[0]

    def body(ua_ref, wc_ref, wbd_ref, ps_ref, y_ref):
        cb = ua_ref[:, 0:cw]
        uu = ua_ref[:, cw:2 * cw] * ua_ref[:, 2 * cw:3 * cw]
        y_ref[:, 0:cw] = (cb * _conv(uu, wc_ref)).astype(BF16)
        grp, cnt, real = _pool_geometry(lp, pw)
        pooled = _pooled(ua_ref[:, 3 * cw:3 * cw + pw], grp, cnt, real)
        y_ref[:, cw:cw + pw] = (_dot(pooled.astype(BF16), wbd_ref[...]) * ps_ref[...]).astype(BF16)

    return pl.pallas_call(
        body, name="mixer_fwd", grid=(nb,),
        in_specs=[pl.BlockSpec((lp, ua.shape[1]), lambda s: (s, 0)), _full(wconv), _full(wbd), _full(pscale)],
        out_specs=pl.BlockSpec((lp, cw + pw), lambda s: (s, 0)),
        out_shape=jax.ShapeDtypeStruct((t, cw + pw), BF16),
        compiler_params=_params("arbitrary"),
    )(ua, wconv, wbd, pscale)


def _split_bf16(x, parts):
    out = []
    for _ in range(parts):
        hi = x.astype(BF16)
        out.append(hi)
        x = x - hi.astype(F32)
    return out


ATT_NW = 2
ATT_W = ATT_NW * BLK
ATT_PAIRS_FWD = 4
ATT_PAIRS_BWD = 4
HW = 2 * HEAD_DIM
ATT_SCALE = HEAD_DIM ** -0.5


def _attn_lk(z, valid):
    nz = -z
    return jnp.where(valid, jnp.minimum(nz, 0.0) - jnp.log(1.0 + jnp.exp(jnp.minimum(z, nz))), 0.0)


def _tri_sum(x, tri):
    hi, lo = _split_bf16(x, 2)
    return _dot(hi, tri) + _dot(lo, tri)


def _attn_weights(z, suffix, carry, valid):
    return jnp.where(valid, jnp.exp(z + suffix + carry), 0.0)


def _attn_consts():
    r = lax.broadcasted_iota(jnp.int32, (2 * BLK, ATT_W), 0)
    c = lax.broadcasted_iota(jnp.int32, (2 * BLK, ATT_W), 1)
    cmr = c - (r & (BLK - 1))
    kr = lax.broadcasted_iota(jnp.int32, (ATT_W, ATT_W), 0)
    kc = lax.broadcasted_iota(jnp.int32, (ATT_W, ATT_W), 1)
    lane = lax.broadcasted_iota(jnp.int32, (1, HW), 1)
    heads = (lane < HEAD_DIM, lane >= HEAD_DIM)
    return c, cmr, kr, kc, heads


def _tri(cond):
    return jnp.where(cond, 1.0, 0.0).astype(BF16)


def _rows(i):
    return pl.ds(pl.multiple_of(i * BLK, BLK), BLK)


def _stack_heads(x, heads):
    zero = jnp.zeros_like(x)
    return jnp.concatenate([jnp.where(heads[0], x, zero), jnp.where(heads[1], x, zero)], axis=0)


def _unstack_heads(x, heads):
    return jnp.where(heads[0], x[:BLK], x[BLK:])


def _window(i, s):
    start = jnp.maximum(i + 1 - ATT_NW * (s + 1), 0) * BLK
    return start, (i + 1 - ATT_NW * s) * BLK


def _attn_fwd(q, k, v, nb, gather=None):
    t, aw = q.shape
    lp = t // nb
    nblk = lp // BLK
    pp = min(ATT_PAIRS_FWD, aw // HW)
    cols = [slice(p * HW, (p + 1) * HW) for p in range(pp)]
    ncol = aw // (pp * HW)
    nsteps = nb * ncol

    def body(q_ref, k_ref, v_ref, *rest):
        if gather is None:
            (o_ref,) = rest
        else:
            x_ref, o_ref, g_ref, send_sems, recv_sems, local_sem = rest
            step = pl.program_id(0) * ncol + pl.program_id(1)
            start, forward, finish = _gather_phases(x_ref, g_ref, send_sems, recv_sems, local_sem)
            pl.when(step == 0)(start)
            pl.when(step == nsteps // 2)(forward)
        c, cmr, kr, kc, heads = _attn_consts()
        m_from = _tri(kr >= kc)

        def qblock(i, _):
            qs = [_stack_heads(q_ref[_rows(i), cs] * ATT_SCALE, heads) for cs in cols]

            def window(s, st):
                accs, carries = st
                start, end = _window(i, s)
                keys = pl.ds(pl.multiple_of(start, BLK), ATT_W)
                valid = (cmr < i * BLK - start) & (c < end - start)
                zs = [_dot_nt(qs[p], k_ref[keys, cs]) for p, cs in enumerate(cols)]
                suffix = [_tri_sum(_attn_lk(zs[p], valid), m_from) for p in range(pp)]
                new_accs, new_carries = [], []
                for p, cs in enumerate(cols):
                    a = _attn_weights(zs[p], suffix[p], carries[p], valid)
                    new_accs.append(accs[p] + _dot(a.astype(BF16), v_ref[keys, cs]))
                    new_carries.append(carries[p] + suffix[p][:, 0:1])
                return tuple(new_accs), tuple(new_carries)

            init = (tuple(jnp.zeros((2 * BLK, HW), F32) for _ in cols), tuple(jnp.zeros((2 * BLK, 1), F32) for _ in cols))
            accs, _ = lax.fori_loop(0, (i + ATT_NW) // ATT_NW, window, init)
            for p, cs in enumerate(cols):
                o_ref[_rows(i), cs] = _unstack_heads(accs[p], heads).astype(BF16)
            return 0

        lax.fori_loop(0, nblk, qblock, 0)
        if gather is not None:
            pl.when(step == nsteps - 1)(finish)

    spec = pl.BlockSpec((lp, pp * HW), lambda s, p: (s, p))
    y_shape = jax.ShapeDtypeStruct((t, aw), BF16)
    if gather is None:
        return pl.pallas_call(
            body, name="attn_fwd", grid=(nb, ncol), in_specs=[spec, spec, spec], out_specs=spec, out_shape=y_shape,
            compiler_params=_params("arbitrary", "arbitrary"),
        )(q, k, v)
    return pl.pallas_call(
        body, name="attn_fwd_gather", grid=(nb, ncol), in_specs=[spec, spec, spec, ANY], out_specs=[spec, ANY],
        out_shape=[y_shape, jax.ShapeDtypeStruct((N_DEV,) + gather.shape, gather.dtype)], scratch_shapes=COMM_SEMS,
        compiler_params=_params("arbitrary", "arbitrary"),
    )(q, k, v, gather)


def _outproj(h, ya, yb, wa, wb):
    t, d = h.shape
    tm = _row_tile(t)

    def body(h_ref, ya_ref, yb_ref, wa_ref, wb_ref, o_ref):
        o_ref[...] = h_ref[...] + _dot(ya_ref[...], wa_ref[...]) + _dot(yb_ref[...], wb_ref[...])

    row = lambda w: pl.BlockSpec((tm, w), lambda i: (i, 0))
    return pl.pallas_call(
        body, name="outproj", grid=(t // tm,),
        in_specs=[row(d), row(ya.shape[1]), row(yb.shape[1]), _full(wa), _full(wb)],
        out_specs=row(d), out_shape=jax.ShapeDtypeStruct((t, d), F32),
        compiler_params=_params("arbitrary"),
    )(h, ya, yb, wa, wb)


def _mlp_fwd(h, g, wup, wdown):
    t, d = h.shape
    tm = _row_tile(t, ROW_TILE_MLP)
    nff, _, fc = wup.shape

    def body(h_ref, g_ref, wu_ref, wd_ref, o_ref, hn_ref, acc):
        kk = pl.program_id(1)

        @pl.when(kk == 0)
        def _():
            x = h_ref[...]
            hn_ref[...] = _rms_fwd(x, g_ref[...]).astype(BF16)
            acc[...] = x

        m = _dot(hn_ref[...], wu_ref[...])
        a = jnp.square(jnp.maximum(m, 0.0)).astype(BF16)
        acc[...] += _dot(a, wd_ref[...])

        @pl.when(kk == nff - 1)
        def _():
            o_ref[...] = acc[...]

    row = pl.BlockSpec((tm, d), lambda i, kk: (i, 0))
    return pl.pallas_call(
        body, name="mlp_fwd", grid=(t // tm, nff),
        in_specs=[row, _full(g), pl.BlockSpec((None, d, fc), lambda i, kk: (kk, 0, 0)),
                  pl.BlockSpec((fc, d), lambda i, kk: (kk, 0))],
        out_specs=[row, row],
        out_shape=[jax.ShapeDtypeStruct((t, d), F32), jax.ShapeDtypeStruct((t, d), BF16)],
        scratch_shapes=[pltpu.VMEM((tm, d), F32)],
        compiler_params=_params("arbitrary", "arbitrary"),
    )(h, g, wup, wdown)


def _loss_head(h, g, target, nb):
    t, d = h.shape
    nblk = t // nb // BLK
    nx = nblk - 1

    def body(h_ref, g_ref, t_ref, loss_ref, dh_ref, dg_ref):
        s, i = pl.program_id(0), pl.program_id(1)

        @pl.when((s == 0) & (i == 0))
        def _():
            loss_ref[...] = jnp.zeros_like(loss_ref)
            dg_ref[...] = jnp.zeros_like(dg_ref)

        @pl.when(i == 0)
        def _():
            dh_ref[...] = jnp.zeros_like(dh_ref)

        @pl.when(i > 0)
        def _():
            x, gg = h_ref[...], g_ref[...]
            err = _rms_fwd(x, gg) - t_ref[...]
            loss_ref[...] += jnp.sum(err * err) * (0.5 / d)
            dx, dg = _rms_bwd(err * (1.0 / d), x, gg)
            dh_ref[...] = dx
            dg_ref[...] += dg

    hspec = pl.BlockSpec((BLK, d), lambda s, i: (s * nblk + i, 0))
    return pl.pallas_call(
        body, name="loss_head", grid=(nb, nblk),
        in_specs=[hspec, _full(g), pl.BlockSpec((BLK, d), lambda s, i: (s * nx + jnp.maximum(i - 1, 0), 0))],
        out_specs=[pl.BlockSpec((1, BLK), lambda s, i: (0, 0)), hspec, pl.BlockSpec((1, d), lambda s, i: (0, 0))],
        out_shape=[jax.ShapeDtypeStruct((1, BLK), F32), jax.ShapeDtypeStruct((t, d), F32), jax.ShapeDtypeStruct((1, d), F32)],
        compiler_params=_params("arbitrary", "arbitrary"),
    )(h, g, target)


def _mlp_bwd(dh, h, hn, g, wup, wdown):
    t, d = h.shape
    tm = _row_tile(t, ROW_TILE_MLP)
    nff, _, fc = wup.shape

    def body(dh_ref, h_ref, hn_ref, g_ref, wu_ref, wd_ref, dhm_ref, a_ref, dm_ref, dg_ref, dhn, dhb):
        i, kk = pl.program_id(0), pl.program_id(1)

        @pl.when((i == 0) & (kk == 0))
        def _():
            dg_ref[...] = jnp.zeros_like(dg_ref)

        @pl.when(kk == 0)
        def _():
            dhb[...] = dh_ref[...].astype(BF16)
            dhn[...] = jnp.zeros_like(dhn)

        r = jnp.maximum(_dot(hn_ref[...], wu_ref[...]), 0.0)
        a_ref[...] = (r * r).astype(BF16)
        dm = (_dot_nt(dhb[...], wd_ref[...]) * (2.0 * r)).astype(BF16)
        dm_ref[...] = dm
        dhn[...] += _dot_nt(dm, wu_ref[...])

        @pl.when(kk == nff - 1)
        def _():
            dx, dg = _rms_bwd(dhn[...], h_ref[...], g_ref[...])
            dhm_ref[...] = dh_ref[...] + dx
            dg_ref[...] += dg

    row = pl.BlockSpec((tm, d), lambda i, kk: (i, 0))
    ff = pl.BlockSpec((tm, fc), lambda i, kk: (i, kk))
    return pl.pallas_call(
        body, name="mlp_bwd", grid=(t // tm, nff),
        in_specs=[row, row, row, _full(g), pl.BlockSpec((None, d, fc), lambda i, kk: (kk, 0, 0)),
                  pl.BlockSpec((fc, d), lambda i, kk: (kk, 0))],
        out_specs=[row, ff, ff, pl.BlockSpec((1, d), lambda i, kk: (0, 0))],
        out_shape=[jax.ShapeDtypeStruct((t, d), F32), jax.ShapeDtypeStruct((t, nff * fc), BF16),
                   jax.ShapeDtypeStruct((t, nff * fc), BF16), jax.ShapeDtypeStruct((1, d), F32)],
        scratch_shapes=[pltpu.VMEM((tm, d), F32), pltpu.VMEM((tm, d), BF16)],
        compiler_params=_params("arbitrary", "arbitrary"),
    )(dh, h, hn, g, wup, wdown)


def _matmul_tn(x, y, name):
    t, k1 = x.shape
    n = y.shape[1]
    tt = _row_tile(t)
    tn = min(n, max(512, TN_ACC_BYTES // (4 * k1)))
    steps = t // tt

    def body(x_ref, y_ref, o_ref, acc):
        @pl.when(pl.program_id(1) == 0)
        def _():
            acc[...] = jnp.zeros_like(acc)

        acc[...] += _dot_tn(x_ref[...].astype(BF16), y_ref[...].astype(BF16))

        @pl.when(pl.program_id(1) == steps - 1)
        def _():
            o_ref[...] = acc[...].astype(BF16)

    return pl.pallas_call(
        body, name=name, grid=(n // tn, steps),
        in_specs=[pl.BlockSpec((tt, k1), lambda j, r: (r, 0)), pl.BlockSpec((tt, tn), lambda j, r: (r, j))],
        out_specs=pl.BlockSpec((k1, tn), lambda j, r: (0, j)),
        out_shape=jax.ShapeDtypeStruct((k1, n), BF16),
        scratch_shapes=[pltpu.VMEM((k1, tn), F32)],
        compiler_params=_params("arbitrary", "arbitrary"),
    )(x, y)


def _outproj_bwd(dh, wa, wb):
    t, d = dh.shape
    tm = _row_tile(t)
    na, nbw = wa.shape[0], wb.shape[0]

    def body(dh_ref, wa_ref, wb_ref, da_ref, db_ref):
        x = dh_ref[...].astype(BF16)
        da_ref[...] = _dot_nt(x, wa_ref[...])
        db_ref[...] = _dot_nt(x, wb_ref[...]).astype(BF16)

    row = lambda w: pl.BlockSpec((tm, w), lambda i: (i, 0))
    return pl.pallas_call(
        body, name="outproj_bwd", grid=(t // tm,),
        in_specs=[row(d), _full(wa), _full(wb)], out_specs=[row(na), row(nbw)],
        out_shape=[jax.ShapeDtypeStruct((t, na), F32), jax.ShapeDtypeStruct((t, nbw), BF16)],
        compiler_params=_params("arbitrary"),
    )(dh, wa, wb)


def _attn_bwd(q, k, v, do, nb, exchange=None):
    t, aw = q.shape
    lp = t // nb
    nblk = lp // BLK
    pp = min(ATT_PAIRS_BWD, aw // HW)
    cols = [slice(p * HW, (p + 1) * HW) for p in range(pp)]
    ncol = aw // (pp * HW)
    nsteps = nb * ncol
    max_windows = (nblk + ATT_NW - 1) // ATT_NW

    def body(q_ref, k_ref, v_ref, do_ref, *rest):
        if exchange is None:
            dq_ref, dk_ref, dv_ref, dk_acc, dv_acc, g_s, sig_s = rest
        else:
            x_ref, dq_ref, dk_ref, dv_ref, e_ref, dk_acc, dv_acc, g_s, sig_s, send_sems, recv_sems, local_sem = rest
            step = pl.program_id(0) * ncol + pl.program_id(1)
            start, finish = _exchange_phases(x_ref, e_ref, send_sems, recv_sems, local_sem)
            pl.when(step == 0)(start)
        c, cmr, kr, kc, heads = _attn_consts()
        m_from = _tri(kr >= kc)
        m_before = _tri(kr < kc)
        dk_acc[...] = jnp.zeros_like(dk_acc)
        dv_acc[...] = jnp.zeros_like(dv_acc)

        def qblock(i, _):
            qs = [_stack_heads(q_ref[_rows(i), cs] * ATT_SCALE, heads) for cs in cols]
            dos = [_stack_heads(do_ref[_rows(i), cs], heads) for cs in cols]
            nwin = (i + ATT_NW) // ATT_NW

            def geometry(s):
                start, end = _window(i, s)
                return (cmr < i * BLK - start) & (c < end - start), pl.ds(pl.multiple_of(start, BLK), ATT_W)

            def down(s, carries):
                valid, keys = geometry(s)
                zs = [_dot_nt(qs[p], k_ref[keys, cs]) for p, cs in enumerate(cols)]
                das = [_dot_nt(dos[p], v_ref[keys, cs]) for p, cs in enumerate(cols)]
                suffix = []
                for p in range(pp):
                    lk = _attn_lk(zs[p], valid)
                    sig_s[p, s] = jnp.exp(zs[p] + lk)
                    suffix.append(_tri_sum(lk, m_from))
                new_carries = []
                for p, cs in enumerate(cols):
                    a = _attn_weights(zs[p], suffix[p], carries[p], valid)
                    g_s[p, s] = das[p] * a
                    dv_acc[keys, cs] += _dot_tn(a.astype(BF16), dos[p])
                    new_carries.append(carries[p] + suffix[p][:, 0:1])
                return tuple(new_carries)

            lax.fori_loop(0, nwin, down, tuple(jnp.zeros((2 * BLK, 1), F32) for _ in cols))

            def up(ss, st):
                accs, carries = st
                s = nwin - 1 - ss
                valid, keys = geometry(s)
                gs = [g_s[p, s] for p in range(pp)]
                before = [_tri_sum(gs[p], m_before) for p in range(pp)]
                new_accs, new_carries = [], []
                for p, cs in enumerate(cols):
                    g, sig = gs[p], sig_s[p, s]
                    dz = jnp.where(valid, g * (1.0 - sig) - (before[p] + carries[p]) * sig, 0.0).astype(BF16)
                    new_accs.append(accs[p] + _dot(dz, k_ref[keys, cs]))
                    dk_acc[keys, cs] += _dot_tn(dz, qs[p])
                    new_carries.append(carries[p] + jnp.sum(g, axis=1, keepdims=True))
                return tuple(new_accs), tuple(new_carries)

            init = (tuple(jnp.zeros((2 * BLK, HW), F32) for _ in cols), tuple(jnp.zeros((2 * BLK, 1), F32) for _ in cols))
            accs, _ = lax.fori_loop(0, nwin, up, init)
            for p, cs in enumerate(cols):
                dq_ref[_rows(i), cs] = (_unstack_heads(accs[p], heads) * ATT_SCALE).astype(BF16)
            return 0

        lax.fori_loop(0, nblk, qblock, 0)
        dk_ref[...] = dk_acc[...].astype(BF16)
        dv_ref[...] = dv_acc[...].astype(BF16)
        if exchange is not None:
            pl.when(step == nsteps - 1)(finish)

    spec = pl.BlockSpec((lp, pp * HW), lambda s, p: (s, p), pipeline_mode=pl.Buffered(1))
    d_shape = jax.ShapeDtypeStruct((t, aw), BF16)
    scratch = [pltpu.VMEM((lp, pp * HW), F32), pltpu.VMEM((lp, pp * HW), F32),
               pltpu.VMEM((pp, max_windows, 2 * BLK, ATT_W), F32), pltpu.VMEM((pp, max_windows, 2 * BLK, ATT_W), F32)]
    if exchange is None:
        return pl.pallas_call(
            body, name="attn_bwd", grid=(nb, ncol), in_specs=[spec] * 4, out_specs=[spec] * 3, out_shape=[d_shape] * 3,
            scratch_shapes=scratch, compiler_params=_params("arbitrary", "arbitrary"),
        )(q, k, v, do)
    return pl.pallas_call(
        body, name="attn_bwd_exchange", grid=(nb, ncol), in_specs=[spec] * 4 + [ANY], out_specs=[spec] * 3 + [ANY],
        out_shape=[d_shape] * 3 + [jax.ShapeDtypeStruct(exchange.shape, exchange.dtype)],
        scratch_shapes=scratch + COMM_SEMS, compiler_params=_params("arbitrary", "arbitrary"),
    )(q, k, v, do, exchange)


def _mixer_bwd(ua, dy, wconv, wbd, pscale, nb):
    t = ua.shape[0]
    lp = t // nb
    cw, pw = wconv.shape[1], wbd.shape[0]

    def body(ua_ref, dy_ref, wc_ref, wbd_ref, ps_ref, du_ref, dwc_ref, dps_ref, dwb_ref):
        @pl.when(pl.program_id(0) == 0)
        def _():
            dwc_ref[...] = jnp.zeros_like(dwc_ref)
            dps_ref[...] = jnp.zeros_like(dps_ref)
            dwb_ref[...] = jnp.zeros_like(dwb_ref)

        up = lambda x, n: pltpu.roll(x, lp - n, 0)
        cb, cc, cx = ua_ref[:, 0:cw], ua_ref[:, cw:2 * cw], ua_ref[:, 2 * cw:3 * cw]
        uu = cc * cx
        dyc = dy_ref[:, 0:cw]
        du_ref[:, 0:cw] = (dyc * _conv(uu, wc_ref)).astype(BF16)
        dcv = dyc * cb
        duu = wc_ref[2:3, :] * dcv + wc_ref[1:2, :] * up(dcv, 1) + wc_ref[0:1, :] * up(dcv, 2)
        du_ref[:, cw:2 * cw] = (duu * cx).astype(BF16)
        du_ref[:, 2 * cw:3 * cw] = (duu * cc).astype(BF16)
        dwc_ref[0:1, :] += jnp.sum(dcv * pltpu.roll(uu, 2, 0), axis=0, keepdims=True)
        dwc_ref[1:2, :] += jnp.sum(dcv * pltpu.roll(uu, 1, 0), axis=0, keepdims=True)
        dwc_ref[2:3, :] += jnp.sum(dcv * uu, axis=0, keepdims=True)
        grp, cnt, real = _pool_geometry(lp, pw)
        p = ua_ref[:, 3 * cw:3 * cw + pw]
        pooled = _pooled(p, grp, cnt, real).astype(BF16)
        dyp = dy_ref[:, cw:cw + pw]
        dps_ref[...] += jnp.sum(dyp * _dot(pooled, wbd_ref[...]), axis=0, keepdims=True)
        dpre = (dyp * ps_ref[...]).astype(BF16)
        dwb_ref[...] += _dot_tn(pooled, dpre)
        dpooled = jnp.where(real, _dot_nt(dpre, wbd_ref[...]), 0.0)
        xm = dpooled / cnt
        l2 = xm + up(xm, 1)
        l4 = l2 + up(l2, 2)
        l8 = l4 + up(l4, 4)
        l16 = l8 + up(l8, 8)
        du_ref[:, 3 * cw:3 * cw + pw] = jnp.where(real, _by_group(grp, (l2, l4, l8, l16)) - dpooled, 0.0).astype(BF16)

    seq = lambda w: pl.BlockSpec((lp, w), lambda s: (s, 0))
    return pl.pallas_call(
        body, name="mixer_bwd", grid=(nb,),
        in_specs=[seq(ua.shape[1]), seq(cw + pw), _full(wconv), _full(wbd), _full(pscale)],
        out_specs=[seq(ua.shape[1]), pl.BlockSpec((3, cw), lambda s: (0, 0)), pl.BlockSpec((1, pw), lambda s: (0, 0)),
                   pl.BlockSpec((pw, pw), lambda s: (0, 0))],
        out_shape=[jax.ShapeDtypeStruct(ua.shape, BF16), jax.ShapeDtypeStruct((3, cw), F32),
                   jax.ShapeDtypeStruct((1, pw), F32), jax.ShapeDtypeStruct((pw, pw), F32)],
        compiler_params=_params("arbitrary"),
    )(ua, dy, wconv, wbd, pscale)


def _inproj_bwd(dh, h, g, dua, dq, dk, dv, wa, wq, wk, wv):
    t, d = h.shape
    tm = _row_tile(t)

    def body(dh_ref, h_ref, g_ref, dua_ref, dq_ref, dk_ref, dv_ref, wa_ref, wq_ref, wk_ref, wv_ref, o_ref, dg_ref):
        @pl.when(pl.program_id(0) == 0)
        def _():
            dg_ref[...] = jnp.zeros_like(dg_ref)

        dhn = (_dot_nt(dua_ref[...], wa_ref[...]) + _dot_nt(dq_ref[...], wq_ref[...])
               + _dot_nt(dk_ref[...], wk_ref[...]) + _dot_nt(dv_ref[...], wv_ref[...]))
        dx, dg = _rms_bwd(dhn, h_ref[...], g_ref[...])
        o_ref[...] = dh_ref[...] + dx
        dg_ref[...] += dg

    row = lambda w: pl.BlockSpec((tm, w), lambda i: (i, 0))
    return pl.pallas_call(
        body, name="inproj_bwd", grid=(t // tm,),
        in_specs=[row(d), row(d), _full(g), row(dua.shape[1]), row(dq.shape[1]), row(dk.shape[1]), row(dv.shape[1]),
                  _full(wa), _full(wq), _full(wk), _full(wv)],
        out_specs=[row(d), pl.BlockSpec((1, d), lambda i: (0, 0))],
        out_shape=[jax.ShapeDtypeStruct((t, d), F32), jax.ShapeDtypeStruct((1, d), F32)],
        compiler_params=_params("arbitrary"),
    )(dh, h, g, dua, dq, dk, dv, wa, wq, wk, wv)


def _part_tile(r):
    for tr in (256, 128, 64, 32, 16, 8):
        if r % tr == 0:
            return tr
    return r


def _sum_parts(x, name):
    n, r, c = x.shape
    tr = _part_tile(r)

    def body(x_ref, o_ref):
        acc = x_ref[0]
        for p in range(1, n):
            acc = acc + x_ref[p]
        o_ref[...] = acc

    return pl.pallas_call(
        body, name=name, grid=(r // tr,),
        in_specs=[pl.BlockSpec((n, tr, c), lambda i: (0, i, 0))], out_specs=pl.BlockSpec((tr, c), lambda i: (i, 0)),
        out_shape=jax.ShapeDtypeStruct((r, c), F32), compiler_params=_params("arbitrary"),
    )(x)


def _adamw(w, gparts, m, v, name):
    n, r, c = gparts.shape
    tr = _part_tile(r)

    def body(w_ref, g_ref, m_ref, v_ref, go_ref, d_ref, mo_ref, vo_ref):
        g = g_ref[0].astype(F32)
        for p in range(1, n):
            g = g + g_ref[p].astype(F32)
        go_ref[...] = g
        mm = ADAM_B1 * m_ref[...] + (1.0 - ADAM_B1) * g
        vv = ADAM_B2 * v_ref[...] + (1.0 - ADAM_B2) * jnp.square(g)
        mo_ref[...] = mm
        vo_ref[...] = vv
        m_hat = mm / (1.0 - ADAM_B1 ** ADAM_STEP)
        v_hat = vv / (1.0 - ADAM_B2 ** ADAM_STEP)
        d_ref[...] = -ADAM_LR * (m_hat / (jnp.sqrt(v_hat) + ADAM_EPS) + ADAM_WD * w_ref[...])

    row = pl.BlockSpec((tr, c), lambda i: (i, 0))
    return pl.pallas_call(
        body, name=name, grid=(r // tr,),
        in_specs=[row, pl.BlockSpec((n, tr, c), lambda i: (0, i, 0)), row, row], out_specs=[row] * 4,
        out_shape=[jax.ShapeDtypeStruct((r, c), F32)] * 4, compiler_params=_params("arbitrary"),
    )(w, gparts, m, v)


MESH = pl.DeviceIdType.MESH
ANY = pl.BlockSpec(memory_space=pl.ANY)


def _all_gather(x, name):
    r, c = x.shape

    def body(x_ref, out_ref, send_sems, recv_sems, local_sem):
        start, forward, finish = _gather_phases(x_ref, out_ref, send_sems, recv_sems, local_sem)
        start()
        forward()
        finish()

    return pl.pallas_call(
        body, name=name, in_specs=[ANY], out_specs=ANY,
        out_shape=jax.ShapeDtypeStruct((N_DEV, r, c), x.dtype), scratch_shapes=COMM_SEMS,
    )(x)


COMM_SEMS = [pltpu.SemaphoreType.DMA((7,)), pltpu.SemaphoreType.DMA((7,)), pltpu.SemaphoreType.DMA(())]


def _gather_phases(x_ref, out_ref, send_sems, recv_sems, local_sem):
    mx, my, mc = lax.axis_index("x"), lax.axis_index("y"), lax.axis_index("c")
    me, sibling = (mx, my, mc), (mx, my, 1 - mc)
    chips = [(1 - mx, my), (mx, 1 - my), (1 - mx, 1 - my)]

    def slot(px, py, pc):
        return out_ref.at[4 * px + 2 * py + pc]

    def copy(kk, block, to, src=None):
        return pltpu.make_async_remote_copy(
            src_ref=slot(*block) if src is None else src, dst_ref=slot(*block),
            send_sem=send_sems.at[kk], recv_sem=recv_sems.at[kk], device_id=to, device_id_type=MESH)

    mine = pltpu.make_async_copy(x_ref, slot(*me), local_sem)
    first = [copy(0, me, sibling, src=x_ref)] + [copy(1 + j, me, (*chip, mc), src=x_ref) for j, chip in enumerate(chips)]
    passed = [copy(4 + j, (*chip, mc), sibling) for j, chip in enumerate(chips)]

    def start():
        mine.start()
        for cp in first:
            cp.start()

    def forward():
        for j, chip in enumerate(chips):
            copy(1 + j, (*chip, mc), me).wait_recv()
            passed[j].start()

    def finish():
        copy(0, sibling, me).wait_recv()
        for j, chip in enumerate(chips):
            copy(4 + j, (*chip, 1 - mc), me).wait_recv()
        for cp in first + passed:
            cp.wait_send()
        mine.wait()

    return start, forward, finish


def _exchange_phases(x_ref, out_ref, send_sems, recv_sems, local_sem):
    mx, my, mc = lax.axis_index("x"), lax.axis_index("y"), lax.axis_index("c")
    me = 4 * mx + 2 * my + mc
    mine = pltpu.make_async_copy(x_ref.at[me], out_ref.at[me], local_sem)
    copies = []
    for kk in range(1, N_DEV):
        px, py, pc = mx ^ (kk >> 2), my ^ ((kk >> 1) & 1), mc ^ (kk & 1)
        peer = 4 * px + 2 * py + pc
        copies.append((peer, pltpu.make_async_remote_copy(
            src_ref=x_ref.at[peer], dst_ref=out_ref.at[me], send_sem=send_sems.at[kk - 1],
            recv_sem=recv_sems.at[kk - 1], device_id=(px, py, pc), device_id_type=MESH)))

    def start():
        mine.start()
        for _, cp in copies:
            cp.start()

    def finish():
        for kk, (peer, _) in enumerate(copies):
            pltpu.make_async_remote_copy(
                src_ref=x_ref.at[peer], dst_ref=out_ref.at[peer], send_sem=send_sems.at[kk],
                recv_sem=recv_sems.at[kk], device_id=(mx, my, mc), device_id_type=MESH).wait_recv()
        for _, cp in copies:
            cp.wait_send()
        mine.wait()

    return start, finish


def _all_to_all(x, name):
    def body(x_ref, out_ref, send_sems, recv_sems, local_sem):
        start, finish = _exchange_phases(x_ref, out_ref, send_sems, recv_sems, local_sem)
        start()
        finish()

    return pl.pallas_call(
        body, name=name, in_specs=[ANY], out_specs=ANY,
        out_shape=jax.ShapeDtypeStruct(x.shape, x.dtype), scratch_shapes=COMM_SEMS,
    )(x)


def _rows_of(a):
    return a.reshape(-1, PACK_W)


def _unpack_big(buf, shapes):
    out, r0 = [], 0
    for shp in shapes:
        n = 1
        for s in shp:
            n *= s
        out.append(buf[r0:r0 + n // PACK_W].reshape(shp))
        r0 += n // PACK_W
    return out


def _shards_last(a):
    lead = a.shape[:-1]
    return jnp.moveaxis(a.reshape(*lead, N_DEV, a.shape[-1] // N_DEV), -2, 0)


def _shards_rows(a):
    l, n, c = a.shape
    return jnp.moveaxis(a.reshape(l, N_DEV, n // N_DEV, c), 1, 0)


def _flat_pad(parts, mult):
    flat = jnp.concatenate([p.reshape(-1) for p in parts])
    return jnp.pad(flat, (0, (-flat.shape[0]) % mult))


def _unflatten(flat, shapes):
    out, o = [], 0
    for shp in shapes:
        n = 1
        for s in shp:
            n *= s
        out.append(flat[o:o + n].reshape(shp))
        o += n
    return out


def kernel(x, meta_tokens, g_mix, w_in, w_conv, w_pool, pool_scale, w_out, g_mlp, w_up, w_down, g_final, loss_target, m_meta_tokens, m_g_mix, m_w_in, m_w_conv, m_w_pool, m_pool_scale, m_w_out, m_g_mlp, m_w_up, m_w_down, m_g_final, v_meta_tokens, v_g_mix, v_w_in, v_w_conv, v_w_pool, v_pool_scale, v_w_out, v_g_mlp, v_w_up, v_w_down, v_g_final):
    nb, seq, d = x.shape
    depth = g_mix.shape[0]
    lp = BLK + seq
    t = nb * lp
    cw = w_conv.shape[2] * N_DEV
    pw = pool_scale.shape[1]
    pg = pw // N_POOL_GROUPS
    aw = (w_in.shape[2] * N_DEV - 3 * cw - pw) // 3
    ua_w = 3 * cw + pw
    d_ff = w_up.shape[2] * N_DEV
    me = 4 * lax.axis_index("x") + 2 * lax.axis_index("y") + lax.axis_index("c")
    first_shapes = [w_in[:1].shape]
    rest_shapes = [w_in[1:].shape, w_out.shape, w_up.shape, w_down.shape]

    def pack_first(a_in):
        return _rows_of(a_in[:1])

    def pack_rest(a_in, a_out, a_up, a_down):
        return jnp.concatenate([_rows_of(a_in[1:]), _rows_of(a_out), _rows_of(a_up), _rows_of(a_down)], axis=0)

    g_first = _all_gather(pack_first(w_in).astype(BF16), "gather_w_in0")
    w_in0_f = jnp.concatenate([g_first[p].reshape(first_shapes[0]) for p in range(N_DEV)], axis=2)
    small_in = _all_gather(_flat_pad([meta_tokens, w_conv], 8 * BLK).reshape(-1, BLK), "gather_small_weights").reshape(N_DEV, -1)
    meta_f = jnp.moveaxis(small_in[:, :meta_tokens.size].reshape(N_DEV, N_META, d // N_DEV), 0, 1).reshape(N_META, d)
    w_conv_f = small_in[:, meta_tokens.size:meta_tokens.size + w_conv.size].reshape((N_DEV,) + w_conv.shape)
    w_conv_f = jnp.moveaxis(w_conv_f, 0, 2).reshape(depth, w_conv.shape[1], cw)

    def split_w_in(w):
        return w[:, :ua_w], w[:, ua_w:ua_w + aw], w[:, ua_w + aw:ua_w + 2 * aw], w[:, ua_w + 2 * aw:]

    wbd = jnp.zeros((depth, pw, pw), F32)
    for gi in range(N_POOL_GROUPS):
        wbd = wbd.at[:, gi * pg:(gi + 1) * pg, gi * pg:(gi + 1) * pg].set(w_pool[:, gi])
    wbd = wbd.astype(BF16)

    h = jnp.concatenate([jnp.zeros((nb, PAD, d), F32), jnp.broadcast_to(meta_f[None], (nb, N_META, d)), x], axis=1).reshape(t, d)
    saved, w_in_parts = [], [None] * depth
    w_in_parts[0] = split_w_in(w_in0_f[0])
    for l in range(depth):
        wa, wq, wk, wv = w_in_parts[l]
        ua, q, k, v, hn1 = _inproj_fwd(h, g_mix[l][None], wa, wq, wk, wv)
        ycp = _mixer_fwd(ua, w_conv_f[l], wbd[l], pool_scale[l][None], nb)
        if l == 0:
            yat, g_rest = _attn_fwd(q, k, v, nb, gather=pack_rest(w_in, w_out, w_up, w_down).astype(BF16))
            per_dev = [_unpack_big(g_rest[p], rest_shapes) for p in range(N_DEV)]
            w_in_rest = jnp.concatenate([pd[0] for pd in per_dev], axis=2)
            for ll in range(1, depth):
                w_in_parts[ll] = split_w_in(w_in_rest[ll - 1])
            w_out_f = jnp.concatenate([pd[1] for pd in per_dev], axis=1)
            w_up_f = jnp.stack([pd[2] for pd in per_dev], axis=1)
            w_down_f = jnp.concatenate([pd[3] for pd in per_dev], axis=1)
            wo_a, wo_b = w_out_f[:, :cw + pw], w_out_f[:, cw + pw:]
        else:
            yat = _attn_fwd(q, k, v, nb)
        h_mid =_outproj(h, ycp, yat, wo_a[l], wo_b[l])
        h_next, hn2 = _mlp_fwd(h_mid, g_mlp[l][None], w_up_f[l], w_down_f[l])
        saved.append((h, ua, q, k, v, hn1, ycp, yat, h_mid, hn2))
        h = h_next

    loss_part, dh, dg_final = _loss_head(h, g_final[None], loss_target.reshape(nb * seq, d), nb)

    g_gmix, g_gmlp, g_wconv, g_pscale, g_wpool = [None] * depth, [None] * depth, [None] * depth, [None] * depth, [None] * depth
    g_win, g_wout, g_wup, g_wdown = [None] * depth, [None] * depth, [None] * depth, [None] * depth
    for l in reversed(range(depth)):
        h_in, ua, q, k, v, hn1, ycp, yat, h_mid, hn2 = saved[l]
        wa, wq, wk, wv = w_in_parts[l]
        dh_mid, act, dm, g_gmlp[l] = _mlp_bwd(dh, h_mid, hn2, g_mlp[l][None], w_up_f[l], w_down_f[l])
        g_wup[l] = _matmul_tn(hn2, dm, "grad_w_up")
        g_wdown[l] = _matmul_tn(act, dh, "grad_w_down")
        dycp, do = _outproj_bwd(dh_mid, wo_a[l], wo_b[l])
        g_wout[l] = jnp.concatenate([_matmul_tn(ycp, dh_mid, "grad_w_out_a"), _matmul_tn(yat, dh_mid, "grad_w_out_b")], axis=0)
        if l == 0:
            send_rest = jnp.concatenate([
                _shards_last(jnp.stack(g_win[1:])).reshape(N_DEV, -1, PACK_W), _shards_rows(jnp.stack(g_wout)).reshape(N_DEV, -1, PACK_W),
                _shards_last(jnp.stack(g_wup)).reshape(N_DEV, -1, PACK_W), _shards_rows(jnp.stack(g_wdown)).reshape(N_DEV, -1, PACK_W)], axis=1)
            dq, dk, dv, parts_rest = _attn_bwd(q, k, v, do, nb, exchange=send_rest)
        else:
            dq, dk, dv = _attn_bwd(q, k, v, do, nb)
        dua, g_wconv[l], g_pscale[l], dwb = _mixer_bwd(ua, dycp, w_conv_f[l], wbd[l], pool_scale[l][None], nb)
        g_wpool[l] = jnp.stack([dwb[gi * pg:(gi + 1) * pg, gi * pg:(gi + 1) * pg] for gi in range(N_POOL_GROUPS)])
        dh, g_gmix[l] = _inproj_bwd(dh_mid, h_in, g_mix[l][None], dua, dq, dk, dv, wa, wq, wk, wv)
        g_win[l] = jnp.concatenate([_matmul_tn(hn1, dua, "grad_w_in_a"), _matmul_tn(hn1, dq, "grad_w_in_q"),
                                    _matmul_tn(hn1, dk, "grad_w_in_k"), _matmul_tn(hn1, dv, "grad_w_in_v")], axis=1)
    dh3 = dh.reshape(nb, lp, d)
    grad_x = dh3[:, BLK:]
    g_meta = _sum_parts(dh3[:, PAD:BLK], "sum_meta_grad")

    parts_first = _all_to_all(_shards_last(g_win[0][None]).reshape(N_DEV, -1, PACK_W), "exchange_w_in0_grads")
    rest = _adamw(pack_rest(w_in, w_out, w_up, w_down), parts_rest, pack_rest(m_w_in, m_w_out, m_w_up, m_w_down),
                  pack_rest(v_w_in, v_w_out, v_w_up, v_w_down), "adamw_rest")
    first = _adamw(pack_first(w_in), parts_first, pack_first(m_w_in), pack_first(v_w_in), "adamw_w_in0")
    rest = [_unpack_big(b, rest_shapes) for b in rest]
    first = [_unpack_big(b, first_shapes) for b in first]
    (gr_win, gr_wout, gr_wup, gr_wdown), (de_win, de_wout, de_wup, de_wdown), (nm_win, nm_wout, nm_wup, nm_wdown), \
        (nv_win, nv_wout, nv_wup, nv_wdown) = [[jnp.concatenate([f[0], r[0]], axis=0), r[1], r[2], r[3]] for f, r in zip(first, rest)]

    small_full = [g_meta, jnp.concatenate(g_gmix), jnp.stack(g_wconv), jnp.stack(g_wpool), jnp.concatenate(g_pscale),
                  jnp.concatenate(g_gmlp), dg_final.reshape(-1)]
    small_shapes = [a.shape for a in small_full]
    small_sum = _sum_parts(_all_gather(_flat_pad(small_full, 8 * BLK).reshape(-1, BLK), "gather_small_grads"), "sum_small_grads")
    gr_meta, gr_gmix, gr_wconv, gr_wpool, gr_pscale, gr_gmlp, gr_gfinal = _unflatten(small_sum.reshape(-1), small_shapes)
    gr_meta = lax.dynamic_slice_in_dim(gr_meta, me * (d // N_DEV), d // N_DEV, axis=1)
    gr_wconv = lax.dynamic_slice_in_dim(gr_wconv, me * (cw // N_DEV), cw // N_DEV, axis=2)
    small_g = [gr_meta, gr_gmix, gr_wconv, gr_wpool, gr_pscale, gr_gmlp, gr_gfinal]
    local_shapes = [a.shape for a in small_g]
    pack_small = lambda parts_: _flat_pad(parts_, 8 * BLK).reshape(-1, BLK)
    small = _adamw(pack_small([meta_tokens, g_mix, w_conv, w_pool, pool_scale, g_mlp, g_final]), pack_small(small_g)[None],
                   pack_small([m_meta_tokens, m_g_mix, m_w_conv, m_w_pool, m_pool_scale, m_g_mlp, m_g_final]),
                   pack_small([v_meta_tokens, v_g_mix, v_w_conv, v_w_pool, v_pool_scale, v_g_mlp, v_g_final]), "adamw_small")
    _, de_s, nm_s, nv_s = [_unflatten(b.reshape(-1), local_shapes) for b in small]

    loss = lax.psum(loss_part[0, 0], ("x", "y", "c"))

    def ordered(meta, gmix, wconv, wpool, pscale, gmlp, gfinal, win, wout, wup, wdown):
        return [meta, gmix, win, wconv, wpool, pscale, wout, gmlp, wup, wdown, gfinal]

    grads = ordered(*small_g, gr_win, gr_wout, gr_wup, gr_wdown)
    deltas = ordered(*de_s, de_win, de_wout, de_wup, de_wdown)
    new_m = ordered(*nm_s, nm_win, nm_wout, nm_wup, nm_wdown)
    new_v = ordered(*nv_s, nv_win, nv_wout, nv_wup, nv_wdown)
    return (loss, grad_x, *grads, *deltas, *new_m, *new_v)
```

```python
import functools

import jax
import jax.numpy as jnp
from jax import lax
from jax.experimental import pallas as pl
from jax.experimental.pallas import tpu as pltpu

F32 = jnp.float32
BF16 = jnp.bfloat16

N_DEV = 8
N_META = 16
HEAD_DIM = 64
BLK = 128
PAD = BLK - N_META
N_POOL_GROUPS = 4
EPS = 1e-6
PACK_W = 1024

ADAM_LR = 0.001
ADAM_B1 = 0.9
ADAM_B2 = 0.999
ADAM_EPS = 1e-08
ADAM_WD = 0.01
ADAM_STEP = 10

V7X_VMEM_LIMIT = 56 * 1024 * 1024


def _params(*sem):
    return pltpu.CompilerParams(dimension_semantics=sem, vmem_limit_bytes=V7X_VMEM_LIMIT)


SUBLANES = 8
ROW_TILE = 512
ROW_TILE_MLP = 1152
TN_ACC_BYTES = 8 * 1024 * 1024


def _row_tile(t, cap=ROW_TILE):
    for tm in range(min(cap, t) // SUBLANES * SUBLANES, 0, -SUBLANES):
        if t % tm == 0:
            return tm
    raise ValueError(f"row count {t} is not a multiple of 8")


def _full(a):
    nd = a.ndim
    return pl.BlockSpec(a.shape, lambda *_: (0,) * nd)


def _dot(a, b):
    return jnp.dot(a, b, preferred_element_type=F32)


def _dot_nt(a, b):
    return lax.dot_general(a, b, (((1,), (1,)), ((), ())), preferred_element_type=F32)


def _dot_tn(a, b):
    return lax.dot_general(a, b, (((0,), (0,)), ((), ())), preferred_element_type=F32)


def _rms_fwd(x, g):
    rstd = lax.rsqrt(jnp.mean(x * x, axis=-1, keepdims=True) + EPS)
    return x * rstd * g


def _rms_bwd(dy, x, g):
    rstd = lax.rsqrt(jnp.mean(x * x, axis=-1, keepdims=True) + EPS)
    xhat = x * rstd
    dxhat = dy * g
    dx = rstd * (dxhat - xhat * jnp.mean(dxhat * xhat, axis=-1, keepdims=True))
    return dx, jnp.sum(dy * xhat, axis=0, keepdims=True)


def _inproj_fwd(h, g, wa, wq, wk, wv):
    t, d = h.shape
    tm = _row_tile(t)
    ua_w, aw = wa.shape[1], wq.shape[1]

    def body(h_ref, g_ref, wa_ref, wq_ref, wk_ref, wv_ref, ua_ref, q_ref, k_ref, v_ref, hn_ref):
        hn = _rms_fwd(h_ref[...], g_ref[...]).astype(BF16)
        hn_ref[...] = hn
        ua_ref[...] = _dot(hn, wa_ref[...])
        q_ref[...] = _dot(hn, wq_ref[...]).astype(BF16)
        k_ref[...] = _dot(hn, wk_ref[...]).astype(BF16)
        v_ref[...] = _dot(hn, wv_ref[...]).astype(BF16)

    row = lambda w: pl.BlockSpec((tm, w), lambda i: (i, 0))
    return pl.pallas_call(
        body, name="inproj_fwd", grid=(t // tm,),
        in_specs=[row(d), _full(g), _full(wa), _full(wq), _full(wk), _full(wv)],
        out_specs=[row(ua_w), row(aw), row(aw), row(aw), row(d)],
        out_shape=[jax.ShapeDtypeStruct((t, ua_w), F32)] + [jax.ShapeDtypeStruct((t, aw), BF16)] * 3
        + [jax.ShapeDtypeStruct((t, d), BF16)],
        compiler_params=_params("arbitrary"),
    )(h, g, wa, wq, wk, wv)


def _pool_geometry(lp, pw):
    pg = pw // N_POOL_GROUPS
    row = lax.broadcasted_iota(jnp.int32, (lp, pw), 0)
    lane = lax.broadcasted_iota(jnp.int32, (lp, pw), 1)
    grp = [(lane >= g * pg) & (lane < (g + 1) * pg) for g in range(N_POOL_GROUPS)]
    wlen = jnp.where(grp[0], 2, jnp.where(grp[1], 4, jnp.where(grp[2], 8, 16)))
    cnt = jnp.clip(row - (PAD - 1), 1, wlen).astype(F32)
    return grp, cnt, row >= PAD


def _by_group(grp, vals):
    return jnp.where(grp[0], vals[0], jnp.where(grp[1], vals[1], jnp.where(grp[2], vals[2], vals[3])))


def _pooled(p, grp, cnt, real):
    s2 = p + pltpu.roll(p, 1, 0)
    s4 = s2 + pltpu.roll(s2, 2, 0)
    s8 = s4 + pltpu.roll(s4, 4, 0)
    s16 = s8 + pltpu.roll(s8, 8, 0)
    return jnp.where(real, _by_group(grp, (s2, s4, s8, s16)) / cnt - p, 0.0)


def _conv(uu, wc_ref):
    return wc_ref[2:3, :] * uu + wc_ref[1:2, :] * pltpu.roll(uu, 1, 0) + wc_ref[0:1, :] * pltpu.roll(uu, 2, 0)


def _mixer_fwd(ua, wconv, wbd, pscale, nb):
    t = ua.shape[0]
    lp = t // nb
    cw, pw = wconv.shape[1], wbd.shape[0]

    def body(ua_ref, wc_ref, wbd_ref, ps_ref, y_ref):
        cb = ua_ref[:, 0:cw]
        uu = ua_ref[:, cw:2 * cw] * ua_ref[:, 2 * cw:3 * cw]
        y_ref[:, 0:cw] = (cb * _conv(uu, wc_ref)).astype(BF16)
        grp, cnt, real = _pool_geometry(lp, pw)
        pooled = _pooled(ua_ref[:, 3 * cw:3 * cw + pw], grp, cnt, real)
        y_ref[:, cw:cw + pw] = (_dot(pooled.astype(BF16), wbd_ref[...]) * ps_ref[...]).astype(BF16)

    return pl.pallas_call(
        body, name="mixer_fwd", grid=(nb,),
        in_specs=[pl.BlockSpec((lp, ua.shape[1]), lambda s: (s, 0)), _full(wconv), _full(wbd), _full(pscale)],
        out_specs=pl.BlockSpec((lp, cw + pw), lambda s: (s, 0)),
        out_shape=jax.ShapeDtypeStruct((t, cw + pw), BF16),
        compiler_params=_params("arbitrary"),
    )(ua, wconv, wbd, pscale)


ATT_NW = 2
ATT_W = ATT_NW * BLK
ATT_PAIRS_FWD = 4
ATT_PAIRS_BWD = 4
HW = 2 * HEAD_DIM
ATT_SCALE = HEAD_DIM ** -0.5


def _attn_lk(z, valid):
    nz = -z
    return jnp.where(valid, jnp.minimum(nz, 0.0) - jnp.log(1.0 + jnp.exp(jnp.minimum(z, nz))), 0.0)


def _tri_sum(x, tri):
    hi = x.astype(BF16)
    lo = (x - hi.astype(F32)).astype(BF16)
    return _dot(hi, tri) + _dot(lo, tri)


def _attn_weights(z, suffix, carry, valid):
    return jnp.where(valid, jnp.exp(z + suffix + carry), 0.0)


def _attn_consts():
    r = lax.broadcasted_iota(jnp.int32, (2 * BLK, ATT_W), 0)
    c = lax.broadcasted_iota(jnp.int32, (2 * BLK, ATT_W), 1)
    cmr = c - (r & (BLK - 1))
    kr = lax.broadcasted_iota(jnp.int32, (ATT_W, ATT_W), 0)
    kc = lax.broadcasted_iota(jnp.int32, (ATT_W, ATT_W), 1)
    lane = lax.broadcasted_iota(jnp.int32, (1, HW), 1)
    heads = (lane < HEAD_DIM, lane >= HEAD_DIM)
    return c, cmr, kr, kc, heads


def _tri(cond):
    return jnp.where(cond, 1.0, 0.0).astype(BF16)


def _rows(i):
    return pl.ds(pl.multiple_of(i * BLK, BLK), BLK)


def _stack_heads(x, heads):
    zero = jnp.zeros_like(x)
    return jnp.concatenate([jnp.where(heads[0], x, zero), jnp.where(heads[1], x, zero)], axis=0)


def _unstack_heads(x, heads):
    return jnp.where(heads[0], x[:BLK], x[BLK:])


def _window(i, s):
    start = jnp.maximum(i + 1 - ATT_NW * (s + 1), 0) * BLK
    return start, (i + 1 - ATT_NW * s) * BLK


def _attn_fwd(q, k, v, nb, gather=None):
    t, aw = q.shape
    lp = t // nb
    nblk = lp // BLK
    pp = min(ATT_PAIRS_FWD, aw // HW)
    cols = [slice(p * HW, (p + 1) * HW) for p in range(pp)]
    ncol = aw // (pp * HW)
    nsteps = nb * ncol

    def body(q_ref, k_ref, v_ref, *rest):
        if gather is None:
            (o_ref,) = rest
        else:
            x_ref, o_ref, g_ref, send_sems, recv_sems, local_sem = rest
            step = pl.program_id(0) * ncol + pl.program_id(1)
            start, forward, finish = _gather_phases(x_ref, g_ref, send_sems, recv_sems, local_sem)
            pl.when(step == 0)(start)
            pl.when(step == nsteps // 2)(forward)
        c, cmr, kr, kc, heads = _attn_consts()
        m_from = _tri(kr >= kc)

        def qblock(i, _):
            qs = [_stack_heads(q_ref[_rows(i), cs] * ATT_SCALE, heads) for cs in cols]

            def window(s, st):
                accs, carries = st
                start, end = _window(i, s)
                keys = pl.ds(pl.multiple_of(start, BLK), ATT_W)
                valid = (cmr < i * BLK - start) & (c < end - start)
                zs = [_dot_nt(qs[p], k_ref[keys, cs]) for p, cs in enumerate(cols)]
                suffix = [_tri_sum(_attn_lk(zs[p], valid), m_from) for p in range(pp)]
                new_accs, new_carries = [], []
                for p, cs in enumerate(cols):
                    a = _attn_weights(zs[p], suffix[p], carries[p], valid)
                    new_accs.append(accs[p] + _dot(a.astype(BF16), v_ref[keys, cs]))
                    new_carries.append(carries[p] + suffix[p][:, 0:1])
                return tuple(new_accs), tuple(new_carries)

            init = (tuple(jnp.zeros((2 * BLK, HW), F32) for _ in cols), tuple(jnp.zeros((2 * BLK, 1), F32) for _ in cols))
            accs, _ = lax.fori_loop(0, (i + ATT_NW) // ATT_NW, window, init)
            for p, cs in enumerate(cols):
                o_ref[_rows(i), cs] = _unstack_heads(accs[p], heads).astype(BF16)
            return 0

        lax.fori_loop(0, nblk, qblock, 0)
        if gather is not None:
            pl.when(step == nsteps - 1)(finish)

    spec = pl.BlockSpec((lp, pp * HW), lambda s, p: (s, p))
    y_shape = jax.ShapeDtypeStruct((t, aw), BF16)
    if gather is None:
        return pl.pallas_call(
            body, name="attn_fwd", grid=(nb, ncol), in_specs=[spec, spec, spec], out_specs=spec, out_shape=y_shape,
            compiler_params=_params("arbitrary", "arbitrary"),
        )(q, k, v)
    return pl.pallas_call(
        body, name="attn_fwd_gather", grid=(nb, ncol), in_specs=[spec, spec, spec, ANY], out_specs=[spec, ANY],
        out_shape=[y_shape, jax.ShapeDtypeStruct((N_DEV,) + gather.shape, gather.dtype)], scratch_shapes=COMM_SEMS,
        compiler_params=_params("arbitrary", "arbitrary"),
    )(q, k, v, gather)


def _outproj(h, ya, yb, wa, wb):
    t, d = h.shape
    tm = _row_tile(t)

    def body(h_ref, ya_ref, yb_ref, wa_ref, wb_ref, o_ref):
        o_ref[...] = h_ref[...] + _dot(ya_ref[...], wa_ref[...]) + _dot(yb_ref[...], wb_ref[...])

    row = lambda w: pl.BlockSpec((tm, w), lambda i: (i, 0))
    return pl.pallas_call(
        body, name="outproj", grid=(t // tm,),
        in_specs=[row(d), row(ya.shape[1]), row(yb.shape[1]), _full(wa), _full(wb)],
        out_specs=row(d), out_shape=jax.ShapeDtypeStruct((t, d), F32),
        compiler_params=_params("arbitrary"),
    )(h, ya, yb, wa, wb)


def _mlp_weight_specs(wup, wdown, layer):
    nff, _, d, fc = wup.shape
    return nff, fc, [pl.BlockSpec((None, None, d, fc), lambda i, kk: (kk, layer, 0, 0)),
                     pl.BlockSpec((None, None, fc, d), lambda i, kk: (kk, layer, 0, 0))]


def _mlp_fwd(h, g, wup, wdown, layer):
    t, d = h.shape
    tm = _row_tile(t, ROW_TILE_MLP)
    nff, fc, wspecs = _mlp_weight_specs(wup, wdown, layer)

    def body(h_ref, g_ref, wu_ref, wd_ref, o_ref, hn_ref, acc):
        kk = pl.program_id(1)

        @pl.when(kk == 0)
        def _():
            x = h_ref[...]
            hn_ref[...] = _rms_fwd(x, g_ref[...]).astype(BF16)
            acc[...] = x

        m = _dot(hn_ref[...], wu_ref[...])
        a = jnp.square(jnp.maximum(m, 0.0)).astype(BF16)
        acc[...] += _dot(a, wd_ref[...])

        @pl.when(kk == nff - 1)
        def _():
            o_ref[...] = acc[...]

    row = pl.BlockSpec((tm, d), lambda i, kk: (i, 0))
    return pl.pallas_call(
        body, name="mlp_fwd", grid=(t // tm, nff),
        in_specs=[row, _full(g)] + wspecs,
        out_specs=[row, row],
        out_shape=[jax.ShapeDtypeStruct((t, d), F32), jax.ShapeDtypeStruct((t, d), BF16)],
        scratch_shapes=[pltpu.VMEM((tm, d), F32)],
        compiler_params=_params("arbitrary", "arbitrary"),
    )(h, g, wup, wdown)


def _loss_head(h, g, target, nb):
    t, d = h.shape
    nblk = t // nb // BLK
    nx = nblk - 1

    def body(h_ref, g_ref, t_ref, loss_ref, dh_ref, dg_ref):
        s, i = pl.program_id(0), pl.program_id(1)

        @pl.when((s == 0) & (i == 0))
        def _():
            loss_ref[...] = jnp.zeros_like(loss_ref)
            dg_ref[...] = jnp.zeros_like(dg_ref)

        @pl.when(i == 0)
        def _():
            dh_ref[...] = jnp.zeros_like(dh_ref)

        @pl.when(i > 0)
        def _():
            x, gg = h_ref[...], g_ref[...]
            err = _rms_fwd(x, gg) - t_ref[...]
            loss_ref[...] += jnp.sum(err * err) * (0.5 / d)
            dx, dg = _rms_bwd(err * (1.0 / d), x, gg)
            dh_ref[...] = dx
            dg_ref[...] += dg

    hspec = pl.BlockSpec((BLK, d), lambda s, i: (s * nblk + i, 0))
    return pl.pallas_call(
        body, name="loss_head", grid=(nb, nblk),
        in_specs=[hspec, _full(g), pl.BlockSpec((BLK, d), lambda s, i: (s * nx + jnp.maximum(i - 1, 0), 0))],
        out_specs=[pl.BlockSpec((1, BLK), lambda s, i: (0, 0)), hspec, pl.BlockSpec((1, d), lambda s, i: (0, 0))],
        out_shape=[jax.ShapeDtypeStruct((1, BLK), F32), jax.ShapeDtypeStruct((t, d), F32), jax.ShapeDtypeStruct((1, d), F32)],
        compiler_params=_params("arbitrary", "arbitrary"),
    )(h, g, target)


def _mlp_bwd(dh, h, hn, g, wup, wdown, layer):
    t, d = h.shape
    tm = _row_tile(t, ROW_TILE_MLP)
    nff, fc, wspecs = _mlp_weight_specs(wup, wdown, layer)

    def body(dh_ref, h_ref, hn_ref, g_ref, wu_ref, wd_ref, dhm_ref, a_ref, dm_ref, dg_ref, dhn, dhb):
        i, kk = pl.program_id(0), pl.program_id(1)

        @pl.when((i == 0) & (kk == 0))
        def _():
            dg_ref[...] = jnp.zeros_like(dg_ref)

        @pl.when(kk == 0)
        def _():
            dhb[...] = dh_ref[...].astype(BF16)
            dhn[...] = jnp.zeros_like(dhn)

        r = jnp.maximum(_dot(hn_ref[...], wu_ref[...]), 0.0)
        a_ref[...] = (r * r).astype(BF16)
        dm = (_dot_nt(dhb[...], wd_ref[...]) * (2.0 * r)).astype(BF16)
        dm_ref[...] = dm
        dhn[...] += _dot_nt(dm, wu_ref[...])

        @pl.when(kk == nff - 1)
        def _():
            dx, dg = _rms_bwd(dhn[...], h_ref[...], g_ref[...])
            dhm_ref[...] = dh_ref[...] + dx
            dg_ref[...] += dg

    row = pl.BlockSpec((tm, d), lambda i, kk: (i, 0))
    ff = pl.BlockSpec((tm, fc), lambda i, kk: (i, kk))
    return pl.pallas_call(
        body, name="mlp_bwd", grid=(t // tm, nff),
        in_specs=[row, row, row, _full(g)] + wspecs,
        out_specs=[row, ff, ff, pl.BlockSpec((1, d), lambda i, kk: (0, 0))],
        out_shape=[jax.ShapeDtypeStruct((t, d), F32), jax.ShapeDtypeStruct((t, nff * fc), BF16),
                   jax.ShapeDtypeStruct((t, nff * fc), BF16), jax.ShapeDtypeStruct((1, d), F32)],
        scratch_shapes=[pltpu.VMEM((tm, d), F32), pltpu.VMEM((tm, d), BF16)],
        compiler_params=_params("arbitrary", "arbitrary"),
    )(dh, h, hn, g, wup, wdown)


def _matmul_tn(x, y, name):
    t, k1 = x.shape
    n = y.shape[1]
    tt = _row_tile(t)
    tn = min(n, max(512, TN_ACC_BYTES // (4 * k1)))
    steps = t // tt

    def body(x_ref, y_ref, o_ref, acc):
        @pl.when(pl.program_id(1) == 0)
        def _():
            acc[...] = jnp.zeros_like(acc)

        acc[...] += _dot_tn(x_ref[...].astype(BF16), y_ref[...].astype(BF16))

        @pl.when(pl.program_id(1) == steps - 1)
        def _():
            o_ref[...] = acc[...].astype(BF16)

    return pl.pallas_call(
        body, name=name, grid=(n // tn, steps),
        in_specs=[pl.BlockSpec((tt, k1), lambda j, r: (r, 0)), pl.BlockSpec((tt, tn), lambda j, r: (r, j))],
        out_specs=pl.BlockSpec((k1, tn), lambda j, r: (0, j)),
        out_shape=jax.ShapeDtypeStruct((k1, n), BF16),
        scratch_shapes=[pltpu.VMEM((k1, tn), F32)],
        compiler_params=_params("arbitrary", "arbitrary"),
    )(x, y)


def _outproj_bwd(dh, wa, wb):
    t, d = dh.shape
    tm = _row_tile(t)
    na, nbw = wa.shape[0], wb.shape[0]

    def body(dh_ref, wa_ref, wb_ref, da_ref, db_ref):
        x = dh_ref[...].astype(BF16)
        da_ref[...] = _dot_nt(x, wa_ref[...])
        db_ref[...] = _dot_nt(x, wb_ref[...]).astype(BF16)

    row = lambda w: pl.BlockSpec((tm, w), lambda i: (i, 0))
    return pl.pallas_call(
        body, name="outproj_bwd", grid=(t // tm,),
        in_specs=[row(d), _full(wa), _full(wb)], out_specs=[row(na), row(nbw)],
        out_shape=[jax.ShapeDtypeStruct((t, na), F32), jax.ShapeDtypeStruct((t, nbw), BF16)],
        compiler_params=_params("arbitrary"),
    )(dh, wa, wb)


def _attn_bwd(q, k, v, do, nb, exchange=None):
    t, aw = q.shape
    lp = t // nb
    nblk = lp // BLK
    pp = min(ATT_PAIRS_BWD, aw // HW)
    cols = [slice(p * HW, (p + 1) * HW) for p in range(pp)]
    ncol = aw // (pp * HW)
    nsteps = nb * ncol
    max_windows = (nblk + ATT_NW - 1) // ATT_NW

    def body(q_ref, k_ref, v_ref, do_ref, *rest):
        if exchange is None:
            dq_ref, dk_ref, dv_ref, dk_acc, dv_acc, g_s, sig_s = rest
        else:
            x_ref, dq_ref, dk_ref, dv_ref, e_ref, dk_acc, dv_acc, g_s, sig_s, send_sems, recv_sems, local_sem = rest
            step = pl.program_id(0) * ncol + pl.program_id(1)
            start, finish = _exchange_phases(x_ref, e_ref, send_sems, recv_sems, local_sem)
            pl.when(step == 0)(start)
        c, cmr, kr, kc, heads = _attn_consts()
        m_from = _tri(kr >= kc)
        m_before = _tri(kr < kc)
        dk_acc[...] = jnp.zeros_like(dk_acc)
        dv_acc[...] = jnp.zeros_like(dv_acc)

        def qblock(i, _):
            qs = [_stack_heads(q_ref[_rows(i), cs] * ATT_SCALE, heads) for cs in cols]
            dos = [_stack_heads(do_ref[_rows(i), cs], heads) for cs in cols]
            nwin = (i + ATT_NW) // ATT_NW

            def geometry(s):
                start, end = _window(i, s)
                return (cmr < i * BLK - start) & (c < end - start), pl.ds(pl.multiple_of(start, BLK), ATT_W)

            def down(s, carries):
                valid, keys = geometry(s)
                zs = [_dot_nt(qs[p], k_ref[keys, cs]) for p, cs in enumerate(cols)]
                das = [_dot_nt(dos[p], v_ref[keys, cs]) for p, cs in enumerate(cols)]
                suffix = []
                for p in range(pp):
                    lk = _attn_lk(zs[p], valid)
                    sig_s[p, s] = jnp.exp(zs[p] + lk)
                    suffix.append(_tri_sum(lk, m_from))
                new_carries = []
                for p, cs in enumerate(cols):
                    a = _attn_weights(zs[p], suffix[p], carries[p], valid)
                    g_s[p, s] = das[p] * a
                    dv_acc[keys, cs] += _dot_tn(a.astype(BF16), dos[p])
                    new_carries.append(carries[p] + suffix[p][:, 0:1])
                return tuple(new_carries)

            lax.fori_loop(0, nwin, down, tuple(jnp.zeros((2 * BLK, 1), F32) for _ in cols))

            def up(ss, st):
                accs, carries = st
                s = nwin - 1 - ss
                valid, keys = geometry(s)
                gs = [g_s[p, s] for p in range(pp)]
                before = [_tri_sum(gs[p], m_before) for p in range(pp)]
                new_accs, new_carries = [], []
                for p, cs in enumerate(cols):
                    g, sig = gs[p], sig_s[p, s]
                    dz = jnp.where(valid, g * (1.0 - sig) - (before[p] + carries[p]) * sig, 0.0).astype(BF16)
                    new_accs.append(accs[p] + _dot(dz, k_ref[keys, cs]))
                    dk_acc[keys, cs] += _dot_tn(dz, qs[p])
                    new_carries.append(carries[p] + jnp.sum(g, axis=1, keepdims=True))
                return tuple(new_accs), tuple(new_carries)

            init = (tuple(jnp.zeros((2 * BLK, HW), F32) for _ in cols), tuple(jnp.zeros((2 * BLK, 1), F32) for _ in cols))
            accs, _ = lax.fori_loop(0, nwin, up, init)
            for p, cs in enumerate(cols):
                dq_ref[_rows(i), cs] = (_unstack_heads(accs[p], heads) * ATT_SCALE).astype(BF16)
            return 0

        lax.fori_loop(0, nblk, qblock, 0)
        dk_ref[...] = dk_acc[...].astype(BF16)
        dv_ref[...] = dv_acc[...].astype(BF16)
        if exchange is not None:
            pl.when(step == nsteps - 1)(finish)

    spec = pl.BlockSpec((lp, pp * HW), lambda s, p: (s, p), pipeline_mode=pl.Buffered(1))
    d_shape = jax.ShapeDtypeStruct((t, aw), BF16)
    scratch = [pltpu.VMEM((lp, pp * HW), F32), pltpu.VMEM((lp, pp * HW), F32),
               pltpu.VMEM((pp, max_windows, 2 * BLK, ATT_W), F32), pltpu.VMEM((pp, max_windows, 2 * BLK, ATT_W), F32)]
    if exchange is None:
        return pl.pallas_call(
            body, name="attn_bwd", grid=(nb, ncol), in_specs=[spec] * 4, out_specs=[spec] * 3, out_shape=[d_shape] * 3,
            scratch_shapes=scratch, compiler_params=_params("arbitrary", "arbitrary"),
        )(q, k, v, do)
    return pl.pallas_call(
        body, name="attn_bwd_exchange", grid=(nb, ncol), in_specs=[spec] * 4 + [ANY], out_specs=[spec] * 3 + [ANY],
        out_shape=[d_shape] * 3 + [jax.ShapeDtypeStruct(exchange.shape, exchange.dtype)],
        scratch_shapes=scratch + COMM_SEMS, compiler_params=_params("arbitrary", "arbitrary"),
    )(q, k, v, do, exchange)


def _mixer_bwd(ua, dy, wconv, wbd, pscale, nb):
    t = ua.shape[0]
    lp = t // nb
    cw, pw = wconv.shape[1], wbd.shape[0]

    def body(ua_ref, dy_ref, wc_ref, wbd_ref, ps_ref, du_ref, dwc_ref, dps_ref, dwb_ref):
        @pl.when(pl.program_id(0) == 0)
        def _():
            dwc_ref[...] = jnp.zeros_like(dwc_ref)
            dps_ref[...] = jnp.zeros_like(dps_ref)
            dwb_ref[...] = jnp.zeros_like(dwb_ref)

        up = lambda x, n: pltpu.roll(x, lp - n, 0)
        cb, cc, cx = ua_ref[:, 0:cw], ua_ref[:, cw:2 * cw], ua_ref[:, 2 * cw:3 * cw]
        uu = cc * cx
        dyc = dy_ref[:, 0:cw]
        du_ref[:, 0:cw] = (dyc * _conv(uu, wc_ref)).astype(BF16)
        dcv = dyc * cb
        duu = wc_ref[2:3, :] * dcv + wc_ref[1:2, :] * up(dcv, 1) + wc_ref[0:1, :] * up(dcv, 2)
        du_ref[:, cw:2 * cw] = (duu * cx).astype(BF16)
        du_ref[:, 2 * cw:3 * cw] = (duu * cc).astype(BF16)
        dwc_ref[0:1, :] += jnp.sum(dcv * pltpu.roll(uu, 2, 0), axis=0, keepdims=True)
        dwc_ref[1:2, :] += jnp.sum(dcv * pltpu.roll(uu, 1, 0), axis=0, keepdims=True)
        dwc_ref[2:3, :] += jnp.sum(dcv * uu, axis=0, keepdims=True)
        grp, cnt, real = _pool_geometry(lp, pw)
        p = ua_ref[:, 3 * cw:3 * cw + pw]
        pooled = _pooled(p, grp, cnt, real).astype(BF16)
        dyp = dy_ref[:, cw:cw + pw]
        dps_ref[...] += jnp.sum(dyp * _dot(pooled, wbd_ref[...]), axis=0, keepdims=True)
        dpre = (dyp * ps_ref[...]).astype(BF16)
        dwb_ref[...] += _dot_tn(pooled, dpre)
        dpooled = jnp.where(real, _dot_nt(dpre, wbd_ref[...]), 0.0)
        xm = dpooled / cnt
        l2 = xm + up(xm, 1)
        l4 = l2 + up(l2, 2)
        l8 = l4 + up(l4, 4)
        l16 = l8 + up(l8, 8)
        du_ref[:, 3 * cw:3 * cw + pw] = jnp.where(real, _by_group(grp, (l2, l4, l8, l16)) - dpooled, 0.0).astype(BF16)

    seq = lambda w: pl.BlockSpec((lp, w), lambda s: (s, 0))
    return pl.pallas_call(
        body, name="mixer_bwd", grid=(nb,),
        in_specs=[seq(ua.shape[1]), seq(cw + pw), _full(wconv), _full(wbd), _full(pscale)],
        out_specs=[seq(ua.shape[1]), pl.BlockSpec((3, cw), lambda s: (0, 0)), pl.BlockSpec((1, pw), lambda s: (0, 0)),
                   pl.BlockSpec((pw, pw), lambda s: (0, 0))],
        out_shape=[jax.ShapeDtypeStruct(ua.shape, BF16), jax.ShapeDtypeStruct((3, cw), F32),
                   jax.ShapeDtypeStruct((1, pw), F32), jax.ShapeDtypeStruct((pw, pw), F32)],
        compiler_params=_params("arbitrary"),
    )(ua, dy, wconv, wbd, pscale)


def _inproj_bwd(dh, h, g, dua, dq, dk, dv, wa, wq, wk, wv):
    t, d = h.shape
    tm = _row_tile(t)

    def body(dh_ref, h_ref, g_ref, dua_ref, dq_ref, dk_ref, dv_ref, wa_ref, wq_ref, wk_ref, wv_ref, o_ref, dg_ref):
        @pl.when(pl.program_id(0) == 0)
        def _():
            dg_ref[...] = jnp.zeros_like(dg_ref)

        dhn = (_dot_nt(dua_ref[...], wa_ref[...]) + _dot_nt(dq_ref[...], wq_ref[...])
               + _dot_nt(dk_ref[...], wk_ref[...]) + _dot_nt(dv_ref[...], wv_ref[...]))
        dx, dg = _rms_bwd(dhn, h_ref[...], g_ref[...])
        o_ref[...] = dh_ref[...] + dx
        dg_ref[...] += dg

    row = lambda w: pl.BlockSpec((tm, w), lambda i: (i, 0))
    return pl.pallas_call(
        body, name="inproj_bwd", grid=(t // tm,),
        in_specs=[row(d), row(d), _full(g), row(dua.shape[1]), row(dq.shape[1]), row(dk.shape[1]), row(dv.shape[1]),
                  _full(wa), _full(wq), _full(wk), _full(wv)],
        out_specs=[row(d), pl.BlockSpec((1, d), lambda i: (0, 0))],
        out_shape=[jax.ShapeDtypeStruct((t, d), F32), jax.ShapeDtypeStruct((1, d), F32)],
        compiler_params=_params("arbitrary"),
    )(dh, h, g, dua, dq, dk, dv, wa, wq, wk, wv)


def _part_tile(r):
    if r <= 512:
        return r
    for tr in (256, 128, 64, 32, 16, 8):
        if r % tr == 0:
            return tr
    return r


def _sum_parts(x, name):
    n, r, c = x.shape
    tr = _part_tile(r)

    def body(x_ref, o_ref):
        acc = x_ref[0]
        for p in range(1, n):
            acc = acc + x_ref[p]
        o_ref[...] = acc

    return pl.pallas_call(
        body, name=name, grid=(r // tr,),
        in_specs=[pl.BlockSpec((n, tr, c), lambda i: (0, i, 0))], out_specs=pl.BlockSpec((tr, c), lambda i: (i, 0)),
        out_shape=jax.ShapeDtypeStruct((r, c), F32), compiler_params=_params("arbitrary"),
    )(x)


def _adamw(w, gparts, m, v, name, exchange=None):
    n, r, c = gparts.shape
    tr = _part_tile(r)
    nsteps = r // tr

    def body(w_ref, g_ref, m_ref, v_ref, *rest):
        if exchange is None:
            go_ref, d_ref, mo_ref, vo_ref = rest
        else:
            x_ref, go_ref, d_ref, mo_ref, vo_ref, e_ref, send_sems, recv_sems, local_sem = rest
            start, finish = _exchange_phases(x_ref, e_ref, send_sems, recv_sems, local_sem)
            pl.when(pl.program_id(0) == 0)(start)
        g = g_ref[0].astype(F32)
        for p in range(1, n):
            g = g + g_ref[p].astype(F32)
        go_ref[...] = g
        mm = ADAM_B1 * m_ref[...] + (1.0 - ADAM_B1) * g
        vv = ADAM_B2 * v_ref[...] + (1.0 - ADAM_B2) * jnp.square(g)
        mo_ref[...] = mm
        vo_ref[...] = vv
        m_hat = mm / (1.0 - ADAM_B1 ** ADAM_STEP)
        v_hat = vv / (1.0 - ADAM_B2 ** ADAM_STEP)
        d_ref[...] = -ADAM_LR * (m_hat / (jnp.sqrt(v_hat) + ADAM_EPS) + ADAM_WD * w_ref[...])
        if exchange is not None:
            pl.when(pl.program_id(0) == nsteps - 1)(finish)

    row = pl.BlockSpec((tr, c), lambda i: (i, 0))
    in_specs = [row, pl.BlockSpec((n, tr, c), lambda i: (0, i, 0)), row, row]
    out_shape = [jax.ShapeDtypeStruct((r, c), F32)] * 4
    if exchange is None:
        return pl.pallas_call(
            body, name=name, grid=(nsteps,), in_specs=in_specs, out_specs=[row] * 4, out_shape=out_shape,
            compiler_params=_params("arbitrary"),
        )(w, gparts, m, v)
    return pl.pallas_call(
        body, name=name, grid=(nsteps,), in_specs=in_specs + [ANY], out_specs=[row] * 4 + [ANY],
        out_shape=out_shape + [jax.ShapeDtypeStruct(exchange.shape, exchange.dtype)], scratch_shapes=COMM_SEMS,
        compiler_params=_params("arbitrary"),
    )(w, gparts, m, v, exchange)


MESH = pl.DeviceIdType.MESH
ANY = pl.BlockSpec(memory_space=pl.ANY)


def _all_gather(x, name):
    r, c = x.shape

    def body(x_ref, out_ref, send_sems, recv_sems, local_sem):
        start, forward, finish = _gather_phases(x_ref, out_ref, send_sems, recv_sems, local_sem)
        start()
        forward()
        finish()

    return pl.pallas_call(
        body, name=name, in_specs=[ANY], out_specs=ANY,
        out_shape=jax.ShapeDtypeStruct((N_DEV, r, c), x.dtype), scratch_shapes=COMM_SEMS,
    )(x)


COMM_SEMS = [pltpu.SemaphoreType.DMA((7,)), pltpu.SemaphoreType.DMA((7,)), pltpu.SemaphoreType.DMA(())]


def _gather_phases(x_ref, out_ref, send_sems, recv_sems, local_sem):
    mx, my, mc = lax.axis_index("x"), lax.axis_index("y"), lax.axis_index("c")
    me, sibling = (mx, my, mc), (mx, my, 1 - mc)
    chips = [(1 - mx, my), (mx, 1 - my), (1 - mx, 1 - my)]

    def slot(px, py, pc):
        return out_ref.at[4 * px + 2 * py + pc]

    def copy(kk, block, to, src=None):
        return pltpu.make_async_remote_copy(
            src_ref=slot(*block) if src is None else src, dst_ref=slot(*block),
            send_sem=send_sems.at[kk], recv_sem=recv_sems.at[kk], device_id=to, device_id_type=MESH)

    mine = pltpu.make_async_copy(x_ref, slot(*me), local_sem)
    first = [copy(0, me, sibling, src=x_ref)] + [copy(1 + j, me, (*chip, mc), src=x_ref) for j, chip in enumerate(chips)]
    passed = [copy(4 + j, (*chip, mc), sibling) for j, chip in enumerate(chips)]

    def start():
        mine.start()
        for cp in first:
            cp.start()

    def forward():
        for j, chip in enumerate(chips):
            copy(1 + j, (*chip, mc), me).wait_recv()
            passed[j].start()

    def finish():
        copy(0, sibling, me).wait_recv()
        for j, chip in enumerate(chips):
            copy(4 + j, (*chip, 1 - mc), me).wait_recv()
        for cp in first + passed:
            cp.wait_send()
        mine.wait()

    return start, forward, finish


def _exchange_phases(x_ref, out_ref, send_sems, recv_sems, local_sem):
    mx, my, mc = lax.axis_index("x"), lax.axis_index("y"), lax.axis_index("c")
    me = 4 * mx + 2 * my + mc
    mine = pltpu.make_async_copy(x_ref.at[me], out_ref.at[me], local_sem)
    copies = []
    for kk in range(1, N_DEV):
        px, py, pc = mx ^ (kk >> 2), my ^ ((kk >> 1) & 1), mc ^ (kk & 1)
        peer = 4 * px + 2 * py + pc
        copies.append((peer, pltpu.make_async_remote_copy(
            src_ref=x_ref.at[peer], dst_ref=out_ref.at[me], send_sem=send_sems.at[kk - 1],
            recv_sem=recv_sems.at[kk - 1], device_id=(px, py, pc), device_id_type=MESH)))

    def start():
        mine.start()
        for _, cp in copies:
            cp.start()

    def finish():
        for kk, (peer, _) in enumerate(copies):
            pltpu.make_async_remote_copy(
                src_ref=x_ref.at[peer], dst_ref=out_ref.at[peer], send_sem=send_sems.at[kk],
                recv_sem=recv_sems.at[kk], device_id=(mx, my, mc), device_id_type=MESH).wait_recv()
        for _, cp in copies:
            cp.wait_send()
        mine.wait()

    return start, finish


def _all_to_all(x, name):
    def body(x_ref, out_ref, send_sems, recv_sems, local_sem):
        start, finish = _exchange_phases(x_ref, out_ref, send_sems, recv_sems, local_sem)
        start()
        finish()

    return pl.pallas_call(
        body, name=name, in_specs=[ANY], out_specs=ANY,
        out_shape=jax.ShapeDtypeStruct(x.shape, x.dtype), scratch_shapes=COMM_SEMS,
    )(x)


def _rows_of(a):
    return a.reshape(-1, PACK_W)


def _unpack_big(buf, shapes):
    out, r0 = [], 0
    for shp in shapes:
        n = 1
        for s in shp:
            n *= s
        out.append(buf[r0:r0 + n // PACK_W].reshape(shp))
        r0 += n // PACK_W
    return out


def _shards_last(a):
    lead = a.shape[:-1]
    return jnp.moveaxis(a.reshape(*lead, N_DEV, a.shape[-1] // N_DEV), -2, 0)


def _shards_rows(a):
    l, n, c = a.shape
    return jnp.moveaxis(a.reshape(l, N_DEV, n // N_DEV, c), 1, 0)


def _flat_pad(parts, mult):
    flat = jnp.concatenate([p.reshape(-1) for p in parts])
    return jnp.pad(flat, (0, (-flat.shape[0]) % mult))


def _unflatten(flat, shapes):
    out, o = [], 0
    for shp in shapes:
        n = 1
        for s in shp:
            n *= s
        out.append(flat[o:o + n].reshape(shp))
        o += n
    return out


def kernel(x, meta_tokens, g_mix, w_in, w_conv, w_pool, pool_scale, w_out, g_mlp, w_up, w_down, g_final, loss_target, m_meta_tokens, m_g_mix, m_w_in, m_w_conv, m_w_pool, m_pool_scale, m_w_out, m_g_mlp, m_w_up, m_w_down, m_g_final, v_meta_tokens, v_g_mix, v_w_in, v_w_conv, v_w_pool, v_pool_scale, v_w_out, v_g_mlp, v_w_up, v_w_down, v_g_final):
    nb, seq, d = x.shape
    depth = g_mix.shape[0]
    lp = BLK + seq
    t = nb * lp
    cw = w_conv.shape[2] * N_DEV
    pw = pool_scale.shape[1]
    pg = pw // N_POOL_GROUPS
    aw = (w_in.shape[2] * N_DEV - 3 * cw - pw) // 3
    ua_w = 3 * cw + pw
    d_ff = w_up.shape[2] * N_DEV
    me = 4 * lax.axis_index("x") + 2 * lax.axis_index("y") + lax.axis_index("c")
    first_shapes = [w_in[:1].shape]
    rest_shapes = [w_in[1:].shape, w_out.shape, w_up.shape, w_down.shape]
    rest_rows = [a.size // PACK_W for a in (w_in[1:], w_out, w_up, w_down)]

    def pack_first(a_in):
        return _rows_of(a_in[:1])

    def pack_rest(a_in, a_out, a_up, a_down):
        return jnp.concatenate([_rows_of(a_in[1:]), _rows_of(a_out), _rows_of(a_up), _rows_of(a_down)], axis=0)

    g_first = _all_gather(pack_first(w_in).astype(BF16), "gather_w_in0")
    w_in0_f = jnp.moveaxis(g_first.reshape((N_DEV,) + first_shapes[0]), 0, 2).reshape(1, d, -1)
    small_in = _all_gather(_flat_pad([meta_tokens, w_conv], 8 * BLK).reshape(-1, BLK), "gather_small_weights").reshape(N_DEV, -1)
    meta_f = jnp.moveaxis(small_in[:, :meta_tokens.size].reshape(N_DEV, N_META, d // N_DEV), 0, 1).reshape(N_META, d)
    w_conv_f = small_in[:, meta_tokens.size:meta_tokens.size + w_conv.size].reshape((N_DEV,) + w_conv.shape)
    w_conv_f = jnp.moveaxis(w_conv_f, 0, 2).reshape(depth, w_conv.shape[1], cw)

    def split_w_in(w):
        return w[:, :ua_w], w[:, ua_w:ua_w + aw], w[:, ua_w + aw:ua_w + 2 * aw], w[:, ua_w + 2 * aw:]

    wbd = jnp.zeros((depth, pw, pw), F32)
    for gi in range(N_POOL_GROUPS):
        wbd = wbd.at[:, gi * pg:(gi + 1) * pg, gi * pg:(gi + 1) * pg].set(w_pool[:, gi])
    wbd = wbd.astype(BF16)

    h = jnp.concatenate([jnp.zeros((nb, PAD, d), F32), jnp.broadcast_to(meta_f[None], (nb, N_META, d)), x], axis=1).reshape(t, d)
    saved, w_in_parts = [], [None] * depth
    w_in_parts[0] = split_w_in(w_in0_f[0])
    for l in range(depth):
        wa, wq, wk, wv = w_in_parts[l]
        ua, q, k, v, hn1 = _inproj_fwd(h, g_mix[l][None], wa, wq, wk, wv)
        ycp = _mixer_fwd(ua, w_conv_f[l], wbd[l], pool_scale[l][None], nb)
        if l == 0:
            yat, g_rest = _attn_fwd(q, k, v, nb, gather=pack_rest(w_in, w_out, w_up, w_down).astype(BF16))
            r_in, r_out, r_up = [r_ + sum(rest_rows[:j]) for j, r_ in enumerate(rest_rows[:3])]
            w_in_rest = g_rest[:, :r_in].reshape((N_DEV,) + rest_shapes[0])
            w_in_rest = jnp.moveaxis(w_in_rest, 0, 2).reshape(depth - 1, d, -1)
            for ll in range(1, depth):
                w_in_parts[ll] = split_w_in(w_in_rest[ll - 1])
            w_out_f = jnp.moveaxis(g_rest[:, r_in:r_out].reshape((N_DEV,) + rest_shapes[1]), 0, 1).reshape(depth, -1, d)
            wo_a, wo_b = w_out_f[:, :cw + pw], w_out_f[:, cw + pw:]
            w_up_f = g_rest[:, r_out:r_up].reshape((N_DEV,) + rest_shapes[2])
            w_down_f = g_rest[:, r_up:].reshape((N_DEV,) + rest_shapes[3])
        else:
            yat = _attn_fwd(q, k, v, nb)
        h_mid =_outproj(h, ycp, yat, wo_a[l], wo_b[l])
        h_next, hn2 = _mlp_fwd(h_mid, g_mlp[l][None], w_up_f, w_down_f, l)
        saved.append((h, ua, q, k, v, hn1, ycp, yat, h_mid, hn2))
        h = h_next

    loss_part, dh, dg_final = _loss_head(h, g_final[None], loss_target.reshape(nb * seq, d), nb)

    g_gmix, g_gmlp, g_wconv, g_pscale, g_wpool = [None] * depth, [None] * depth, [None] * depth, [None] * depth, [None] * depth
    g_win, g_wout, g_wup, g_wdown = [None] * depth, [None] * depth, [None] * depth, [None] * depth
    for l in reversed(range(depth)):
        h_in, ua, q, k, v, hn1, ycp, yat, h_mid, hn2 = saved[l]
        wa, wq, wk, wv = w_in_parts[l]
        dh_mid, act, dm, g_gmlp[l] = _mlp_bwd(dh, h_mid, hn2, g_mlp[l][None], w_up_f, w_down_f, l)
        g_wup[l] = _matmul_tn(hn2, dm, "grad_w_up")
        g_wdown[l] = _matmul_tn(act, dh, "grad_w_down")
        dycp, do = _outproj_bwd(dh_mid, wo_a[l], wo_b[l])
        g_wout[l] = jnp.concatenate([_matmul_tn(ycp, dh_mid, "grad_w_out_a"), _matmul_tn(yat, dh_mid, "grad_w_out_b")], axis=0)
        if l == 0:
            send_rest = jnp.concatenate([
                _shards_last(jnp.stack(g_win[1:])).reshape(N_DEV, -1, PACK_W), _shards_rows(jnp.stack(g_wout)).reshape(N_DEV, -1, PACK_W),
                _shards_last(jnp.stack(g_wup)).reshape(N_DEV, -1, PACK_W), _shards_rows(jnp.stack(g_wdown)).reshape(N_DEV, -1, PACK_W)], axis=1)
            dq, dk, dv, parts_rest = _attn_bwd(q, k, v, do, nb, exchange=send_rest)
        else:
            dq, dk, dv = _attn_bwd(q, k, v, do, nb)
        dua, g_wconv[l], g_pscale[l], dwb = _mixer_bwd(ua, dycp, w_conv_f[l], wbd[l], pool_scale[l][None], nb)
        g_wpool[l] = jnp.stack([dwb[gi * pg:(gi + 1) * pg, gi * pg:(gi + 1) * pg] for gi in range(N_POOL_GROUPS)])
        dh, g_gmix[l] = _inproj_bwd(dh_mid, h_in, g_mix[l][None], dua, dq, dk, dv, wa, wq, wk, wv)
        g_win[l] = jnp.concatenate([_matmul_tn(hn1, dua, "grad_w_in_a"), _matmul_tn(hn1, dq, "grad_w_in_q"),
                                    _matmul_tn(hn1, dk, "grad_w_in_k"), _matmul_tn(hn1, dv, "grad_w_in_v")], axis=1)
    dh3 = dh.reshape(nb, lp, d)
    grad_x = dh3[:, BLK:]
    g_meta = _sum_parts(dh3[:, PAD:BLK], "sum_meta_grad")

    *rest, parts_first = _adamw(pack_rest(w_in, w_out, w_up, w_down), parts_rest, pack_rest(m_w_in, m_w_out, m_w_up, m_w_down),
                                pack_rest(v_w_in, v_w_out, v_w_up, v_w_down), "adamw_rest_exchange",
                                exchange=_shards_last(g_win[0][None]).reshape(N_DEV, -1, PACK_W))
    first = _adamw(pack_first(w_in), parts_first, pack_first(m_w_in), pack_first(v_w_in), "adamw_w_in0")
    rest = [_unpack_big(b, rest_shapes) for b in rest]
    first = [_unpack_big(b, first_shapes) for b in first]
    (gr_win, gr_wout, gr_wup, gr_wdown), (de_win, de_wout, de_wup, de_wdown), (nm_win, nm_wout, nm_wup, nm_wdown), \
        (nv_win, nv_wout, nv_wup, nv_wdown) = [[jnp.concatenate([f[0], r[0]], axis=0), r[1], r[2], r[3]] for f, r in zip(first, rest)]

    small_full = [g_meta, jnp.concatenate(g_gmix), jnp.stack(g_wconv), jnp.stack(g_wpool), jnp.concatenate(g_pscale),
                  jnp.concatenate(g_gmlp), dg_final.reshape(-1)]
    small_shapes = [a.shape for a in small_full]
    small_sum = _sum_parts(_all_gather(_flat_pad(small_full, 8 * BLK).reshape(-1, BLK), "gather_small_grads"), "sum_small_grads")
    gr_meta, gr_gmix, gr_wconv, gr_wpool, gr_pscale, gr_gmlp, gr_gfinal = _unflatten(small_sum.reshape(-1), small_shapes)
    gr_meta = lax.dynamic_slice_in_dim(gr_meta, me * (d // N_DEV), d // N_DEV, axis=1)
    gr_wconv = lax.dynamic_slice_in_dim(gr_wconv, me * (cw // N_DEV), cw // N_DEV, axis=2)
    small_g = [gr_meta, gr_gmix, gr_wconv, gr_wpool, gr_pscale, gr_gmlp, gr_gfinal]
    local_shapes = [a.shape for a in small_g]
    pack_small = lambda parts_: _flat_pad(parts_, 8 * BLK).reshape(-1, BLK)
    small = _adamw(pack_small([meta_tokens, g_mix, w_conv, w_pool, pool_scale, g_mlp, g_final]), pack_small(small_g)[None],
                   pack_small([m_meta_tokens, m_g_mix, m_w_conv, m_w_pool, m_pool_scale, m_g_mlp, m_g_final]),
                   pack_small([v_meta_tokens, v_g_mix, v_w_conv, v_w_pool, v_pool_scale, v_g_mlp, v_g_final]), "adamw_small")
    _, de_s, nm_s, nv_s = [_unflatten(b.reshape(-1), local_shapes) for b in small]

    loss = lax.psum(loss_part[0, 0], ("x", "y", "c"))

    def ordered(meta, gmix, wconv, wpool, pscale, gmlp, gfinal, win, wout, wup, wdown):
        return [meta, gmix, win, wconv, wpool, pscale, wout, gmlp, wup, wdown, gfinal]

    grads = ordered(*small_g, gr_win, gr_wout, gr_wup, gr_wdown)
    deltas = ordered(*de_s, de_win, de_wout, de_wup, de_wdown)
    new_m = ordered(*nm_s, nm_win, nm_wout, nm_wup, nm_wdown)
    new_v = ordered(*nv_s, nv_win, nv_wout, nv_wup, nv_wdown)
    return (loss, grad_x, *grads, *deltas, *new_m, *new_v)
```

```python
import functools

import jax
import jax.numpy as jnp
from jax import lax
from jax.experimental import pallas as pl
from jax.experimental.pallas import tpu as pltpu

F32 = jnp.float32
BF16 = jnp.bfloat16

N_DEV = 8
N_META = 16
HEAD_DIM = 64
BLK = 128
PAD = BLK - N_META
N_POOL_GROUPS = 4
EPS = 1e-6

ADAM_LR = 0.001
ADAM_B1 = 0.9
ADAM_B2 = 0.999
ADAM_EPS = 1e-08
ADAM_WD = 0.01
ADAM_STEP = 10

V7X_VMEM_LIMIT = 56 * 1024 * 1024


def _params(*sem):
    return pltpu.CompilerParams(dimension_semantics=sem, vmem_limit_bytes=V7X_VMEM_LIMIT)


SUBLANES = 8
ROW_TILE = 512
ROW_TILE_MLP = 1152
TN_ACC_BYTES = 8 * 1024 * 1024


def _row_tile(t, cap=ROW_TILE):
    for tm in range(min(cap, t) // SUBLANES * SUBLANES, 0, -SUBLANES):
        if t % tm == 0:
            return tm
    raise ValueError(f"row count {t} is not a multiple of 8")


def _full(a):
    nd = a.ndim
    return pl.BlockSpec(a.shape, lambda *_: (0,) * nd)


def _dot(a, b):
    return jnp.dot(a, b, preferred_element_type=F32)


def _dot_nt(a, b):
    return lax.dot_general(a, b, (((1,), (1,)), ((), ())), preferred_element_type=F32)


def _dot_tn(a, b):
    return lax.dot_general(a, b, (((0,), (0,)), ((), ())), preferred_element_type=F32)


def _rms_fwd(x, g):
    rstd = lax.rsqrt(jnp.mean(x * x, axis=-1, keepdims=True) + EPS)
    return x * rstd * g


def _rms_bwd(dy, x, g):
    rstd = lax.rsqrt(jnp.mean(x * x, axis=-1, keepdims=True) + EPS)
    xhat = x * rstd
    dxhat = dy * g
    dx = rstd * (dxhat - xhat * jnp.mean(dxhat * xhat, axis=-1, keepdims=True))
    return dx, jnp.sum(dy * xhat, axis=0, keepdims=True)


def _inproj_fwd(h, g, wa, wq, wk, wv):
    t, d = h.shape
    tm = _row_tile(t)
    ua_w, aw = wa.shape[1], wq.shape[1]

    def body(h_ref, g_ref, wa_ref, wq_ref, wk_ref, wv_ref, ua_ref, q_ref, k_ref, v_ref, hn_ref):
        hn = _rms_fwd(h_ref[...], g_ref[...]).astype(BF16)
        hn_ref[...] = hn
        ua_ref[...] = _dot(hn, wa_ref[...])
        q_ref[...] = _dot(hn, wq_ref[...]).astype(BF16)
        k_ref[...] = _dot(hn, wk_ref[...]).astype(BF16)
        v_ref[...] = _dot(hn, wv_ref[...]).astype(BF16)

    row = lambda w: pl.BlockSpec((tm, w), lambda i: (i, 0))
    return pl.pallas_call(
        body, name="inproj_fwd", grid=(t // tm,),
        in_specs=[row(d), _full(g), _full(wa), _full(wq), _full(wk), _full(wv)],
        out_specs=[row(ua_w), row(aw), row(aw), row(aw), row(d)],
        out_shape=[jax.ShapeDtypeStruct((t, ua_w), F32)] + [jax.ShapeDtypeStruct((t, aw), BF16)] * 3
        + [jax.ShapeDtypeStruct((t, d), BF16)],
        compiler_params=_params("arbitrary"),
    )(h, g, wa, wq, wk, wv)


def _pool_geometry(lp, pw):
    pg = pw // N_POOL_GROUPS
    row = lax.broadcasted_iota(jnp.int32, (lp, pw), 0)
    lane = lax.broadcasted_iota(jnp.int32, (lp, pw), 1)
    grp = [(lane >= g * pg) & (lane < (g + 1) * pg) for g in range(N_POOL_GROUPS)]
    wlen = jnp.where(grp[0], 2, jnp.where(grp[1], 4, jnp.where(grp[2], 8, 16)))
    cnt = jnp.clip(row - (PAD - 1), 1, wlen).astype(F32)
    return grp, cnt, row >= PAD


def _by_group(grp, vals):
    return jnp.where(grp[0], vals[0], jnp.where(grp[1], vals[1], jnp.where(grp[2], vals[2], vals[3])))


def _pooled(p, grp, cnt, real):
    s2 = p + pltpu.roll(p, 1, 0)
    s4 = s2 + pltpu.roll(s2, 2, 0)
    s8 = s4 + pltpu.roll(s4, 4, 0)
    s16 = s8 + pltpu.roll(s8, 8, 0)
    return jnp.where(real, _by_group(grp, (s2, s4, s8, s16)) / cnt - p, 0.0)


def _conv(uu, wc_ref):
    return wc_ref[2:3, :] * uu + wc_ref[1:2, :] * pltpu.roll(uu, 1, 0) + wc_ref[0:1, :] * pltpu.roll(uu, 2, 0)


def _mixer_fwd(ua, wconv, wbd, pscale, nb):
    t = ua.shape[0]
    lp = t // nb
    cw, pw = wconv.shape[1], wbd.shape[0]

    def body(ua_ref, wc_ref, wbd_ref, ps_ref, y_ref):
        cb = ua_ref[:, 0:cw]
        uu = ua_ref[:, cw:2 * cw] * ua_ref[:, 2 * cw:3 * cw]
        y_ref[:, 0:cw] = (cb * _conv(uu, wc_ref)).astype(BF16)
        grp, cnt, real = _pool_geometry(lp, pw)
        pooled = _pooled(ua_ref[:, 3 * cw:3 * cw + pw], grp, cnt, real)
        y_ref[:, cw:cw + pw] = (_dot(pooled.astype(BF16), wbd_ref[...]) * ps_ref[...]).astype(BF16)

    return pl.pallas_call(
        body, name="mixer_fwd", grid=(nb,),
        in_specs=[pl.BlockSpec((lp, ua.shape[1]), lambda s: (s, 0)), _full(wconv), _full(wbd), _full(pscale)],
        out_specs=pl.BlockSpec((lp, cw + pw), lambda s: (s, 0)),
        out_shape=jax.ShapeDtypeStruct((t, cw + pw), BF16),
        compiler_params=_params("arbitrary"),
    )(ua, wconv, wbd, pscale)


ATT_NW = 2
ATT_W = ATT_NW * BLK
ATT_PAIRS_FWD = 4
ATT_PAIRS_BWD = 4
HW = 2 * HEAD_DIM
ATT_SCALE = HEAD_DIM ** -0.5


def _attn_lk(z, valid):
    nz = -z
    return jnp.where(valid, jnp.minimum(nz, 0.0) - jnp.log(1.0 + jnp.exp(jnp.minimum(z, nz))), 0.0)


def _tri_sum(x, tri):
    hi = x.astype(BF16)
    lo = (x - hi.astype(F32)).astype(BF16)
    return _dot(hi, tri) + _dot(lo, tri)


def _attn_weights(z, suffix, carry, valid):
    return jnp.where(valid, jnp.exp(z + suffix + carry), 0.0)


def _attn_consts():
    r = lax.broadcasted_iota(jnp.int32, (2 * BLK, ATT_W), 0)
    c = lax.broadcasted_iota(jnp.int32, (2 * BLK, ATT_W), 1)
    cmr = c - (r & (BLK - 1))
    kr = lax.broadcasted_iota(jnp.int32, (ATT_W, ATT_W), 0)
    kc = lax.broadcasted_iota(jnp.int32, (ATT_W, ATT_W), 1)
    lane = lax.broadcasted_iota(jnp.int32, (1, HW), 1)
    heads = (lane < HEAD_DIM, lane >= HEAD_DIM)
    return c, cmr, kr, kc, heads


def _tri(cond):
    return jnp.where(cond, 1.0, 0.0).astype(BF16)


def _rows(i):
    return pl.ds(pl.multiple_of(i * BLK, BLK), BLK)


def _stack_heads(x, heads):
    zero = jnp.zeros_like(x)
    return jnp.concatenate([jnp.where(heads[0], x, zero), jnp.where(heads[1], x, zero)], axis=0)


def _unstack_heads(x, heads):
    return jnp.where(heads[0], x[:BLK], x[BLK:])


def _window(i, s):
    start = jnp.maximum(i + 1 - ATT_NW * (s + 1), 0) * BLK
    return start, (i + 1 - ATT_NW * s) * BLK


def _attn_fwd(q, k, v, nb, gather=None):
    t, aw = q.shape
    lp = t // nb
    nblk = lp // BLK
    pp = min(ATT_PAIRS_FWD, aw // HW)
    cols = [slice(p * HW, (p + 1) * HW) for p in range(pp)]
    ncol = aw // (pp * HW)
    nsteps = nb * ncol

    def body(q_ref, k_ref, v_ref, *rest):
        if gather is None:
            (o_ref,) = rest
        else:
            ng = len(gather)
            x_refs, o_ref, g_refs, sems = rest[:ng], rest[ng], rest[ng + 1:2 * ng + 1], rest[2 * ng + 1:]
            step = pl.program_id(0) * ncol + pl.program_id(1)
            start, forward, finish = _all_of([_gather_phases(x_refs[j], g_refs[j], *sems[3 * j:3 * j + 3]) for j in range(ng)])
            pl.when(step == 0)(start)
            pl.when(step == nsteps // 2)(forward)
        c, cmr, kr, kc, heads = _attn_consts()
        m_from = _tri(kr >= kc)

        def qblock(i, _):
            qs = [_stack_heads(q_ref[_rows(i), cs] * ATT_SCALE, heads) for cs in cols]

            def window(s, st):
                accs, carries = st
                start, end = _window(i, s)
                keys = pl.ds(pl.multiple_of(start, BLK), ATT_W)
                valid = (cmr < i * BLK - start) & (c < end - start)
                zs = [_dot_nt(qs[p], k_ref[keys, cs]) for p, cs in enumerate(cols)]
                suffix = [_tri_sum(_attn_lk(zs[p], valid), m_from) for p in range(pp)]
                new_accs, new_carries = [], []
                for p, cs in enumerate(cols):
                    a = _attn_weights(zs[p], suffix[p], carries[p], valid)
                    new_accs.append(accs[p] + _dot(a.astype(BF16), v_ref[keys, cs]))
                    new_carries.append(carries[p] + suffix[p][:, 0:1])
                return tuple(new_accs), tuple(new_carries)

            init = (tuple(jnp.zeros((2 * BLK, HW), F32) for _ in cols), tuple(jnp.zeros((2 * BLK, 1), F32) for _ in cols))
            accs, _ = lax.fori_loop(0, (i + ATT_NW) // ATT_NW, window, init)
            for p, cs in enumerate(cols):
                o_ref[_rows(i), cs] = _unstack_heads(accs[p], heads).astype(BF16)
            return 0

        lax.fori_loop(0, nblk, qblock, 0)
        if gather is not None:
            pl.when(step == nsteps - 1)(finish)

    spec = pl.BlockSpec((lp, pp * HW), lambda s, p: (s, p))
    y_shape = jax.ShapeDtypeStruct((t, aw), BF16)
    if gather is None:
        return pl.pallas_call(
            body, name="attn_fwd", grid=(nb, ncol), in_specs=[spec, spec, spec], out_specs=spec, out_shape=y_shape,
            compiler_params=_params("arbitrary", "arbitrary"),
        )(q, k, v)
    return pl.pallas_call(
        body, name="attn_fwd_gather", grid=(nb, ncol), in_specs=[spec, spec, spec] + [ANY] * len(gather),
        out_specs=[spec] + [ANY] * len(gather),
        out_shape=[y_shape] + [jax.ShapeDtypeStruct((N_DEV,) + x.shape, x.dtype) for x in gather],
        scratch_shapes=COMM_SEMS * len(gather), compiler_params=_params("arbitrary", "arbitrary"),
    )(q, k, v, *gather)


def _outproj(h, ya, yb, wa, wb):
    t, d = h.shape
    tm = _row_tile(t)

    def body(h_ref, ya_ref, yb_ref, wa_ref, wb_ref, o_ref):
        o_ref[...] = h_ref[...] + _dot(ya_ref[...], wa_ref[...]) + _dot(yb_ref[...], wb_ref[...])

    row = lambda w: pl.BlockSpec((tm, w), lambda i: (i, 0))
    return pl.pallas_call(
        body, name="outproj", grid=(t // tm,),
        in_specs=[row(d), row(ya.shape[1]), row(yb.shape[1]), _full(wa), _full(wb)],
        out_specs=row(d), out_shape=jax.ShapeDtypeStruct((t, d), F32),
        compiler_params=_params("arbitrary"),
    )(h, ya, yb, wa, wb)


def _mlp_weight_specs(wup, wdown, layer):
    nff, _, d, fc = wup.shape
    return nff, fc, [pl.BlockSpec((None, None, d, fc), lambda i, kk: (kk, layer, 0, 0)),
                     pl.BlockSpec((None, None, fc, d), lambda i, kk: (kk, layer, 0, 0))]


def _mlp_fwd(h, g, wup, wdown, layer):
    t, d = h.shape
    tm = _row_tile(t, ROW_TILE_MLP)
    nff, fc, wspecs = _mlp_weight_specs(wup, wdown, layer)

    def body(h_ref, g_ref, wu_ref, wd_ref, o_ref, hn_ref, acc):
        kk = pl.program_id(1)

        @pl.when(kk == 0)
        def _():
            x = h_ref[...]
            hn_ref[...] = _rms_fwd(x, g_ref[...]).astype(BF16)
            acc[...] = x

        m = _dot(hn_ref[...], wu_ref[...])
        a = jnp.square(jnp.maximum(m, 0.0)).astype(BF16)
        acc[...] += _dot(a, wd_ref[...])

        @pl.when(kk == nff - 1)
        def _():
            o_ref[...] = acc[...]

    row = pl.BlockSpec((tm, d), lambda i, kk: (i, 0))
    return pl.pallas_call(
        body, name="mlp_fwd", grid=(t // tm, nff),
        in_specs=[row, _full(g)] + wspecs,
        out_specs=[row, row],
        out_shape=[jax.ShapeDtypeStruct((t, d), F32), jax.ShapeDtypeStruct((t, d), BF16)],
        scratch_shapes=[pltpu.VMEM((tm, d), F32)],
        compiler_params=_params("arbitrary", "arbitrary"),
    )(h, g, wup, wdown)


def _loss_head(h, g, target, nb):
    t, d = h.shape
    nblk = t // nb // BLK
    nx = nblk - 1

    def body(h_ref, g_ref, t_ref, loss_ref, dh_ref, dg_ref):
        s, i = pl.program_id(0), pl.program_id(1)

        @pl.when((s == 0) & (i == 0))
        def _():
            loss_ref[...] = jnp.zeros_like(loss_ref)
            dg_ref[...] = jnp.zeros_like(dg_ref)

        @pl.when(i == 0)
        def _():
            dh_ref[...] = jnp.zeros_like(dh_ref)

        @pl.when(i > 0)
        def _():
            x, gg = h_ref[...], g_ref[...]
            err = _rms_fwd(x, gg) - t_ref[...]
            loss_ref[...] += jnp.sum(err * err) * (0.5 / d)
            dx, dg = _rms_bwd(err * (1.0 / d), x, gg)
            dh_ref[...] = dx
            dg_ref[...] += dg

    hspec = pl.BlockSpec((BLK, d), lambda s, i: (s * nblk + i, 0))
    return pl.pallas_call(
        body, name="loss_head", grid=(nb, nblk),
        in_specs=[hspec, _full(g), pl.BlockSpec((BLK, d), lambda s, i: (s * nx + jnp.maximum(i - 1, 0), 0))],
        out_specs=[pl.BlockSpec((1, BLK), lambda s, i: (0, 0)), hspec, pl.BlockSpec((1, d), lambda s, i: (0, 0))],
        out_shape=[jax.ShapeDtypeStruct((1, BLK), F32), jax.ShapeDtypeStruct((t, d), F32), jax.ShapeDtypeStruct((1, d), F32)],
        compiler_params=_params("arbitrary", "arbitrary"),
    )(h, g, target)


def _mlp_bwd(dh, h, hn, g, wup, wdown, layer):
    t, d = h.shape
    tm = _row_tile(t, ROW_TILE_MLP)
    nff, fc, wspecs = _mlp_weight_specs(wup, wdown, layer)

    def body(dh_ref, h_ref, hn_ref, g_ref, wu_ref, wd_ref, dhm_ref, a_ref, dm_ref, dg_ref, dhn, dhb):
        i, kk = pl.program_id(0), pl.program_id(1)

        @pl.when((i == 0) & (kk == 0))
        def _():
            dg_ref[...] = jnp.zeros_like(dg_ref)

        @pl.when(kk == 0)
        def _():
            dhb[...] = dh_ref[...].astype(BF16)
            dhn[...] = jnp.zeros_like(dhn)

        r = jnp.maximum(_dot(hn_ref[...], wu_ref[...]), 0.0)
        a_ref[...] = (r * r).astype(BF16)
        dm = (_dot_nt(dhb[...], wd_ref[...]) * (2.0 * r)).astype(BF16)
        dm_ref[...] = dm
        dhn[...] += _dot_nt(dm, wu_ref[...])

        @pl.when(kk == nff - 1)
        def _():
            dx, dg = _rms_bwd(dhn[...], h_ref[...], g_ref[...])
            dhm_ref[...] = dh_ref[...] + dx
            dg_ref[...] += dg

    row = pl.BlockSpec((tm, d), lambda i, kk: (i, 0))
    ff = pl.BlockSpec((tm, fc), lambda i, kk: (i, kk))
    return pl.pallas_call(
        body, name="mlp_bwd", grid=(t // tm, nff),
        in_specs=[row, row, row, _full(g)] + wspecs,
        out_specs=[row, ff, ff, pl.BlockSpec((1, d), lambda i, kk: (0, 0))],
        out_shape=[jax.ShapeDtypeStruct((t, d), F32), jax.ShapeDtypeStruct((t, nff * fc), BF16),
                   jax.ShapeDtypeStruct((t, nff * fc), BF16), jax.ShapeDtypeStruct((1, d), F32)],
        scratch_shapes=[pltpu.VMEM((tm, d), F32), pltpu.VMEM((tm, d), BF16)],
        compiler_params=_params("arbitrary", "arbitrary"),
    )(dh, h, hn, g, wup, wdown)


def _matmul_tn(x, y, name):
    t, k1 = x.shape
    n = y.shape[1]
    tt = _row_tile(t)
    tn = min(n, max(512, TN_ACC_BYTES // (4 * k1)))
    steps = t // tt

    def body(x_ref, y_ref, o_ref, acc):
        @pl.when(pl.program_id(1) == 0)
        def _():
            acc[...] = jnp.zeros_like(acc)

        acc[...] += _dot_tn(x_ref[...].astype(BF16), y_ref[...].astype(BF16))

        @pl.when(pl.program_id(1) == steps - 1)
        def _():
            o_ref[...] = acc[...].astype(BF16)

    return pl.pallas_call(
        body, name=name, grid=(n // tn, steps),
        in_specs=[pl.BlockSpec((tt, k1), lambda j, r: (r, 0)), pl.BlockSpec((tt, tn), lambda j, r: (r, j))],
        out_specs=pl.BlockSpec((k1, tn), lambda j, r: (0, j)),
        out_shape=jax.ShapeDtypeStruct((k1, n), BF16),
        scratch_shapes=[pltpu.VMEM((k1, tn), F32)],
        compiler_params=_params("arbitrary", "arbitrary"),
    )(x, y)


def _outproj_bwd(dh, wa, wb):
    t, d = dh.shape
    tm = _row_tile(t)
    na, nbw = wa.shape[0], wb.shape[0]

    def body(dh_ref, wa_ref, wb_ref, da_ref, db_ref):
        x = dh_ref[...].astype(BF16)
        da_ref[...] = _dot_nt(x, wa_ref[...])
        db_ref[...] = _dot_nt(x, wb_ref[...]).astype(BF16)

    row = lambda w: pl.BlockSpec((tm, w), lambda i: (i, 0))
    return pl.pallas_call(
        body, name="outproj_bwd", grid=(t // tm,),
        in_specs=[row(d), _full(wa), _full(wb)], out_specs=[row(na), row(nbw)],
        out_shape=[jax.ShapeDtypeStruct((t, na), F32), jax.ShapeDtypeStruct((t, nbw), BF16)],
        compiler_params=_params("arbitrary"),
    )(dh, wa, wb)


def _attn_bwd(q, k, v, do, nb, exchange=None):
    t, aw = q.shape
    lp = t // nb
    nblk = lp // BLK
    pp = min(ATT_PAIRS_BWD, aw // HW)
    cols = [slice(p * HW, (p + 1) * HW) for p in range(pp)]
    ncol = aw // (pp * HW)
    nsteps = nb * ncol
    max_windows = (nblk + ATT_NW - 1) // ATT_NW

    def body(q_ref, k_ref, v_ref, do_ref, *rest):
        if exchange is None:
            dq_ref, dk_ref, dv_ref, dk_acc, dv_acc, g_s, sig_s = rest
        else:
            ne = len(exchange)
            x_refs, (dq_ref, dk_ref, dv_ref), e_refs = rest[:ne], rest[ne:ne + 3], rest[ne + 3:2 * ne + 3]
            dk_acc, dv_acc, g_s, sig_s = rest[2 * ne + 3:2 * ne + 7]
            sems = rest[2 * ne + 7:]
            step = pl.program_id(0) * ncol + pl.program_id(1)
            start, finish = _all_of([_exchange_phases(x_refs[j], e_refs[j], *sems[3 * j:3 * j + 3]) for j in range(ne)])
            pl.when(step == 0)(start)
        c, cmr, kr, kc, heads = _attn_consts()
        m_from = _tri(kr >= kc)
        m_before = _tri(kr < kc)
        dk_acc[...] = jnp.zeros_like(dk_acc)
        dv_acc[...] = jnp.zeros_like(dv_acc)

        def qblock(i, _):
            qs = [_stack_heads(q_ref[_rows(i), cs] * ATT_SCALE, heads) for cs in cols]
            dos = [_stack_heads(do_ref[_rows(i), cs], heads) for cs in cols]
            nwin = (i + ATT_NW) // ATT_NW

            def geometry(s):
                start, end = _window(i, s)
                return (cmr < i * BLK - start) & (c < end - start), pl.ds(pl.multiple_of(start, BLK), ATT_W)

            def down(s, carries):
                valid, keys = geometry(s)
                zs = [_dot_nt(qs[p], k_ref[keys, cs]) for p, cs in enumerate(cols)]
                das = [_dot_nt(dos[p], v_ref[keys, cs]) for p, cs in enumerate(cols)]
                suffix = []
                for p in range(pp):
                    lk = _attn_lk(zs[p], valid)
                    sig_s[p, s] = jnp.exp(zs[p] + lk)
                    suffix.append(_tri_sum(lk, m_from))
                new_carries = []
                for p, cs in enumerate(cols):
                    a = _attn_weights(zs[p], suffix[p], carries[p], valid)
                    g_s[p, s] = das[p] * a
                    dv_acc[keys, cs] += _dot_tn(a.astype(BF16), dos[p])
                    new_carries.append(carries[p] + suffix[p][:, 0:1])
                return tuple(new_carries)

            lax.fori_loop(0, nwin, down, tuple(jnp.zeros((2 * BLK, 1), F32) for _ in cols))

            def up(ss, st):
                accs, carries = st
                s = nwin - 1 - ss
                valid, keys = geometry(s)
                gs = [g_s[p, s] for p in range(pp)]
                before = [_tri_sum(gs[p], m_before) for p in range(pp)]
                new_accs, new_carries = [], []
                for p, cs in enumerate(cols):
                    g, sig = gs[p], sig_s[p, s]
                    dz = jnp.where(valid, g * (1.0 - sig) - (before[p] + carries[p]) * sig, 0.0).astype(BF16)
                    new_accs.append(accs[p] + _dot(dz, k_ref[keys, cs]))
                    dk_acc[keys, cs] += _dot_tn(dz, qs[p])
                    new_carries.append(carries[p] + jnp.sum(g, axis=1, keepdims=True))
                return tuple(new_accs), tuple(new_carries)

            init = (tuple(jnp.zeros((2 * BLK, HW), F32) for _ in cols), tuple(jnp.zeros((2 * BLK, 1), F32) for _ in cols))
            accs, _ = lax.fori_loop(0, nwin, up, init)
            for p, cs in enumerate(cols):
                dq_ref[_rows(i), cs] = (_unstack_heads(accs[p], heads) * ATT_SCALE).astype(BF16)
            return 0

        lax.fori_loop(0, nblk, qblock, 0)
        dk_ref[...] = dk_acc[...].astype(BF16)
        dv_ref[...] = dv_acc[...].astype(BF16)
        if exchange is not None:
            pl.when(step == nsteps - 1)(finish)

    spec = pl.BlockSpec((lp, pp * HW), lambda s, p: (s, p), pipeline_mode=pl.Buffered(1))
    d_shape = jax.ShapeDtypeStruct((t, aw), BF16)
    scratch = [pltpu.VMEM((lp, pp * HW), F32), pltpu.VMEM((lp, pp * HW), F32),
               pltpu.VMEM((pp, max_windows, 2 * BLK, ATT_W), F32), pltpu.VMEM((pp, max_windows, 2 * BLK, ATT_W), F32)]
    if exchange is None:
        return pl.pallas_call(
            body, name="attn_bwd", grid=(nb, ncol), in_specs=[spec] * 4, out_specs=[spec] * 3, out_shape=[d_shape] * 3,
            scratch_shapes=scratch, compiler_params=_params("arbitrary", "arbitrary"),
        )(q, k, v, do)
    return pl.pallas_call(
        body, name="attn_bwd_exchange", grid=(nb, ncol), in_specs=[spec] * 4 + [ANY] * len(exchange),
        out_specs=[spec] * 3 + [ANY] * len(exchange),
        out_shape=[d_shape] * 3 + [jax.ShapeDtypeStruct(x.shape, x.dtype) for x in exchange],
        scratch_shapes=scratch + COMM_SEMS * len(exchange), compiler_params=_params("arbitrary", "arbitrary"),
    )(q, k, v, do, *exchange)


def _mixer_bwd(ua, dy, wconv, wbd, pscale, nb):
    t = ua.shape[0]
    lp = t // nb
    cw, pw = wconv.shape[1], wbd.shape[0]

    def body(ua_ref, dy_ref, wc_ref, wbd_ref, ps_ref, du_ref, dwc_ref, dps_ref, dwb_ref):
        @pl.when(pl.program_id(0) == 0)
        def _():
            dwc_ref[...] = jnp.zeros_like(dwc_ref)
            dps_ref[...] = jnp.zeros_like(dps_ref)
            dwb_ref[...] = jnp.zeros_like(dwb_ref)

        up = lambda x, n: pltpu.roll(x, lp - n, 0)
        cb, cc, cx = ua_ref[:, 0:cw], ua_ref[:, cw:2 * cw], ua_ref[:, 2 * cw:3 * cw]
        uu = cc * cx
        dyc = dy_ref[:, 0:cw]
        du_ref[:, 0:cw] = (dyc * _conv(uu, wc_ref)).astype(BF16)
        dcv = dyc * cb
        duu = wc_ref[2:3, :] * dcv + wc_ref[1:2, :] * up(dcv, 1) + wc_ref[0:1, :] * up(dcv, 2)
        du_ref[:, cw:2 * cw] = (duu * cx).astype(BF16)
        du_ref[:, 2 * cw:3 * cw] = (duu * cc).astype(BF16)
        dwc_ref[0:1, :] += jnp.sum(dcv * pltpu.roll(uu, 2, 0), axis=0, keepdims=True)
        dwc_ref[1:2, :] += jnp.sum(dcv * pltpu.roll(uu, 1, 0), axis=0, keepdims=True)
        dwc_ref[2:3, :] += jnp.sum(dcv * uu, axis=0, keepdims=True)
        grp, cnt, real = _pool_geometry(lp, pw)
        p = ua_ref[:, 3 * cw:3 * cw + pw]
        pooled = _pooled(p, grp, cnt, real).astype(BF16)
        dyp = dy_ref[:, cw:cw + pw]
        dps_ref[...] += jnp.sum(dyp * _dot(pooled, wbd_ref[...]), axis=0, keepdims=True)
        dpre = (dyp * ps_ref[...]).astype(BF16)
        dwb_ref[...] += _dot_tn(pooled, dpre)
        dpooled = jnp.where(real, _dot_nt(dpre, wbd_ref[...]), 0.0)
        xm = dpooled / cnt
        l2 = xm + up(xm, 1)
        l4 = l2 + up(l2, 2)
        l8 = l4 + up(l4, 4)
        l16 = l8 + up(l8, 8)
        du_ref[:, 3 * cw:3 * cw + pw] = jnp.where(real, _by_group(grp, (l2, l4, l8, l16)) - dpooled, 0.0).astype(BF16)

    seq = lambda w: pl.BlockSpec((lp, w), lambda s: (s, 0))
    return pl.pallas_call(
        body, name="mixer_bwd", grid=(nb,),
        in_specs=[seq(ua.shape[1]), seq(cw + pw), _full(wconv), _full(wbd), _full(pscale)],
        out_specs=[seq(ua.shape[1]), pl.BlockSpec((3, cw), lambda s: (0, 0)), pl.BlockSpec((1, pw), lambda s: (0, 0)),
                   pl.BlockSpec((pw, pw), lambda s: (0, 0))],
        out_shape=[jax.ShapeDtypeStruct(ua.shape, BF16), jax.ShapeDtypeStruct((3, cw), F32),
                   jax.ShapeDtypeStruct((1, pw), F32), jax.ShapeDtypeStruct((pw, pw), F32)],
        compiler_params=_params("arbitrary"),
    )(ua, dy, wconv, wbd, pscale)


def _inproj_bwd(dh, h, g, dua, dq, dk, dv, wa, wq, wk, wv):
    t, d = h.shape
    tm = _row_tile(t)

    def body(dh_ref, h_ref, g_ref, dua_ref, dq_ref, dk_ref, dv_ref, wa_ref, wq_ref, wk_ref, wv_ref, o_ref, dg_ref):
        @pl.when(pl.program_id(0) == 0)
        def _():
            dg_ref[...] = jnp.zeros_like(dg_ref)

        dhn = (_dot_nt(dua_ref[...], wa_ref[...]) + _dot_nt(dq_ref[...], wq_ref[...])
               + _dot_nt(dk_ref[...], wk_ref[...]) + _dot_nt(dv_ref[...], wv_ref[...]))
        dx, dg = _rms_bwd(dhn, h_ref[...], g_ref[...])
        o_ref[...] = dh_ref[...] + dx
        dg_ref[...] += dg

    row = lambda w: pl.BlockSpec((tm, w), lambda i: (i, 0))
    return pl.pallas_call(
        body, name="inproj_bwd", grid=(t // tm,),
        in_specs=[row(d), row(d), _full(g), row(dua.shape[1]), row(dq.shape[1]), row(dk.shape[1]), row(dv.shape[1]),
                  _full(wa), _full(wq), _full(wk), _full(wv)],
        out_specs=[row(d), pl.BlockSpec((1, d), lambda i: (0, 0))],
        out_shape=[jax.ShapeDtypeStruct((t, d), F32), jax.ShapeDtypeStruct((1, d), F32)],
        compiler_params=_params("arbitrary"),
    )(dh, h, g, dua, dq, dk, dv, wa, wq, wk, wv)


def _part_tile(r):
    if r <= 512:
        return r
    for tr in (256, 128, 64, 32, 16, 8):
        if r % tr == 0:
            return tr
    return r


def _sum_parts(x, name):
    n, r, c = x.shape
    tr = _part_tile(r)

    def body(x_ref, o_ref):
        acc = x_ref[0]
        for p in range(1, n):
            acc = acc + x_ref[p]
        o_ref[...] = acc

    return pl.pallas_call(
        body, name=name, grid=(r // tr,),
        in_specs=[pl.BlockSpec((n, tr, c), lambda i: (0, i, 0))], out_specs=pl.BlockSpec((tr, c), lambda i: (i, 0)),
        out_shape=jax.ShapeDtypeStruct((r, c), F32), compiler_params=_params("arbitrary"),
    )(x)


def _adamw(w, gparts, m, v, name, exchange=None):
    n, r, c = gparts.shape
    tr = _part_tile(r)
    nsteps = r // tr

    def body(w_ref, g_ref, m_ref, v_ref, *rest):
        if exchange is None:
            go_ref, d_ref, mo_ref, vo_ref = rest
        else:
            x_ref, go_ref, d_ref, mo_ref, vo_ref, e_ref, send_sems, recv_sems, local_sem = rest
            start, finish = _exchange_phases(x_ref, e_ref, send_sems, recv_sems, local_sem)
            pl.when(pl.program_id(0) == 0)(start)
        g = g_ref[0].astype(F32)
        for p in range(1, n):
            g = g + g_ref[p].astype(F32)
        go_ref[...] = g
        mm = ADAM_B1 * m_ref[...] + (1.0 - ADAM_B1) * g
        vv = ADAM_B2 * v_ref[...] + (1.0 - ADAM_B2) * jnp.square(g)
        mo_ref[...] = mm
        vo_ref[...] = vv
        m_hat = mm / (1.0 - ADAM_B1 ** ADAM_STEP)
        v_hat = vv / (1.0 - ADAM_B2 ** ADAM_STEP)
        d_ref[...] = -ADAM_LR * (m_hat / (jnp.sqrt(v_hat) + ADAM_EPS) + ADAM_WD * w_ref[...])
        if exchange is not None:
            pl.when(pl.program_id(0) == nsteps - 1)(finish)

    row = pl.BlockSpec((tr, c), lambda i: (i, 0))
    in_specs = [row, pl.BlockSpec((n, tr, c), lambda i: (0, i, 0)), row, row]
    out_shape = [jax.ShapeDtypeStruct((r, c), F32)] * 4
    if exchange is None:
        return pl.pallas_call(
            body, name=name, grid=(nsteps,), in_specs=in_specs, out_specs=[row] * 4, out_shape=out_shape,
            compiler_params=_params("arbitrary"),
        )(w, gparts, m, v)
    return pl.pallas_call(
        body, name=name, grid=(nsteps,), in_specs=in_specs + [ANY], out_specs=[row] * 4 + [ANY],
        out_shape=out_shape + [jax.ShapeDtypeStruct(exchange.shape, exchange.dtype)], scratch_shapes=COMM_SEMS,
        compiler_params=_params("arbitrary"),
    )(w, gparts, m, v, exchange)


MESH = pl.DeviceIdType.MESH
ANY = pl.BlockSpec(memory_space=pl.ANY)


def _all_gather(x, name):
    r, c = x.shape

    def body(x_ref, out_ref, send_sems, recv_sems, local_sem):
        start, forward, finish = _gather_phases(x_ref, out_ref, send_sems, recv_sems, local_sem)
        start()
        forward()
        finish()

    return pl.pallas_call(
        body, name=name, in_specs=[ANY], out_specs=ANY,
        out_shape=jax.ShapeDtypeStruct((N_DEV, r, c), x.dtype), scratch_shapes=COMM_SEMS,
    )(x)


COMM_SEMS = [pltpu.SemaphoreType.DMA((7,)), pltpu.SemaphoreType.DMA((7,)), pltpu.SemaphoreType.DMA(())]


def _gather_phases(x_ref, out_ref, send_sems, recv_sems, local_sem):
    mx, my, mc = lax.axis_index("x"), lax.axis_index("y"), lax.axis_index("c")
    me, sibling = (mx, my, mc), (mx, my, 1 - mc)
    chips = [(1 - mx, my), (mx, 1 - my), (1 - mx, 1 - my)]

    def slot(px, py, pc):
        return out_ref.at[4 * px + 2 * py + pc]

    def copy(kk, block, to, src=None):
        return pltpu.make_async_remote_copy(
            src_ref=slot(*block) if src is None else src, dst_ref=slot(*block),
            send_sem=send_sems.at[kk], recv_sem=recv_sems.at[kk], device_id=to, device_id_type=MESH)

    mine = pltpu.make_async_copy(x_ref, slot(*me), local_sem)
    first = [copy(0, me, sibling, src=x_ref)] + [copy(1 + j, me, (*chip, mc), src=x_ref) for j, chip in enumerate(chips)]
    passed = [copy(4 + j, (*chip, mc), sibling) for j, chip in enumerate(chips)]

    def start():
        mine.start()
        for cp in first:
            cp.start()

    def forward():
        for j, chip in enumerate(chips):
            copy(1 + j, (*chip, mc), me).wait_recv()
            passed[j].start()

    def finish():
        copy(0, sibling, me).wait_recv()
        for j, chip in enumerate(chips):
            copy(4 + j, (*chip, 1 - mc), me).wait_recv()
        for cp in first + passed:
            cp.wait_send()
        mine.wait()

    return start, forward, finish


def _all_of(phase_lists):
    def run(fns):
        def every():
            for fn in fns:
                fn()
        return every
    return tuple(run(fns) for fns in zip(*phase_lists))


def _exchange_phases(x_ref, out_ref, send_sems, recv_sems, local_sem):
    mx, my, mc = lax.axis_index("x"), lax.axis_index("y"), lax.axis_index("c")
    me = 4 * mx + 2 * my + mc
    mine = pltpu.make_async_copy(x_ref.at[me], out_ref.at[me], local_sem)
    copies = []
    for kk in range(1, N_DEV):
        px, py, pc = mx ^ (kk >> 2), my ^ ((kk >> 1) & 1), mc ^ (kk & 1)
        peer = 4 * px + 2 * py + pc
        copies.append((peer, pltpu.make_async_remote_copy(
            src_ref=x_ref.at[peer], dst_ref=out_ref.at[me], send_sem=send_sems.at[kk - 1],
            recv_sem=recv_sems.at[kk - 1], device_id=(px, py, pc), device_id_type=MESH)))

    def start():
        mine.start()
        for _, cp in copies:
            cp.start()

    def finish():
        for kk, (peer, _) in enumerate(copies):
            pltpu.make_async_remote_copy(
                src_ref=x_ref.at[peer], dst_ref=out_ref.at[peer], send_sem=send_sems.at[kk],
                recv_sem=recv_sems.at[kk], device_id=(mx, my, mc), device_id_type=MESH).wait_recv()
        for _, cp in copies:
            cp.wait_send()
        mine.wait()

    return start, finish


def _all_to_all(x, name):
    def body(x_ref, out_ref, send_sems, recv_sems, local_sem):
        start, finish = _exchange_phases(x_ref, out_ref, send_sems, recv_sems, local_sem)
        start()
        finish()

    return pl.pallas_call(
        body, name=name, in_specs=[ANY], out_specs=ANY,
        out_shape=jax.ShapeDtypeStruct(x.shape, x.dtype), scratch_shapes=COMM_SEMS,
    )(x)


def _shards_last(a):
    lead = a.shape[:-1]
    return jnp.moveaxis(a.reshape(*lead, N_DEV, a.shape[-1] // N_DEV), -2, 0)


def _shards_rows(a):
    l, n, c = a.shape
    return jnp.moveaxis(a.reshape(l, N_DEV, n // N_DEV, c), 1, 0)


def _flat_pad(parts, mult):
    flat = jnp.concatenate([p.reshape(-1) for p in parts])
    return jnp.pad(flat, (0, (-flat.shape[0]) % mult))


def _unflatten(flat, shapes):
    out, o = [], 0
    for shp in shapes:
        n = 1
        for s in shp:
            n *= s
        out.append(flat[o:o + n].reshape(shp))
        o += n
    return out


def kernel(x, meta_tokens, g_mix, w_in, w_conv, w_pool, pool_scale, w_out, g_mlp, w_up, w_down, g_final, loss_target, m_meta_tokens, m_g_mix, m_w_in, m_w_conv, m_w_pool, m_pool_scale, m_w_out, m_g_mlp, m_w_up, m_w_down, m_g_final, v_meta_tokens, v_g_mix, v_w_in, v_w_conv, v_w_pool, v_pool_scale, v_w_out, v_g_mlp, v_w_up, v_w_down, v_g_final):
    nb, seq, d = x.shape
    depth = g_mix.shape[0]
    lp = BLK + seq
    t = nb * lp
    cw = w_conv.shape[2] * N_DEV
    pw = pool_scale.shape[1]
    pg = pw // N_POOL_GROUPS
    aw = (w_in.shape[2] * N_DEV - 3 * cw - pw) // 3
    ua_w = 3 * cw + pw
    d_ff = w_up.shape[2] * N_DEV
    me = 4 * lax.axis_index("x") + 2 * lax.axis_index("y") + lax.axis_index("c")

    def rows(a):
        return a.reshape(-1, a.shape[-1])

    def gathered_w_in(g, layers):
        return jnp.moveaxis(g.reshape(N_DEV, layers, d, -1), 0, 2).reshape(layers, d, -1)

    w_in0_f = gathered_w_in(_all_gather(rows(w_in[:1]).astype(BF16), "gather_w_in0"), 1)
    small_in = _all_gather(_flat_pad([meta_tokens, w_conv], 8 * BLK).reshape(-1, BLK), "gather_small_weights").reshape(N_DEV, -1)
    meta_f = jnp.moveaxis(small_in[:, :meta_tokens.size].reshape(N_DEV, N_META, d // N_DEV), 0, 1).reshape(N_META, d)
    w_conv_f = small_in[:, meta_tokens.size:meta_tokens.size + w_conv.size].reshape((N_DEV,) + w_conv.shape)
    w_conv_f = jnp.moveaxis(w_conv_f, 0, 2).reshape(depth, w_conv.shape[1], cw)

    def split_w_in(w):
        return w[:, :ua_w], w[:, ua_w:ua_w + aw], w[:, ua_w + aw:ua_w + 2 * aw], w[:, ua_w + 2 * aw:]

    wbd = jnp.zeros((depth, pw, pw), F32)
    for gi in range(N_POOL_GROUPS):
        wbd = wbd.at[:, gi * pg:(gi + 1) * pg, gi * pg:(gi + 1) * pg].set(w_pool[:, gi])
    wbd = wbd.astype(BF16)

    h = jnp.concatenate([jnp.zeros((nb, PAD, d), F32), jnp.broadcast_to(meta_f[None], (nb, N_META, d)), x], axis=1).reshape(t, d)
    saved, w_in_parts = [], [None] * depth
    w_in_parts[0] = split_w_in(w_in0_f[0])
    for l in range(depth):
        wa, wq, wk, wv = w_in_parts[l]
        ua, q, k, v, hn1 = _inproj_fwd(h, g_mix[l][None], wa, wq, wk, wv)
        ycp = _mixer_fwd(ua, w_conv_f[l], wbd[l], pool_scale[l][None], nb)
        if l == 0:
            yat, g_in, g_out, g_up, g_down = _attn_fwd(
                q, k, v, nb, gather=[rows(a).astype(BF16) for a in (w_in[1:], w_out, w_up, w_down)])
            w_in_rest = gathered_w_in(g_in, depth - 1)
            for ll in range(1, depth):
                w_in_parts[ll] = split_w_in(w_in_rest[ll - 1])
            w_out_f = jnp.moveaxis(g_out.reshape((N_DEV,) + w_out.shape), 0, 1).reshape(depth, -1, d)
            wo_a, wo_b = w_out_f[:, :cw + pw], w_out_f[:, cw + pw:]
            w_up_f = g_up.reshape((N_DEV,) + w_up.shape)
            w_down_f = g_down.reshape((N_DEV,) + w_down.shape)
        else:
            yat = _attn_fwd(q, k, v, nb)
        h_mid =_outproj(h, ycp, yat, wo_a[l], wo_b[l])
        h_next, hn2 = _mlp_fwd(h_mid, g_mlp[l][None], w_up_f, w_down_f, l)
        saved.append((h, ua, q, k, v, hn1, ycp, yat, h_mid, hn2))
        h = h_next

    loss_part, dh, dg_final = _loss_head(h, g_final[None], loss_target.reshape(nb * seq, d), nb)

    g_gmix, g_gmlp, g_wconv, g_pscale, g_wpool = [None] * depth, [None] * depth, [None] * depth, [None] * depth, [None] * depth
    g_win, g_wout, g_wup, g_wdown = [None] * depth, [None] * depth, [None] * depth, [None] * depth
    for l in reversed(range(depth)):
        h_in, ua, q, k, v, hn1, ycp, yat, h_mid, hn2 = saved[l]
        wa, wq, wk, wv = w_in_parts[l]
        dh_mid, act, dm, g_gmlp[l] = _mlp_bwd(dh, h_mid, hn2, g_mlp[l][None], w_up_f, w_down_f, l)
        g_wup[l] = _matmul_tn(hn2, dm, "grad_w_up")
        g_wdown[l] = _matmul_tn(act, dh, "grad_w_down")
        dycp, do = _outproj_bwd(dh_mid, wo_a[l], wo_b[l])
        g_wout[l] = jnp.concatenate([_matmul_tn(ycp, dh_mid, "grad_w_out_a"), _matmul_tn(yat, dh_mid, "grad_w_out_b")], axis=0)
        if l == 0:
            send = [_shards_last(jnp.stack(g_win[1:])), _shards_rows(jnp.stack(g_wout)),
                    _shards_last(jnp.stack(g_wup)), _shards_rows(jnp.stack(g_wdown))]
            dq, dk, dv, p_in, p_out, p_up, p_down = _attn_bwd(
                q, k, v, do, nb, exchange=[s.reshape(N_DEV, -1, s.shape[-1]) for s in send])
        else:
            dq, dk, dv = _attn_bwd(q, k, v, do, nb)
        dua, g_wconv[l], g_pscale[l], dwb = _mixer_bwd(ua, dycp, w_conv_f[l], wbd[l], pool_scale[l][None], nb)
        g_wpool[l] = jnp.stack([dwb[gi * pg:(gi + 1) * pg, gi * pg:(gi + 1) * pg] for gi in range(N_POOL_GROUPS)])
        dh, g_gmix[l] = _inproj_bwd(dh_mid, h_in, g_mix[l][None], dua, dq, dk, dv, wa, wq, wk, wv)
        g_win[l] = jnp.concatenate([_matmul_tn(hn1, dua, "grad_w_in_a"), _matmul_tn(hn1, dq, "grad_w_in_q"),
                                    _matmul_tn(hn1, dk, "grad_w_in_k"), _matmul_tn(hn1, dv, "grad_w_in_v")], axis=1)
    dh3 = dh.reshape(nb, lp, d)
    grad_x = dh3[:, BLK:]
    g_meta = _sum_parts(dh3[:, PAD:BLK], "sum_meta_grad")

    send_in0 = _shards_last(g_win[0][None]).reshape(N_DEV, -1, w_in.shape[-1])
    half = send_in0.shape[1] // 2
    o_out = _adamw(rows(w_out), p_out, rows(m_w_out), rows(v_w_out), "adamw_w_out")
    *o_up, p_in0a = _adamw(rows(w_up), p_up, rows(m_w_up), rows(v_w_up), "adamw_w_up_exchange", exchange=send_in0[:, :half])
    *o_down, p_in0b = _adamw(rows(w_down), p_down, rows(m_w_down), rows(v_w_down), "adamw_w_down_exchange",
                             exchange=send_in0[:, half:])
    o_in1 = _adamw(rows(w_in[1:]), p_in, rows(m_w_in[1:]), rows(v_w_in[1:]), "adamw_w_in_rest")
    o_in0 = _adamw(rows(w_in[:1]), jnp.concatenate([p_in0a, p_in0b], axis=1), rows(m_w_in[:1]), rows(v_w_in[:1]), "adamw_w_in0")
    (gr_win, gr_wout, gr_wup, gr_wdown), (de_win, de_wout, de_wup, de_wdown), (nm_win, nm_wout, nm_wup, nm_wdown), \
        (nv_win, nv_wout, nv_wup, nv_wdown) = [
            [jnp.concatenate([i0, i1], axis=0).reshape(w_in.shape), o.reshape(w_out.shape), u.reshape(w_up.shape), dn.reshape(w_down.shape)]
            for i0, i1, o, u, dn in zip(o_in0, o_in1, o_out, o_up, o_down)]

    small_full = [g_meta, jnp.concatenate(g_gmix), jnp.stack(g_wconv), jnp.stack(g_wpool), jnp.concatenate(g_pscale),
                  jnp.concatenate(g_gmlp), dg_final.reshape(-1)]
    small_shapes = [a.shape for a in small_full]
    small_sum = _sum_parts(_all_gather(_flat_pad(small_full, 8 * BLK).reshape(-1, BLK), "gather_small_grads"), "sum_small_grads")
    gr_meta, gr_gmix, gr_wconv, gr_wpool, gr_pscale, gr_gmlp, gr_gfinal = _unflatten(small_sum.reshape(-1), small_shapes)
    gr_meta = lax.dynamic_slice_in_dim(gr_meta, me * (d // N_DEV), d // N_DEV, axis=1)
    gr_wconv = lax.dynamic_slice_in_dim(gr_wconv, me * (cw // N_DEV), cw // N_DEV, axis=2)
    small_g = [gr_meta, gr_gmix, gr_wconv, gr_wpool, gr_pscale, gr_gmlp, gr_gfinal]
    local_shapes = [a.shape for a in small_g]
    pack_small = lambda parts_: _flat_pad(parts_, 8 * BLK).reshape(-1, BLK)
    small = _adamw(pack_small([meta_tokens, g_mix, w_conv, w_pool, pool_scale, g_mlp, g_final]), pack_small(small_g)[None],
                   pack_small([m_meta_tokens, m_g_mix, m_w_conv, m_w_pool, m_pool_scale, m_g_mlp, m_g_final]),
                   pack_small([v_meta_tokens, v_g_mix, v_w_conv, v_w_pool, v_pool_scale, v_g_mlp, v_g_final]), "adamw_small")
    _, de_s, nm_s, nv_s = [_unflatten(b.reshape(-1), local_shapes) for b in small]

    loss = lax.psum(loss_part[0, 0], ("x", "y", "c"))

    def ordered(meta, gmix, wconv, wpool, pscale, gmlp, gfinal, win, wout, wup, wdown):
        return [meta, gmix, win, wconv, wpool, pscale, wout, gmlp, wup, wdown, gfinal]

    grads = ordered(*small_g, gr_win, gr_wout, gr_wup, gr_wdown)
    deltas = ordered(*de_s, de_win, de_wout, de_wup, de_wdown)
    new_m = ordered(*nm_s, nm_win, nm_wout, nm_wup, nm_wdown)
    new_v = ordered(*nv_s, nv_win, nv_wout, nv_wup, nv_wdown)
    return (loss, grad_x, *grads, *deltas, *new_m, *new_v)
```

```python
import functools

import jax
import jax.numpy as jnp
from jax import lax
from jax.experimental import pallas as pl
from jax.experimental.pallas import tpu as pltpu

F32 = jnp.float32
BF16 = jnp.bfloat16

N_DEV = 8
N_META = 16
HEAD_DIM = 64
BLK = 128
PAD = BLK - N_META
N_POOL_GROUPS = 4
EPS = 1e-6

ADAM_LR = 0.001
ADAM_B1 = 0.9
ADAM_B2 = 0.999
ADAM_EPS = 1e-08
ADAM_WD = 0.01
ADAM_STEP = 10

V7X_VMEM_LIMIT = 56 * 1024 * 1024


def _params(*sem):
    return pltpu.CompilerParams(dimension_semantics=sem, vmem_limit_bytes=V7X_VMEM_LIMIT)


SUBLANES = 8
ROW_TILE = 512
ROW_TILE_MLP = 1152
ROW_TILE_WIDE = 1152
TN_ACC_BYTES = 8 * 1024 * 1024


def _row_tile(t, cap=ROW_TILE):
    for tm in range(min(cap, t) // SUBLANES * SUBLANES, 0, -SUBLANES):
        if t % tm == 0:
            return tm
    raise ValueError(f"row count {t} is not a multiple of 8")


def _full(a):
    nd = a.ndim
    return pl.BlockSpec(a.shape, lambda *_: (0,) * nd)


def _dot(a, b):
    return jnp.dot(a, b, preferred_element_type=F32)


def _dot_nt(a, b):
    return lax.dot_general(a, b, (((1,), (1,)), ((), ())), preferred_element_type=F32)


def _dot_tn(a, b):
    return lax.dot_general(a, b, (((0,), (0,)), ((), ())), preferred_element_type=F32)


def _rms_fwd(x, g):
    rstd = lax.rsqrt(jnp.mean(x * x, axis=-1, keepdims=True) + EPS)
    return x * rstd * g


def _rms_bwd(dy, x, g):
    rstd = lax.rsqrt(jnp.mean(x * x, axis=-1, keepdims=True) + EPS)
    xhat = x * rstd
    dxhat = dy * g
    dx = rstd * (dxhat - xhat * jnp.mean(dxhat * xhat, axis=-1, keepdims=True))
    return dx, jnp.sum(dy * xhat, axis=0, keepdims=True)


def _inproj_fwd(h, g, wa, wq, wk, wv):
    t, d = h.shape
    tm = _row_tile(t, ROW_TILE_WIDE)
    ua_w, aw = wa.shape[1], wq.shape[1]

    def body(h_ref, g_ref, wa_ref, wq_ref, wk_ref, wv_ref, ua_ref, q_ref, k_ref, v_ref, hn_ref):
        hn = _rms_fwd(h_ref[...], g_ref[...]).astype(BF16)
        hn_ref[...] = hn
        ua_ref[...] = _dot(hn, wa_ref[...])
        q_ref[...] = _dot(hn, wq_ref[...]).astype(BF16)
        k_ref[...] = _dot(hn, wk_ref[...]).astype(BF16)
        v_ref[...] = _dot(hn, wv_ref[...]).astype(BF16)

    row = lambda w: pl.BlockSpec((tm, w), lambda i: (i, 0))
    return pl.pallas_call(
        body, name="inproj_fwd", grid=(t // tm,),
        in_specs=[row(d), _full(g), _full(wa), _full(wq), _full(wk), _full(wv)],
        out_specs=[row(ua_w), row(aw), row(aw), row(aw), row(d)],
        out_shape=[jax.ShapeDtypeStruct((t, ua_w), F32)] + [jax.ShapeDtypeStruct((t, aw), BF16)] * 3
        + [jax.ShapeDtypeStruct((t, d), BF16)],
        compiler_params=_params("arbitrary"),
    )(h, g, wa, wq, wk, wv)


def _pool_geometry(lp, pw):
    pg = pw // N_POOL_GROUPS
    row = lax.broadcasted_iota(jnp.int32, (lp, pw), 0)
    lane = lax.broadcasted_iota(jnp.int32, (lp, pw), 1)
    grp = [(lane >= g * pg) & (lane < (g + 1) * pg) for g in range(N_POOL_GROUPS)]
    wlen = jnp.where(grp[0], 2, jnp.where(grp[1], 4, jnp.where(grp[2], 8, 16)))
    cnt = jnp.clip(row - (PAD - 1), 1, wlen).astype(F32)
    return grp, cnt, row >= PAD


def _by_group(grp, vals):
    return jnp.where(grp[0], vals[0], jnp.where(grp[1], vals[1], jnp.where(grp[2], vals[2], vals[3])))


def _pooled(p, grp, cnt, real):
    s2 = p + pltpu.roll(p, 1, 0)
    s4 = s2 + pltpu.roll(s2, 2, 0)
    s8 = s4 + pltpu.roll(s4, 4, 0)
    s16 = s8 + pltpu.roll(s8, 8, 0)
    return jnp.where(real, _by_group(grp, (s2, s4, s8, s16)) / cnt - p, 0.0)


def _conv(uu, wc_ref):
    return wc_ref[2:3, :] * uu + wc_ref[1:2, :] * pltpu.roll(uu, 1, 0) + wc_ref[0:1, :] * pltpu.roll(uu, 2, 0)


def _mixer_fwd(ua, wconv, wbd, pscale, nb):
    t = ua.shape[0]
    lp = t // nb
    cw, pw = wconv.shape[1], wbd.shape[0]

    def body(ua_ref, wc_ref, wbd_ref, ps_ref, y_ref):
        cb = ua_ref[:, 0:cw]
        uu = ua_ref[:, cw:2 * cw] * ua_ref[:, 2 * cw:3 * cw]
        y_ref[:, 0:cw] = (cb * _conv(uu, wc_ref)).astype(BF16)
        grp, cnt, real = _pool_geometry(lp, pw)
        pooled = _pooled(ua_ref[:, 3 * cw:3 * cw + pw], grp, cnt, real)
        y_ref[:, cw:cw + pw] = (_dot(pooled.astype(BF16), wbd_ref[...]) * ps_ref[...]).astype(BF16)

    return pl.pallas_call(
        body, name="mixer_fwd", grid=(nb,),
        in_specs=[pl.BlockSpec((lp, ua.shape[1]), lambda s: (s, 0)), _full(wconv), _full(wbd), _full(pscale)],
        out_specs=pl.BlockSpec((lp, cw + pw), lambda s: (s, 0)),
        out_shape=jax.ShapeDtypeStruct((t, cw + pw), BF16),
        compiler_params=_params("arbitrary"),
    )(ua, wconv, wbd, pscale)


ATT_NW = 2
ATT_W = ATT_NW * BLK
ATT_PAIRS_FWD = 4
ATT_PAIRS_BWD = 4
HW = 2 * HEAD_DIM
ATT_SCALE = HEAD_DIM ** -0.5


def _attn_lk(z, valid):
    nz = -z
    return jnp.where(valid, jnp.minimum(nz, 0.0) - jnp.log(1.0 + jnp.exp(jnp.minimum(z, nz))), 0.0)


def _tri_sum_bf16(x, tri):
    return _dot(x.astype(BF16), tri)


def _tri_sum(x, tri):
    hi = x.astype(BF16)
    lo = (x - hi.astype(F32)).astype(BF16)
    return _dot(hi, tri) + _dot(lo, tri)


def _attn_weights(z, suffix, carry, valid):
    return jnp.where(valid, jnp.exp(z + suffix + carry), 0.0)


def _attn_consts():
    r = lax.broadcasted_iota(jnp.int32, (2 * BLK, ATT_W), 0)
    c = lax.broadcasted_iota(jnp.int32, (2 * BLK, ATT_W), 1)
    cmr = c - (r & (BLK - 1))
    kr = lax.broadcasted_iota(jnp.int32, (ATT_W, ATT_W), 0)
    kc = lax.broadcasted_iota(jnp.int32, (ATT_W, ATT_W), 1)
    lane = lax.broadcasted_iota(jnp.int32, (1, HW), 1)
    heads = (lane < HEAD_DIM, lane >= HEAD_DIM)
    return c, cmr, kr, kc, heads


def _tri(cond):
    return jnp.where(cond, 1.0, 0.0).astype(BF16)


def _rows(i):
    return pl.ds(pl.multiple_of(i * BLK, BLK), BLK)


def _stack_heads(x, heads):
    zero = jnp.zeros_like(x)
    return jnp.concatenate([jnp.where(heads[0], x, zero), jnp.where(heads[1], x, zero)], axis=0)


def _unstack_heads(x, heads):
    return jnp.where(heads[0], x[:BLK], x[BLK:])


def _window(i, s):
    start = jnp.maximum(i + 1 - ATT_NW * (s + 1), 0) * BLK
    return start, (i + 1 - ATT_NW * s) * BLK


def _attn_fwd(q, k, v, nb, gather=None):
    t, aw = q.shape
    lp = t // nb
    nblk = lp // BLK
    pp = min(ATT_PAIRS_FWD, aw // HW)
    cols = [slice(p * HW, (p + 1) * HW) for p in range(pp)]
    ncol = aw // (pp * HW)
    nsteps = nb * ncol

    def body(q_ref, k_ref, v_ref, *rest):
        if gather is None:
            (o_ref,) = rest
        else:
            ng = len(gather)
            x_refs, o_ref, g_refs, sems = rest[:ng], rest[ng], rest[ng + 1:2 * ng + 1], rest[2 * ng + 1:]
            step = pl.program_id(0) * ncol + pl.program_id(1)
            start, forward, finish = _all_of([_gather_phases(x_refs[j], g_refs[j], *sems[3 * j:3 * j + 3]) for j in range(ng)])
            pl.when(step == 0)(start)
            pl.when(step == nsteps // 2)(forward)
        c, cmr, kr, kc, heads = _attn_consts()
        m_from = _tri(kr >= kc)

        def qblock(i, _):
            qs = [_stack_heads(q_ref[_rows(i), cs] * ATT_SCALE, heads) for cs in cols]

            def window(s, st):
                accs, carries = st
                start, end = _window(i, s)
                keys = pl.ds(pl.multiple_of(start, BLK), ATT_W)
                valid = (cmr < i * BLK - start) & (c < end - start)
                zs = [_dot_nt(qs[p], k_ref[keys, cs]) for p, cs in enumerate(cols)]
                suffix = [_tri_sum_bf16(_attn_lk(zs[p], valid), m_from) for p in range(pp)]
                new_accs, new_carries = [], []
                for p, cs in enumerate(cols):
                    a = _attn_weights(zs[p], suffix[p], carries[p], valid)
                    new_accs.append(accs[p] + _dot(a.astype(BF16), v_ref[keys, cs]))
                    new_carries.append(carries[p] + suffix[p][:, 0:1])
                return tuple(new_accs), tuple(new_carries)

            init = (tuple(jnp.zeros((2 * BLK, HW), F32) for _ in cols), tuple(jnp.zeros((2 * BLK, 1), F32) for _ in cols))
            accs, _ = lax.fori_loop(0, (i + ATT_NW) // ATT_NW, window, init)
            for p, cs in enumerate(cols):
                o_ref[_rows(i), cs] = _unstack_heads(accs[p], heads).astype(BF16)
            return 0

        lax.fori_loop(0, nblk, qblock, 0)
        if gather is not None:
            pl.when(step == nsteps - 1)(finish)

    spec = pl.BlockSpec((lp, pp * HW), lambda s, p: (s, p))
    y_shape = jax.ShapeDtypeStruct((t, aw), BF16)
    if gather is None:
        return pl.pallas_call(
            body, name="attn_fwd", grid=(nb, ncol), in_specs=[spec, spec, spec], out_specs=spec, out_shape=y_shape,
            compiler_params=_params("arbitrary", "arbitrary"),
        )(q, k, v)
    return pl.pallas_call(
        body, name="attn_fwd_gather", grid=(nb, ncol), in_specs=[spec, spec, spec] + [ANY] * len(gather),
        out_specs=[spec] + [ANY] * len(gather),
        out_shape=[y_shape] + [jax.ShapeDtypeStruct((N_DEV,) + x.shape, x.dtype) for x in gather],
        scratch_shapes=COMM_SEMS * len(gather), compiler_params=_params("arbitrary", "arbitrary"),
    )(q, k, v, *gather)


def _outproj(h, ya, yb, wa, wb):
    t, d = h.shape
    tm = _row_tile(t, ROW_TILE_WIDE)

    def body(h_ref, ya_ref, yb_ref, wa_ref, wb_ref, o_ref):
        o_ref[...] = h_ref[...] + _dot(ya_ref[...], wa_ref[...]) + _dot(yb_ref[...], wb_ref[...])

    row = lambda w: pl.BlockSpec((tm, w), lambda i: (i, 0))
    return pl.pallas_call(
        body, name="outproj", grid=(t // tm,),
        in_specs=[row(d), row(ya.shape[1]), row(yb.shape[1]), _full(wa), _full(wb)],
        out_specs=row(d), out_shape=jax.ShapeDtypeStruct((t, d), F32),
        compiler_params=_params("arbitrary"),
    )(h, ya, yb, wa, wb)


def _mlp_weight_specs(wup, wdown, layer):
    nff, _, d, fc = wup.shape
    return nff, fc, [pl.BlockSpec((None, None, d, fc), lambda i, kk: (kk, layer, 0, 0)),
                     pl.BlockSpec((None, None, fc, d), lambda i, kk: (kk, layer, 0, 0))]


def _mlp_fwd(h, g, wup, wdown, layer):
    t, d = h.shape
    tm = _row_tile(t, ROW_TILE_MLP)
    nff, fc, wspecs = _mlp_weight_specs(wup, wdown, layer)

    def body(h_ref, g_ref, wu_ref, wd_ref, o_ref, hn_ref, acc):
        kk = pl.program_id(1)

        @pl.when(kk == 0)
        def _():
            x = h_ref[...]
            hn_ref[...] = _rms_fwd(x, g_ref[...]).astype(BF16)
            acc[...] = x

        m = _dot(hn_ref[...], wu_ref[...])
        a = jnp.square(jnp.maximum(m, 0.0)).astype(BF16)
        acc[...] += _dot(a, wd_ref[...])

        @pl.when(kk == nff - 1)
        def _():
            o_ref[...] = acc[...]

    row = pl.BlockSpec((tm, d), lambda i, kk: (i, 0))
    return pl.pallas_call(
        body, name="mlp_fwd", grid=(t // tm, nff),
        in_specs=[row, _full(g)] + wspecs,
        out_specs=[row, row],
        out_shape=[jax.ShapeDtypeStruct((t, d), F32), jax.ShapeDtypeStruct((t, d), BF16)],
        scratch_shapes=[pltpu.VMEM((tm, d), F32)],
        compiler_params=_params("arbitrary", "arbitrary"),
    )(h, g, wup, wdown)


def _loss_head(h, g, target, nb):
    t, d = h.shape
    nblk = t // nb // BLK
    nx = nblk - 1

    def body(h_ref, g_ref, t_ref, loss_ref, dh_ref, dg_ref):
        s, i = pl.program_id(0), pl.program_id(1)

        @pl.when((s == 0) & (i == 0))
        def _():
            loss_ref[...] = jnp.zeros_like(loss_ref)
            dg_ref[...] = jnp.zeros_like(dg_ref)

        @pl.when(i == 0)
        def _():
            dh_ref[...] = jnp.zeros_like(dh_ref)

        @pl.when(i > 0)
        def _():
            x, gg = h_ref[...], g_ref[...]
            err = _rms_fwd(x, gg) - t_ref[...]
            loss_ref[...] += jnp.sum(err * err) * (0.5 / d)
            dx, dg = _rms_bwd(err * (1.0 / d), x, gg)
            dh_ref[...] = dx
            dg_ref[...] += dg

    hspec = pl.BlockSpec((BLK, d), lambda s, i: (s * nblk + i, 0))
    return pl.pallas_call(
        body, name="loss_head", grid=(nb, nblk),
        in_specs=[hspec, _full(g), pl.BlockSpec((BLK, d), lambda s, i: (s * nx + jnp.maximum(i - 1, 0), 0))],
        out_specs=[pl.BlockSpec((1, BLK), lambda s, i: (0, 0)), hspec, pl.BlockSpec((1, d), lambda s, i: (0, 0))],
        out_shape=[jax.ShapeDtypeStruct((1, BLK), F32), jax.ShapeDtypeStruct((t, d), F32), jax.ShapeDtypeStruct((1, d), F32)],
        compiler_params=_params("arbitrary", "arbitrary"),
    )(h, g, target)


def _mlp_bwd(dh, h, hn, g, wup, wdown, layer):
    t, d = h.shape
    tm = _row_tile(t, ROW_TILE_MLP)
    nff, fc, wspecs = _mlp_weight_specs(wup, wdown, layer)

    def body(dh_ref, h_ref, hn_ref, g_ref, wu_ref, wd_ref, dhm_ref, a_ref, dm_ref, dg_ref, dhn, dhb):
        i, kk = pl.program_id(0), pl.program_id(1)

        @pl.when((i == 0) & (kk == 0))
        def _():
            dg_ref[...] = jnp.zeros_like(dg_ref)

        @pl.when(kk == 0)
        def _():
            dhb[...] = dh_ref[...].astype(BF16)
            dhn[...] = jnp.zeros_like(dhn)

        r = jnp.maximum(_dot(hn_ref[...], wu_ref[...]), 0.0)
        a_ref[...] = (r * r).astype(BF16)
        dm = (_dot_nt(dhb[...], wd_ref[...]) * (2.0 * r)).astype(BF16)
        dm_ref[...] = dm
        dhn[...] += _dot_nt(dm, wu_ref[...])

        @pl.when(kk == nff - 1)
        def _():
            dx, dg = _rms_bwd(dhn[...], h_ref[...], g_ref[...])
            dhm_ref[...] = dh_ref[...] + dx
            dg_ref[...] += dg

    row = pl.BlockSpec((tm, d), lambda i, kk: (i, 0))
    ff = pl.BlockSpec((tm, fc), lambda i, kk: (i, kk))
    return pl.pallas_call(
        body, name="mlp_bwd", grid=(t // tm, nff),
        in_specs=[row, row, row, _full(g)] + wspecs,
        out_specs=[row, ff, ff, pl.BlockSpec((1, d), lambda i, kk: (0, 0))],
        out_shape=[jax.ShapeDtypeStruct((t, d), F32), jax.ShapeDtypeStruct((t, nff * fc), BF16),
                   jax.ShapeDtypeStruct((t, nff * fc), BF16), jax.ShapeDtypeStruct((1, d), F32)],
        scratch_shapes=[pltpu.VMEM((tm, d), F32), pltpu.VMEM((tm, d), BF16)],
        compiler_params=_params("arbitrary", "arbitrary"),
    )(dh, h, hn, g, wup, wdown)


def _matmul_tn(x, y, name):
    t, k1 = x.shape
    n = y.shape[1]
    tt = _row_tile(t)
    tn = min(n, max(512, TN_ACC_BYTES // (4 * k1)))
    steps = t // tt

    def body(x_ref, y_ref, o_ref, acc):
        @pl.when(pl.program_id(1) == 0)
        def _():
            acc[...] = jnp.zeros_like(acc)

        acc[...] += _dot_tn(x_ref[...].astype(BF16), y_ref[...].astype(BF16))

        @pl.when(pl.program_id(1) == steps - 1)
        def _():
            o_ref[...] = acc[...].astype(BF16)

    return pl.pallas_call(
        body, name=name, grid=(n // tn, steps),
        in_specs=[pl.BlockSpec((tt, k1), lambda j, r: (r, 0)), pl.BlockSpec((tt, tn), lambda j, r: (r, j))],
        out_specs=pl.BlockSpec((k1, tn), lambda j, r: (0, j)),
        out_shape=jax.ShapeDtypeStruct((k1, n), BF16),
        scratch_shapes=[pltpu.VMEM((k1, tn), F32)],
        compiler_params=_params("arbitrary", "arbitrary"),
    )(x, y)


def _outproj_bwd(dh, wa, wb):
    t, d = dh.shape
    tm = _row_tile(t, ROW_TILE_WIDE)
    na, nbw = wa.shape[0], wb.shape[0]

    def body(dh_ref, wa_ref, wb_ref, da_ref, db_ref):
        x = dh_ref[...].astype(BF16)
        da_ref[...] = _dot_nt(x, wa_ref[...])
        db_ref[...] = _dot_nt(x, wb_ref[...]).astype(BF16)

    row = lambda w: pl.BlockSpec((tm, w), lambda i: (i, 0))
    return pl.pallas_call(
        body, name="outproj_bwd", grid=(t // tm,),
        in_specs=[row(d), _full(wa), _full(wb)], out_specs=[row(na), row(nbw)],
        out_shape=[jax.ShapeDtypeStruct((t, na), F32), jax.ShapeDtypeStruct((t, nbw), BF16)],
        compiler_params=_params("arbitrary"),
    )(dh, wa, wb)


def _attn_bwd(q, k, v, do, nb, exchange=None):
    t, aw = q.shape
    lp = t // nb
    nblk = lp // BLK
    pp = min(ATT_PAIRS_BWD, aw // HW)
    cols = [slice(p * HW, (p + 1) * HW) for p in range(pp)]
    ncol = aw // (pp * HW)
    nsteps = nb * ncol
    max_windows = (nblk + ATT_NW - 1) // ATT_NW

    def body(q_ref, k_ref, v_ref, do_ref, *rest):
        if exchange is None:
            dq_ref, dk_ref, dv_ref, dk_acc, dv_acc, g_s, sig_s = rest
        else:
            ne = len(exchange)
            x_refs, (dq_ref, dk_ref, dv_ref), e_refs = rest[:ne], rest[ne:ne + 3], rest[ne + 3:2 * ne + 3]
            dk_acc, dv_acc, g_s, sig_s = rest[2 * ne + 3:2 * ne + 7]
            sems = rest[2 * ne + 7:]
            step = pl.program_id(0) * ncol + pl.program_id(1)
            start, finish = _all_of([_exchange_phases(x_refs[j], e_refs[j], *sems[3 * j:3 * j + 3]) for j in range(ne)])
            pl.when(step == 0)(start)
        c, cmr, kr, kc, heads = _attn_consts()
        m_from = _tri(kr >= kc)
        m_before = _tri(kr < kc)
        dk_acc[...] = jnp.zeros_like(dk_acc)
        dv_acc[...] = jnp.zeros_like(dv_acc)

        def qblock(i, _):
            qs = [_stack_heads(q_ref[_rows(i), cs] * ATT_SCALE, heads) for cs in cols]
            dos = [_stack_heads(do_ref[_rows(i), cs], heads) for cs in cols]
            nwin = (i + ATT_NW) // ATT_NW

            def geometry(s):
                start, end = _window(i, s)
                return (cmr < i * BLK - start) & (c < end - start), pl.ds(pl.multiple_of(start, BLK), ATT_W)

            def down(s, carries):
                valid, keys = geometry(s)
                zs = [_dot_nt(qs[p], k_ref[keys, cs]) for p, cs in enumerate(cols)]
                das = [_dot_nt(dos[p], v_ref[keys, cs]) for p, cs in enumerate(cols)]
                suffix = []
                for p in range(pp):
                    lk = _attn_lk(zs[p], valid)
                    sig_s[p, s] = jnp.exp(zs[p] + lk)
                    suffix.append(_tri_sum_bf16(lk, m_from))
                new_carries = []
                for p, cs in enumerate(cols):
                    a = _attn_weights(zs[p], suffix[p], carries[p], valid)
                    g_s[p, s] = das[p] * a
                    dv_acc[keys, cs] += _dot_tn(a.astype(BF16), dos[p])
                    new_carries.append(carries[p] + suffix[p][:, 0:1])
                return tuple(new_carries)

            lax.fori_loop(0, nwin, down, tuple(jnp.zeros((2 * BLK, 1), F32) for _ in cols))

            def up(ss, st):
                accs, carries = st
                s = nwin - 1 - ss
                valid, keys = geometry(s)
                gs = [g_s[p, s] for p in range(pp)]
                before = [_tri_sum(gs[p], m_before) for p in range(pp)]
                new_accs, new_carries = [], []
                for p, cs in enumerate(cols):
                    g, sig = gs[p], sig_s[p, s]
                    dz = jnp.where(valid, g * (1.0 - sig) - (before[p] + carries[p]) * sig, 0.0).astype(BF16)
                    new_accs.append(accs[p] + _dot(dz, k_ref[keys, cs]))
                    dk_acc[keys, cs] += _dot_tn(dz, qs[p])
                    new_carries.append(carries[p] + jnp.sum(g, axis=1, keepdims=True))
                return tuple(new_accs), tuple(new_carries)

            init = (tuple(jnp.zeros((2 * BLK, HW), F32) for _ in cols), tuple(jnp.zeros((2 * BLK, 1), F32) for _ in cols))
            accs, _ = lax.fori_loop(0, nwin, up, init)
            for p, cs in enumerate(cols):
                dq_ref[_rows(i), cs] = (_unstack_heads(accs[p], heads) * ATT_SCALE).astype(BF16)
            return 0

        lax.fori_loop(0, nblk, qblock, 0)
        dk_ref[...] = dk_acc[...].astype(BF16)
        dv_ref[...] = dv_acc[...].astype(BF16)
        if exchange is not None:
            pl.when(step == nsteps - 1)(finish)

    spec = pl.BlockSpec((lp, pp * HW), lambda s, p: (s, p), pipeline_mode=pl.Buffered(1))
    d_shape = jax.ShapeDtypeStruct((t, aw), BF16)
    scratch = [pltpu.VMEM((lp, pp * HW), F32), pltpu.VMEM((lp, pp * HW), F32),
               pltpu.VMEM((pp, max_windows, 2 * BLK, ATT_W), F32), pltpu.VMEM((pp, max_windows, 2 * BLK, ATT_W), F32)]
    if exchange is None:
        return pl.pallas_call(
            body, name="attn_bwd", grid=(nb, ncol), in_specs=[spec] * 4, out_specs=[spec] * 3, out_shape=[d_shape] * 3,
            scratch_shapes=scratch, compiler_params=_params("arbitrary", "arbitrary"),
        )(q, k, v, do)
    return pl.pallas_call(
        body, name="attn_bwd_exchange", grid=(nb, ncol), in_specs=[spec] * 4 + [ANY] * len(exchange),
        out_specs=[spec] * 3 + [ANY] * len(exchange),
        out_shape=[d_shape] * 3 + [jax.ShapeDtypeStruct(x.shape, x.dtype) for x in exchange],
        scratch_shapes=scratch + COMM_SEMS * len(exchange), compiler_params=_params("arbitrary", "arbitrary"),
    )(q, k, v, do, *exchange)


def _mixer_bwd(ua, dy, wconv, wbd, pscale, nb):
    t = ua.shape[0]
    lp = t // nb
    cw, pw = wconv.shape[1], wbd.shape[0]

    def body(ua_ref, dy_ref, wc_ref, wbd_ref, ps_ref, du_ref, dwc_ref, dps_ref, dwb_ref):
        @pl.when(pl.program_id(0) == 0)
        def _():
            dwc_ref[...] = jnp.zeros_like(dwc_ref)
            dps_ref[...] = jnp.zeros_like(dps_ref)
            dwb_ref[...] = jnp.zeros_like(dwb_ref)

        up = lambda x, n: pltpu.roll(x, lp - n, 0)
        cb, cc, cx = ua_ref[:, 0:cw], ua_ref[:, cw:2 * cw], ua_ref[:, 2 * cw:3 * cw]
        uu = cc * cx
        dyc = dy_ref[:, 0:cw]
        du_ref[:, 0:cw] = (dyc * _conv(uu, wc_ref)).astype(BF16)
        dcv = dyc * cb
        duu = wc_ref[2:3, :] * dcv + wc_ref[1:2, :] * up(dcv, 1) + wc_ref[0:1, :] * up(dcv, 2)
        du_ref[:, cw:2 * cw] = (duu * cx).astype(BF16)
        du_ref[:, 2 * cw:3 * cw] = (duu * cc).astype(BF16)
        dwc_ref[0:1, :] += jnp.sum(dcv * pltpu.roll(uu, 2, 0), axis=0, keepdims=True)
        dwc_ref[1:2, :] += jnp.sum(dcv * pltpu.roll(uu, 1, 0), axis=0, keepdims=True)
        dwc_ref[2:3, :] += jnp.sum(dcv * uu, axis=0, keepdims=True)
        grp, cnt, real = _pool_geometry(lp, pw)
        p = ua_ref[:, 3 * cw:3 * cw + pw]
        pooled = _pooled(p, grp, cnt, real).astype(BF16)
        dyp = dy_ref[:, cw:cw + pw]
        dps_ref[...] += jnp.sum(dyp * _dot(pooled, wbd_ref[...]), axis=0, keepdims=True)
        dpre = (dyp * ps_ref[...]).astype(BF16)
        dwb_ref[...] += _dot_tn(pooled, dpre)
        dpooled = jnp.where(real, _dot_nt(dpre, wbd_ref[...]), 0.0)
        xm = dpooled / cnt
        l2 = xm + up(xm, 1)
        l4 = l2 + up(l2, 2)
        l8 = l4 + up(l4, 4)
        l16 = l8 + up(l8, 8)
        du_ref[:, 3 * cw:3 * cw + pw] = jnp.where(real, _by_group(grp, (l2, l4, l8, l16)) - dpooled, 0.0).astype(BF16)

    seq = lambda w: pl.BlockSpec((lp, w), lambda s: (s, 0))
    return pl.pallas_call(
        body, name="mixer_bwd", grid=(nb,),
        in_specs=[seq(ua.shape[1]), seq(cw + pw), _full(wconv), _full(wbd), _full(pscale)],
        out_specs=[seq(ua.shape[1]), pl.BlockSpec((3, cw), lambda s: (0, 0)), pl.BlockSpec((1, pw), lambda s: (0, 0)),
                   pl.BlockSpec((pw, pw), lambda s: (0, 0))],
        out_shape=[jax.ShapeDtypeStruct(ua.shape, BF16), jax.ShapeDtypeStruct((3, cw), F32),
                   jax.ShapeDtypeStruct((1, pw), F32), jax.ShapeDtypeStruct((pw, pw), F32)],
        compiler_params=_params("arbitrary"),
    )(ua, dy, wconv, wbd, pscale)


def _inproj_bwd(dh, h, g, dua, dq, dk, dv, wa, wq, wk, wv):
    t, d = h.shape
    tm = _row_tile(t, ROW_TILE_WIDE)

    def body(dh_ref, h_ref, g_ref, dua_ref, dq_ref, dk_ref, dv_ref, wa_ref, wq_ref, wk_ref, wv_ref, o_ref, dg_ref):
        @pl.when(pl.program_id(0) == 0)
        def _():
            dg_ref[...] = jnp.zeros_like(dg_ref)

        dhn = (_dot_nt(dua_ref[...], wa_ref[...]) + _dot_nt(dq_ref[...], wq_ref[...])
               + _dot_nt(dk_ref[...], wk_ref[...]) + _dot_nt(dv_ref[...], wv_ref[...]))
        dx, dg = _rms_bwd(dhn, h_ref[...], g_ref[...])
        o_ref[...] = dh_ref[...] + dx
        dg_ref[...] += dg

    row = lambda w: pl.BlockSpec((tm, w), lambda i: (i, 0))
    return pl.pallas_call(
        body, name="inproj_bwd", grid=(t // tm,),
        in_specs=[row(d), row(d), _full(g), row(dua.shape[1]), row(dq.shape[1]), row(dk.shape[1]), row(dv.shape[1]),
                  _full(wa), _full(wq), _full(wk), _full(wv)],
        out_specs=[row(d), pl.BlockSpec((1, d), lambda i: (0, 0))],
        out_shape=[jax.ShapeDtypeStruct((t, d), F32), jax.ShapeDtypeStruct((1, d), F32)],
        compiler_params=_params("arbitrary"),
    )(dh, h, g, dua, dq, dk, dv, wa, wq, wk, wv)


def _part_tile(r):
    if r <= 512:
        return r
    for tr in (256, 128, 64, 32, 16, 8):
        if r % tr == 0:
            return tr
    return r


def _sum_parts(x, name):
    n, r, c = x.shape
    tr = _part_tile(r)

    def body(x_ref, o_ref):
        acc = x_ref[0]
        for p in range(1, n):
            acc = acc + x_ref[p]
        o_ref[...] = acc

    return pl.pallas_call(
        body, name=name, grid=(r // tr,),
        in_specs=[pl.BlockSpec((n, tr, c), lambda i: (0, i, 0))], out_specs=pl.BlockSpec((tr, c), lambda i: (i, 0)),
        out_shape=jax.ShapeDtypeStruct((r, c), F32), compiler_params=_params("arbitrary"),
    )(x)


def _adamw(w, gparts, m, v, name, exchange=None):
    n, r, c = gparts.shape
    tr = _part_tile(r)
    nsteps = r // tr

    def body(w_ref, g_ref, m_ref, v_ref, *rest):
        if exchange is None:
            go_ref, d_ref, mo_ref, vo_ref = rest
        else:
            x_ref, go_ref, d_ref, mo_ref, vo_ref, e_ref, send_sems, recv_sems, local_sem = rest
            start, finish = _exchange_phases(x_ref, e_ref, send_sems, recv_sems, local_sem)
            pl.when(pl.program_id(0) == 0)(start)
        g = g_ref[0].astype(F32)
        for p in range(1, n):
            g = g + g_ref[p].astype(F32)
        go_ref[...] = g
        mm = ADAM_B1 * m_ref[...] + (1.0 - ADAM_B1) * g
        vv = ADAM_B2 * v_ref[...] + (1.0 - ADAM_B2) * jnp.square(g)
        mo_ref[...] = mm
        vo_ref[...] = vv
        m_hat = mm / (1.0 - ADAM_B1 ** ADAM_STEP)
        v_hat = vv / (1.0 - ADAM_B2 ** ADAM_STEP)
        d_ref[...] = -ADAM_LR * (m_hat / (jnp.sqrt(v_hat) + ADAM_EPS) + ADAM_WD * w_ref[...])
        if exchange is not None:
            pl.when(pl.program_id(0) == nsteps - 1)(finish)

    row = pl.BlockSpec((tr, c), lambda i: (i, 0))
    in_specs = [row, pl.BlockSpec((n, tr, c), lambda i: (0, i, 0)), row, row]
    out_shape = [jax.ShapeDtypeStruct((r, c), F32)] * 4
    if exchange is None:
        return pl.pallas_call(
            body, name=name, grid=(nsteps,), in_specs=in_specs, out_specs=[row] * 4, out_shape=out_shape,
            compiler_params=_params("arbitrary"),
        )(w, gparts, m, v)
    return pl.pallas_call(
        body, name=name, grid=(nsteps,), in_specs=in_specs + [ANY], out_specs=[row] * 4 + [ANY],
        out_shape=out_shape + [jax.ShapeDtypeStruct(exchange.shape, exchange.dtype)], scratch_shapes=COMM_SEMS,
        compiler_params=_params("arbitrary"),
    )(w, gparts, m, v, exchange)


MESH = pl.DeviceIdType.MESH
ANY = pl.BlockSpec(memory_space=pl.ANY)


def _all_gather(x, name):
    r, c = x.shape

    def body(x_ref, out_ref, send_sems, recv_sems, local_sem):
        start, forward, finish = _gather_phases(x_ref, out_ref, send_sems, recv_sems, local_sem)
        start()
        forward()
        finish()

    return pl.pallas_call(
        body, name=name, in_specs=[ANY], out_specs=ANY,
        out_shape=jax.ShapeDtypeStruct((N_DEV, r, c), x.dtype), scratch_shapes=COMM_SEMS,
    )(x)


COMM_SEMS = [pltpu.SemaphoreType.DMA((7,)), pltpu.SemaphoreType.DMA((7,)), pltpu.SemaphoreType.DMA(())]


def _gather_phases(x_ref, out_ref, send_sems, recv_sems, local_sem):
    mx, my, mc = lax.axis_index("x"), lax.axis_index("y"), lax.axis_index("c")
    me, sibling = (mx, my, mc), (mx, my, 1 - mc)
    chips = [(1 - mx, my), (mx, 1 - my), (1 - mx, 1 - my)]

    def slot(px, py, pc):
        return out_ref.at[4 * px + 2 * py + pc]

    def copy(kk, block, to, src=None):
        return pltpu.make_async_remote_copy(
            src_ref=slot(*block) if src is None else src, dst_ref=slot(*block),
            send_sem=send_sems.at[kk], recv_sem=recv_sems.at[kk], device_id=to, device_id_type=MESH)

    mine = pltpu.make_async_copy(x_ref, slot(*me), local_sem)
    first = [copy(0, me, sibling, src=x_ref)] + [copy(1 + j, me, (*chip, mc), src=x_ref) for j, chip in enumerate(chips)]
    passed = [copy(4 + j, (*chip, mc), sibling) for j, chip in enumerate(chips)]

    def start():
        mine.start()
        for cp in first:
            cp.start()

    def forward():
        for j, chip in enumerate(chips):
            copy(1 + j, (*chip, mc), me).wait_recv()
            passed[j].start()

    def finish():
        copy(0, sibling, me).wait_recv()
        for j, chip in enumerate(chips):
            copy(4 + j, (*chip, 1 - mc), me).wait_recv()
        for cp in first + passed:
            cp.wait_send()
        mine.wait()

    return start, forward, finish


def _all_of(phase_lists):
    def run(fns):
        def every():
            for fn in fns:
                fn()
        return every
    return tuple(run(fns) for fns in zip(*phase_lists))


def _exchange_phases(x_ref, out_ref, send_sems, recv_sems, local_sem):
    mx, my, mc = lax.axis_index("x"), lax.axis_index("y"), lax.axis_index("c")
    me = 4 * mx + 2 * my + mc
    mine = pltpu.make_async_copy(x_ref.at[me], out_ref.at[me], local_sem)
    copies = []
    for kk in range(1, N_DEV):
        px, py, pc = mx ^ (kk >> 2), my ^ ((kk >> 1) & 1), mc ^ (kk & 1)
        peer = 4 * px + 2 * py + pc
        copies.append((peer, pltpu.make_async_remote_copy(
            src_ref=x_ref.at[peer], dst_ref=out_ref.at[me], send_sem=send_sems.at[kk - 1],
            recv_sem=recv_sems.at[kk - 1], device_id=(px, py, pc), device_id_type=MESH)))

    def start():
        mine.start()
        for _, cp in copies:
            cp.start()

    def finish():
        for kk, (peer, _) in enumerate(copies):
            pltpu.make_async_remote_copy(
                src_ref=x_ref.at[peer], dst_ref=out_ref.at[peer], send_sem=send_sems.at[kk],
                recv_sem=recv_sems.at[kk], device_id=(mx, my, mc), device_id_type=MESH).wait_recv()
        for _, cp in copies:
            cp.wait_send()
        mine.wait()

    return start, finish


def _all_to_all(x, name):
    def body(x_ref, out_ref, send_sems, recv_sems, local_sem):
        start, finish = _exchange_phases(x_ref, out_ref, send_sems, recv_sems, local_sem)
        start()
        finish()

    return pl.pallas_call(
        body, name=name, in_specs=[ANY], out_specs=ANY,
        out_shape=jax.ShapeDtypeStruct(x.shape, x.dtype), scratch_shapes=COMM_SEMS,
    )(x)


def _shards_last(a):
    lead = a.shape[:-1]
    return jnp.moveaxis(a.reshape(*lead, N_DEV, a.shape[-1] // N_DEV), -2, 0)


def _shards_rows(a):
    l, n, c = a.shape
    return jnp.moveaxis(a.reshape(l, N_DEV, n // N_DEV, c), 1, 0)


def _flat_pad(parts, mult):
    flat = jnp.concatenate([p.reshape(-1) for p in parts])
    return jnp.pad(flat, (0, (-flat.shape[0]) % mult))


def _unflatten(flat, shapes):
    out, o = [], 0
    for shp in shapes:
        n = 1
        for s in shp:
            n *= s
        out.append(flat[o:o + n].reshape(shp))
        o += n
    return out


def kernel(x, meta_tokens, g_mix, w_in, w_conv, w_pool, pool_scale, w_out, g_mlp, w_up, w_down, g_final, loss_target, m_meta_tokens, m_g_mix, m_w_in, m_w_conv, m_w_pool, m_pool_scale, m_w_out, m_g_mlp, m_w_up, m_w_down, m_g_final, v_meta_tokens, v_g_mix, v_w_in, v_w_conv, v_w_pool, v_pool_scale, v_w_out, v_g_mlp, v_w_up, v_w_down, v_g_final):
    nb, seq, d = x.shape
    depth = g_mix.shape[0]
    lp = BLK + seq
    t = nb * lp
    cw = w_conv.shape[2] * N_DEV
    pw = pool_scale.shape[1]
    pg = pw // N_POOL_GROUPS
    aw = (w_in.shape[2] * N_DEV - 3 * cw - pw) // 3
    ua_w = 3 * cw + pw
    d_ff = w_up.shape[2] * N_DEV
    me = 4 * lax.axis_index("x") + 2 * lax.axis_index("y") + lax.axis_index("c")

    def rows(a):
        return a.reshape(-1, a.shape[-1])

    def gathered_w_in(g, layers):
        return jnp.moveaxis(g.reshape(N_DEV, layers, d, -1), 0, 2).reshape(layers, d, -1)

    w_in0_f = gathered_w_in(_all_gather(rows(w_in[:1]).astype(BF16), "gather_w_in0"), 1)
    small_in = _all_gather(_flat_pad([meta_tokens, w_conv], 8 * BLK).reshape(-1, BLK), "gather_small_weights").reshape(N_DEV, -1)
    meta_f = jnp.moveaxis(small_in[:, :meta_tokens.size].reshape(N_DEV, N_META, d // N_DEV), 0, 1).reshape(N_META, d)
    w_conv_f = small_in[:, meta_tokens.size:meta_tokens.size + w_conv.size].reshape((N_DEV,) + w_conv.shape)
    w_conv_f = jnp.moveaxis(w_conv_f, 0, 2).reshape(depth, w_conv.shape[1], cw)

    def split_w_in(w):
        return w[:, :ua_w], w[:, ua_w:ua_w + aw], w[:, ua_w + aw:ua_w + 2 * aw], w[:, ua_w + 2 * aw:]

    wbd = jnp.zeros((depth, pw, pw), F32)
    for gi in range(N_POOL_GROUPS):
        wbd = wbd.at[:, gi * pg:(gi + 1) * pg, gi * pg:(gi + 1) * pg].set(w_pool[:, gi])
    wbd = wbd.astype(BF16)

    h = jnp.concatenate([jnp.zeros((nb, PAD, d), F32), jnp.broadcast_to(meta_f[None], (nb, N_META, d)), x], axis=1).reshape(t, d)
    saved, w_in_parts = [], [None] * depth
    w_in_parts[0] = split_w_in(w_in0_f[0])
    for l in range(depth):
        wa, wq, wk, wv = w_in_parts[l]
        ua, q, k, v, hn1 = _inproj_fwd(h, g_mix[l][None], wa, wq, wk, wv)
        ycp = _mixer_fwd(ua, w_conv_f[l], wbd[l], pool_scale[l][None], nb)
        if l == 0:
            yat, g_in, g_out, g_up, g_down = _attn_fwd(
                q, k, v, nb, gather=[rows(a).astype(BF16) for a in (w_in[1:], w_out, w_up, w_down)])
            w_in_rest = gathered_w_in(g_in, depth - 1)
            for ll in range(1, depth):
                w_in_parts[ll] = split_w_in(w_in_rest[ll - 1])
            w_out_f = jnp.moveaxis(g_out.reshape((N_DEV,) + w_out.shape), 0, 1).reshape(depth, -1, d)
            wo_a, wo_b = w_out_f[:, :cw + pw], w_out_f[:, cw + pw:]
            w_up_f = g_up.reshape((N_DEV,) + w_up.shape)
            w_down_f = g_down.reshape((N_DEV,) + w_down.shape)
        else:
            yat = _attn_fwd(q, k, v, nb)
        h_mid =_outproj(h, ycp, yat, wo_a[l], wo_b[l])
        h_next, hn2 = _mlp_fwd(h_mid, g_mlp[l][None], w_up_f, w_down_f, l)
        saved.append((h, ua, q, k, v, hn1, ycp, yat, h_mid, hn2))
        h = h_next

    loss_part, dh, dg_final = _loss_head(h, g_final[None], loss_target.reshape(nb * seq, d), nb)

    g_gmix, g_gmlp, g_wconv, g_pscale, g_wpool = [None] * depth, [None] * depth, [None] * depth, [None] * depth, [None] * depth
    g_win, g_wout, g_wup, g_wdown = [None] * depth, [None] * depth, [None] * depth, [None] * depth
    for l in reversed(range(depth)):
        h_in, ua, q, k, v, hn1, ycp, yat, h_mid, hn2 = saved[l]
        wa, wq, wk, wv = w_in_parts[l]
        dh_mid, act, dm, g_gmlp[l] = _mlp_bwd(dh, h_mid, hn2, g_mlp[l][None], w_up_f, w_down_f, l)
        g_wup[l] = _matmul_tn(hn2, dm, "grad_w_up")
        g_wdown[l] = _matmul_tn(act, dh, "grad_w_down")
        dycp, do = _outproj_bwd(dh_mid, wo_a[l], wo_b[l])
        g_wout[l] = jnp.concatenate([_matmul_tn(ycp, dh_mid, "grad_w_out_a"), _matmul_tn(yat, dh_mid, "grad_w_out_b")], axis=0)
        if l == 0:
            send = [_shards_last(jnp.stack(g_win[1:])), _shards_rows(jnp.stack(g_wout)),
                    _shards_last(jnp.stack(g_wup)), _shards_rows(jnp.stack(g_wdown))]
            dq, dk, dv, p_in, p_out, p_up, p_down = _attn_bwd(
                q, k, v, do, nb, exchange=[s.reshape(N_DEV, -1, s.shape[-1]) for s in send])
        else:
            dq, dk, dv = _attn_bwd(q, k, v, do, nb)
        dua, g_wconv[l], g_pscale[l], dwb = _mixer_bwd(ua, dycp, w_conv_f[l], wbd[l], pool_scale[l][None], nb)
        g_wpool[l] = jnp.stack([dwb[gi * pg:(gi + 1) * pg, gi * pg:(gi + 1) * pg] for gi in range(N_POOL_GROUPS)])
        dh, g_gmix[l] = _inproj_bwd(dh_mid, h_in, g_mix[l][None], dua, dq, dk, dv, wa, wq, wk, wv)
        g_win[l] = jnp.concatenate([_matmul_tn(hn1, dua, "grad_w_in_a"), _matmul_tn(hn1, dq, "grad_w_in_q"),
                                    _matmul_tn(hn1, dk, "grad_w_in_k"), _matmul_tn(hn1, dv, "grad_w_in_v")], axis=1)
    dh3 = dh.reshape(nb, lp, d)
    grad_x = dh3[:, BLK:]
    g_meta = _sum_parts(dh3[:, PAD:BLK], "sum_meta_grad")

    send_in0 = _shards_last(g_win[0][None]).reshape(N_DEV, -1, w_in.shape[-1])
    half = send_in0.shape[1] // 2
    o_out = _adamw(rows(w_out), p_out, rows(m_w_out), rows(v_w_out), "adamw_w_out")
    *o_up, p_in0a = _adamw(rows(w_up), p_up, rows(m_w_up), rows(v_w_up), "adamw_w_up_exchange", exchange=send_in0[:, :half])
    *o_down, p_in0b = _adamw(rows(w_down), p_down, rows(m_w_down), rows(v_w_down), "adamw_w_down_exchange",
                             exchange=send_in0[:, half:])
    o_in1 = _adamw(rows(w_in[1:]), p_in, rows(m_w_in[1:]), rows(v_w_in[1:]), "adamw_w_in_rest")
    o_in0 = _adamw(rows(w_in[:1]), jnp.concatenate([p_in0a, p_in0b], axis=1), rows(m_w_in[:1]), rows(v_w_in[:1]), "adamw_w_in0")
    (gr_win, gr_wout, gr_wup, gr_wdown), (de_win, de_wout, de_wup, de_wdown), (nm_win, nm_wout, nm_wup, nm_wdown), \
        (nv_win, nv_wout, nv_wup, nv_wdown) = [
            [jnp.concatenate([i0, i1], axis=0).reshape(w_in.shape), o.reshape(w_out.shape), u.reshape(w_up.shape), dn.reshape(w_down.shape)]
            for i0, i1, o, u, dn in zip(o_in0, o_in1, o_out, o_up, o_down)]

    small_full = [g_meta, jnp.concatenate(g_gmix), jnp.stack(g_wconv), jnp.stack(g_wpool), jnp.concatenate(g_pscale),
                  jnp.concatenate(g_gmlp), dg_final.reshape(-1)]
    small_shapes = [a.shape for a in small_full]
    small_sum = _sum_parts(_all_gather(_flat_pad(small_full, 8 * BLK).reshape(-1, BLK), "gather_small_grads"), "sum_small_grads")
    gr_meta, gr_gmix, gr_wconv, gr_wpool, gr_pscale, gr_gmlp, gr_gfinal = _unflatten(small_sum.reshape(-1), small_shapes)
    gr_meta = lax.dynamic_slice_in_dim(gr_meta, me * (d // N_DEV), d // N_DEV, axis=1)
    gr_wconv = lax.dynamic_slice_in_dim(gr_wconv, me * (cw // N_DEV), cw // N_DEV, axis=2)
    small_g = [gr_meta, gr_gmix, gr_wconv, gr_wpool, gr_pscale, gr_gmlp, gr_gfinal]
    local_shapes = [a.shape for a in small_g]
    pack_small = lambda parts_: _flat_pad(parts_, 8 * BLK).reshape(-1, BLK)
    small = _adamw(pack_small([meta_tokens, g_mix, w_conv, w_pool, pool_scale, g_mlp, g_final]), pack_small(small_g)[None],
                   pack_small([m_meta_tokens, m_g_mix, m_w_conv, m_w_pool, m_pool_scale, m_g_mlp, m_g_final]),
                   pack_small([v_meta_tokens, v_g_mix, v_w_conv, v_w_pool, v_pool_scale, v_g_mlp, v_g_final]), "adamw_small")
    _, de_s, nm_s, nv_s = [_unflatten(b.reshape(-1), local_shapes) for b in small]

    loss = lax.psum(loss_part[0, 0], ("x", "y", "c"))

    def ordered(meta, gmix, wconv, wpool, pscale, gmlp, gfinal, win, wout, wup, wdown):
        return [meta, gmix, win, wconv, wpool, pscale, wout, gmlp, wup, wdown, gfinal]

    grads = ordered(*small_g, gr_win, gr_wout, gr_wup, gr_wdown)
    deltas = ordered(*de_s, de_win, de_wout, de_wup, de_wdown)
    new_m = ordered(*nm_s, nm_win, nm_wout, nm_wup, nm_wdown)
    new_v = ordered(*nv_s, nv_win, nv_wout, nv_wup, nv_wdown)
    return (loss, grad_x, *grads, *deltas, *new_m, *new_v)
```

```python
import functools

import jax
import jax.numpy as jnp
from jax import lax
from jax.experimental import pallas as pl
from jax.experimental.pallas import tpu as pltpu

F32 = jnp.float32
BF16 = jnp.bfloat16

N_DEV = 8
N_META = 16
HEAD_DIM = 64
BLK = 128
PAD = BLK - N_META
N_POOL_GROUPS = 4
EPS = 1e-6

ADAM_LR = 0.001
ADAM_B1 = 0.9
ADAM_B2 = 0.999
ADAM_EPS = 1e-08
ADAM_WD = 0.01
ADAM_STEP = 10

V7X_VMEM_LIMIT = 56 * 1024 * 1024


def _params(*sem):
    return pltpu.CompilerParams(dimension_semantics=sem, vmem_limit_bytes=V7X_VMEM_LIMIT)


SUBLANES = 8
ROW_TILE = 512
ROW_TILE_MLP = 1152
ROW_TILE_WIDE = 1152
TN_ACC_BYTES = 8 * 1024 * 1024


def _row_tile(t, cap=ROW_TILE):
    for tm in range(min(cap, t) // SUBLANES * SUBLANES, 0, -SUBLANES):
        if t % tm == 0:
            return tm
    raise ValueError(f"row count {t} is not a multiple of 8")


def _full(a):
    nd = a.ndim
    return pl.BlockSpec(a.shape, lambda *_: (0,) * nd)


def _dot(a, b):
    return jnp.dot(a, b, preferred_element_type=F32)


def _dot_nt(a, b):
    return lax.dot_general(a, b, (((1,), (1,)), ((), ())), preferred_element_type=F32)


def _dot_tn(a, b):
    return lax.dot_general(a, b, (((0,), (0,)), ((), ())), preferred_element_type=F32)


def _rms_fwd(x, g):
    rstd = lax.rsqrt(jnp.mean(x * x, axis=-1, keepdims=True) + EPS)
    return x * rstd * g


def _rms_bwd(dy, x, g):
    rstd = lax.rsqrt(jnp.mean(x * x, axis=-1, keepdims=True) + EPS)
    xhat = x * rstd
    dxhat = dy * g
    dx = rstd * (dxhat - xhat * jnp.mean(dxhat * xhat, axis=-1, keepdims=True))
    return dx, jnp.sum(dy * xhat, axis=0, keepdims=True)


def _inproj_fwd(h, g, wa, wq, wk, wv):
    t, d = h.shape
    tm = _row_tile(t, ROW_TILE_WIDE)
    ua_w, aw = wa.shape[1], wq.shape[1]

    def body(h_ref, g_ref, wa_ref, wq_ref, wk_ref, wv_ref, ua_ref, q_ref, k_ref, v_ref, hn_ref):
        hn = _rms_fwd(h_ref[...], g_ref[...]).astype(BF16)
        hn_ref[...] = hn
        ua_ref[...] = _dot(hn, wa_ref[...])
        q_ref[...] = _dot(hn, wq_ref[...]).astype(BF16)
        k_ref[...] = _dot(hn, wk_ref[...]).astype(BF16)
        v_ref[...] = _dot(hn, wv_ref[...]).astype(BF16)

    row = lambda w: pl.BlockSpec((tm, w), lambda i: (i, 0))
    return pl.pallas_call(
        body, name="inproj_fwd", grid=(t // tm,),
        in_specs=[row(d), _full(g), _full(wa), _full(wq), _full(wk), _full(wv)],
        out_specs=[row(ua_w), row(aw), row(aw), row(aw), row(d)],
        out_shape=[jax.ShapeDtypeStruct((t, ua_w), F32)] + [jax.ShapeDtypeStruct((t, aw), BF16)] * 3
        + [jax.ShapeDtypeStruct((t, d), BF16)],
        compiler_params=_params("arbitrary"),
    )(h, g, wa, wq, wk, wv)


def _pool_geometry(lp, pw):
    pg = pw // N_POOL_GROUPS
    row = lax.broadcasted_iota(jnp.int32, (lp, pw), 0)
    lane = lax.broadcasted_iota(jnp.int32, (lp, pw), 1)
    grp = [(lane >= g * pg) & (lane < (g + 1) * pg) for g in range(N_POOL_GROUPS)]
    wlen = jnp.where(grp[0], 2, jnp.where(grp[1], 4, jnp.where(grp[2], 8, 16)))
    cnt = jnp.clip(row - (PAD - 1), 1, wlen).astype(F32)
    return grp, cnt, row >= PAD


def _by_group(grp, vals):
    return jnp.where(grp[0], vals[0], jnp.where(grp[1], vals[1], jnp.where(grp[2], vals[2], vals[3])))


def _pooled(p, grp, cnt, real):
    s2 = p + pltpu.roll(p, 1, 0)
    s4 = s2 + pltpu.roll(s2, 2, 0)
    s8 = s4 + pltpu.roll(s4, 4, 0)
    s16 = s8 + pltpu.roll(s8, 8, 0)
    return jnp.where(real, _by_group(grp, (s2, s4, s8, s16)) / cnt - p, 0.0)


def _conv(uu, wc_ref):
    return wc_ref[2:3, :] * uu + wc_ref[1:2, :] * pltpu.roll(uu, 1, 0) + wc_ref[0:1, :] * pltpu.roll(uu, 2, 0)


def _mixer_fwd(ua, wconv, wbd, pscale, nb):
    t = ua.shape[0]
    lp = t // nb
    cw, pw = wconv.shape[1], wbd.shape[0]

    def body(ua_ref, wc_ref, wbd_ref, ps_ref, y_ref):
        cb = ua_ref[:, 0:cw]
        uu = ua_ref[:, cw:2 * cw] * ua_ref[:, 2 * cw:3 * cw]
        y_ref[:, 0:cw] = (cb * _conv(uu, wc_ref)).astype(BF16)
        grp, cnt, real = _pool_geometry(lp, pw)
        pooled = _pooled(ua_ref[:, 3 * cw:3 * cw + pw], grp, cnt, real)
        y_ref[:, cw:cw + pw] = (_dot(pooled.astype(BF16), wbd_ref[...]) * ps_ref[...]).astype(BF16)

    return pl.pallas_call(
        body, name="mixer_fwd", grid=(nb,),
        in_specs=[pl.BlockSpec((lp, ua.shape[1]), lambda s: (s, 0)), _full(wconv), _full(wbd), _full(pscale)],
        out_specs=pl.BlockSpec((lp, cw + pw), lambda s: (s, 0)),
        out_shape=jax.ShapeDtypeStruct((t, cw + pw), BF16),
        compiler_params=_params("arbitrary"),
    )(ua, wconv, wbd, pscale)


ATT_NW = 2
ATT_W = ATT_NW * BLK
ATT_PAIRS_FWD = 4
ATT_PAIRS_BWD = 4
HW = 2 * HEAD_DIM
ATT_SCALE = HEAD_DIM ** -0.5


def _attn_lk(z, valid):
    nz = -z
    return jnp.where(valid, jnp.minimum(nz, 0.0) - jnp.log(1.0 + jnp.exp(jnp.minimum(z, nz))), 0.0)


def _tri_sum_bf16(x, tri):
    return _dot(x.astype(BF16), tri)


def _tri_sum(x, tri):
    hi = x.astype(BF16)
    lo = (x - hi.astype(F32)).astype(BF16)
    return _dot(hi, tri) + _dot(lo, tri)


def _attn_weights(z, suffix, carry, valid):
    return jnp.where(valid, jnp.exp(z + suffix + carry), 0.0)


def _attn_consts():
    r = lax.broadcasted_iota(jnp.int32, (2 * BLK, ATT_W), 0)
    c = lax.broadcasted_iota(jnp.int32, (2 * BLK, ATT_W), 1)
    cmr = c - (r & (BLK - 1))
    kr = lax.broadcasted_iota(jnp.int32, (ATT_W, ATT_W), 0)
    kc = lax.broadcasted_iota(jnp.int32, (ATT_W, ATT_W), 1)
    lane = lax.broadcasted_iota(jnp.int32, (1, HW), 1)
    heads = (lane < HEAD_DIM, lane >= HEAD_DIM)
    return c, cmr, kr, kc, heads


def _tri(cond):
    return jnp.where(cond, 1.0, 0.0).astype(BF16)


def _rows(i):
    return pl.ds(pl.multiple_of(i * BLK, BLK), BLK)


def _stack_heads(x, heads):
    zero = jnp.zeros_like(x)
    return jnp.concatenate([jnp.where(heads[0], x, zero), jnp.where(heads[1], x, zero)], axis=0)


def _unstack_heads(x, heads):
    return jnp.where(heads[0], x[:BLK], x[BLK:])


def _window(i, s):
    start = jnp.maximum(i + 1 - ATT_NW * (s + 1), 0) * BLK
    return start, (i + 1 - ATT_NW * s) * BLK


def _attn_fwd(q, k, v, nb, gather=None):
    t, aw = q.shape
    lp = t // nb
    nblk = lp // BLK
    pp = min(ATT_PAIRS_FWD, aw // HW)
    cols = [slice(p * HW, (p + 1) * HW) for p in range(pp)]
    ncol = aw // (pp * HW)
    nsteps = nb * ncol

    def body(q_ref, k_ref, v_ref, *rest):
        if gather is None:
            (o_ref,) = rest
        else:
            ng = len(gather)
            x_refs, o_ref, g_refs, sems = rest[:ng], rest[ng], rest[ng + 1:2 * ng + 1], rest[2 * ng + 1:]
            step = pl.program_id(0) * ncol + pl.program_id(1)
            start, forward, finish = _all_of([_gather_phases(x_refs[j], g_refs[j], *sems[3 * j:3 * j + 3]) for j in range(ng)])
            pl.when(step == 0)(start)
            pl.when(step == nsteps - 1)(forward)
        c, cmr, kr, kc, heads = _attn_consts()
        m_from = _tri(kr >= kc)

        def qblock(i, _):
            qs = [_stack_heads(q_ref[_rows(i), cs] * ATT_SCALE, heads) for cs in cols]

            def window(s, st):
                accs, carries = st
                start, end = _window(i, s)
                keys = pl.ds(pl.multiple_of(start, BLK), ATT_W)
                valid = (cmr < i * BLK - start) & (c < end - start)
                zs = [_dot_nt(qs[p], k_ref[keys, cs]) for p, cs in enumerate(cols)]
                suffix = [_tri_sum_bf16(_attn_lk(zs[p], valid), m_from) for p in range(pp)]
                new_accs, new_carries = [], []
                for p, cs in enumerate(cols):
                    a = _attn_weights(zs[p], suffix[p], carries[p], valid)
                    new_accs.append(accs[p] + _dot(a.astype(BF16), v_ref[keys, cs]))
                    new_carries.append(carries[p] + suffix[p][:, 0:1])
                return tuple(new_accs), tuple(new_carries)

            init = (tuple(jnp.zeros((2 * BLK, HW), F32) for _ in cols), tuple(jnp.zeros((2 * BLK, 1), F32) for _ in cols))
            accs, _ = lax.fori_loop(0, (i + ATT_NW) // ATT_NW, window, init)
            for p, cs in enumerate(cols):
                o_ref[_rows(i), cs] = _unstack_heads(accs[p], heads).astype(BF16)
            return 0

        lax.fori_loop(0, nblk, qblock, 0)
        if gather is not None:
            pl.when(step == nsteps - 1)(finish)

    spec = pl.BlockSpec((lp, pp * HW), lambda s, p: (s, p))
    y_shape = jax.ShapeDtypeStruct((t, aw), BF16)
    if gather is None:
        return pl.pallas_call(
            body, name="attn_fwd", grid=(nb, ncol), in_specs=[spec, spec, spec], out_specs=spec, out_shape=y_shape,
            compiler_params=_params("arbitrary", "arbitrary"),
        )(q, k, v)
    return pl.pallas_call(
        body, name="attn_fwd_gather", grid=(nb, ncol), in_specs=[spec, spec, spec] + [ANY] * len(gather),
        out_specs=[spec] + [ANY] * len(gather),
        out_shape=[y_shape] + [jax.ShapeDtypeStruct((N_DEV,) + x.shape, x.dtype) for x in gather],
        scratch_shapes=COMM_SEMS * len(gather), compiler_params=_params("arbitrary", "arbitrary"),
    )(q, k, v, *gather)


def _outproj(h, ya, yb, wa, wb):
    t, d = h.shape
    tm = _row_tile(t, ROW_TILE_WIDE)

    def body(h_ref, ya_ref, yb_ref, wa_ref, wb_ref, o_ref):
        o_ref[...] = h_ref[...] + _dot(ya_ref[...], wa_ref[...]) + _dot(yb_ref[...], wb_ref[...])

    row = lambda w: pl.BlockSpec((tm, w), lambda i: (i, 0))
    return pl.pallas_call(
        body, name="outproj", grid=(t // tm,),
        in_specs=[row(d), row(ya.shape[1]), row(yb.shape[1]), _full(wa), _full(wb)],
        out_specs=row(d), out_shape=jax.ShapeDtypeStruct((t, d), F32),
        compiler_params=_params("arbitrary"),
    )(h, ya, yb, wa, wb)


def _mlp_weight_specs(wup, wdown, layer):
    nff, _, d, fc = wup.shape
    return nff, fc, [pl.BlockSpec((None, None, d, fc), lambda i, kk: (kk, layer, 0, 0)),
                     pl.BlockSpec((None, None, fc, d), lambda i, kk: (kk, layer, 0, 0))]


def _mlp_fwd(h, g, wup, wdown, layer):
    t, d = h.shape
    tm = _row_tile(t, ROW_TILE_MLP)
    nff, fc, wspecs = _mlp_weight_specs(wup, wdown, layer)

    def body(h_ref, g_ref, wu_ref, wd_ref, o_ref, hn_ref, acc):
        kk = pl.program_id(1)

        @pl.when(kk == 0)
        def _():
            x = h_ref[...]
            hn_ref[...] = _rms_fwd(x, g_ref[...]).astype(BF16)
            acc[...] = x

        m = _dot(hn_ref[...], wu_ref[...])
        a = jnp.square(jnp.maximum(m, 0.0)).astype(BF16)
        acc[...] += _dot(a, wd_ref[...])

        @pl.when(kk == nff - 1)
        def _():
            o_ref[...] = acc[...]

    row = pl.BlockSpec((tm, d), lambda i, kk: (i, 0))
    return pl.pallas_call(
        body, name="mlp_fwd", grid=(t // tm, nff),
        in_specs=[row, _full(g)] + wspecs,
        out_specs=[row, row],
        out_shape=[jax.ShapeDtypeStruct((t, d), F32), jax.ShapeDtypeStruct((t, d), BF16)],
        scratch_shapes=[pltpu.VMEM((tm, d), F32)],
        compiler_params=_params("arbitrary", "arbitrary"),
    )(h, g, wup, wdown)


def _loss_head(h, g, target, nb):
    t, d = h.shape
    nblk = t // nb // BLK
    nx = nblk - 1

    def body(h_ref, g_ref, t_ref, loss_ref, dh_ref, dg_ref):
        s, i = pl.program_id(0), pl.program_id(1)

        @pl.when((s == 0) & (i == 0))
        def _():
            loss_ref[...] = jnp.zeros_like(loss_ref)
            dg_ref[...] = jnp.zeros_like(dg_ref)

        @pl.when(i == 0)
        def _():
            dh_ref[...] = jnp.zeros_like(dh_ref)

        @pl.when(i > 0)
        def _():
            x, gg = h_ref[...], g_ref[...]
            err = _rms_fwd(x, gg) - t_ref[...]
            loss_ref[...] += jnp.sum(err * err) * (0.5 / d)
            dx, dg = _rms_bwd(err * (1.0 / d), x, gg)
            dh_ref[...] = dx
            dg_ref[...] += dg

    hspec = pl.BlockSpec((BLK, d), lambda s, i: (s * nblk + i, 0))
    return pl.pallas_call(
        body, name="loss_head", grid=(nb, nblk),
        in_specs=[hspec, _full(g), pl.BlockSpec((BLK, d), lambda s, i: (s * nx + jnp.maximum(i - 1, 0), 0))],
        out_specs=[pl.BlockSpec((1, BLK), lambda s, i: (0, 0)), hspec, pl.BlockSpec((1, d), lambda s, i: (0, 0))],
        out_shape=[jax.ShapeDtypeStruct((1, BLK), F32), jax.ShapeDtypeStruct((t, d), F32), jax.ShapeDtypeStruct((1, d), F32)],
        compiler_params=_params("arbitrary", "arbitrary"),
    )(h, g, target)


def _mlp_bwd(dh, h, hn, g, wup, wdown, layer):
    t, d = h.shape
    tm = _row_tile(t, ROW_TILE_MLP)
    nff, fc, wspecs = _mlp_weight_specs(wup, wdown, layer)

    def body(dh_ref, h_ref, hn_ref, g_ref, wu_ref, wd_ref, dhm_ref, a_ref, dm_ref, dg_ref, dhn, dhb):
        i, kk = pl.program_id(0), pl.program_id(1)

        @pl.when((i == 0) & (kk == 0))
        def _():
            dg_ref[...] = jnp.zeros_like(dg_ref)

        @pl.when(kk == 0)
        def _():
            dhb[...] = dh_ref[...].astype(BF16)
            dhn[...] = jnp.zeros_like(dhn)

        r = jnp.maximum(_dot(hn_ref[...], wu_ref[...]), 0.0)
        a_ref[...] = (r * r).astype(BF16)
        dm = (_dot_nt(dhb[...], wd_ref[...]) * (2.0 * r)).astype(BF16)
        dm_ref[...] = dm
        dhn[...] += _dot_nt(dm, wu_ref[...])

        @pl.when(kk == nff - 1)
        def _():
            dx, dg = _rms_bwd(dhn[...], h_ref[...], g_ref[...])
            dhm_ref[...] = dh_ref[...] + dx
            dg_ref[...] += dg

    row = pl.BlockSpec((tm, d), lambda i, kk: (i, 0))
    ff = pl.BlockSpec((tm, fc), lambda i, kk: (i, kk))
    return pl.pallas_call(
        body, name="mlp_bwd", grid=(t // tm, nff),
        in_specs=[row, row, row, _full(g)] + wspecs,
        out_specs=[row, ff, ff, pl.BlockSpec((1, d), lambda i, kk: (0, 0))],
        out_shape=[jax.ShapeDtypeStruct((t, d), F32), jax.ShapeDtypeStruct((t, nff * fc), BF16),
                   jax.ShapeDtypeStruct((t, nff * fc), BF16), jax.ShapeDtypeStruct((1, d), F32)],
        scratch_shapes=[pltpu.VMEM((tm, d), F32), pltpu.VMEM((tm, d), BF16)],
        compiler_params=_params("arbitrary", "arbitrary"),
    )(dh, h, hn, g, wup, wdown)


def _matmul_tn(x, y, name):
    t, k1 = x.shape
    n = y.shape[1]
    tt = _row_tile(t)
    tn = min(n, max(512, TN_ACC_BYTES // (4 * k1)))
    steps = t // tt

    def body(x_ref, y_ref, o_ref, acc):
        @pl.when(pl.program_id(1) == 0)
        def _():
            acc[...] = jnp.zeros_like(acc)

        acc[...] += _dot_tn(x_ref[...].astype(BF16), y_ref[...].astype(BF16))

        @pl.when(pl.program_id(1) == steps - 1)
        def _():
            o_ref[...] = acc[...].astype(BF16)

    return pl.pallas_call(
        body, name=name, grid=(n // tn, steps),
        in_specs=[pl.BlockSpec((tt, k1), lambda j, r: (r, 0)), pl.BlockSpec((tt, tn), lambda j, r: (r, j))],
        out_specs=pl.BlockSpec((k1, tn), lambda j, r: (0, j)),
        out_shape=jax.ShapeDtypeStruct((k1, n), BF16),
        scratch_shapes=[pltpu.VMEM((k1, tn), F32)],
        compiler_params=_params("arbitrary", "arbitrary"),
    )(x, y)


def _outproj_bwd(dh, wa, wb):
    t, d = dh.shape
    tm = _row_tile(t, ROW_TILE_WIDE)
    na, nbw = wa.shape[0], wb.shape[0]

    def body(dh_ref, wa_ref, wb_ref, da_ref, db_ref):
        x = dh_ref[...].astype(BF16)
        da_ref[...] = _dot_nt(x, wa_ref[...])
        db_ref[...] = _dot_nt(x, wb_ref[...]).astype(BF16)

    row = lambda w: pl.BlockSpec((tm, w), lambda i: (i, 0))
    return pl.pallas_call(
        body, name="outproj_bwd", grid=(t // tm,),
        in_specs=[row(d), _full(wa), _full(wb)], out_specs=[row(na), row(nbw)],
        out_shape=[jax.ShapeDtypeStruct((t, na), F32), jax.ShapeDtypeStruct((t, nbw), BF16)],
        compiler_params=_params("arbitrary"),
    )(dh, wa, wb)


def _attn_bwd(q, k, v, do, nb, exchange=None):
    t, aw = q.shape
    lp = t // nb
    nblk = lp // BLK
    pp = min(ATT_PAIRS_BWD, aw // HW)
    cols = [slice(p * HW, (p + 1) * HW) for p in range(pp)]
    ncol = aw // (pp * HW)
    nsteps = nb * ncol
    max_windows = (nblk + ATT_NW - 1) // ATT_NW

    def body(q_ref, k_ref, v_ref, do_ref, *rest):
        if exchange is None:
            dq_ref, dk_ref, dv_ref, dk_acc, dv_acc, g_s, sig_s = rest
        else:
            ne = len(exchange)
            x_refs, (dq_ref, dk_ref, dv_ref), e_refs = rest[:ne], rest[ne:ne + 3], rest[ne + 3:2 * ne + 3]
            dk_acc, dv_acc, g_s, sig_s = rest[2 * ne + 3:2 * ne + 7]
            sems = rest[2 * ne + 7:]
            step = pl.program_id(0) * ncol + pl.program_id(1)
            start, finish = _all_of([_exchange_phases(x_refs[j], e_refs[j], *sems[3 * j:3 * j + 3]) for j in range(ne)])
            pl.when(step == 0)(start)
        c, cmr, kr, kc, heads = _attn_consts()
        m_from = _tri(kr >= kc)
        m_before = _tri(kr < kc)
        dk_acc[...] = jnp.zeros_like(dk_acc)
        dv_acc[...] = jnp.zeros_like(dv_acc)

        def qblock(i, _):
            qs = [_stack_heads(q_ref[_rows(i), cs] * ATT_SCALE, heads) for cs in cols]
            dos = [_stack_heads(do_ref[_rows(i), cs], heads) for cs in cols]
            nwin = (i + ATT_NW) // ATT_NW

            def geometry(s):
                start, end = _window(i, s)
                return (cmr < i * BLK - start) & (c < end - start), pl.ds(pl.multiple_of(start, BLK), ATT_W)

            def down(s, carries):
                valid, keys = geometry(s)
                zs = [_dot_nt(qs[p], k_ref[keys, cs]) for p, cs in enumerate(cols)]
                das = [_dot_nt(dos[p], v_ref[keys, cs]) for p, cs in enumerate(cols)]
                suffix = []
                for p in range(pp):
                    lk = _attn_lk(zs[p], valid)
                    sig_s[p, s] = jnp.exp(zs[p] + lk)
                    suffix.append(_tri_sum_bf16(lk, m_from))
                new_carries = []
                for p, cs in enumerate(cols):
                    a = _attn_weights(zs[p], suffix[p], carries[p], valid)
                    g_s[p, s] = das[p] * a
                    dv_acc[keys, cs] += _dot_tn(a.astype(BF16), dos[p])
                    new_carries.append(carries[p] + suffix[p][:, 0:1])
                return tuple(new_carries)

            lax.fori_loop(0, nwin, down, tuple(jnp.zeros((2 * BLK, 1), F32) for _ in cols))

            def up(ss, st):
                accs, carries = st
                s = nwin - 1 - ss
                valid, keys = geometry(s)
                gs = [g_s[p, s] for p in range(pp)]
                before = [_tri_sum(gs[p], m_before) for p in range(pp)]
                new_accs, new_carries = [], []
                for p, cs in enumerate(cols):
                    g, sig = gs[p], sig_s[p, s]
                    dz = jnp.where(valid, g * (1.0 - sig) - (before[p] + carries[p]) * sig, 0.0).astype(BF16)
                    new_accs.append(accs[p] + _dot(dz, k_ref[keys, cs]))
                    dk_acc[keys, cs] += _dot_tn(dz, qs[p])
                    new_carries.append(carries[p] + jnp.sum(g, axis=1, keepdims=True))
                return tuple(new_accs), tuple(new_carries)

            init = (tuple(jnp.zeros((2 * BLK, HW), F32) for _ in cols), tuple(jnp.zeros((2 * BLK, 1), F32) for _ in cols))
            accs, _ = lax.fori_loop(0, nwin, up, init)
            for p, cs in enumerate(cols):
                dq_ref[_rows(i), cs] = (_unstack_heads(accs[p], heads) * ATT_SCALE).astype(BF16)
            return 0

        lax.fori_loop(0, nblk, qblock, 0)
        dk_ref[...] = dk_acc[...].astype(BF16)
        dv_ref[...] = dv_acc[...].astype(BF16)
        if exchange is not None:
            pl.when(step == nsteps - 1)(finish)

    spec = pl.BlockSpec((lp, pp * HW), lambda s, p: (s, p), pipeline_mode=pl.Buffered(1))
    d_shape = jax.ShapeDtypeStruct((t, aw), BF16)
    scratch = [pltpu.VMEM((lp, pp * HW), F32), pltpu.VMEM((lp, pp * HW), F32),
               pltpu.VMEM((pp, max_windows, 2 * BLK, ATT_W), F32), pltpu.VMEM((pp, max_windows, 2 * BLK, ATT_W), F32)]
    if exchange is None:
        return pl.pallas_call(
            body, name="attn_bwd", grid=(nb, ncol), in_specs=[spec] * 4, out_specs=[spec] * 3, out_shape=[d_shape] * 3,
            scratch_shapes=scratch, compiler_params=_params("arbitrary", "arbitrary"),
        )(q, k, v, do)
    return pl.pallas_call(
        body, name="attn_bwd_exchange", grid=(nb, ncol), in_specs=[spec] * 4 + [ANY] * len(exchange),
        out_specs=[spec] * 3 + [ANY] * len(exchange),
        out_shape=[d_shape] * 3 + [jax.ShapeDtypeStruct(x.shape, x.dtype) for x in exchange],
        scratch_shapes=scratch + COMM_SEMS * len(exchange), compiler_params=_params("arbitrary", "arbitrary"),
    )(q, k, v, do, *exchange)


def _mixer_bwd(ua, dy, wconv, wbd, pscale, nb):
    t = ua.shape[0]
    lp = t // nb
    cw, pw = wconv.shape[1], wbd.shape[0]

    def body(ua_ref, dy_ref, wc_ref, wbd_ref, ps_ref, du_ref, dwc_ref, dps_ref, dwb_ref):
        @pl.when(pl.program_id(0) == 0)
        def _():
            dwc_ref[...] = jnp.zeros_like(dwc_ref)
            dps_ref[...] = jnp.zeros_like(dps_ref)
            dwb_ref[...] = jnp.zeros_like(dwb_ref)

        up = lambda x, n: pltpu.roll(x, lp - n, 0)
        cb, cc, cx = ua_ref[:, 0:cw], ua_ref[:, cw:2 * cw], ua_ref[:, 2 * cw:3 * cw]
        uu = cc * cx
        dyc = dy_ref[:, 0:cw]
        du_ref[:, 0:cw] = (dyc * _conv(uu, wc_ref)).astype(BF16)
        dcv = dyc * cb
        duu = wc_ref[2:3, :] * dcv + wc_ref[1:2, :] * up(dcv, 1) + wc_ref[0:1, :] * up(dcv, 2)
        du_ref[:, cw:2 * cw] = (duu * cx).astype(BF16)
        du_ref[:, 2 * cw:3 * cw] = (duu * cc).astype(BF16)
        dwc_ref[0:1, :] += jnp.sum(dcv * pltpu.roll(uu, 2, 0), axis=0, keepdims=True)
        dwc_ref[1:2, :] += jnp.sum(dcv * pltpu.roll(uu, 1, 0), axis=0, keepdims=True)
        dwc_ref[2:3, :] += jnp.sum(dcv * uu, axis=0, keepdims=True)
        grp, cnt, real = _pool_geometry(lp, pw)
        p = ua_ref[:, 3 * cw:3 * cw + pw]
        pooled = _pooled(p, grp, cnt, real).astype(BF16)
        dyp = dy_ref[:, cw:cw + pw]
        dps_ref[...] += jnp.sum(dyp * _dot(pooled, wbd_ref[...]), axis=0, keepdims=True)
        dpre = (dyp * ps_ref[...]).astype(BF16)
        dwb_ref[...] += _dot_tn(pooled, dpre)
        dpooled = jnp.where(real, _dot_nt(dpre, wbd_ref[...]), 0.0)
        xm = dpooled / cnt
        l2 = xm + up(xm, 1)
        l4 = l2 + up(l2, 2)
        l8 = l4 + up(l4, 4)
        l16 = l8 + up(l8, 8)
        du_ref[:, 3 * cw:3 * cw + pw] = jnp.where(real, _by_group(grp, (l2, l4, l8, l16)) - dpooled, 0.0).astype(BF16)

    seq = lambda w: pl.BlockSpec((lp, w), lambda s: (s, 0))
    return pl.pallas_call(
        body, name="mixer_bwd", grid=(nb,),
        in_specs=[seq(ua.shape[1]), seq(cw + pw), _full(wconv), _full(wbd), _full(pscale)],
        out_specs=[seq(ua.shape[1]), pl.BlockSpec((3, cw), lambda s: (0, 0)), pl.BlockSpec((1, pw), lambda s: (0, 0)),
                   pl.BlockSpec((pw, pw), lambda s: (0, 0))],
        out_shape=[jax.ShapeDtypeStruct(ua.shape, BF16), jax.ShapeDtypeStruct((3, cw), F32),
                   jax.ShapeDtypeStruct((1, pw), F32), jax.ShapeDtypeStruct((pw, pw), F32)],
        compiler_params=_params("arbitrary"),
    )(ua, dy, wconv, wbd, pscale)


def _inproj_bwd(dh, h, g, dua, dq, dk, dv, wa, wq, wk, wv):
    t, d = h.shape
    tm = _row_tile(t)

    def body(dh_ref, h_ref, g_ref, dua_ref, dq_ref, dk_ref, dv_ref, wa_ref, wq_ref, wk_ref, wv_ref, o_ref, dg_ref):
        @pl.when(pl.program_id(0) == 0)
        def _():
            dg_ref[...] = jnp.zeros_like(dg_ref)

        dhn = (_dot_nt(dua_ref[...], wa_ref[...]) + _dot_nt(dq_ref[...], wq_ref[...])
               + _dot_nt(dk_ref[...], wk_ref[...]) + _dot_nt(dv_ref[...], wv_ref[...]))
        dx, dg = _rms_bwd(dhn, h_ref[...], g_ref[...])
        o_ref[...] = dh_ref[...] + dx
        dg_ref[...] += dg

    row = lambda w: pl.BlockSpec((tm, w), lambda i: (i, 0))
    return pl.pallas_call(
        body, name="inproj_bwd", grid=(t // tm,),
        in_specs=[row(d), row(d), _full(g), row(dua.shape[1]), row(dq.shape[1]), row(dk.shape[1]), row(dv.shape[1]),
                  _full(wa), _full(wq), _full(wk), _full(wv)],
        out_specs=[row(d), pl.BlockSpec((1, d), lambda i: (0, 0))],
        out_shape=[jax.ShapeDtypeStruct((t, d), F32), jax.ShapeDtypeStruct((1, d), F32)],
        compiler_params=_params("arbitrary"),
    )(dh, h, g, dua, dq, dk, dv, wa, wq, wk, wv)


def _part_tile(r):
    if r <= 512:
        return r
    for tr in (256, 128, 64, 32, 16, 8):
        if r % tr == 0:
            return tr
    return r


def _sum_parts(x, name):
    n, r, c = x.shape
    tr = _part_tile(r)

    def body(x_ref, o_ref):
        acc = x_ref[0]
        for p in range(1, n):
            acc = acc + x_ref[p]
        o_ref[...] = acc

    return pl.pallas_call(
        body, name=name, grid=(r // tr,),
        in_specs=[pl.BlockSpec((n, tr, c), lambda i: (0, i, 0))], out_specs=pl.BlockSpec((tr, c), lambda i: (i, 0)),
        out_shape=jax.ShapeDtypeStruct((r, c), F32), compiler_params=_params("arbitrary"),
    )(x)


def _adamw(w, gparts, m, v, name, exchange=None):
    n, r, c = gparts.shape
    tr = _part_tile(r)
    nsteps = r // tr

    def body(w_ref, g_ref, m_ref, v_ref, *rest):
        if exchange is None:
            go_ref, d_ref, mo_ref, vo_ref = rest
        else:
            x_ref, go_ref, d_ref, mo_ref, vo_ref, e_ref, send_sems, recv_sems, local_sem = rest
            start, finish = _exchange_phases(x_ref, e_ref, send_sems, recv_sems, local_sem)
            pl.when(pl.program_id(0) == 0)(start)
        g = g_ref[0].astype(F32)
        for p in range(1, n):
            g = g + g_ref[p].astype(F32)
        go_ref[...] = g
        mm = ADAM_B1 * m_ref[...] + (1.0 - ADAM_B1) * g
        vv = ADAM_B2 * v_ref[...] + (1.0 - ADAM_B2) * jnp.square(g)
        mo_ref[...] = mm
        vo_ref[...] = vv
        m_hat = mm / (1.0 - ADAM_B1 ** ADAM_STEP)
        v_hat = vv / (1.0 - ADAM_B2 ** ADAM_STEP)
        d_ref[...] = -ADAM_LR * (m_hat / (jnp.sqrt(v_hat) + ADAM_EPS) + ADAM_WD * w_ref[...])
        if exchange is not None:
            pl.when(pl.program_id(0) == nsteps - 1)(finish)

    row = pl.BlockSpec((tr, c), lambda i: (i, 0))
    in_specs = [row, pl.BlockSpec((n, tr, c), lambda i: (0, i, 0)), row, row]
    out_shape = [jax.ShapeDtypeStruct((r, c), F32)] * 4
    if exchange is None:
        return pl.pallas_call(
            body, name=name, grid=(nsteps,), in_specs=in_specs, out_specs=[row] * 4, out_shape=out_shape,
            compiler_params=_params("arbitrary"),
        )(w, gparts, m, v)
    return pl.pallas_call(
        body, name=name, grid=(nsteps,), in_specs=in_specs + [ANY], out_specs=[row] * 4 + [ANY],
        out_shape=out_shape + [jax.ShapeDtypeStruct(exchange.shape, exchange.dtype)], scratch_shapes=COMM_SEMS,
        compiler_params=_params("arbitrary"),
    )(w, gparts, m, v, exchange)


MESH = pl.DeviceIdType.MESH
ANY = pl.BlockSpec(memory_space=pl.ANY)


def _all_gather(xs, name):
    n = len(xs)

    def body(*refs):
        x_refs, out_refs, sems = refs[:n], refs[n:2 * n], refs[2 * n:]
        start, forward, finish = _all_of([_gather_phases(x_refs[j], out_refs[j], *sems[3 * j:3 * j + 3]) for j in range(n)])
        start()
        forward()
        finish()

    return pl.pallas_call(
        body, name=name, in_specs=[ANY] * n, out_specs=[ANY] * n,
        out_shape=[jax.ShapeDtypeStruct((N_DEV,) + x.shape, x.dtype) for x in xs], scratch_shapes=COMM_SEMS * n,
    )(*xs)


COMM_SEMS = [pltpu.SemaphoreType.DMA((7,)), pltpu.SemaphoreType.DMA((7,)), pltpu.SemaphoreType.DMA(())]


def _gather_phases(x_ref, out_ref, send_sems, recv_sems, local_sem):
    mx, my, mc = lax.axis_index("x"), lax.axis_index("y"), lax.axis_index("c")
    me, sibling = (mx, my, mc), (mx, my, 1 - mc)
    chips = [(1 - mx, my), (mx, 1 - my), (1 - mx, 1 - my)]

    def slot(px, py, pc):
        return out_ref.at[4 * px + 2 * py + pc]

    def copy(kk, block, to, src=None):
        return pltpu.make_async_remote_copy(
            src_ref=slot(*block) if src is None else src, dst_ref=slot(*block),
            send_sem=send_sems.at[kk], recv_sem=recv_sems.at[kk], device_id=to, device_id_type=MESH)

    mine = pltpu.make_async_copy(x_ref, slot(*me), local_sem)
    first = [copy(0, me, sibling, src=x_ref)] + [copy(1 + j, me, (*chip, mc), src=x_ref) for j, chip in enumerate(chips)]
    passed = [copy(4 + j, (*chip, mc), sibling) for j, chip in enumerate(chips)]

    def start():
        mine.start()
        for cp in first:
            cp.start()

    def forward():
        for j, chip in enumerate(chips):
            copy(1 + j, (*chip, mc), me).wait_recv()
            passed[j].start()

    def finish():
        copy(0, sibling, me).wait_recv()
        for j, chip in enumerate(chips):
            copy(4 + j, (*chip, 1 - mc), me).wait_recv()
        for cp in first + passed:
            cp.wait_send()
        mine.wait()

    return start, forward, finish


def _all_of(phase_lists):
    def run(fns):
        def every():
            for fn in fns:
                fn()
        return every
    return tuple(run(fns) for fns in zip(*phase_lists))


def _exchange_phases(x_ref, out_ref, send_sems, recv_sems, local_sem):
    mx, my, mc = lax.axis_index("x"), lax.axis_index("y"), lax.axis_index("c")
    me = 4 * mx + 2 * my + mc
    mine = pltpu.make_async_copy(x_ref.at[me], out_ref.at[me], local_sem)
    copies = []
    for kk in range(1, N_DEV):
        px, py, pc = mx ^ (kk >> 2), my ^ ((kk >> 1) & 1), mc ^ (kk & 1)
        peer = 4 * px + 2 * py + pc
        copies.append((peer, pltpu.make_async_remote_copy(
            src_ref=x_ref.at[peer], dst_ref=out_ref.at[me], send_sem=send_sems.at[kk - 1],
            recv_sem=recv_sems.at[kk - 1], device_id=(px, py, pc), device_id_type=MESH)))

    def start():
        mine.start()
        for _, cp in copies:
            cp.start()

    def finish():
        for kk, (peer, _) in enumerate(copies):
            pltpu.make_async_remote_copy(
                src_ref=x_ref.at[peer], dst_ref=out_ref.at[peer], send_sem=send_sems.at[kk],
                recv_sem=recv_sems.at[kk], device_id=(mx, my, mc), device_id_type=MESH).wait_recv()
        for _, cp in copies:
            cp.wait_send()
        mine.wait()

    return start, finish


def _all_to_all(x, name):
    def body(x_ref, out_ref, send_sems, recv_sems, local_sem):
        start, finish = _exchange_phases(x_ref, out_ref, send_sems, recv_sems, local_sem)
        start()
        finish()

    return pl.pallas_call(
        body, name=name, in_specs=[ANY], out_specs=ANY,
        out_shape=jax.ShapeDtypeStruct(x.shape, x.dtype), scratch_shapes=COMM_SEMS,
    )(x)


def _shards_last(a):
    lead = a.shape[:-1]
    return jnp.moveaxis(a.reshape(*lead, N_DEV, a.shape[-1] // N_DEV), -2, 0)


def _shards_rows(a):
    l, n, c = a.shape
    return jnp.moveaxis(a.reshape(l, N_DEV, n // N_DEV, c), 1, 0)


def _flat_pad(parts, mult):
    flat = jnp.concatenate([p.reshape(-1) for p in parts])
    return jnp.pad(flat, (0, (-flat.shape[0]) % mult))


def _unflatten(flat, shapes):
    out, o = [], 0
    for shp in shapes:
        n = 1
        for s in shp:
            n *= s
        out.append(flat[o:o + n].reshape(shp))
        o += n
    return out


def kernel(x, meta_tokens, g_mix, w_in, w_conv, w_pool, pool_scale, w_out, g_mlp, w_up, w_down, g_final, loss_target, m_meta_tokens, m_g_mix, m_w_in, m_w_conv, m_w_pool, m_pool_scale, m_w_out, m_g_mlp, m_w_up, m_w_down, m_g_final, v_meta_tokens, v_g_mix, v_w_in, v_w_conv, v_w_pool, v_pool_scale, v_w_out, v_g_mlp, v_w_up, v_w_down, v_g_final):
    nb, seq, d = x.shape
    depth = g_mix.shape[0]
    lp = BLK + seq
    t = nb * lp
    cw = w_conv.shape[2] * N_DEV
    pw = pool_scale.shape[1]
    pg = pw // N_POOL_GROUPS
    aw = (w_in.shape[2] * N_DEV - 3 * cw - pw) // 3
    ua_w = 3 * cw + pw
    d_ff = w_up.shape[2] * N_DEV
    me = 4 * lax.axis_index("x") + 2 * lax.axis_index("y") + lax.axis_index("c")

    def rows(a):
        return a.reshape(-1, a.shape[-1])

    def gathered_w_in(g, layers):
        return jnp.moveaxis(g.reshape(N_DEV, layers, d, -1), 0, 2).reshape(layers, d, -1)

    g_in0, small_in = _all_gather([rows(w_in[:1]).astype(BF16), _flat_pad([meta_tokens, w_conv], 8 * BLK).reshape(-1, BLK)],
                                  "gather_first_weights")
    w_in0_f = gathered_w_in(g_in0, 1)
    small_in = small_in.reshape(N_DEV, -1)
    meta_f = jnp.moveaxis(small_in[:, :meta_tokens.size].reshape(N_DEV, N_META, d // N_DEV), 0, 1).reshape(N_META, d)
    w_conv_f = small_in[:, meta_tokens.size:meta_tokens.size + w_conv.size].reshape((N_DEV,) + w_conv.shape)
    w_conv_f = jnp.moveaxis(w_conv_f, 0, 2).reshape(depth, w_conv.shape[1], cw)

    def split_w_in(w):
        return w[:, :ua_w], w[:, ua_w:ua_w + aw], w[:, ua_w + aw:ua_w + 2 * aw], w[:, ua_w + 2 * aw:]

    wbd = jnp.zeros((depth, pw, pw), F32)
    for gi in range(N_POOL_GROUPS):
        wbd = wbd.at[:, gi * pg:(gi + 1) * pg, gi * pg:(gi + 1) * pg].set(w_pool[:, gi])
    wbd = wbd.astype(BF16)

    h = jnp.concatenate([jnp.zeros((nb, PAD, d), F32), jnp.broadcast_to(meta_f[None], (nb, N_META, d)), x], axis=1).reshape(t, d)
    saved, w_in_parts = [], [None] * depth
    w_in_parts[0] = split_w_in(w_in0_f[0])
    for l in range(depth):
        wa, wq, wk, wv = w_in_parts[l]
        ua, q, k, v, hn1 = _inproj_fwd(h, g_mix[l][None], wa, wq, wk, wv)
        ycp = _mixer_fwd(ua, w_conv_f[l], wbd[l], pool_scale[l][None], nb)
        if l == 0:
            yat, g_in, g_out, g_up, g_down = _attn_fwd(
                q, k, v, nb, gather=[rows(a).astype(BF16) for a in (w_in[1:], w_out, w_up, w_down)])
            w_in_rest = gathered_w_in(g_in, depth - 1)
            for ll in range(1, depth):
                w_in_parts[ll] = split_w_in(w_in_rest[ll - 1])
            w_out_f = jnp.moveaxis(g_out.reshape((N_DEV,) + w_out.shape), 0, 1).reshape(depth, -1, d)
            wo_a, wo_b = w_out_f[:, :cw + pw], w_out_f[:, cw + pw:]
            w_up_f = g_up.reshape((N_DEV,) + w_up.shape)
            w_down_f = g_down.reshape((N_DEV,) + w_down.shape)
        else:
            yat = _attn_fwd(q, k, v, nb)
        h_mid =_outproj(h, ycp, yat, wo_a[l], wo_b[l])
        h_next, hn2 = _mlp_fwd(h_mid, g_mlp[l][None], w_up_f, w_down_f, l)
        saved.append((h, ua, q, k, v, hn1, ycp, yat, h_mid, hn2))
        h = h_next

    loss_part, dh, dg_final = _loss_head(h, g_final[None], loss_target.reshape(nb * seq, d), nb)

    g_gmix, g_gmlp, g_wconv, g_pscale, g_wpool = [None] * depth, [None] * depth, [None] * depth, [None] * depth, [None] * depth
    g_win, g_wout, g_wup, g_wdown = [None] * depth, [None] * depth, [None] * depth, [None] * depth
    for l in reversed(range(depth)):
        h_in, ua, q, k, v, hn1, ycp, yat, h_mid, hn2 = saved[l]
        wa, wq, wk, wv = w_in_parts[l]
        dh_mid, act, dm, g_gmlp[l] = _mlp_bwd(dh, h_mid, hn2, g_mlp[l][None], w_up_f, w_down_f, l)
        g_wup[l] = _matmul_tn(hn2, dm, "grad_w_up")
        g_wdown[l] = _matmul_tn(act, dh, "grad_w_down")
        dycp, do = _outproj_bwd(dh_mid, wo_a[l], wo_b[l])
        g_wout[l] = jnp.concatenate([_matmul_tn(ycp, dh_mid, "grad_w_out_a"), _matmul_tn(yat, dh_mid, "grad_w_out_b")], axis=0)
        if l == 0:
            send = [_shards_last(jnp.stack(g_win[1:])), _shards_rows(jnp.stack(g_wout)),
                    _shards_last(jnp.stack(g_wup)), _shards_rows(jnp.stack(g_wdown))]
            dq, dk, dv, p_in, p_out, p_up, p_down = _attn_bwd(
                q, k, v, do, nb, exchange=[s.reshape(N_DEV, -1, s.shape[-1]) for s in send])
        else:
            dq, dk, dv = _attn_bwd(q, k, v, do, nb)
        dua, g_wconv[l], g_pscale[l], dwb = _mixer_bwd(ua, dycp, w_conv_f[l], wbd[l], pool_scale[l][None], nb)
        g_wpool[l] = jnp.stack([dwb[gi * pg:(gi + 1) * pg, gi * pg:(gi + 1) * pg] for gi in range(N_POOL_GROUPS)])
        dh, g_gmix[l] = _inproj_bwd(dh_mid, h_in, g_mix[l][None], dua, dq, dk, dv, wa, wq, wk, wv)
        g_win[l] = jnp.concatenate([_matmul_tn(hn1, dua, "grad_w_in_a"), _matmul_tn(hn1, dq, "grad_w_in_q"),
                                    _matmul_tn(hn1, dk, "grad_w_in_k"), _matmul_tn(hn1, dv, "grad_w_in_v")], axis=1)
    dh3 = dh.reshape(nb, lp, d)
    grad_x = dh3[:, BLK:]
    g_meta = _sum_parts(dh3[:, PAD:BLK], "sum_meta_grad")

    send_in0 = _shards_last(g_win[0][None]).reshape(N_DEV, -1, w_in.shape[-1])
    half = send_in0.shape[1] // 2
    o_out = _adamw(rows(w_out), p_out, rows(m_w_out), rows(v_w_out), "adamw_w_out")
    *o_up, p_in0a = _adamw(rows(w_up), p_up, rows(m_w_up), rows(v_w_up), "adamw_w_up_exchange", exchange=send_in0[:, :half])
    *o_down, p_in0b = _adamw(rows(w_down), p_down, rows(m_w_down), rows(v_w_down), "adamw_w_down_exchange",
                             exchange=send_in0[:, half:])
    o_in1 = _adamw(rows(w_in[1:]), p_in, rows(m_w_in[1:]), rows(v_w_in[1:]), "adamw_w_in_rest")
    o_in0 = _adamw(rows(w_in[:1]), jnp.concatenate([p_in0a, p_in0b], axis=1), rows(m_w_in[:1]), rows(v_w_in[:1]), "adamw_w_in0")
    (gr_win, gr_wout, gr_wup, gr_wdown), (de_win, de_wout, de_wup, de_wdown), (nm_win, nm_wout, nm_wup, nm_wdown), \
        (nv_win, nv_wout, nv_wup, nv_wdown) = [
            [jnp.concatenate([i0, i1], axis=0).reshape(w_in.shape), o.reshape(w_out.shape), u.reshape(w_up.shape), dn.reshape(w_down.shape)]
            for i0, i1, o, u, dn in zip(o_in0, o_in1, o_out, o_up, o_down)]

    small_full = [g_meta, jnp.concatenate(g_gmix), jnp.stack(g_wconv), jnp.stack(g_wpool), jnp.concatenate(g_pscale),
                  jnp.concatenate(g_gmlp), dg_final.reshape(-1)]
    small_shapes = [a.shape for a in small_full]
    small_sum = _sum_parts(_all_gather([_flat_pad(small_full, 8 * BLK).reshape(-1, BLK)], "gather_small_grads")[0], "sum_small_grads")
    gr_meta, gr_gmix, gr_wconv, gr_wpool, gr_pscale, gr_gmlp, gr_gfinal = _unflatten(small_sum.reshape(-1), small_shapes)
    gr_meta = lax.dynamic_slice_in_dim(gr_meta, me * (d // N_DEV), d // N_DEV, axis=1)
    gr_wconv = lax.dynamic_slice_in_dim(gr_wconv, me * (cw // N_DEV), cw // N_DEV, axis=2)
    small_g = [gr_meta, gr_gmix, gr_wconv, gr_wpool, gr_pscale, gr_gmlp, gr_gfinal]
    local_shapes = [a.shape for a in small_g]
    pack_small = lambda parts_: _flat_pad(parts_, 8 * BLK).reshape(-1, BLK)
    small = _adamw(pack_small([meta_tokens, g_mix, w_conv, w_pool, pool_scale, g_mlp, g_final]), pack_small(small_g)[None],
                   pack_small([m_meta_tokens, m_g_mix, m_w_conv, m_w_pool, m_pool_scale, m_g_mlp, m_g_final]),
                   pack_small([v_meta_tokens, v_g_mix, v_w_conv, v_w_pool, v_pool_scale, v_g_mlp, v_g_final]), "adamw_small")
    _, de_s, nm_s, nv_s = [_unflatten(b.reshape(-1), local_shapes) for b in small]

    loss = lax.psum(loss_part[0, 0], ("x", "y", "c"))

    def ordered(meta, gmix, wconv, wpool, pscale, gmlp, gfinal, win, wout, wup, wdown):
        return [meta, gmix, win, wconv, wpool, pscale, wout, gmlp, wup, wdown, gfinal]

    grads = ordered(*small_g, gr_win, gr_wout, gr_wup, gr_wdown)
    deltas = ordered(*de_s, de_win, de_wout, de_wup, de_wdown)
    new_m = ordered(*nm_s, nm_win, nm_wout, nm_wup, nm_wdown)
    new_v = ordered(*nv_s, nv_win, nv_wout, nv_wup, nv_wdown)
    return (loss, grad_x, *grads, *deltas, *new_m, *new_v)
```

```python
import jax
import jax.numpy as jnp
from jax import lax
from jax.experimental import pallas as pl
from jax.experimental.pallas import tpu as pltpu

F32 = jnp.float32
BF16 = jnp.bfloat16

N_DEV = 8
N_META = 16
HEAD_DIM = 64
BLK = 128
PAD = BLK - N_META
N_POOL_GROUPS = 4
EPS = 1e-6

ADAM_LR = 0.001
ADAM_B1 = 0.9
ADAM_B2 = 0.999
ADAM_EPS = 1e-08
ADAM_WD = 0.01
ADAM_STEP = 10

V7X_VMEM_LIMIT = 56 * 1024 * 1024


def _params(*sem):
    return pltpu.CompilerParams(dimension_semantics=sem, vmem_limit_bytes=V7X_VMEM_LIMIT)


SUBLANES = 8
ROW_TILE = 512
ROW_TILE_MLP = 1152
ROW_TILE_WIDE = 1152
TN_ACC_BYTES = 8 * 1024 * 1024


def _row_tile(t, cap=ROW_TILE):
    for tm in range(min(cap, t) // SUBLANES * SUBLANES, 0, -SUBLANES):
        if t % tm == 0:
            return tm
    raise ValueError(f"row count {t} is not a multiple of 8")


def _full(a):
    nd = a.ndim
    return pl.BlockSpec(a.shape, lambda *_: (0,) * nd)


def _dot(a, b):
    return jnp.dot(a, b, preferred_element_type=F32)


def _dot_nt(a, b):
    return lax.dot_general(a, b, (((1,), (1,)), ((), ())), preferred_element_type=F32)


def _dot_tn(a, b):
    return lax.dot_general(a, b, (((0,), (0,)), ((), ())), preferred_element_type=F32)


def _rms_fwd(x, g):
    rstd = lax.rsqrt(jnp.mean(x * x, axis=-1, keepdims=True) + EPS)
    return x * rstd * g


def _rms_bwd(dy, x, g):
    rstd = lax.rsqrt(jnp.mean(x * x, axis=-1, keepdims=True) + EPS)
    xhat = x * rstd
    dxhat = dy * g
    dx = rstd * (dxhat - xhat * jnp.mean(dxhat * xhat, axis=-1, keepdims=True))
    return dx, jnp.sum(dy * xhat, axis=0, keepdims=True)


def _inproj_fwd(h, g, wa, wq, wk, wv):
    t, d = h.shape
    tm = _row_tile(t, ROW_TILE_WIDE)
    ua_w, aw = wa.shape[1], wq.shape[1]

    def body(h_ref, g_ref, wa_ref, wq_ref, wk_ref, wv_ref, ua_ref, q_ref, k_ref, v_ref, hn_ref):
        hn = _rms_fwd(h_ref[...], g_ref[...]).astype(BF16)
        hn_ref[...] = hn
        ua_ref[...] = _dot(hn, wa_ref[...])
        q_ref[...] = _dot(hn, wq_ref[...]).astype(BF16)
        k_ref[...] = _dot(hn, wk_ref[...]).astype(BF16)
        v_ref[...] = _dot(hn, wv_ref[...]).astype(BF16)

    row = lambda w: pl.BlockSpec((tm, w), lambda i: (i, 0))
    return pl.pallas_call(
        body, name="inproj_fwd", grid=(t // tm,),
        in_specs=[row(d), _full(g), _full(wa), _full(wq), _full(wk), _full(wv)],
        out_specs=[row(ua_w), row(aw), row(aw), row(aw), row(d)],
        out_shape=[jax.ShapeDtypeStruct((t, ua_w), F32)] + [jax.ShapeDtypeStruct((t, aw), BF16)] * 3
        + [jax.ShapeDtypeStruct((t, d), BF16)],
        compiler_params=_params("arbitrary"),
    )(h, g, wa, wq, wk, wv)


def _pool_geometry(lp, pw):
    pg = pw // N_POOL_GROUPS
    row = lax.broadcasted_iota(jnp.int32, (lp, pw), 0)
    lane = lax.broadcasted_iota(jnp.int32, (lp, pw), 1)
    grp = [(lane >= g * pg) & (lane < (g + 1) * pg) for g in range(N_POOL_GROUPS)]
    wlen = jnp.where(grp[0], 2, jnp.where(grp[1], 4, jnp.where(grp[2], 8, 16)))
    cnt = jnp.clip(row - (PAD - 1), 1, wlen).astype(F32)
    return grp, cnt, row >= PAD


def _by_group(grp, vals):
    return jnp.where(grp[0], vals[0], jnp.where(grp[1], vals[1], jnp.where(grp[2], vals[2], vals[3])))


def _pooled(p, grp, cnt, real):
    s2 = p + pltpu.roll(p, 1, 0)
    s4 = s2 + pltpu.roll(s2, 2, 0)
    s8 = s4 + pltpu.roll(s4, 4, 0)
    s16 = s8 + pltpu.roll(s8, 8, 0)
    return jnp.where(real, _by_group(grp, (s2, s4, s8, s16)) / cnt - p, 0.0)


def _conv(uu, wc_ref):
    return wc_ref[2:3, :] * uu + wc_ref[1:2, :] * pltpu.roll(uu, 1, 0) + wc_ref[0:1, :] * pltpu.roll(uu, 2, 0)


def _mixer_fwd(ua, wconv, wbd, pscale, nb):
    t = ua.shape[0]
    lp = t // nb
    cw, pw = wconv.shape[1], wbd.shape[0]

    def body(ua_ref, wc_ref, wbd_ref, ps_ref, y_ref):
        cb = ua_ref[:, 0:cw]
        uu = ua_ref[:, cw:2 * cw] * ua_ref[:, 2 * cw:3 * cw]
        y_ref[:, 0:cw] = (cb * _conv(uu, wc_ref)).astype(BF16)
        grp, cnt, real = _pool_geometry(lp, pw)
        pooled = _pooled(ua_ref[:, 3 * cw:3 * cw + pw], grp, cnt, real)
        y_ref[:, cw:cw + pw] = (_dot(pooled.astype(BF16), wbd_ref[...]) * ps_ref[...]).astype(BF16)

    return pl.pallas_call(
        body, name="mixer_fwd", grid=(nb,),
        in_specs=[pl.BlockSpec((lp, ua.shape[1]), lambda s: (s, 0)), _full(wconv), _full(wbd), _full(pscale)],
        out_specs=pl.BlockSpec((lp, cw + pw), lambda s: (s, 0)),
        out_shape=jax.ShapeDtypeStruct((t, cw + pw), BF16),
        compiler_params=_params("arbitrary"),
    )(ua, wconv, wbd, pscale)


ATT_NW = 2
ATT_W = ATT_NW * BLK
ATT_PAIRS_FWD = 4
ATT_PAIRS_BWD = 4
HW = 2 * HEAD_DIM
ATT_SCALE = HEAD_DIM ** -0.5


def _attn_lk(z, valid):
    nz = -z
    return jnp.where(valid, jnp.minimum(nz, 0.0) - jnp.log(1.0 + jnp.exp(jnp.minimum(z, nz))), 0.0)


def _tri_sum_bf16(x, tri):
    return _dot(x.astype(BF16), tri)


def _tri_sum(x, tri):
    hi = x.astype(BF16)
    lo = (x - hi.astype(F32)).astype(BF16)
    return _dot(hi, tri) + _dot(lo, tri)


def _attn_weights(z, suffix, carry, valid):
    return jnp.where(valid, jnp.exp(z + suffix + carry), 0.0)


def _attn_consts():
    r = lax.broadcasted_iota(jnp.int32, (2 * BLK, ATT_W), 0)
    c = lax.broadcasted_iota(jnp.int32, (2 * BLK, ATT_W), 1)
    cmr = c - (r & (BLK - 1))
    kr = lax.broadcasted_iota(jnp.int32, (ATT_W, ATT_W), 0)
    kc = lax.broadcasted_iota(jnp.int32, (ATT_W, ATT_W), 1)
    lane = lax.broadcasted_iota(jnp.int32, (1, HW), 1)
    heads = (lane < HEAD_DIM, lane >= HEAD_DIM)
    return c, cmr, kr, kc, heads


def _tri(cond):
    return jnp.where(cond, 1.0, 0.0).astype(BF16)


def _rows(i):
    return pl.ds(pl.multiple_of(i * BLK, BLK), BLK)


def _stack_heads(x, heads):
    zero = jnp.zeros_like(x)
    return jnp.concatenate([jnp.where(heads[0], x, zero), jnp.where(heads[1], x, zero)], axis=0)


def _unstack_heads(x, heads):
    return jnp.where(heads[0], x[:BLK], x[BLK:])


def _window(i, s):
    start = jnp.maximum(i + 1 - ATT_NW * (s + 1), 0) * BLK
    return start, (i + 1 - ATT_NW * s) * BLK


def _attn_fwd(q, k, v, nb, gather=None):
    t, aw = q.shape
    lp = t // nb
    nblk = lp // BLK
    pp = min(ATT_PAIRS_FWD, aw // HW)
    cols = [slice(p * HW, (p + 1) * HW) for p in range(pp)]
    ncol = aw // (pp * HW)
    nsteps = nb * ncol

    def body(q_ref, k_ref, v_ref, *rest):
        if gather is None:
            (o_ref,) = rest
        else:
            ng = len(gather)
            x_refs, o_ref, g_refs, sems = rest[:ng], rest[ng], rest[ng + 1:2 * ng + 1], rest[2 * ng + 1:]
            step = pl.program_id(0) * ncol + pl.program_id(1)
            start, forward, finish = _all_of([_gather_phases(x_refs[j], g_refs[j], *sems[3 * j:3 * j + 3]) for j in range(ng)])
            pl.when(step == 0)(start)
            pl.when(step == nsteps - 1)(forward)
        c, cmr, kr, kc, heads = _attn_consts()
        m_from = _tri(kr >= kc)

        def qblock(i, _):
            qs = [_stack_heads(q_ref[_rows(i), cs] * ATT_SCALE, heads) for cs in cols]

            def window(s, st):
                accs, carries = st
                start, end = _window(i, s)
                keys = pl.ds(pl.multiple_of(start, BLK), ATT_W)
                valid = (cmr < i * BLK - start) & (c < end - start)
                zs = [_dot_nt(qs[p], k_ref[keys, cs]) for p, cs in enumerate(cols)]
                suffix = [_tri_sum_bf16(_attn_lk(zs[p], valid), m_from) for p in range(pp)]
                new_accs, new_carries = [], []
                for p, cs in enumerate(cols):
                    a = _attn_weights(zs[p], suffix[p], carries[p], valid)
                    new_accs.append(accs[p] + _dot(a.astype(BF16), v_ref[keys, cs]))
                    new_carries.append(carries[p] + suffix[p][:, 0:1])
                return tuple(new_accs), tuple(new_carries)

            init = (tuple(jnp.zeros((2 * BLK, HW), F32) for _ in cols), tuple(jnp.zeros((2 * BLK, 1), F32) for _ in cols))
            accs, _ = lax.fori_loop(0, (i + ATT_NW) // ATT_NW, window, init)
            for p, cs in enumerate(cols):
                o_ref[_rows(i), cs] = _unstack_heads(accs[p], heads).astype(BF16)
            return 0

        lax.fori_loop(0, nblk, qblock, 0)
        if gather is not None:
            pl.when(step == nsteps - 1)(finish)

    spec = pl.BlockSpec((lp, pp * HW), lambda s, p: (s, p))
    y_shape = jax.ShapeDtypeStruct((t, aw), BF16)
    if gather is None:
        return pl.pallas_call(
            body, name="attn_fwd", grid=(nb, ncol), in_specs=[spec, spec, spec], out_specs=spec, out_shape=y_shape,
            compiler_params=_params("arbitrary", "arbitrary"),
        )(q, k, v)
    return pl.pallas_call(
        body, name="attn_fwd_gather", grid=(nb, ncol), in_specs=[spec, spec, spec] + [ANY] * len(gather),
        out_specs=[spec] + [ANY] * len(gather),
        out_shape=[y_shape] + [jax.ShapeDtypeStruct((N_DEV,) + x.shape, x.dtype) for x in gather],
        scratch_shapes=COMM_SEMS * len(gather), compiler_params=_params("arbitrary", "arbitrary"),
    )(q, k, v, *gather)


def _outproj(h, ya, yb, wa, wb):
    t, d = h.shape
    tm = _row_tile(t, ROW_TILE_WIDE)

    def body(h_ref, ya_ref, yb_ref, wa_ref, wb_ref, o_ref):
        o_ref[...] = h_ref[...] + _dot(ya_ref[...], wa_ref[...]) + _dot(yb_ref[...], wb_ref[...])

    row = lambda w: pl.BlockSpec((tm, w), lambda i: (i, 0))
    return pl.pallas_call(
        body, name="outproj", grid=(t // tm,),
        in_specs=[row(d), row(ya.shape[1]), row(yb.shape[1]), _full(wa), _full(wb)],
        out_specs=row(d), out_shape=jax.ShapeDtypeStruct((t, d), F32),
        compiler_params=_params("arbitrary"),
    )(h, ya, yb, wa, wb)


def _mlp_weight_specs(wup, wdown, layer):
    nff, _, d, fc = wup.shape
    return nff, fc, [pl.BlockSpec((None, None, d, fc), lambda i, kk: (kk, layer, 0, 0)),
                     pl.BlockSpec((None, None, fc, d), lambda i, kk: (kk, layer, 0, 0))]


def _mlp_fwd(h, g, wup, wdown, layer):
    t, d = h.shape
    tm = _row_tile(t, ROW_TILE_MLP)
    nff, fc, wspecs = _mlp_weight_specs(wup, wdown, layer)

    def body(h_ref, g_ref, wu_ref, wd_ref, o_ref, hn_ref, acc):
        kk = pl.program_id(1)

        @pl.when(kk == 0)
        def _():
            x = h_ref[...]
            hn_ref[...] = _rms_fwd(x, g_ref[...]).astype(BF16)
            acc[...] = x

        m = _dot(hn_ref[...], wu_ref[...])
        a = jnp.square(jnp.maximum(m, 0.0)).astype(BF16)
        acc[...] += _dot(a, wd_ref[...])

        @pl.when(kk == nff - 1)
        def _():
            o_ref[...] = acc[...]

    row = pl.BlockSpec((tm, d), lambda i, kk: (i, 0))
    return pl.pallas_call(
        body, name="mlp_fwd", grid=(t // tm, nff),
        in_specs=[row, _full(g)] + wspecs,
        out_specs=[row, row],
        out_shape=[jax.ShapeDtypeStruct((t, d), F32), jax.ShapeDtypeStruct((t, d), BF16)],
        scratch_shapes=[pltpu.VMEM((tm, d), F32)],
        compiler_params=_params("arbitrary", "arbitrary"),
    )(h, g, wup, wdown)


def _loss_head(h, g, target, nb):
    t, d = h.shape
    nblk = t // nb // BLK
    nx = nblk - 1

    def body(h_ref, g_ref, t_ref, loss_ref, dh_ref, dg_ref):
        s, i = pl.program_id(0), pl.program_id(1)

        @pl.when((s == 0) & (i == 0))
        def _():
            loss_ref[...] = jnp.zeros_like(loss_ref)
            dg_ref[...] = jnp.zeros_like(dg_ref)

        @pl.when(i == 0)
        def _():
            dh_ref[...] = jnp.zeros_like(dh_ref)

        @pl.when(i > 0)
        def _():
            x, gg = h_ref[...], g_ref[...]
            err = _rms_fwd(x, gg) - t_ref[...]
            loss_ref[...] += jnp.sum(err * err) * (0.5 / d)
            dx, dg = _rms_bwd(err * (1.0 / d), x, gg)
            dh_ref[...] = dx
            dg_ref[...] += dg

    hspec = pl.BlockSpec((BLK, d), lambda s, i: (s * nblk + i, 0))
    return pl.pallas_call(
        body, name="loss_head", grid=(nb, nblk),
        in_specs=[hspec, _full(g), pl.BlockSpec((BLK, d), lambda s, i: (s * nx + jnp.maximum(i - 1, 0), 0))],
        out_specs=[pl.BlockSpec((1, BLK), lambda s, i: (0, 0)), hspec, pl.BlockSpec((1, d), lambda s, i: (0, 0))],
        out_shape=[jax.ShapeDtypeStruct((1, BLK), F32), jax.ShapeDtypeStruct((t, d), F32), jax.ShapeDtypeStruct((1, d), F32)],
        compiler_params=_params("arbitrary", "arbitrary"),
    )(h, g, target)


def _mlp_bwd(dh, h, hn, g, wup, wdown, layer):
    t, d = h.shape
    tm = _row_tile(t, ROW_TILE_MLP)
    nff, fc, wspecs = _mlp_weight_specs(wup, wdown, layer)

    def body(dh_ref, h_ref, hn_ref, g_ref, wu_ref, wd_ref, dhm_ref, a_ref, dm_ref, dg_ref, dhn, dhb):
        i, kk = pl.program_id(0), pl.program_id(1)

        @pl.when((i == 0) & (kk == 0))
        def _():
            dg_ref[...] = jnp.zeros_like(dg_ref)

        @pl.when(kk == 0)
        def _():
            dhb[...] = dh_ref[...].astype(BF16)
            dhn[...] = jnp.zeros_like(dhn)

        r = jnp.maximum(_dot(hn_ref[...], wu_ref[...]), 0.0)
        a_ref[...] = (r * r).astype(BF16)
        dm = (_dot_nt(dhb[...], wd_ref[...]) * (2.0 * r)).astype(BF16)
        dm_ref[...] = dm
        dhn[...] += _dot_nt(dm, wu_ref[...])

        @pl.when(kk == nff - 1)
        def _():
            dx, dg = _rms_bwd(dhn[...], h_ref[...], g_ref[...])
            dhm_ref[...] = dh_ref[...] + dx
            dg_ref[...] += dg

    row = pl.BlockSpec((tm, d), lambda i, kk: (i, 0))
    ff = pl.BlockSpec((tm, fc), lambda i, kk: (i, kk))
    return pl.pallas_call(
        body, name="mlp_bwd", grid=(t // tm, nff),
        in_specs=[row, row, row, _full(g)] + wspecs,
        out_specs=[row, ff, ff, pl.BlockSpec((1, d), lambda i, kk: (0, 0))],
        out_shape=[jax.ShapeDtypeStruct((t, d), F32), jax.ShapeDtypeStruct((t, nff * fc), BF16),
                   jax.ShapeDtypeStruct((t, nff * fc), BF16), jax.ShapeDtypeStruct((1, d), F32)],
        scratch_shapes=[pltpu.VMEM((tm, d), F32), pltpu.VMEM((tm, d), BF16)],
        compiler_params=_params("arbitrary", "arbitrary"),
    )(dh, h, hn, g, wup, wdown)


def _matmul_tn(x, y, name):
    t, k1 = x.shape
    n = y.shape[1]
    tt = _row_tile(t)
    tn = min(n, max(512, TN_ACC_BYTES // (4 * k1)))
    steps = t // tt

    def body(x_ref, y_ref, o_ref, acc):
        @pl.when(pl.program_id(1) == 0)
        def _():
            acc[...] = jnp.zeros_like(acc)

        acc[...] += _dot_tn(x_ref[...].astype(BF16), y_ref[...].astype(BF16))

        @pl.when(pl.program_id(1) == steps - 1)
        def _():
            o_ref[...] = acc[...].astype(BF16)

    return pl.pallas_call(
        body, name=name, grid=(n // tn, steps),
        in_specs=[pl.BlockSpec((tt, k1), lambda j, r: (r, 0)), pl.BlockSpec((tt, tn), lambda j, r: (r, j))],
        out_specs=pl.BlockSpec((k1, tn), lambda j, r: (0, j)),
        out_shape=jax.ShapeDtypeStruct((k1, n), BF16),
        scratch_shapes=[pltpu.VMEM((k1, tn), F32)],
        compiler_params=_params("arbitrary", "arbitrary"),
    )(x, y)


def _outproj_bwd(dh, wa, wb):
    t, d = dh.shape
    tm = _row_tile(t, ROW_TILE_WIDE)
    na, nbw = wa.shape[0], wb.shape[0]

    def body(dh_ref, wa_ref, wb_ref, da_ref, db_ref):
        x = dh_ref[...].astype(BF16)
        da_ref[...] = _dot_nt(x, wa_ref[...])
        db_ref[...] = _dot_nt(x, wb_ref[...]).astype(BF16)

    row = lambda w: pl.BlockSpec((tm, w), lambda i: (i, 0))
    return pl.pallas_call(
        body, name="outproj_bwd", grid=(t // tm,),
        in_specs=[row(d), _full(wa), _full(wb)], out_specs=[row(na), row(nbw)],
        out_shape=[jax.ShapeDtypeStruct((t, na), F32), jax.ShapeDtypeStruct((t, nbw), BF16)],
        compiler_params=_params("arbitrary"),
    )(dh, wa, wb)


def _attn_bwd(q, k, v, do, nb, exchange=None):
    t, aw = q.shape
    lp = t // nb
    nblk = lp // BLK
    pp = min(ATT_PAIRS_BWD, aw // HW)
    cols = [slice(p * HW, (p + 1) * HW) for p in range(pp)]
    ncol = aw // (pp * HW)
    nsteps = nb * ncol
    max_windows = (nblk + ATT_NW - 1) // ATT_NW

    def body(q_ref, k_ref, v_ref, do_ref, *rest):
        if exchange is None:
            dq_ref, dk_ref, dv_ref, dk_acc, dv_acc, g_s, sig_s = rest
        else:
            ne = len(exchange)
            x_refs, (dq_ref, dk_ref, dv_ref), e_refs = rest[:ne], rest[ne:ne + 3], rest[ne + 3:2 * ne + 3]
            dk_acc, dv_acc, g_s, sig_s = rest[2 * ne + 3:2 * ne + 7]
            sems = rest[2 * ne + 7:]
            step = pl.program_id(0) * ncol + pl.program_id(1)
            start, finish = _all_of([_exchange_phases(x_refs[j], e_refs[j], *sems[3 * j:3 * j + 3]) for j in range(ne)])
            pl.when(step == 0)(start)
        c, cmr, kr, kc, heads = _attn_consts()
        m_from = _tri(kr >= kc)
        m_before = _tri(kr < kc)
        dk_acc[...] = jnp.zeros_like(dk_acc)
        dv_acc[...] = jnp.zeros_like(dv_acc)

        def qblock(i, _):
            qs = [_stack_heads(q_ref[_rows(i), cs] * ATT_SCALE, heads) for cs in cols]
            dos = [_stack_heads(do_ref[_rows(i), cs], heads) for cs in cols]
            nwin = (i + ATT_NW) // ATT_NW

            def geometry(s):
                start, end = _window(i, s)
                return (cmr < i * BLK - start) & (c < end - start), pl.ds(pl.multiple_of(start, BLK), ATT_W)

            def down(s, carries):
                valid, keys = geometry(s)
                zs = [_dot_nt(qs[p], k_ref[keys, cs]) for p, cs in enumerate(cols)]
                das = [_dot_nt(dos[p], v_ref[keys, cs]) for p, cs in enumerate(cols)]
                suffix = []
                for p in range(pp):
                    lk = _attn_lk(zs[p], valid)
                    sig_s[p, s] = jnp.exp(zs[p] + lk)
                    suffix.append(_tri_sum_bf16(lk, m_from))
                new_carries = []
                for p, cs in enumerate(cols):
                    a = _attn_weights(zs[p], suffix[p], carries[p], valid)
                    g_s[p, s] = das[p] * a
                    dv_acc[keys, cs] += _dot_tn(a.astype(BF16), dos[p])
                    new_carries.append(carries[p] + suffix[p][:, 0:1])
                return tuple(new_carries)

            lax.fori_loop(0, nwin, down, tuple(jnp.zeros((2 * BLK, 1), F32) for _ in cols))

            def up(ss, st):
                accs, carries = st
                s = nwin - 1 - ss
                valid, keys = geometry(s)
                gs = [g_s[p, s] for p in range(pp)]
                before = [_tri_sum(gs[p], m_before) for p in range(pp)]
                new_accs, new_carries = [], []
                for p, cs in enumerate(cols):
                    g, sig = gs[p], sig_s[p, s]
                    dz = jnp.where(valid, g * (1.0 - sig) - (before[p] + carries[p]) * sig, 0.0).astype(BF16)
                    new_accs.append(accs[p] + _dot(dz, k_ref[keys, cs]))
                    dk_acc[keys, cs] += _dot_tn(dz, qs[p])
                    new_carries.append(carries[p] + jnp.sum(g, axis=1, keepdims=True))
                return tuple(new_accs), tuple(new_carries)

            init = (tuple(jnp.zeros((2 * BLK, HW), F32) for _ in cols), tuple(jnp.zeros((2 * BLK, 1), F32) for _ in cols))
            accs, _ = lax.fori_loop(0, nwin, up, init)
            for p, cs in enumerate(cols):
                dq_ref[_rows(i), cs] = (_unstack_heads(accs[p], heads) * ATT_SCALE).astype(BF16)
            return 0

        lax.fori_loop(0, nblk, qblock, 0)
        dk_ref[...] = dk_acc[...].astype(BF16)
        dv_ref[...] = dv_acc[...].astype(BF16)
        if exchange is not None:
            pl.when(step == nsteps - 1)(finish)

    spec = pl.BlockSpec((lp, pp * HW), lambda s, p: (s, p), pipeline_mode=pl.Buffered(1))
    d_shape = jax.ShapeDtypeStruct((t, aw), BF16)
    scratch = [pltpu.VMEM((lp, pp * HW), F32), pltpu.VMEM((lp, pp * HW), F32),
               pltpu.VMEM((pp, max_windows, 2 * BLK, ATT_W), F32), pltpu.VMEM((pp, max_windows, 2 * BLK, ATT_W), F32)]
    if exchange is None:
        return pl.pallas_call(
            body, name="attn_bwd", grid=(nb, ncol), in_specs=[spec] * 4, out_specs=[spec] * 3, out_shape=[d_shape] * 3,
            scratch_shapes=scratch, compiler_params=_params("arbitrary", "arbitrary"),
        )(q, k, v, do)
    return pl.pallas_call(
        body, name="attn_bwd_exchange", grid=(nb, ncol), in_specs=[spec] * 4 + [ANY] * len(exchange),
        out_specs=[spec] * 3 + [ANY] * len(exchange),
        out_shape=[d_shape] * 3 + [jax.ShapeDtypeStruct(x.shape, x.dtype) for x in exchange],
        scratch_shapes=scratch + COMM_SEMS * len(exchange), compiler_params=_params("arbitrary", "arbitrary"),
    )(q, k, v, do, *exchange)


def _mixer_bwd(ua, dy, wconv, wbd, pscale, nb):
    t = ua.shape[0]
    lp = t // nb
    cw, pw = wconv.shape[1], wbd.shape[0]

    def body(ua_ref, dy_ref, wc_ref, wbd_ref, ps_ref, du_ref, dwc_ref, dps_ref, dwb_ref):
        @pl.when(pl.program_id(0) == 0)
        def _():
            dwc_ref[...] = jnp.zeros_like(dwc_ref)
            dps_ref[...] = jnp.zeros_like(dps_ref)
            dwb_ref[...] = jnp.zeros_like(dwb_ref)

        up = lambda x, n: pltpu.roll(x, lp - n, 0)
        cb, cc, cx = ua_ref[:, 0:cw], ua_ref[:, cw:2 * cw], ua_ref[:, 2 * cw:3 * cw]
        uu = cc * cx
        dyc = dy_ref[:, 0:cw]
        du_ref[:, 0:cw] = (dyc * _conv(uu, wc_ref)).astype(BF16)
        dcv = dyc * cb
        duu = wc_ref[2:3, :] * dcv + wc_ref[1:2, :] * up(dcv, 1) + wc_ref[0:1, :] * up(dcv, 2)
        du_ref[:, cw:2 * cw] = (duu * cx).astype(BF16)
        du_ref[:, 2 * cw:3 * cw] = (duu * cc).astype(BF16)
        dwc_ref[0:1, :] += jnp.sum(dcv * pltpu.roll(uu, 2, 0), axis=0, keepdims=True)
        dwc_ref[1:2, :] += jnp.sum(dcv * pltpu.roll(uu, 1, 0), axis=0, keepdims=True)
        dwc_ref[2:3, :] += jnp.sum(dcv * uu, axis=0, keepdims=True)
        grp, cnt, real = _pool_geometry(lp, pw)
        p = ua_ref[:, 3 * cw:3 * cw + pw]
        pooled = _pooled(p, grp, cnt, real).astype(BF16)
        dyp = dy_ref[:, cw:cw + pw]
        dps_ref[...] += jnp.sum(dyp * _dot(pooled, wbd_ref[...]), axis=0, keepdims=True)
        dpre = (dyp * ps_ref[...]).astype(BF16)
        dwb_ref[...] += _dot_tn(pooled, dpre)
        dpooled = jnp.where(real, _dot_nt(dpre, wbd_ref[...]), 0.0)
        xm = dpooled / cnt
        l2 = xm + up(xm, 1)
        l4 = l2 + up(l2, 2)
        l8 = l4 + up(l4, 4)
        l16 = l8 + up(l8, 8)
        du_ref[:, 3 * cw:3 * cw + pw] = jnp.where(real, _by_group(grp, (l2, l4, l8, l16)) - dpooled, 0.0).astype(BF16)

    seq = lambda w: pl.BlockSpec((lp, w), lambda s: (s, 0))
    return pl.pallas_call(
        body, name="mixer_bwd", grid=(nb,),
        in_specs=[seq(ua.shape[1]), seq(cw + pw), _full(wconv), _full(wbd), _full(pscale)],
        out_specs=[seq(ua.shape[1]), pl.BlockSpec((3, cw), lambda s: (0, 0)), pl.BlockSpec((1, pw), lambda s: (0, 0)),
                   pl.BlockSpec((pw, pw), lambda s: (0, 0))],
        out_shape=[jax.ShapeDtypeStruct(ua.shape, BF16), jax.ShapeDtypeStruct((3, cw), F32),
                   jax.ShapeDtypeStruct((1, pw), F32), jax.ShapeDtypeStruct((pw, pw), F32)],
        compiler_params=_params("arbitrary"),
    )(ua, dy, wconv, wbd, pscale)


def _inproj_bwd(dh, h, g, dua, dq, dk, dv, wa, wq, wk, wv, exchange=None):
    t, d = h.shape
    tm = _row_tile(t)
    nsteps = t // tm

    def body(dh_ref, h_ref, g_ref, dua_ref, dq_ref, dk_ref, dv_ref, wa_ref, wq_ref, wk_ref, wv_ref, *rest):
        if exchange is None:
            o_ref, dg_ref = rest
        else:
            x_ref, o_ref, dg_ref, e_ref, send_sems, recv_sems, local_sem = rest
            start, finish = _exchange_phases(x_ref, e_ref, send_sems, recv_sems, local_sem)
            pl.when(pl.program_id(0) == 0)(start)

        @pl.when(pl.program_id(0) == 0)
        def _():
            dg_ref[...] = jnp.zeros_like(dg_ref)

        dhn = (_dot_nt(dua_ref[...], wa_ref[...]) + _dot_nt(dq_ref[...], wq_ref[...])
               + _dot_nt(dk_ref[...], wk_ref[...]) + _dot_nt(dv_ref[...], wv_ref[...]))
        dx, dg = _rms_bwd(dhn, h_ref[...], g_ref[...])
        o_ref[...] = dh_ref[...] + dx
        dg_ref[...] += dg
        if exchange is not None:
            pl.when(pl.program_id(0) == nsteps - 1)(finish)

    row = lambda w: pl.BlockSpec((tm, w), lambda i: (i, 0))
    in_specs = [row(d), row(d), _full(g), row(dua.shape[1]), row(dq.shape[1]), row(dk.shape[1]), row(dv.shape[1]),
                _full(wa), _full(wq), _full(wk), _full(wv)]
    out_specs = [row(d), pl.BlockSpec((1, d), lambda i: (0, 0))]
    out_shape = [jax.ShapeDtypeStruct((t, d), F32), jax.ShapeDtypeStruct((1, d), F32)]
    if exchange is None:
        return pl.pallas_call(
            body, name="inproj_bwd", grid=(nsteps,), in_specs=in_specs, out_specs=out_specs, out_shape=out_shape,
            compiler_params=_params("arbitrary"),
        )(dh, h, g, dua, dq, dk, dv, wa, wq, wk, wv)
    return pl.pallas_call(
        body, name="inproj_bwd_exchange", grid=(nsteps,), in_specs=in_specs + [ANY], out_specs=out_specs + [ANY],
        out_shape=out_shape + [jax.ShapeDtypeStruct(exchange.shape, exchange.dtype)], scratch_shapes=COMM_SEMS,
        compiler_params=_params("arbitrary"),
    )(dh, h, g, dua, dq, dk, dv, wa, wq, wk, wv, exchange)


def _part_tile(r):
    if r <= 512:
        return r
    for tr in (256, 128, 64, 32, 16, 8):
        if r % tr == 0:
            return tr
    return r


def _sum_parts(x, name):
    n, r, c = x.shape
    tr = _part_tile(r)

    def body(x_ref, o_ref):
        acc = x_ref[0]
        for p in range(1, n):
            acc = acc + x_ref[p]
        o_ref[...] = acc

    return pl.pallas_call(
        body, name=name, grid=(r // tr,),
        in_specs=[pl.BlockSpec((n, tr, c), lambda i: (0, i, 0))], out_specs=pl.BlockSpec((tr, c), lambda i: (i, 0)),
        out_shape=jax.ShapeDtypeStruct((r, c), F32), compiler_params=_params("arbitrary"),
    )(x)


def _adamw(w, gparts, m, v, name):
    n, r, c = gparts.shape
    tr = _part_tile(r)

    def body(w_ref, g_ref, m_ref, v_ref, go_ref, d_ref, mo_ref, vo_ref):
        g = g_ref[0].astype(F32)
        for p in range(1, n):
            g = g + g_ref[p].astype(F32)
        go_ref[...] = g
        mm = ADAM_B1 * m_ref[...] + (1.0 - ADAM_B1) * g
        vv = ADAM_B2 * v_ref[...] + (1.0 - ADAM_B2) * jnp.square(g)
        mo_ref[...] = mm
        vo_ref[...] = vv
        m_hat = mm / (1.0 - ADAM_B1 ** ADAM_STEP)
        v_hat = vv / (1.0 - ADAM_B2 ** ADAM_STEP)
        d_ref[...] = -ADAM_LR * (m_hat / (jnp.sqrt(v_hat) + ADAM_EPS) + ADAM_WD * w_ref[...])

    row = pl.BlockSpec((tr, c), lambda i: (i, 0))
    return pl.pallas_call(
        body, name=name, grid=(r // tr,),
        in_specs=[row, pl.BlockSpec((n, tr, c), lambda i: (0, i, 0)), row, row], out_specs=[row] * 4,
        out_shape=[jax.ShapeDtypeStruct((r, c), F32)] * 4, compiler_params=_params("arbitrary"),
    )(w, gparts, m, v)


MESH = pl.DeviceIdType.MESH
ANY = pl.BlockSpec(memory_space=pl.ANY)


def _all_gather(xs, name):
    n = len(xs)

    def body(*refs):
        x_refs, out_refs, sems = refs[:n], refs[n:2 * n], refs[2 * n:]
        start, forward, finish = _all_of([_gather_phases(x_refs[j], out_refs[j], *sems[3 * j:3 * j + 3]) for j in range(n)])
        start()
        forward()
        finish()

    return pl.pallas_call(
        body, name=name, in_specs=[ANY] * n, out_specs=[ANY] * n,
        out_shape=[jax.ShapeDtypeStruct((N_DEV,) + x.shape, x.dtype) for x in xs], scratch_shapes=COMM_SEMS * n,
    )(*xs)


COMM_SEMS = [pltpu.SemaphoreType.DMA((7,)), pltpu.SemaphoreType.DMA((7,)), pltpu.SemaphoreType.DMA(())]


def _gather_phases(x_ref, out_ref, send_sems, recv_sems, local_sem):
    mx, my, mc = lax.axis_index("x"), lax.axis_index("y"), lax.axis_index("c")
    me, sibling = (mx, my, mc), (mx, my, 1 - mc)
    chips = [(1 - mx, my), (mx, 1 - my), (1 - mx, 1 - my)]

    def slot(px, py, pc):
        return out_ref.at[4 * px + 2 * py + pc]

    def copy(kk, block, to, src=None):
        return pltpu.make_async_remote_copy(
            src_ref=slot(*block) if src is None else src, dst_ref=slot(*block),
            send_sem=send_sems.at[kk], recv_sem=recv_sems.at[kk], device_id=to, device_id_type=MESH)

    mine = pltpu.make_async_copy(x_ref, slot(*me), local_sem)
    first = [copy(0, me, sibling, src=x_ref)] + [copy(1 + j, me, (*chip, mc), src=x_ref) for j, chip in enumerate(chips)]
    passed = [copy(4 + j, (*chip, mc), sibling) for j, chip in enumerate(chips)]

    def start():
        mine.start()
        for cp in first:
            cp.start()

    def forward():
        for j, chip in enumerate(chips):
            copy(1 + j, (*chip, mc), me).wait_recv()
            passed[j].start()

    def finish():
        copy(0, sibling, me).wait_recv()
        for j, chip in enumerate(chips):
            copy(4 + j, (*chip, 1 - mc), me).wait_recv()
        for cp in first + passed:
            cp.wait_send()
        mine.wait()

    return start, forward, finish


def _all_of(phase_lists):
    def run(fns):
        def every():
            for fn in fns:
                fn()
        return every
    return tuple(run(fns) for fns in zip(*phase_lists))


def _exchange_phases(x_ref, out_ref, send_sems, recv_sems, local_sem):
    mx, my, mc = lax.axis_index("x"), lax.axis_index("y"), lax.axis_index("c")
    me = 4 * mx + 2 * my + mc
    mine = pltpu.make_async_copy(x_ref.at[me], out_ref.at[me], local_sem)
    copies = []
    for kk in range(1, N_DEV):
        px, py, pc = mx ^ (kk >> 2), my ^ ((kk >> 1) & 1), mc ^ (kk & 1)
        peer = 4 * px + 2 * py + pc
        copies.append((peer, pltpu.make_async_remote_copy(
            src_ref=x_ref.at[peer], dst_ref=out_ref.at[me], send_sem=send_sems.at[kk - 1],
            recv_sem=recv_sems.at[kk - 1], device_id=(px, py, pc), device_id_type=MESH)))

    def start():
        mine.start()
        for _, cp in copies:
            cp.start()

    def finish():
        for kk, (peer, _) in enumerate(copies):
            pltpu.make_async_remote_copy(
                src_ref=x_ref.at[peer], dst_ref=out_ref.at[peer], send_sem=send_sems.at[kk],
                recv_sem=recv_sems.at[kk], device_id=(mx, my, mc), device_id_type=MESH).wait_recv()
        for _, cp in copies:
            cp.wait_send()
        mine.wait()

    return start, finish


def _shards_last(a):
    lead = a.shape[:-1]
    return jnp.moveaxis(a.reshape(*lead, N_DEV, a.shape[-1] // N_DEV), -2, 0)


def _shards_rows(a):
    l, n, c = a.shape
    return jnp.moveaxis(a.reshape(l, N_DEV, n // N_DEV, c), 1, 0)


def _flat_pad(parts, mult):
    flat = jnp.concatenate([p.reshape(-1) for p in parts])
    return jnp.pad(flat, (0, (-flat.shape[0]) % mult))


def _unflatten(flat, shapes):
    out, o = [], 0
    for shp in shapes:
        n = 1
        for s in shp:
            n *= s
        out.append(flat[o:o + n].reshape(shp))
        o += n
    return out


def kernel(x, meta_tokens, g_mix, w_in, w_conv, w_pool, pool_scale, w_out, g_mlp, w_up, w_down, g_final, loss_target, m_meta_tokens, m_g_mix, m_w_in, m_w_conv, m_w_pool, m_pool_scale, m_w_out, m_g_mlp, m_w_up, m_w_down, m_g_final, v_meta_tokens, v_g_mix, v_w_in, v_w_conv, v_w_pool, v_pool_scale, v_w_out, v_g_mlp, v_w_up, v_w_down, v_g_final):
    nb, seq, d = x.shape
    depth = g_mix.shape[0]
    lp = BLK + seq
    t = nb * lp
    cw = w_conv.shape[2] * N_DEV
    pw = pool_scale.shape[1]
    pg = pw // N_POOL_GROUPS
    aw = (w_in.shape[2] * N_DEV - 3 * cw - pw) // 3
    ua_w = 3 * cw + pw
    d_ff = w_up.shape[2] * N_DEV
    me = 4 * lax.axis_index("x") + 2 * lax.axis_index("y") + lax.axis_index("c")

    def rows(a):
        return a.reshape(-1, a.shape[-1])

    def gathered_w_in(g, layers):
        return jnp.moveaxis(g.reshape(N_DEV, layers, d, -1), 0, 2).reshape(layers, d, -1)

    g_in0, small_in = _all_gather([rows(w_in[:1]).astype(BF16), _flat_pad([meta_tokens, w_conv], 8 * BLK).reshape(-1, BLK)],
                                  "gather_first_weights")
    w_in0_f = gathered_w_in(g_in0, 1)
    small_in = small_in.reshape(N_DEV, -1)
    meta_f = jnp.moveaxis(small_in[:, :meta_tokens.size].reshape(N_DEV, N_META, d // N_DEV), 0, 1).reshape(N_META, d)
    w_conv_f = small_in[:, meta_tokens.size:meta_tokens.size + w_conv.size].reshape((N_DEV,) + w_conv.shape)
    w_conv_f = jnp.moveaxis(w_conv_f, 0, 2).reshape(depth, w_conv.shape[1], cw)

    def split_w_in(w):
        return w[:, :ua_w], w[:, ua_w:ua_w + aw], w[:, ua_w + aw:ua_w + 2 * aw], w[:, ua_w + 2 * aw:]

    wbd = jnp.zeros((depth, pw, pw), F32)
    for gi in range(N_POOL_GROUPS):
        wbd = wbd.at[:, gi * pg:(gi + 1) * pg, gi * pg:(gi + 1) * pg].set(w_pool[:, gi])
    wbd = wbd.astype(BF16)

    h = jnp.concatenate([jnp.zeros((nb, PAD, d), F32), jnp.broadcast_to(meta_f[None], (nb, N_META, d)), x], axis=1).reshape(t, d)
    saved, w_in_parts = [], [None] * depth
    w_in_parts[0] = split_w_in(w_in0_f[0])
    for l in range(depth):
        wa, wq, wk, wv = w_in_parts[l]
        ua, q, k, v, hn1 = _inproj_fwd(h, g_mix[l][None], wa, wq, wk, wv)
        ycp = _mixer_fwd(ua, w_conv_f[l], wbd[l], pool_scale[l][None], nb)
        if l == 0:
            yat, g_in, g_out, g_up, g_down = _attn_fwd(
                q, k, v, nb, gather=[rows(a).astype(BF16) for a in (w_in[1:], w_out, w_up, w_down)])
            w_in_rest = gathered_w_in(g_in, depth - 1)
            for ll in range(1, depth):
                w_in_parts[ll] = split_w_in(w_in_rest[ll - 1])
            w_out_f = jnp.moveaxis(g_out.reshape((N_DEV,) + w_out.shape), 0, 1).reshape(depth, -1, d)
            wo_a, wo_b = w_out_f[:, :cw + pw], w_out_f[:, cw + pw:]
            w_up_f = g_up.reshape((N_DEV,) + w_up.shape)
            w_down_f = g_down.reshape((N_DEV,) + w_down.shape)
        else:
            yat = _attn_fwd(q, k, v, nb)
        h_mid =_outproj(h, ycp, yat, wo_a[l], wo_b[l])
        h_next, hn2 = _mlp_fwd(h_mid, g_mlp[l][None], w_up_f, w_down_f, l)
        saved.append((h, ua, q, k, v, hn1, ycp, yat, h_mid, hn2))
        h = h_next

    loss_part, dh, dg_final = _loss_head(h, g_final[None], loss_target.reshape(nb * seq, d), nb)

    g_gmix, g_gmlp, g_wconv, g_pscale, g_wpool = [None] * depth, [None] * depth, [None] * depth, [None] * depth, [None] * depth
    g_win, g_wout, g_wup, g_wdown = [None] * depth, [None] * depth, [None] * depth, [None] * depth
    for l in reversed(range(depth)):
        h_in, ua, q, k, v, hn1, ycp, yat, h_mid, hn2 = saved[l]
        wa, wq, wk, wv = w_in_parts[l]
        dh_mid, act, dm, g_gmlp[l] = _mlp_bwd(dh, h_mid, hn2, g_mlp[l][None], w_up_f, w_down_f, l)
        g_wup[l] = _matmul_tn(hn2, dm, "grad_w_up")
        g_wdown[l] = _matmul_tn(act, dh, "grad_w_down")
        dycp, do = _outproj_bwd(dh_mid, wo_a[l], wo_b[l])
        g_wout[l] = jnp.concatenate([_matmul_tn(ycp, dh_mid, "grad_w_out_a"), _matmul_tn(yat, dh_mid, "grad_w_out_b")], axis=0)
        if l == 0:
            send = [_shards_last(jnp.stack(g_win[1:])), _shards_rows(jnp.stack(g_wout)),
                    _shards_last(jnp.stack(g_wup)), _shards_rows(jnp.stack(g_wdown))]
            dq, dk, dv, p_in, p_out, p_up, p_down = _attn_bwd(
                q, k, v, do, nb, exchange=[s.reshape(N_DEV, -1, s.shape[-1]) for s in send])
        else:
            dq, dk, dv = _attn_bwd(q, k, v, do, nb)
        dua, g_wconv[l], g_pscale[l], dwb = _mixer_bwd(ua, dycp, w_conv_f[l], wbd[l], pool_scale[l][None], nb)
        g_wpool[l] = jnp.stack([dwb[gi * pg:(gi + 1) * pg, gi * pg:(gi + 1) * pg] for gi in range(N_POOL_GROUPS)])
        g_win[l] = jnp.concatenate([_matmul_tn(hn1, dua, "grad_w_in_a"), _matmul_tn(hn1, dq, "grad_w_in_q"),
                                    _matmul_tn(hn1, dk, "grad_w_in_k"), _matmul_tn(hn1, dv, "grad_w_in_v")], axis=1)
        if l == 0:
            dh, g_gmix[l], p_in0 = _inproj_bwd(dh_mid, h_in, g_mix[l][None], dua, dq, dk, dv, wa, wq, wk, wv,
                                               exchange=_shards_last(g_win[0][None]).reshape(N_DEV, -1, w_in.shape[-1]))
        else:
            dh, g_gmix[l] = _inproj_bwd(dh_mid, h_in, g_mix[l][None], dua, dq, dk, dv, wa, wq, wk, wv)
    dh3 = dh.reshape(nb, lp, d)
    grad_x = dh3[:, BLK:]
    g_meta = _sum_parts(dh3[:, PAD:BLK], "sum_meta_grad")

    o_out = _adamw(rows(w_out), p_out, rows(m_w_out), rows(v_w_out), "adamw_w_out")
    o_up = _adamw(rows(w_up), p_up, rows(m_w_up), rows(v_w_up), "adamw_w_up")
    o_down = _adamw(rows(w_down), p_down, rows(m_w_down), rows(v_w_down), "adamw_w_down")
    o_in1 = _adamw(rows(w_in[1:]), p_in, rows(m_w_in[1:]), rows(v_w_in[1:]), "adamw_w_in_rest")
    o_in0 = _adamw(rows(w_in[:1]), p_in0, rows(m_w_in[:1]), rows(v_w_in[:1]), "adamw_w_in0")
    (gr_win, gr_wout, gr_wup, gr_wdown), (de_win, de_wout, de_wup, de_wdown), (nm_win, nm_wout, nm_wup, nm_wdown), \
        (nv_win, nv_wout, nv_wup, nv_wdown) = [
            [jnp.concatenate([i0, i1], axis=0).reshape(w_in.shape), o.reshape(w_out.shape), u.reshape(w_up.shape), dn.reshape(w_down.shape)]
            for i0, i1, o, u, dn in zip(o_in0, o_in1, o_out, o_up, o_down)]

    small_full = [g_meta, jnp.concatenate(g_gmix), jnp.stack(g_wconv), jnp.stack(g_wpool), jnp.concatenate(g_pscale),
                  jnp.concatenate(g_gmlp), dg_final.reshape(-1)]
    small_shapes = [a.shape for a in small_full]
    small_sum = _sum_parts(_all_gather([_flat_pad(small_full, 8 * BLK).reshape(-1, BLK)], "gather_small_grads")[0], "sum_small_grads")
    gr_meta, gr_gmix, gr_wconv, gr_wpool, gr_pscale, gr_gmlp, gr_gfinal = _unflatten(small_sum.reshape(-1), small_shapes)
    gr_meta = lax.dynamic_slice_in_dim(gr_meta, me * (d // N_DEV), d // N_DEV, axis=1)
    gr_wconv = lax.dynamic_slice_in_dim(gr_wconv, me * (cw // N_DEV), cw // N_DEV, axis=2)
    small_g = [gr_meta, gr_gmix, gr_wconv, gr_wpool, gr_pscale, gr_gmlp, gr_gfinal]
    local_shapes = [a.shape for a in small_g]
    pack_small = lambda parts_: _flat_pad(parts_, 8 * BLK).reshape(-1, BLK)
    small = _adamw(pack_small([meta_tokens, g_mix, w_conv, w_pool, pool_scale, g_mlp, g_final]), pack_small(small_g)[None],
                   pack_small([m_meta_tokens, m_g_mix, m_w_conv, m_w_pool, m_pool_scale, m_g_mlp, m_g_final]),
                   pack_small([v_meta_tokens, v_g_mix, v_w_conv, v_w_pool, v_pool_scale, v_g_mlp, v_g_final]), "adamw_small")
    _, de_s, nm_s, nv_s = [_unflatten(b.reshape(-1), local_shapes) for b in small]

    loss = lax.psum(loss_part[0, 0], ("x", "y", "c"))

    def ordered(meta, gmix, wconv, wpool, pscale, gmlp, gfinal, win, wout, wup, wdown):
        return [meta, gmix, win, wconv, wpool, pscale, wout, gmlp, wup, wdown, gfinal]

    grads = ordered(*small_g, gr_win, gr_wout, gr_wup, gr_wdown)
    deltas = ordered(*de_s, de_win, de_wout, de_wup, de_wdown)
    new_m = ordered(*nm_s, nm_win, nm_wout, nm_wup, nm_wdown)
    new_v = ordered(*nv_s, nv_win, nv_wout, nv_wup, nv_wdown)
    return (loss, grad_x, *grads, *deltas, *new_m, *new_v)
```

```python
import jax
import jax.numpy as jnp
from jax import lax
from jax.experimental import pallas as pl
from jax.experimental.pallas import tpu as pltpu

F32 = jnp.float32
BF16 = jnp.bfloat16

N_DEV = 8
N_META = 16
HEAD_DIM = 64
BLK = 128
PAD = BLK - N_META
N_POOL_GROUPS = 4
EPS = 1e-6

ADAM_LR = 0.001
ADAM_B1 = 0.9
ADAM_B2 = 0.999
ADAM_EPS = 1e-08
ADAM_WD = 0.01
ADAM_STEP = 10

V7X_VMEM_LIMIT = 56 * 1024 * 1024


def _params(*sem):
    return pltpu.CompilerParams(dimension_semantics=sem, vmem_limit_bytes=V7X_VMEM_LIMIT)


SUBLANES = 8
ROW_TILE = 512
ROW_TILE_MLP = 1152
ROW_TILE_WIDE = 1152
TN_ACC_BYTES = 8 * 1024 * 1024


def _row_tile(t, cap=ROW_TILE):
    for tm in range(min(cap, t) // SUBLANES * SUBLANES, 0, -SUBLANES):
        if t % tm == 0:
            return tm
    raise ValueError(f"row count {t} is not a multiple of 8")


def _full(a):
    nd = a.ndim
    return pl.BlockSpec(a.shape, lambda *_: (0,) * nd)


def _dot(a, b):
    return jnp.dot(a, b, preferred_element_type=F32)


def _dot_nt(a, b):
    return lax.dot_general(a, b, (((1,), (1,)), ((), ())), preferred_element_type=F32)


def _dot_tn(a, b):
    return lax.dot_general(a, b, (((0,), (0,)), ((), ())), preferred_element_type=F32)


def _rms_fwd(x, g):
    rstd = lax.rsqrt(jnp.mean(x * x, axis=-1, keepdims=True) + EPS)
    return x * rstd * g


def _rms_bwd(dy, x, g):
    rstd = lax.rsqrt(jnp.mean(x * x, axis=-1, keepdims=True) + EPS)
    xhat = x * rstd
    dxhat = dy * g
    dx = rstd * (dxhat - xhat * jnp.mean(dxhat * xhat, axis=-1, keepdims=True))
    return dx, jnp.sum(dy * xhat, axis=0, keepdims=True)


def _inproj_fwd(h, g, wa, wq, wk, wv):
    t, d = h.shape
    tm = _row_tile(t, ROW_TILE_WIDE)
    ua_w, aw = wa.shape[1], wq.shape[1]

    def body(h_ref, g_ref, wa_ref, wq_ref, wk_ref, wv_ref, ua_ref, q_ref, k_ref, v_ref, hn_ref):
        hn = _rms_fwd(h_ref[...], g_ref[...]).astype(BF16)
        hn_ref[...] = hn
        ua_ref[...] = _dot(hn, wa_ref[...])
        q_ref[...] = _dot(hn, wq_ref[...]).astype(BF16)
        k_ref[...] = _dot(hn, wk_ref[...]).astype(BF16)
        v_ref[...] = _dot(hn, wv_ref[...]).astype(BF16)

    row = lambda w: pl.BlockSpec((tm, w), lambda i: (i, 0))
    return pl.pallas_call(
        body, name="inproj_fwd", grid=(t // tm,),
        in_specs=[row(d), _full(g), _full(wa), _full(wq), _full(wk), _full(wv)],
        out_specs=[row(ua_w), row(aw), row(aw), row(aw), row(d)],
        out_shape=[jax.ShapeDtypeStruct((t, ua_w), F32)] + [jax.ShapeDtypeStruct((t, aw), BF16)] * 3
        + [jax.ShapeDtypeStruct((t, d), BF16)],
        compiler_params=_params("arbitrary"),
    )(h, g, wa, wq, wk, wv)


def _pool_geometry(lp, pw):
    pg = pw // N_POOL_GROUPS
    row = lax.broadcasted_iota(jnp.int32, (lp, pw), 0)
    lane = lax.broadcasted_iota(jnp.int32, (lp, pw), 1)
    grp = [(lane >= g * pg) & (lane < (g + 1) * pg) for g in range(N_POOL_GROUPS)]
    wlen = jnp.where(grp[0], 2, jnp.where(grp[1], 4, jnp.where(grp[2], 8, 16)))
    cnt = jnp.clip(row - (PAD - 1), 1, wlen).astype(F32)
    return grp, cnt, row >= PAD


def _by_group(grp, vals):
    return jnp.where(grp[0], vals[0], jnp.where(grp[1], vals[1], jnp.where(grp[2], vals[2], vals[3])))


def _pooled(p, grp, cnt, real):
    s2 = p + pltpu.roll(p, 1, 0)
    s4 = s2 + pltpu.roll(s2, 2, 0)
    s8 = s4 + pltpu.roll(s4, 4, 0)
    s16 = s8 + pltpu.roll(s8, 8, 0)
    return jnp.where(real, _by_group(grp, (s2, s4, s8, s16)) / cnt - p, 0.0)


def _conv(uu, wc_ref):
    return wc_ref[2:3, :] * uu + wc_ref[1:2, :] * pltpu.roll(uu, 1, 0) + wc_ref[0:1, :] * pltpu.roll(uu, 2, 0)


def _mixer_fwd(ua, wconv, wbd, pscale, nb):
    t = ua.shape[0]
    lp = t // nb
    cw, pw = wconv.shape[1], wbd.shape[0]

    def body(ua_ref, wc_ref, wbd_ref, ps_ref, y_ref):
        cb = ua_ref[:, 0:cw]
        uu = ua_ref[:, cw:2 * cw] * ua_ref[:, 2 * cw:3 * cw]
        y_ref[:, 0:cw] = (cb * _conv(uu, wc_ref)).astype(BF16)
        grp, cnt, real = _pool_geometry(lp, pw)
        pooled = _pooled(ua_ref[:, 3 * cw:3 * cw + pw], grp, cnt, real)
        y_ref[:, cw:cw + pw] = (_dot(pooled.astype(BF16), wbd_ref[...]) * ps_ref[...]).astype(BF16)

    return pl.pallas_call(
        body, name="mixer_fwd", grid=(nb,),
        in_specs=[pl.BlockSpec((lp, ua.shape[1]), lambda s: (s, 0)), _full(wconv), _full(wbd), _full(pscale)],
        out_specs=pl.BlockSpec((lp, cw + pw), lambda s: (s, 0)),
        out_shape=jax.ShapeDtypeStruct((t, cw + pw), BF16),
        compiler_params=_params("arbitrary"),
    )(ua, wconv, wbd, pscale)


ATT_NW = 2
ATT_W = ATT_NW * BLK
ATT_PAIRS_FWD = 4
ATT_PAIRS_BWD = 4
HW = 2 * HEAD_DIM
ATT_SCALE = HEAD_DIM ** -0.5


def _attn_lk(z, valid):
    nz = -z
    return jnp.where(valid, jnp.minimum(nz, 0.0) - jnp.log(1.0 + jnp.exp(jnp.minimum(z, nz))), 0.0)


def _tri_sum_bf16(x, tri):
    return _dot(x.astype(BF16), tri)


def _tri_sum(x, tri):
    hi = x.astype(BF16)
    lo = (x - hi.astype(F32)).astype(BF16)
    return _dot(hi, tri) + _dot(lo, tri)


def _attn_weights(z, suffix, carry, valid):
    return jnp.where(valid, jnp.exp(z + suffix + carry), 0.0)


def _attn_consts():
    r = lax.broadcasted_iota(jnp.int32, (2 * BLK, ATT_W), 0)
    c = lax.broadcasted_iota(jnp.int32, (2 * BLK, ATT_W), 1)
    cmr = c - (r & (BLK - 1))
    kr = lax.broadcasted_iota(jnp.int32, (ATT_W, ATT_W), 0)
    kc = lax.broadcasted_iota(jnp.int32, (ATT_W, ATT_W), 1)
    lane = lax.broadcasted_iota(jnp.int32, (1, HW), 1)
    heads = (lane < HEAD_DIM, lane >= HEAD_DIM)
    return c, cmr, kr, kc, heads


def _tri(cond):
    return jnp.where(cond, 1.0, 0.0).astype(BF16)


def _rows(i):
    return pl.ds(pl.multiple_of(i * BLK, BLK), BLK)


def _stack_heads(x, heads):
    zero = jnp.zeros_like(x)
    return jnp.concatenate([jnp.where(heads[0], x, zero), jnp.where(heads[1], x, zero)], axis=0)


def _unstack_heads(x, heads):
    return jnp.where(heads[0], x[:BLK], x[BLK:])


def _window(i, s):
    start = jnp.maximum(i + 1 - ATT_NW * (s + 1), 0) * BLK
    return start, (i + 1 - ATT_NW * s) * BLK


def _attn_fwd(q, k, v, nb, gather=None):
    t, aw = q.shape
    lp = t // nb
    nblk = lp // BLK
    pp = min(ATT_PAIRS_FWD, aw // HW)
    cols = [slice(p * HW, (p + 1) * HW) for p in range(pp)]
    ncol = aw // (pp * HW)
    nsteps = nb * ncol

    def body(q_ref, k_ref, v_ref, *rest):
        if gather is None:
            (o_ref,) = rest
        else:
            ng = len(gather)
            x_refs, o_ref, g_refs, sems = rest[:ng], rest[ng], rest[ng + 1:2 * ng + 1], rest[2 * ng + 1:]
            step = pl.program_id(0) * ncol + pl.program_id(1)
            start, forward, finish = _all_of([_gather_phases(x_refs[j], g_refs[j], *sems[3 * j:3 * j + 3]) for j in range(ng)])
            pl.when(step == 0)(start)
            pl.when(step == nsteps - 1)(forward)
        c, cmr, kr, kc, heads = _attn_consts()
        m_from = _tri(kr >= kc)

        def qblock(i, _):
            qs = [_stack_heads(q_ref[_rows(i), cs] * ATT_SCALE, heads) for cs in cols]

            def window(s, st):
                accs, carries = st
                start, end = _window(i, s)
                keys = pl.ds(pl.multiple_of(start, BLK), ATT_W)
                valid = (cmr < i * BLK - start) & (c < end - start)
                zs = [_dot_nt(qs[p], k_ref[keys, cs]) for p, cs in enumerate(cols)]
                suffix = [_tri_sum_bf16(_attn_lk(zs[p], valid), m_from) for p in range(pp)]
                new_accs, new_carries = [], []
                for p, cs in enumerate(cols):
                    a = _attn_weights(zs[p], suffix[p], carries[p], valid)
                    new_accs.append(accs[p] + _dot(a.astype(BF16), v_ref[keys, cs]))
                    new_carries.append(carries[p] + suffix[p][:, 0:1])
                return tuple(new_accs), tuple(new_carries)

            init = (tuple(jnp.zeros((2 * BLK, HW), F32) for _ in cols), tuple(jnp.zeros((2 * BLK, 1), F32) for _ in cols))
            accs, _ = lax.fori_loop(0, (i + ATT_NW) // ATT_NW, window, init)
            for p, cs in enumerate(cols):
                o_ref[_rows(i), cs] = _unstack_heads(accs[p], heads).astype(BF16)
            return 0

        lax.fori_loop(0, nblk, qblock, 0)
        if gather is not None:
            pl.when(step == nsteps - 1)(finish)

    spec = pl.BlockSpec((lp, pp * HW), lambda s, p: (s, p))
    y_shape = jax.ShapeDtypeStruct((t, aw), BF16)
    if gather is None:
        return pl.pallas_call(
            body, name="attn_fwd", grid=(nb, ncol), in_specs=[spec, spec, spec], out_specs=spec, out_shape=y_shape,
            compiler_params=_params("arbitrary", "arbitrary"),
        )(q, k, v)
    return pl.pallas_call(
        body, name="attn_fwd_gather", grid=(nb, ncol), in_specs=[spec, spec, spec] + [ANY] * len(gather),
        out_specs=[spec] + [ANY] * len(gather),
        out_shape=[y_shape] + [jax.ShapeDtypeStruct((N_DEV,) + x.shape, x.dtype) for x in gather],
        scratch_shapes=COMM_SEMS * len(gather), compiler_params=_params("arbitrary", "arbitrary"),
    )(q, k, v, *gather)


def _outproj(h, ya, yb, wa, wb):
    t, d = h.shape
    tm = _row_tile(t, ROW_TILE_WIDE)

    def body(h_ref, ya_ref, yb_ref, wa_ref, wb_ref, o_ref):
        o_ref[...] = h_ref[...] + _dot(ya_ref[...], wa_ref[...]) + _dot(yb_ref[...], wb_ref[...])

    row = lambda w: pl.BlockSpec((tm, w), lambda i: (i, 0))
    return pl.pallas_call(
        body, name="outproj", grid=(t // tm,),
        in_specs=[row(d), row(ya.shape[1]), row(yb.shape[1]), _full(wa), _full(wb)],
        out_specs=row(d), out_shape=jax.ShapeDtypeStruct((t, d), F32),
        compiler_params=_params("arbitrary"),
    )(h, ya, yb, wa, wb)


def _mlp_weight_specs(wup, wdown, layer):
    nff, _, d, fc = wup.shape
    return nff, fc, [pl.BlockSpec((None, None, d, fc), lambda i, kk: (kk, layer, 0, 0)),
                     pl.BlockSpec((None, None, fc, d), lambda i, kk: (kk, layer, 0, 0))]


def _mlp_fwd(h, g, wup, wdown, layer):
    t, d = h.shape
    tm = _row_tile(t, ROW_TILE_MLP)
    nff, fc, wspecs = _mlp_weight_specs(wup, wdown, layer)

    def body(h_ref, g_ref, wu_ref, wd_ref, o_ref, hn_ref, acc):
        kk = pl.program_id(1)

        @pl.when(kk == 0)
        def _():
            x = h_ref[...]
            hn_ref[...] = _rms_fwd(x, g_ref[...]).astype(BF16)
            acc[...] = x

        m = _dot(hn_ref[...], wu_ref[...])
        a = jnp.square(jnp.maximum(m, 0.0)).astype(BF16)
        acc[...] += _dot(a, wd_ref[...])

        @pl.when(kk == nff - 1)
        def _():
            o_ref[...] = acc[...]

    row = pl.BlockSpec((tm, d), lambda i, kk: (i, 0))
    return pl.pallas_call(
        body, name="mlp_fwd", grid=(t // tm, nff),
        in_specs=[row, _full(g)] + wspecs,
        out_specs=[row, row],
        out_shape=[jax.ShapeDtypeStruct((t, d), F32), jax.ShapeDtypeStruct((t, d), BF16)],
        scratch_shapes=[pltpu.VMEM((tm, d), F32)],
        compiler_params=_params("arbitrary", "arbitrary"),
    )(h, g, wup, wdown)


def _loss_head(h, g, target, nb):
    t, d = h.shape
    nblk = t // nb // BLK
    nx = nblk - 1

    def body(h_ref, g_ref, t_ref, loss_ref, dh_ref, dg_ref):
        s, i = pl.program_id(0), pl.program_id(1)

        @pl.when((s == 0) & (i == 0))
        def _():
            loss_ref[...] = jnp.zeros_like(loss_ref)
            dg_ref[...] = jnp.zeros_like(dg_ref)

        @pl.when(i == 0)
        def _():
            dh_ref[...] = jnp.zeros_like(dh_ref)

        @pl.when(i > 0)
        def _():
            x, gg = h_ref[...], g_ref[...]
            err = _rms_fwd(x, gg) - t_ref[...]
            loss_ref[...] += jnp.sum(err * err) * (0.5 / d)
            dx, dg = _rms_bwd(err * (1.0 / d), x, gg)
            dh_ref[...] = dx
            dg_ref[...] += dg

    hspec = pl.BlockSpec((BLK, d), lambda s, i: (s * nblk + i, 0))
    return pl.pallas_call(
        body, name="loss_head", grid=(nb, nblk),
        in_specs=[hspec, _full(g), pl.BlockSpec((BLK, d), lambda s, i: (s * nx + jnp.maximum(i - 1, 0), 0))],
        out_specs=[pl.BlockSpec((1, BLK), lambda s, i: (0, 0)), hspec, pl.BlockSpec((1, d), lambda s, i: (0, 0))],
        out_shape=[jax.ShapeDtypeStruct((1, BLK), F32), jax.ShapeDtypeStruct((t, d), F32), jax.ShapeDtypeStruct((1, d), F32)],
        compiler_params=_params("arbitrary", "arbitrary"),
    )(h, g, target)


def _mlp_bwd(dh, h, hn, g, wup, wdown, layer):
    t, d = h.shape
    tm = _row_tile(t, ROW_TILE_MLP)
    nff, fc, wspecs = _mlp_weight_specs(wup, wdown, layer)

    def body(dh_ref, h_ref, hn_ref, g_ref, wu_ref, wd_ref, dhm_ref, a_ref, dm_ref, dg_ref, dhn, dhb):
        i, kk = pl.program_id(0), pl.program_id(1)

        @pl.when((i == 0) & (kk == 0))
        def _():
            dg_ref[...] = jnp.zeros_like(dg_ref)

        @pl.when(kk == 0)
        def _():
            dhb[...] = dh_ref[...].astype(BF16)
            dhn[...] = jnp.zeros_like(dhn)

        r = jnp.maximum(_dot(hn_ref[...], wu_ref[...]), 0.0)
        a_ref[...] = (r * r).astype(BF16)
        dm = (_dot_nt(dhb[...], wd_ref[...]) * (2.0 * r)).astype(BF16)
        dm_ref[...] = dm
        dhn[...] += _dot_nt(dm, wu_ref[...])

        @pl.when(kk == nff - 1)
        def _():
            dx, dg = _rms_bwd(dhn[...], h_ref[...], g_ref[...])
            dhm_ref[...] = dh_ref[...] + dx
            dg_ref[...] += dg

    row = pl.BlockSpec((tm, d), lambda i, kk: (i, 0))
    ff = pl.BlockSpec((tm, fc), lambda i, kk: (i, kk))
    return pl.pallas_call(
        body, name="mlp_bwd", grid=(t // tm, nff),
        in_specs=[row, row, row, _full(g)] + wspecs,
        out_specs=[row, ff, ff, pl.BlockSpec((1, d), lambda i, kk: (0, 0))],
        out_shape=[jax.ShapeDtypeStruct((t, d), F32), jax.ShapeDtypeStruct((t, nff * fc), BF16),
                   jax.ShapeDtypeStruct((t, nff * fc), BF16), jax.ShapeDtypeStruct((1, d), F32)],
        scratch_shapes=[pltpu.VMEM((tm, d), F32), pltpu.VMEM((tm, d), BF16)],
        compiler_params=_params("arbitrary", "arbitrary"),
    )(dh, h, hn, g, wup, wdown)


def _matmul_tn(x, y, name):
    t, k1 = x.shape
    n = y.shape[1]
    tt = _row_tile(t)
    tn = min(n, max(512, TN_ACC_BYTES // (4 * k1)))
    steps = t // tt

    def body(x_ref, y_ref, o_ref, acc):
        @pl.when(pl.program_id(1) == 0)
        def _():
            acc[...] = jnp.zeros_like(acc)

        acc[...] += _dot_tn(x_ref[...].astype(BF16), y_ref[...].astype(BF16))

        @pl.when(pl.program_id(1) == steps - 1)
        def _():
            o_ref[...] = acc[...].astype(BF16)

    return pl.pallas_call(
        body, name=name, grid=(n // tn, steps),
        in_specs=[pl.BlockSpec((tt, k1), lambda j, r: (r, 0)), pl.BlockSpec((tt, tn), lambda j, r: (r, j))],
        out_specs=pl.BlockSpec((k1, tn), lambda j, r: (0, j)),
        out_shape=jax.ShapeDtypeStruct((k1, n), BF16),
        scratch_shapes=[pltpu.VMEM((k1, tn), F32)],
        compiler_params=_params("arbitrary", "arbitrary"),
    )(x, y)


def _outproj_bwd(dh, wa, wb):
    t, d = dh.shape
    tm = _row_tile(t, ROW_TILE_WIDE)
    na, nbw = wa.shape[0], wb.shape[0]

    def body(dh_ref, wa_ref, wb_ref, da_ref, db_ref):
        x = dh_ref[...].astype(BF16)
        da_ref[...] = _dot_nt(x, wa_ref[...])
        db_ref[...] = _dot_nt(x, wb_ref[...]).astype(BF16)

    row = lambda w: pl.BlockSpec((tm, w), lambda i: (i, 0))
    return pl.pallas_call(
        body, name="outproj_bwd", grid=(t // tm,),
        in_specs=[row(d), _full(wa), _full(wb)], out_specs=[row(na), row(nbw)],
        out_shape=[jax.ShapeDtypeStruct((t, na), F32), jax.ShapeDtypeStruct((t, nbw), BF16)],
        compiler_params=_params("arbitrary"),
    )(dh, wa, wb)


def _attn_bwd(q, k, v, do, nb, exchange=None):
    t, aw = q.shape
    lp = t // nb
    nblk = lp // BLK
    pp = min(ATT_PAIRS_BWD, aw // HW)
    cols = [slice(p * HW, (p + 1) * HW) for p in range(pp)]
    ncol = aw // (pp * HW)
    assert ncol == 1, "the [dq | dk | dv] output block needs all head pairs in one grid step"
    nsteps = nb * ncol
    max_windows = (nblk + ATT_NW - 1) // ATT_NW

    def body(q_ref, k_ref, v_ref, do_ref, *rest):
        if exchange is None:
            dqkv_ref, dk_acc, dv_acc, g_s, sig_s = rest
        else:
            ne = len(exchange)
            x_refs, dqkv_ref, e_refs = rest[:ne], rest[ne], rest[ne + 1:2 * ne + 1]
            dk_acc, dv_acc, g_s, sig_s = rest[2 * ne + 1:2 * ne + 5]
            sems = rest[2 * ne + 5:]
            step = pl.program_id(0) * ncol + pl.program_id(1)
            start, finish = _all_of([_exchange_phases(x_refs[j], e_refs[j], *sems[3 * j:3 * j + 3]) for j in range(ne)])
            pl.when(step == 0)(start)
        c, cmr, kr, kc, heads = _attn_consts()
        m_from = _tri(kr >= kc)
        m_before = _tri(kr < kc)
        dk_acc[...] = jnp.zeros_like(dk_acc)
        dv_acc[...] = jnp.zeros_like(dv_acc)

        def qblock(i, _):
            qs = [_stack_heads(q_ref[_rows(i), cs] * ATT_SCALE, heads) for cs in cols]
            dos = [_stack_heads(do_ref[_rows(i), cs], heads) for cs in cols]
            nwin = (i + ATT_NW) // ATT_NW

            def geometry(s):
                start, end = _window(i, s)
                return (cmr < i * BLK - start) & (c < end - start), pl.ds(pl.multiple_of(start, BLK), ATT_W)

            def down(s, carries):
                valid, keys = geometry(s)
                zs = [_dot_nt(qs[p], k_ref[keys, cs]) for p, cs in enumerate(cols)]
                das = [_dot_nt(dos[p], v_ref[keys, cs]) for p, cs in enumerate(cols)]
                suffix = []
                for p in range(pp):
                    lk = _attn_lk(zs[p], valid)
                    sig_s[p, s] = jnp.exp(zs[p] + lk)
                    suffix.append(_tri_sum_bf16(lk, m_from))
                new_carries = []
                for p, cs in enumerate(cols):
                    a = _attn_weights(zs[p], suffix[p], carries[p], valid)
                    g_s[p, s] = das[p] * a
                    dv_acc[keys, cs] += _dot_tn(a.astype(BF16), dos[p])
                    new_carries.append(carries[p] + suffix[p][:, 0:1])
                return tuple(new_carries)

            lax.fori_loop(0, nwin, down, tuple(jnp.zeros((2 * BLK, 1), F32) for _ in cols))

            def up(ss, st):
                accs, carries = st
                s = nwin - 1 - ss
                valid, keys = geometry(s)
                gs = [g_s[p, s] for p in range(pp)]
                before = [_tri_sum(gs[p], m_before) for p in range(pp)]
                new_accs, new_carries = [], []
                for p, cs in enumerate(cols):
                    g, sig = gs[p], sig_s[p, s]
                    dz = jnp.where(valid, g * (1.0 - sig) - (before[p] + carries[p]) * sig, 0.0).astype(BF16)
                    new_accs.append(accs[p] + _dot(dz, k_ref[keys, cs]))
                    dk_acc[keys, cs] += _dot_tn(dz, qs[p])
                    new_carries.append(carries[p] + jnp.sum(g, axis=1, keepdims=True))
                return tuple(new_accs), tuple(new_carries)

            init = (tuple(jnp.zeros((2 * BLK, HW), F32) for _ in cols), tuple(jnp.zeros((2 * BLK, 1), F32) for _ in cols))
            accs, _ = lax.fori_loop(0, nwin, up, init)
            for p, cs in enumerate(cols):
                dqkv_ref[_rows(i), cs] = (_unstack_heads(accs[p], heads) * ATT_SCALE).astype(BF16)
            return 0

        lax.fori_loop(0, nblk, qblock, 0)
        dqkv_ref[:, aw:2 * aw] = dk_acc[...].astype(BF16)
        dqkv_ref[:, 2 * aw:3 * aw] = dv_acc[...].astype(BF16)
        if exchange is not None:
            pl.when(step == nsteps - 1)(finish)

    spec = pl.BlockSpec((lp, pp * HW), lambda s, p: (s, p), pipeline_mode=pl.Buffered(1))
    d_spec = pl.BlockSpec((lp, 3 * aw), lambda s, p: (s, 0), pipeline_mode=pl.Buffered(1))
    d_shape = jax.ShapeDtypeStruct((t, 3 * aw), BF16)
    scratch = [pltpu.VMEM((lp, pp * HW), F32), pltpu.VMEM((lp, pp * HW), F32),
               pltpu.VMEM((pp, max_windows, 2 * BLK, ATT_W), F32), pltpu.VMEM((pp, max_windows, 2 * BLK, ATT_W), F32)]
    if exchange is None:
        return pl.pallas_call(
            body, name="attn_bwd", grid=(nb, ncol), in_specs=[spec] * 4, out_specs=d_spec, out_shape=d_shape,
            scratch_shapes=scratch, compiler_params=_params("arbitrary", "arbitrary"),
        )(q, k, v, do)
    return pl.pallas_call(
        body, name="attn_bwd_exchange", grid=(nb, ncol), in_specs=[spec] * 4 + [ANY] * len(exchange),
        out_specs=[d_spec] + [ANY] * len(exchange),
        out_shape=[d_shape] + [jax.ShapeDtypeStruct(x.shape, x.dtype) for x in exchange],
        scratch_shapes=scratch + COMM_SEMS * len(exchange), compiler_params=_params("arbitrary", "arbitrary"),
    )(q, k, v, do, *exchange)


def _mixer_bwd(ua, dy, wconv, wbd, pscale, nb):
    t = ua.shape[0]
    lp = t // nb
    cw, pw = wconv.shape[1], wbd.shape[0]

    def body(ua_ref, dy_ref, wc_ref, wbd_ref, ps_ref, du_ref, dwc_ref, dps_ref, dwb_ref):
        @pl.when(pl.program_id(0) == 0)
        def _():
            dwc_ref[...] = jnp.zeros_like(dwc_ref)
            dps_ref[...] = jnp.zeros_like(dps_ref)
            dwb_ref[...] = jnp.zeros_like(dwb_ref)

        up = lambda x, n: pltpu.roll(x, lp - n, 0)
        cb, cc, cx = ua_ref[:, 0:cw], ua_ref[:, cw:2 * cw], ua_ref[:, 2 * cw:3 * cw]
        uu = cc * cx
        dyc = dy_ref[:, 0:cw]
        du_ref[:, 0:cw] = (dyc * _conv(uu, wc_ref)).astype(BF16)
        dcv = dyc * cb
        duu = wc_ref[2:3, :] * dcv + wc_ref[1:2, :] * up(dcv, 1) + wc_ref[0:1, :] * up(dcv, 2)
        du_ref[:, cw:2 * cw] = (duu * cx).astype(BF16)
        du_ref[:, 2 * cw:3 * cw] = (duu * cc).astype(BF16)
        dwc_ref[0:1, :] += jnp.sum(dcv * pltpu.roll(uu, 2, 0), axis=0, keepdims=True)
        dwc_ref[1:2, :] += jnp.sum(dcv * pltpu.roll(uu, 1, 0), axis=0, keepdims=True)
        dwc_ref[2:3, :] += jnp.sum(dcv * uu, axis=0, keepdims=True)
        grp, cnt, real = _pool_geometry(lp, pw)
        p = ua_ref[:, 3 * cw:3 * cw + pw]
        pooled = _pooled(p, grp, cnt, real).astype(BF16)
        dyp = dy_ref[:, cw:cw + pw]
        dps_ref[...] += jnp.sum(dyp * _dot(pooled, wbd_ref[...]), axis=0, keepdims=True)
        dpre = (dyp * ps_ref[...]).astype(BF16)
        dwb_ref[...] += _dot_tn(pooled, dpre)
        dpooled = jnp.where(real, _dot_nt(dpre, wbd_ref[...]), 0.0)
        xm = dpooled / cnt
        l2 = xm + up(xm, 1)
        l4 = l2 + up(l2, 2)
        l8 = l4 + up(l4, 4)
        l16 = l8 + up(l8, 8)
        du_ref[:, 3 * cw:3 * cw + pw] = jnp.where(real, _by_group(grp, (l2, l4, l8, l16)) - dpooled, 0.0).astype(BF16)

    seq = lambda w: pl.BlockSpec((lp, w), lambda s: (s, 0))
    return pl.pallas_call(
        body, name="mixer_bwd", grid=(nb,),
        in_specs=[seq(ua.shape[1]), seq(cw + pw), _full(wconv), _full(wbd), _full(pscale)],
        out_specs=[seq(ua.shape[1]), pl.BlockSpec((3, cw), lambda s: (0, 0)), pl.BlockSpec((1, pw), lambda s: (0, 0)),
                   pl.BlockSpec((pw, pw), lambda s: (0, 0))],
        out_shape=[jax.ShapeDtypeStruct(ua.shape, BF16), jax.ShapeDtypeStruct((3, cw), F32),
                   jax.ShapeDtypeStruct((1, pw), F32), jax.ShapeDtypeStruct((pw, pw), F32)],
        compiler_params=_params("arbitrary"),
    )(ua, dy, wconv, wbd, pscale)


def _inproj_bwd(dh, h, g, dua, dqkv, wa, wqkv, exchange=None):
    t, d = h.shape
    tm = _row_tile(t)
    nsteps = t // tm

    def body(dh_ref, h_ref, g_ref, dua_ref, dqkv_ref, wa_ref, wqkv_ref, *rest):
        if exchange is None:
            o_ref, dg_ref = rest
        else:
            x_ref, o_ref, dg_ref, e_ref, send_sems, recv_sems, local_sem = rest
            start, finish = _exchange_phases(x_ref, e_ref, send_sems, recv_sems, local_sem)
            pl.when(pl.program_id(0) == 0)(start)

        @pl.when(pl.program_id(0) == 0)
        def _():
            dg_ref[...] = jnp.zeros_like(dg_ref)

        dhn = _dot_nt(dua_ref[...], wa_ref[...]) + _dot_nt(dqkv_ref[...], wqkv_ref[...])
        dx, dg = _rms_bwd(dhn, h_ref[...], g_ref[...])
        o_ref[...] = dh_ref[...] + dx
        dg_ref[...] += dg
        if exchange is not None:
            pl.when(pl.program_id(0) == nsteps - 1)(finish)

    row = lambda w: pl.BlockSpec((tm, w), lambda i: (i, 0))
    in_specs = [row(d), row(d), _full(g), row(dua.shape[1]), row(dqkv.shape[1]), _full(wa), _full(wqkv)]
    out_specs = [row(d), pl.BlockSpec((1, d), lambda i: (0, 0))]
    out_shape = [jax.ShapeDtypeStruct((t, d), F32), jax.ShapeDtypeStruct((1, d), F32)]
    if exchange is None:
        return pl.pallas_call(
            body, name="inproj_bwd", grid=(nsteps,), in_specs=in_specs, out_specs=out_specs, out_shape=out_shape,
            compiler_params=_params("arbitrary"),
        )(dh, h, g, dua, dqkv, wa, wqkv)
    return pl.pallas_call(
        body, name="inproj_bwd_exchange", grid=(nsteps,), in_specs=in_specs + [ANY], out_specs=out_specs + [ANY],
        out_shape=out_shape + [jax.ShapeDtypeStruct(exchange.shape, exchange.dtype)], scratch_shapes=COMM_SEMS,
        compiler_params=_params("arbitrary"),
    )(dh, h, g, dua, dqkv, wa, wqkv, exchange)


def _part_tile(r):
    if r <= 512:
        return r
    for tr in (256, 128, 64, 32, 16, 8):
        if r % tr == 0:
            return tr
    return r


def _sum_parts(x, name):
    n, r, c = x.shape
    tr = _part_tile(r)

    def body(x_ref, o_ref):
        acc = x_ref[0]
        for p in range(1, n):
            acc = acc + x_ref[p]
        o_ref[...] = acc

    return pl.pallas_call(
        body, name=name, grid=(r // tr,),
        in_specs=[pl.BlockSpec((n, tr, c), lambda i: (0, i, 0))], out_specs=pl.BlockSpec((tr, c), lambda i: (i, 0)),
        out_shape=jax.ShapeDtypeStruct((r, c), F32), compiler_params=_params("arbitrary"),
    )(x)


def _adamw(w, gparts, m, v, name):
    n, r, c = gparts.shape
    tr = _part_tile(r)

    def body(w_ref, g_ref, m_ref, v_ref, go_ref, d_ref, mo_ref, vo_ref):
        g = g_ref[0].astype(F32)
        for p in range(1, n):
            g = g + g_ref[p].astype(F32)
        go_ref[...] = g
        mm = ADAM_B1 * m_ref[...] + (1.0 - ADAM_B1) * g
        vv = ADAM_B2 * v_ref[...] + (1.0 - ADAM_B2) * jnp.square(g)
        mo_ref[...] = mm
        vo_ref[...] = vv
        m_hat = mm / (1.0 - ADAM_B1 ** ADAM_STEP)
        v_hat = vv / (1.0 - ADAM_B2 ** ADAM_STEP)
        d_ref[...] = -ADAM_LR * (m_hat / (jnp.sqrt(v_hat) + ADAM_EPS) + ADAM_WD * w_ref[...])

    row = pl.BlockSpec((tr, c), lambda i: (i, 0))
    return pl.pallas_call(
        body, name=name, grid=(r // tr,),
        in_specs=[row, pl.BlockSpec((n, tr, c), lambda i: (0, i, 0)), row, row], out_specs=[row] * 4,
        out_shape=[jax.ShapeDtypeStruct((r, c), F32)] * 4, compiler_params=_params("arbitrary"),
    )(w, gparts, m, v)


MESH = pl.DeviceIdType.MESH
ANY = pl.BlockSpec(memory_space=pl.ANY)


def _all_gather(xs, name):
    n = len(xs)

    def body(*refs):
        x_refs, out_refs, sems = refs[:n], refs[n:2 * n], refs[2 * n:]
        start, forward, finish = _all_of([_gather_phases(x_refs[j], out_refs[j], *sems[3 * j:3 * j + 3]) for j in range(n)])
        start()
        forward()
        finish()

    return pl.pallas_call(
        body, name=name, in_specs=[ANY] * n, out_specs=[ANY] * n,
        out_shape=[jax.ShapeDtypeStruct((N_DEV,) + x.shape, x.dtype) for x in xs], scratch_shapes=COMM_SEMS * n,
    )(*xs)


COMM_SEMS = [pltpu.SemaphoreType.DMA((7,)), pltpu.SemaphoreType.DMA((7,)), pltpu.SemaphoreType.DMA(())]


def _gather_phases(x_ref, out_ref, send_sems, recv_sems, local_sem):
    mx, my, mc = lax.axis_index("x"), lax.axis_index("y"), lax.axis_index("c")
    me, sibling = (mx, my, mc), (mx, my, 1 - mc)
    chips = [(1 - mx, my), (mx, 1 - my), (1 - mx, 1 - my)]

    def slot(px, py, pc):
        return out_ref.at[4 * px + 2 * py + pc]

    def copy(kk, block, to, src=None):
        return pltpu.make_async_remote_copy(
            src_ref=slot(*block) if src is None else src, dst_ref=slot(*block),
            send_sem=send_sems.at[kk], recv_sem=recv_sems.at[kk], device_id=to, device_id_type=MESH)

    mine = pltpu.make_async_copy(x_ref, slot(*me), local_sem)
    first = [copy(0, me, sibling, src=x_ref)] + [copy(1 + j, me, (*chip, mc), src=x_ref) for j, chip in enumerate(chips)]
    passed = [copy(4 + j, (*chip, mc), sibling) for j, chip in enumerate(chips)]

    def start():
        mine.start()
        for cp in first:
            cp.start()

    def forward():
        for j, chip in enumerate(chips):
            copy(1 + j, (*chip, mc), me).wait_recv()
            passed[j].start()

    def finish():
        copy(0, sibling, me).wait_recv()
        for j, chip in enumerate(chips):
            copy(4 + j, (*chip, 1 - mc), me).wait_recv()
        for cp in first + passed:
            cp.wait_send()
        mine.wait()

    return start, forward, finish


def _all_of(phase_lists):
    def run(fns):
        def every():
            for fn in fns:
                fn()
        return every
    return tuple(run(fns) for fns in zip(*phase_lists))


def _exchange_phases(x_ref, out_ref, send_sems, recv_sems, local_sem):
    mx, my, mc = lax.axis_index("x"), lax.axis_index("y"), lax.axis_index("c")
    me = 4 * mx + 2 * my + mc
    mine = pltpu.make_async_copy(x_ref.at[me], out_ref.at[me], local_sem)
    copies = []
    for kk in range(1, N_DEV):
        px, py, pc = mx ^ (kk >> 2), my ^ ((kk >> 1) & 1), mc ^ (kk & 1)
        peer = 4 * px + 2 * py + pc
        copies.append((peer, pltpu.make_async_remote_copy(
            src_ref=x_ref.at[peer], dst_ref=out_ref.at[me], send_sem=send_sems.at[kk - 1],
            recv_sem=recv_sems.at[kk - 1], device_id=(px, py, pc), device_id_type=MESH)))

    def start():
        mine.start()
        for _, cp in copies:
            cp.start()

    def finish():
        for kk, (peer, _) in enumerate(copies):
            pltpu.make_async_remote_copy(
                src_ref=x_ref.at[peer], dst_ref=out_ref.at[peer], send_sem=send_sems.at[kk],
                recv_sem=recv_sems.at[kk], device_id=(mx, my, mc), device_id_type=MESH).wait_recv()
        for _, cp in copies:
            cp.wait_send()
        mine.wait()

    return start, finish


def _shards_last(a):
    lead = a.shape[:-1]
    return jnp.moveaxis(a.reshape(*lead, N_DEV, a.shape[-1] // N_DEV), -2, 0)


def _shards_rows(a):
    l, n, c = a.shape
    return jnp.moveaxis(a.reshape(l, N_DEV, n // N_DEV, c), 1, 0)


def _flat_pad(parts, mult):
    flat = jnp.concatenate([p.reshape(-1) for p in parts])
    return jnp.pad(flat, (0, (-flat.shape[0]) % mult))


def _unflatten(flat, shapes):
    out, o = [], 0
    for shp in shapes:
        n = 1
        for s in shp:
            n *= s
        out.append(flat[o:o + n].reshape(shp))
        o += n
    return out


def kernel(x, meta_tokens, g_mix, w_in, w_conv, w_pool, pool_scale, w_out, g_mlp, w_up, w_down, g_final, loss_target, m_meta_tokens, m_g_mix, m_w_in, m_w_conv, m_w_pool, m_pool_scale, m_w_out, m_g_mlp, m_w_up, m_w_down, m_g_final, v_meta_tokens, v_g_mix, v_w_in, v_w_conv, v_w_pool, v_pool_scale, v_w_out, v_g_mlp, v_w_up, v_w_down, v_g_final):
    nb, seq, d = x.shape
    depth = g_mix.shape[0]
    lp = BLK + seq
    t = nb * lp
    cw = w_conv.shape[2] * N_DEV
    pw = pool_scale.shape[1]
    pg = pw // N_POOL_GROUPS
    aw = (w_in.shape[2] * N_DEV - 3 * cw - pw) // 3
    ua_w = 3 * cw + pw
    d_ff = w_up.shape[2] * N_DEV
    me = 4 * lax.axis_index("x") + 2 * lax.axis_index("y") + lax.axis_index("c")

    def rows(a):
        return a.reshape(-1, a.shape[-1])

    def gathered_w_in(g, layers):
        return jnp.moveaxis(g.reshape(N_DEV, layers, d, -1), 0, 2).reshape(layers, d, -1)

    g_in0, small_in = _all_gather([rows(w_in[:1]).astype(BF16), _flat_pad([meta_tokens, w_conv], 8 * BLK).reshape(-1, BLK)],
                                  "gather_first_weights")
    w_in0_f = gathered_w_in(g_in0, 1)
    small_in = small_in.reshape(N_DEV, -1)
    meta_f = jnp.moveaxis(small_in[:, :meta_tokens.size].reshape(N_DEV, N_META, d // N_DEV), 0, 1).reshape(N_META, d)
    w_conv_f = small_in[:, meta_tokens.size:meta_tokens.size + w_conv.size].reshape((N_DEV,) + w_conv.shape)
    w_conv_f = jnp.moveaxis(w_conv_f, 0, 2).reshape(depth, w_conv.shape[1], cw)

    def split_w_in(w):
        return w[:, :ua_w], w[:, ua_w:ua_w + aw], w[:, ua_w + aw:ua_w + 2 * aw], w[:, ua_w + 2 * aw:]

    wbd = jnp.zeros((depth, pw, pw), F32)
    for gi in range(N_POOL_GROUPS):
        wbd = wbd.at[:, gi * pg:(gi + 1) * pg, gi * pg:(gi + 1) * pg].set(w_pool[:, gi])
    wbd = wbd.astype(BF16)

    h = jnp.concatenate([jnp.zeros((nb, PAD, d), F32), jnp.broadcast_to(meta_f[None], (nb, N_META, d)), x], axis=1).reshape(t, d)
    saved, w_in_parts, w_in_full = [], [None] * depth, [None] * depth
    w_in_full[0] = w_in0_f[0]
    w_in_parts[0] = split_w_in(w_in0_f[0])
    for l in range(depth):
        wa, wq, wk, wv = w_in_parts[l]
        ua, q, k, v, hn1 = _inproj_fwd(h, g_mix[l][None], wa, wq, wk, wv)
        ycp = _mixer_fwd(ua, w_conv_f[l], wbd[l], pool_scale[l][None], nb)
        if l == 0:
            yat, g_in, g_out, g_up, g_down = _attn_fwd(
                q, k, v, nb, gather=[rows(a).astype(BF16) for a in (w_in[1:], w_out, w_up, w_down)])
            w_in_rest = gathered_w_in(g_in, depth - 1)
            for ll in range(1, depth):
                w_in_full[ll] = w_in_rest[ll - 1]
                w_in_parts[ll] = split_w_in(w_in_rest[ll - 1])
            w_out_f = jnp.moveaxis(g_out.reshape((N_DEV,) + w_out.shape), 0, 1).reshape(depth, -1, d)
            wo_a, wo_b = w_out_f[:, :cw + pw], w_out_f[:, cw + pw:]
            w_up_f = g_up.reshape((N_DEV,) + w_up.shape)
            w_down_f = g_down.reshape((N_DEV,) + w_down.shape)
        else:
            yat = _attn_fwd(q, k, v, nb)
        h_mid =_outproj(h, ycp, yat, wo_a[l], wo_b[l])
        h_next, hn2 = _mlp_fwd(h_mid, g_mlp[l][None], w_up_f, w_down_f, l)
        saved.append((h, ua, q, k, v, hn1, ycp, yat, h_mid, hn2))
        h = h_next

    loss_part, dh, dg_final = _loss_head(h, g_final[None], loss_target.reshape(nb * seq, d), nb)

    g_gmix, g_gmlp, g_wconv, g_pscale, g_wpool = [None] * depth, [None] * depth, [None] * depth, [None] * depth, [None] * depth
    g_win, g_wout, g_wup, g_wdown = [None] * depth, [None] * depth, [None] * depth, [None] * depth
    for l in reversed(range(depth)):
        h_in, ua, q, k, v, hn1, ycp, yat, h_mid, hn2 = saved[l]
        wa, wqkv = w_in_parts[l][0], w_in_full[l][:, ua_w:]
        dh_mid, act, dm, g_gmlp[l] = _mlp_bwd(dh, h_mid, hn2, g_mlp[l][None], w_up_f, w_down_f, l)
        g_wup[l] = _matmul_tn(hn2, dm, "grad_w_up")
        g_wdown[l] = _matmul_tn(act, dh, "grad_w_down")
        dycp, do = _outproj_bwd(dh_mid, wo_a[l], wo_b[l])
        g_wout[l] = jnp.concatenate([_matmul_tn(ycp, dh_mid, "grad_w_out_a"), _matmul_tn(yat, dh_mid, "grad_w_out_b")], axis=0)
        if l == 0:
            send = [_shards_last(jnp.stack(g_win[1:])), _shards_rows(jnp.stack(g_wout)),
                    _shards_last(jnp.stack(g_wup)), _shards_rows(jnp.stack(g_wdown))]
            dqkv, p_in, p_out, p_up, p_down = _attn_bwd(
                q, k, v, do, nb, exchange=[s.reshape(N_DEV, -1, s.shape[-1]) for s in send])
        else:
            dqkv = _attn_bwd(q, k, v, do, nb)
        dua, g_wconv[l], g_pscale[l], dwb = _mixer_bwd(ua, dycp, w_conv_f[l], wbd[l], pool_scale[l][None], nb)
        g_wpool[l] = jnp.stack([dwb[gi * pg:(gi + 1) * pg, gi * pg:(gi + 1) * pg] for gi in range(N_POOL_GROUPS)])
        g_win[l] = jnp.concatenate([_matmul_tn(hn1, dua, "grad_w_in_a"), _matmul_tn(hn1, dqkv, "grad_w_in_qkv")], axis=1)
        if l == 0:
            dh, g_gmix[l], p_in0 = _inproj_bwd(dh_mid, h_in, g_mix[l][None], dua, dqkv, wa, wqkv,
                                               exchange=_shards_last(g_win[0][None]).reshape(N_DEV, -1, w_in.shape[-1]))
        else:
            dh, g_gmix[l] = _inproj_bwd(dh_mid, h_in, g_mix[l][None], dua, dqkv, wa, wqkv)
    dh3 = dh.reshape(nb, lp, d)
    grad_x = dh3[:, BLK:]
    g_meta = _sum_parts(dh3[:, PAD:BLK], "sum_meta_grad")

    o_out = _adamw(rows(w_out), p_out, rows(m_w_out), rows(v_w_out), "adamw_w_out")
    o_up = _adamw(rows(w_up), p_up, rows(m_w_up), rows(v_w_up), "adamw_w_up")
    o_down = _adamw(rows(w_down), p_down, rows(m_w_down), rows(v_w_down), "adamw_w_down")
    o_in1 = _adamw(rows(w_in[1:]), p_in, rows(m_w_in[1:]), rows(v_w_in[1:]), "adamw_w_in_rest")
    o_in0 = _adamw(rows(w_in[:1]), p_in0, rows(m_w_in[:1]), rows(v_w_in[:1]), "adamw_w_in0")
    (gr_win, gr_wout, gr_wup, gr_wdown), (de_win, de_wout, de_wup, de_wdown), (nm_win, nm_wout, nm_wup, nm_wdown), \
        (nv_win, nv_wout, nv_wup, nv_wdown) = [
            [jnp.concatenate([i0, i1], axis=0).reshape(w_in.shape), o.reshape(w_out.shape), u.reshape(w_up.shape), dn.reshape(w_down.shape)]
            for i0, i1, o, u, dn in zip(o_in0, o_in1, o_out, o_up, o_down)]

    small_full = [g_meta, jnp.concatenate(g_gmix), jnp.stack(g_wconv), jnp.stack(g_wpool), jnp.concatenate(g_pscale),
                  jnp.concatenate(g_gmlp), dg_final.reshape(-1)]
    small_shapes = [a.shape for a in small_full]
    small_sum = _sum_parts(_all_gather([_flat_pad(small_full, 8 * BLK).reshape(-1, BLK)], "gather_small_grads")[0], "sum_small_grads")
    gr_meta, gr_gmix, gr_wconv, gr_wpool, gr_pscale, gr_gmlp, gr_gfinal = _unflatten(small_sum.reshape(-1), small_shapes)
    gr_meta = lax.dynamic_slice_in_dim(gr_meta, me * (d // N_DEV), d // N_DEV, axis=1)
    gr_wconv = lax.dynamic_slice_in_dim(gr_wconv, me * (cw // N_DEV), cw // N_DEV, axis=2)
    small_g = [gr_meta, gr_gmix, gr_wconv, gr_wpool, gr_pscale, gr_gmlp, gr_gfinal]
    local_shapes = [a.shape for a in small_g]
    pack_small = lambda parts_: _flat_pad(parts_, 8 * BLK).reshape(-1, BLK)
    small = _adamw(pack_small([meta_tokens, g_mix, w_conv, w_pool, pool_scale, g_mlp, g_final]), pack_small(small_g)[None],
                   pack_small([m_meta_tokens, m_g_mix, m_w_conv, m_w_pool, m_pool_scale, m_g_mlp, m_g_final]),
                   pack_small([v_meta_tokens, v_g_mix, v_w_conv, v_w_pool, v_pool_scale, v_g_mlp, v_g_final]), "adamw_small")
    _, de_s, nm_s, nv_s = [_unflatten(b.reshape(-1), local_shapes) for b in small]

    loss = lax.psum(loss_part[0, 0], ("x", "y", "c"))

    def ordered(meta, gmix, wconv, wpool, pscale, gmlp, gfinal, win, wout, wup, wdown):
        return [meta, gmix, win, wconv, wpool, pscale, wout, gmlp, wup, wdown, gfinal]

    grads = ordered(*small_g, gr_win, gr_wout, gr_wup, gr_wdown)
    deltas = ordered(*de_s, de_win, de_wout, de_wup, de_wdown)
    new_m = ordered(*nm_s, nm_win, nm_wout, nm_wup, nm_wdown)
    new_v = ordered(*nv_s, nv_win, nv_wout, nv_wup, nv_wdown)
    return (loss, grad_x, *grads, *deltas, *new_m, *new_v)
```

```python
import jax
import jax.numpy as jnp
from jax import lax
from jax.experimental import pallas as pl
from jax.experimental.pallas import tpu as pltpu

F32 = jnp.float32
BF16 = jnp.bfloat16

N_DEV = 8
N_META = 16
HEAD_DIM = 64
BLK = 128
PAD = BLK - N_META
N_POOL_GROUPS = 4
EPS = 1e-6

ADAM_LR = 0.001
ADAM_B1 = 0.9
ADAM_B2 = 0.999
ADAM_EPS = 1e-08
ADAM_WD = 0.01
ADAM_STEP = 10

V7X_VMEM_LIMIT = 56 * 1024 * 1024


def _params(*sem):
    return pltpu.CompilerParams(dimension_semantics=sem, vmem_limit_bytes=V7X_VMEM_LIMIT)


SUBLANES = 8
ROW_TILE = 512
ROW_TILE_MLP = 1152
ROW_TILE_WIDE = 1152
TN_ACC_BYTES = 8 * 1024 * 1024


def _row_tile(t, cap=ROW_TILE):
    for tm in range(min(cap, t) // SUBLANES * SUBLANES, 0, -SUBLANES):
        if t % tm == 0:
            return tm
    raise ValueError(f"row count {t} is not a multiple of 8")


def _full(a):
    nd = a.ndim
    return pl.BlockSpec(a.shape, lambda *_: (0,) * nd)


def _dot(a, b):
    return jnp.dot(a, b, preferred_element_type=F32)


def _dot_nt(a, b):
    return lax.dot_general(a, b, (((1,), (1,)), ((), ())), preferred_element_type=F32)


def _dot_tn(a, b):
    return lax.dot_general(a, b, (((0,), (0,)), ((), ())), preferred_element_type=F32)


def _rms_fwd(x, g):
    rstd = lax.rsqrt(jnp.mean(x * x, axis=-1, keepdims=True) + EPS)
    return x * rstd * g


def _rms_bwd(dy, x, g):
    rstd = lax.rsqrt(jnp.mean(x * x, axis=-1, keepdims=True) + EPS)
    xhat = x * rstd
    dxhat = dy * g
    dx = rstd * (dxhat - xhat * jnp.mean(dxhat * xhat, axis=-1, keepdims=True))
    return dx, jnp.sum(dy * xhat, axis=0, keepdims=True)


def _inproj_fwd(h, g, wa, wq, wk, wv):
    t, d = h.shape
    tm = _row_tile(t, ROW_TILE_WIDE)
    ua_w, aw = wa.shape[1], wq.shape[1]

    def body(h_ref, g_ref, wa_ref, wq_ref, wk_ref, wv_ref, ua_ref, q_ref, k_ref, v_ref, hn_ref):
        hn = _rms_fwd(h_ref[...], g_ref[...]).astype(BF16)
        hn_ref[...] = hn
        ua_ref[...] = _dot(hn, wa_ref[...])
        q_ref[...] = _dot(hn, wq_ref[...]).astype(BF16)
        k_ref[...] = _dot(hn, wk_ref[...]).astype(BF16)
        v_ref[...] = _dot(hn, wv_ref[...]).astype(BF16)

    row = lambda w: pl.BlockSpec((tm, w), lambda i: (i, 0))
    return pl.pallas_call(
        body, name="inproj_fwd", grid=(t // tm,),
        in_specs=[row(d), _full(g), _full(wa), _full(wq), _full(wk), _full(wv)],
        out_specs=[row(ua_w), row(aw), row(aw), row(aw), row(d)],
        out_shape=[jax.ShapeDtypeStruct((t, ua_w), F32)] + [jax.ShapeDtypeStruct((t, aw), BF16)] * 3
        + [jax.ShapeDtypeStruct((t, d), BF16)],
        compiler_params=_params("arbitrary"),
    )(h, g, wa, wq, wk, wv)


def _pool_geometry(lp, pw):
    pg = pw // N_POOL_GROUPS
    row = lax.broadcasted_iota(jnp.int32, (lp, pw), 0)
    lane = lax.broadcasted_iota(jnp.int32, (lp, pw), 1)
    grp = [(lane >= g * pg) & (lane < (g + 1) * pg) for g in range(N_POOL_GROUPS)]
    wlen = jnp.where(grp[0], 2, jnp.where(grp[1], 4, jnp.where(grp[2], 8, 16)))
    cnt = jnp.clip(row - (PAD - 1), 1, wlen).astype(F32)
    return grp, cnt, row >= PAD


def _by_group(grp, vals):
    return jnp.where(grp[0], vals[0], jnp.where(grp[1], vals[1], jnp.where(grp[2], vals[2], vals[3])))


def _pooled(p, grp, cnt, real):
    s2 = p + pltpu.roll(p, 1, 0)
    s4 = s2 + pltpu.roll(s2, 2, 0)
    s8 = s4 + pltpu.roll(s4, 4, 0)
    s16 = s8 + pltpu.roll(s8, 8, 0)
    return jnp.where(real, _by_group(grp, (s2, s4, s8, s16)) / cnt - p, 0.0)


def _conv(uu, wc_ref):
    return wc_ref[2:3, :] * uu + wc_ref[1:2, :] * pltpu.roll(uu, 1, 0) + wc_ref[0:1, :] * pltpu.roll(uu, 2, 0)


def _mixer_fwd(ua, wconv, wbd, pscale, nb):
    t = ua.shape[0]
    lp = t // nb
    cw, pw = wconv.shape[1], wbd.shape[0]

    def body(ua_ref, wc_ref, wbd_ref, ps_ref, y_ref):
        cb = ua_ref[:, 0:cw]
        uu = ua_ref[:, cw:2 * cw] * ua_ref[:, 2 * cw:3 * cw]
        y_ref[:, 0:cw] = (cb * _conv(uu, wc_ref)).astype(BF16)
        grp, cnt, real = _pool_geometry(lp, pw)
        pooled = _pooled(ua_ref[:, 3 * cw:3 * cw + pw], grp, cnt, real)
        y_ref[:, cw:cw + pw] = (_dot(pooled.astype(BF16), wbd_ref[...]) * ps_ref[...]).astype(BF16)

    return pl.pallas_call(
        body, name="mixer_fwd", grid=(nb,),
        in_specs=[pl.BlockSpec((lp, ua.shape[1]), lambda s: (s, 0)), _full(wconv), _full(wbd), _full(pscale)],
        out_specs=pl.BlockSpec((lp, cw + pw), lambda s: (s, 0)),
        out_shape=jax.ShapeDtypeStruct((t, cw + pw), BF16),
        compiler_params=_params("arbitrary"),
    )(ua, wconv, wbd, pscale)


ATT_NW = 2
ATT_W = ATT_NW * BLK
ATT_PAIRS_FWD = 4
ATT_PAIRS_BWD = 4
HW = 2 * HEAD_DIM
ATT_SCALE = HEAD_DIM ** -0.5


def _attn_lk(z, valid):
    nz = -z
    return jnp.where(valid, jnp.minimum(nz, 0.0) - jnp.log(1.0 + jnp.exp(jnp.minimum(z, nz))), 0.0)


def _tri_sum_bf16(x, tri):
    return _dot(x.astype(BF16), tri)


def _tri_sum(x, tri):
    hi = x.astype(BF16)
    lo = (x - hi.astype(F32)).astype(BF16)
    return _dot(hi, tri) + _dot(lo, tri)


def _attn_weights(z, suffix, carry, valid):
    return jnp.where(valid, jnp.exp(z + suffix + carry), 0.0)


def _attn_consts():
    r = lax.broadcasted_iota(jnp.int32, (2 * BLK, ATT_W), 0)
    c = lax.broadcasted_iota(jnp.int32, (2 * BLK, ATT_W), 1)
    cmr = c - (r & (BLK - 1))
    kr = lax.broadcasted_iota(jnp.int32, (ATT_W, ATT_W), 0)
    kc = lax.broadcasted_iota(jnp.int32, (ATT_W, ATT_W), 1)
    lane = lax.broadcasted_iota(jnp.int32, (1, HW), 1)
    heads = (lane < HEAD_DIM, lane >= HEAD_DIM)
    return c, cmr, kr, kc, heads


def _tri(cond):
    return jnp.where(cond, 1.0, 0.0).astype(BF16)


def _rows(i):
    return pl.ds(pl.multiple_of(i * BLK, BLK), BLK)


def _stack_heads(x, heads):
    zero = jnp.zeros_like(x)
    return jnp.concatenate([jnp.where(heads[0], x, zero), jnp.where(heads[1], x, zero)], axis=0)


def _unstack_heads(x, heads):
    return jnp.where(heads[0], x[:BLK], x[BLK:])


def _window(i, s):
    start = jnp.maximum(i + 1 - ATT_NW * (s + 1), 0) * BLK
    return start, (i + 1 - ATT_NW * s) * BLK


def _attn_fwd(q, k, v, ycp, nb, gather=None):
    t, aw = q.shape
    cpw = ycp.shape[1]
    lp = t // nb
    nblk = lp // BLK
    pp = min(ATT_PAIRS_FWD, aw // HW)
    cols = [slice(p * HW, (p + 1) * HW) for p in range(pp)]
    ncol = aw // (pp * HW)
    assert ncol == 1, "the [y_conv | y_pool | y_attn] output block needs all head pairs in one grid step"
    nsteps = nb * ncol

    def body(q_ref, k_ref, v_ref, ycp_ref, *rest):
        if gather is None:
            (o_ref,) = rest
        else:
            ng = len(gather)
            x_refs, o_ref, g_refs, sems = rest[:ng], rest[ng], rest[ng + 1:2 * ng + 1], rest[2 * ng + 1:]
            step = pl.program_id(0) * ncol + pl.program_id(1)
            start, forward, finish = _all_of([_gather_phases(x_refs[j], g_refs[j], *sems[3 * j:3 * j + 3]) for j in range(ng)])
            pl.when(step == 0)(start)
            pl.when(step == nsteps - 1)(forward)
        c, cmr, kr, kc, heads = _attn_consts()
        m_from = _tri(kr >= kc)
        o_ref[:, 0:cpw] = ycp_ref[...]

        def qblock(i, _):
            qs = [_stack_heads(q_ref[_rows(i), cs] * ATT_SCALE, heads) for cs in cols]

            def window(s, st):
                accs, carries = st
                start, end = _window(i, s)
                keys = pl.ds(pl.multiple_of(start, BLK), ATT_W)
                valid = (cmr < i * BLK - start) & (c < end - start)
                zs = [_dot_nt(qs[p], k_ref[keys, cs]) for p, cs in enumerate(cols)]
                suffix = [_tri_sum_bf16(_attn_lk(zs[p], valid), m_from) for p in range(pp)]
                new_accs, new_carries = [], []
                for p, cs in enumerate(cols):
                    a = _attn_weights(zs[p], suffix[p], carries[p], valid)
                    new_accs.append(accs[p] + _dot(a.astype(BF16), v_ref[keys, cs]))
                    new_carries.append(carries[p] + suffix[p][:, 0:1])
                return tuple(new_accs), tuple(new_carries)

            init = (tuple(jnp.zeros((2 * BLK, HW), F32) for _ in cols), tuple(jnp.zeros((2 * BLK, 1), F32) for _ in cols))
            accs, _ = lax.fori_loop(0, (i + ATT_NW) // ATT_NW, window, init)
            for p in range(pp):
                o_ref[_rows(i), cpw + p * HW:cpw + (p + 1) * HW] = _unstack_heads(accs[p], heads).astype(BF16)
            return 0

        lax.fori_loop(0, nblk, qblock, 0)
        if gather is not None:
            pl.when(step == nsteps - 1)(finish)

    spec = pl.BlockSpec((lp, pp * HW), lambda s, p: (s, p))
    in_specs = [spec, spec, spec, pl.BlockSpec((lp, cpw), lambda s, p: (s, 0))]
    y_spec = pl.BlockSpec((lp, cpw + aw), lambda s, p: (s, 0))
    y_shape = jax.ShapeDtypeStruct((t, cpw + aw), BF16)
    if gather is None:
        return pl.pallas_call(
            body, name="attn_fwd", grid=(nb, ncol), in_specs=in_specs, out_specs=y_spec, out_shape=y_shape,
            compiler_params=_params("arbitrary", "arbitrary"),
        )(q, k, v, ycp)
    return pl.pallas_call(
        body, name="attn_fwd_gather", grid=(nb, ncol), in_specs=in_specs + [ANY] * len(gather),
        out_specs=[y_spec] + [ANY] * len(gather),
        out_shape=[y_shape] + [jax.ShapeDtypeStruct((N_DEV,) + x.shape, x.dtype) for x in gather],
        scratch_shapes=COMM_SEMS * len(gather), compiler_params=_params("arbitrary", "arbitrary"),
    )(q, k, v, ycp, *gather)


def _outproj(h, y, w):
    t, d = h.shape
    tm = _row_tile(t, ROW_TILE_WIDE)

    def body(h_ref, y_ref, w_ref, o_ref):
        o_ref[...] = h_ref[...] + _dot(y_ref[...], w_ref[...])

    row = lambda w_: pl.BlockSpec((tm, w_), lambda i: (i, 0))
    return pl.pallas_call(
        body, name="outproj", grid=(t // tm,),
        in_specs=[row(d), row(y.shape[1]), _full(w)],
        out_specs=row(d), out_shape=jax.ShapeDtypeStruct((t, d), F32),
        compiler_params=_params("arbitrary"),
    )(h, y, w)


def _mlp_weight_specs(wup, wdown, layer):
    nff, _, d, fc = wup.shape
    return nff, fc, [pl.BlockSpec((None, None, d, fc), lambda i, kk: (kk, layer, 0, 0)),
                     pl.BlockSpec((None, None, fc, d), lambda i, kk: (kk, layer, 0, 0))]


def _mlp_fwd(h, g, wup, wdown, layer):
    t, d = h.shape
    tm = _row_tile(t, ROW_TILE_MLP)
    nff, fc, wspecs = _mlp_weight_specs(wup, wdown, layer)

    def body(h_ref, g_ref, wu_ref, wd_ref, o_ref, hn_ref, acc):
        kk = pl.program_id(1)

        @pl.when(kk == 0)
        def _():
            x = h_ref[...]
            hn_ref[...] = _rms_fwd(x, g_ref[...]).astype(BF16)
            acc[...] = x

        m = _dot(hn_ref[...], wu_ref[...])
        a = jnp.square(jnp.maximum(m, 0.0)).astype(BF16)
        acc[...] += _dot(a, wd_ref[...])

        @pl.when(kk == nff - 1)
        def _():
            o_ref[...] = acc[...]

    row = pl.BlockSpec((tm, d), lambda i, kk: (i, 0))
    return pl.pallas_call(
        body, name="mlp_fwd", grid=(t // tm, nff),
        in_specs=[row, _full(g)] + wspecs,
        out_specs=[row, row],
        out_shape=[jax.ShapeDtypeStruct((t, d), F32), jax.ShapeDtypeStruct((t, d), BF16)],
        scratch_shapes=[pltpu.VMEM((tm, d), F32)],
        compiler_params=_params("arbitrary", "arbitrary"),
    )(h, g, wup, wdown)


def _loss_head(h, g, target, nb):
    t, d = h.shape
    nblk = t // nb // BLK
    nx = nblk - 1

    def body(h_ref, g_ref, t_ref, loss_ref, dh_ref, dg_ref):
        s, i = pl.program_id(0), pl.program_id(1)

        @pl.when((s == 0) & (i == 0))
        def _():
            loss_ref[...] = jnp.zeros_like(loss_ref)
            dg_ref[...] = jnp.zeros_like(dg_ref)

        @pl.when(i == 0)
        def _():
            dh_ref[...] = jnp.zeros_like(dh_ref)

        @pl.when(i > 0)
        def _():
            x, gg = h_ref[...], g_ref[...]
            err = _rms_fwd(x, gg) - t_ref[...]
            loss_ref[...] += jnp.sum(err * err) * (0.5 / d)
            dx, dg = _rms_bwd(err * (1.0 / d), x, gg)
            dh_ref[...] = dx
            dg_ref[...] += dg

    hspec = pl.BlockSpec((BLK, d), lambda s, i: (s * nblk + i, 0))
    return pl.pallas_call(
        body, name="loss_head", grid=(nb, nblk),
        in_specs=[hspec, _full(g), pl.BlockSpec((BLK, d), lambda s, i: (s * nx + jnp.maximum(i - 1, 0), 0))],
        out_specs=[pl.BlockSpec((1, BLK), lambda s, i: (0, 0)), hspec, pl.BlockSpec((1, d), lambda s, i: (0, 0))],
        out_shape=[jax.ShapeDtypeStruct((1, BLK), F32), jax.ShapeDtypeStruct((t, d), F32), jax.ShapeDtypeStruct((1, d), F32)],
        compiler_params=_params("arbitrary", "arbitrary"),
    )(h, g, target)


def _mlp_bwd(dh, h, hn, g, wup, wdown, layer):
    t, d = h.shape
    tm = _row_tile(t, ROW_TILE_MLP)
    nff, fc, wspecs = _mlp_weight_specs(wup, wdown, layer)

    def body(dh_ref, h_ref, hn_ref, g_ref, wu_ref, wd_ref, dhm_ref, a_ref, dm_ref, dg_ref, dhn, dhb):
        i, kk = pl.program_id(0), pl.program_id(1)

        @pl.when((i == 0) & (kk == 0))
        def _():
            dg_ref[...] = jnp.zeros_like(dg_ref)

        @pl.when(kk == 0)
        def _():
            dhb[...] = dh_ref[...].astype(BF16)
            dhn[...] = jnp.zeros_like(dhn)

        r = jnp.maximum(_dot(hn_ref[...], wu_ref[...]), 0.0)
        a_ref[...] = (r * r).astype(BF16)
        dm = (_dot_nt(dhb[...], wd_ref[...]) * (2.0 * r)).astype(BF16)
        dm_ref[...] = dm
        dhn[...] += _dot_nt(dm, wu_ref[...])

        @pl.when(kk == nff - 1)
        def _():
            dx, dg = _rms_bwd(dhn[...], h_ref[...], g_ref[...])
            dhm_ref[...] = dh_ref[...] + dx
            dg_ref[...] += dg

    row = pl.BlockSpec((tm, d), lambda i, kk: (i, 0))
    ff = pl.BlockSpec((tm, fc), lambda i, kk: (i, kk))
    return pl.pallas_call(
        body, name="mlp_bwd", grid=(t // tm, nff),
        in_specs=[row, row, row, _full(g)] + wspecs,
        out_specs=[row, ff, ff, pl.BlockSpec((1, d), lambda i, kk: (0, 0))],
        out_shape=[jax.ShapeDtypeStruct((t, d), F32), jax.ShapeDtypeStruct((t, nff * fc), BF16),
                   jax.ShapeDtypeStruct((t, nff * fc), BF16), jax.ShapeDtypeStruct((1, d), F32)],
        scratch_shapes=[pltpu.VMEM((tm, d), F32), pltpu.VMEM((tm, d), BF16)],
        compiler_params=_params("arbitrary", "arbitrary"),
    )(dh, h, hn, g, wup, wdown)


def _matmul_tn(x, y, name):
    t, k1 = x.shape
    n = y.shape[1]
    tt = _row_tile(t)
    tn = min(n, max(512, TN_ACC_BYTES // (4 * k1)))
    steps = t // tt

    def body(x_ref, y_ref, o_ref, acc):
        @pl.when(pl.program_id(1) == 0)
        def _():
            acc[...] = jnp.zeros_like(acc)

        acc[...] += _dot_tn(x_ref[...].astype(BF16), y_ref[...].astype(BF16))

        @pl.when(pl.program_id(1) == steps - 1)
        def _():
            o_ref[...] = acc[...].astype(BF16)

    return pl.pallas_call(
        body, name=name, grid=(n // tn, steps),
        in_specs=[pl.BlockSpec((tt, k1), lambda j, r: (r, 0)), pl.BlockSpec((tt, tn), lambda j, r: (r, j))],
        out_specs=pl.BlockSpec((k1, tn), lambda j, r: (0, j)),
        out_shape=jax.ShapeDtypeStruct((k1, n), BF16),
        scratch_shapes=[pltpu.VMEM((k1, tn), F32)],
        compiler_params=_params("arbitrary", "arbitrary"),
    )(x, y)


def _outproj_bwd(dh, wa, wb):
    t, d = dh.shape
    tm = _row_tile(t, ROW_TILE_WIDE)
    na, nbw = wa.shape[0], wb.shape[0]

    def body(dh_ref, wa_ref, wb_ref, da_ref, db_ref):
        x = dh_ref[...].astype(BF16)
        da_ref[...] = _dot_nt(x, wa_ref[...])
        db_ref[...] = _dot_nt(x, wb_ref[...]).astype(BF16)

    row = lambda w: pl.BlockSpec((tm, w), lambda i: (i, 0))
    return pl.pallas_call(
        body, name="outproj_bwd", grid=(t // tm,),
        in_specs=[row(d), _full(wa), _full(wb)], out_specs=[row(na), row(nbw)],
        out_shape=[jax.ShapeDtypeStruct((t, na), F32), jax.ShapeDtypeStruct((t, nbw), BF16)],
        compiler_params=_params("arbitrary"),
    )(dh, wa, wb)


def _attn_bwd(q, k, v, do, nb, exchange=None):
    t, aw = q.shape
    lp = t // nb
    nblk = lp // BLK
    pp = min(ATT_PAIRS_BWD, aw // HW)
    cols = [slice(p * HW, (p + 1) * HW) for p in range(pp)]
    ncol = aw // (pp * HW)
    assert ncol == 1, "the [dq | dk | dv] output block needs all head pairs in one grid step"
    nsteps = nb * ncol
    max_windows = (nblk + ATT_NW - 1) // ATT_NW

    def body(q_ref, k_ref, v_ref, do_ref, *rest):
        if exchange is None:
            dqkv_ref, dk_acc, dv_acc, g_s, sig_s = rest
        else:
            ne = len(exchange)
            x_refs, dqkv_ref, e_refs = rest[:ne], rest[ne], rest[ne + 1:2 * ne + 1]
            dk_acc, dv_acc, g_s, sig_s = rest[2 * ne + 1:2 * ne + 5]
            sems = rest[2 * ne + 5:]
            step = pl.program_id(0) * ncol + pl.program_id(1)
            start, finish = _all_of([_exchange_phases(x_refs[j], e_refs[j], *sems[3 * j:3 * j + 3]) for j in range(ne)])
            pl.when(step == 0)(start)
        c, cmr, kr, kc, heads = _attn_consts()
        m_from = _tri(kr >= kc)
        m_before = _tri(kr < kc)
        dk_acc[...] = jnp.zeros_like(dk_acc)
        dv_acc[...] = jnp.zeros_like(dv_acc)

        def qblock(i, _):
            qs = [_stack_heads(q_ref[_rows(i), cs] * ATT_SCALE, heads) for cs in cols]
            dos = [_stack_heads(do_ref[_rows(i), cs], heads) for cs in cols]
            nwin = (i + ATT_NW) // ATT_NW

            def geometry(s):
                start, end = _window(i, s)
                return (cmr < i * BLK - start) & (c < end - start), pl.ds(pl.multiple_of(start, BLK), ATT_W)

            def down(s, carries):
                valid, keys = geometry(s)
                zs = [_dot_nt(qs[p], k_ref[keys, cs]) for p, cs in enumerate(cols)]
                das = [_dot_nt(dos[p], v_ref[keys, cs]) for p, cs in enumerate(cols)]
                suffix = []
                for p in range(pp):
                    lk = _attn_lk(zs[p], valid)
                    sig_s[p, s] = jnp.exp(zs[p] + lk)
                    suffix.append(_tri_sum_bf16(lk, m_from))
                new_carries = []
                for p, cs in enumerate(cols):
                    a = _attn_weights(zs[p], suffix[p], carries[p], valid)
                    g_s[p, s] = das[p] * a
                    dv_acc[keys, cs] += _dot_tn(a.astype(BF16), dos[p])
                    new_carries.append(carries[p] + suffix[p][:, 0:1])
                return tuple(new_carries)

            lax.fori_loop(0, nwin, down, tuple(jnp.zeros((2 * BLK, 1), F32) for _ in cols))

            def up(ss, st):
                accs, carries = st
                s = nwin - 1 - ss
                valid, keys = geometry(s)
                gs = [g_s[p, s] for p in range(pp)]
                before = [_tri_sum(gs[p], m_before) for p in range(pp)]
                new_accs, new_carries = [], []
                for p, cs in enumerate(cols):
                    g, sig = gs[p], sig_s[p, s]
                    dz = jnp.where(valid, g * (1.0 - sig) - (before[p] + carries[p]) * sig, 0.0).astype(BF16)
                    new_accs.append(accs[p] + _dot(dz, k_ref[keys, cs]))
                    dk_acc[keys, cs] += _dot_tn(dz, qs[p])
                    new_carries.append(carries[p] + jnp.sum(g, axis=1, keepdims=True))
                return tuple(new_accs), tuple(new_carries)

            init = (tuple(jnp.zeros((2 * BLK, HW), F32) for _ in cols), tuple(jnp.zeros((2 * BLK, 1), F32) for _ in cols))
            accs, _ = lax.fori_loop(0, nwin, up, init)
            for p, cs in enumerate(cols):
                dqkv_ref[_rows(i), cs] = (_unstack_heads(accs[p], heads) * ATT_SCALE).astype(BF16)
            return 0

        lax.fori_loop(0, nblk, qblock, 0)
        dqkv_ref[:, aw:2 * aw] = dk_acc[...].astype(BF16)
        dqkv_ref[:, 2 * aw:3 * aw] = dv_acc[...].astype(BF16)
        if exchange is not None:
            pl.when(step == nsteps - 1)(finish)

    spec = pl.BlockSpec((lp, pp * HW), lambda s, p: (s, p), pipeline_mode=pl.Buffered(1))
    d_spec = pl.BlockSpec((lp, 3 * aw), lambda s, p: (s, 0), pipeline_mode=pl.Buffered(1))
    d_shape = jax.ShapeDtypeStruct((t, 3 * aw), BF16)
    scratch = [pltpu.VMEM((lp, pp * HW), F32), pltpu.VMEM((lp, pp * HW), F32),
               pltpu.VMEM((pp, max_windows, 2 * BLK, ATT_W), F32), pltpu.VMEM((pp, max_windows, 2 * BLK, ATT_W), F32)]
    if exchange is None:
        return pl.pallas_call(
            body, name="attn_bwd", grid=(nb, ncol), in_specs=[spec] * 4, out_specs=d_spec, out_shape=d_shape,
            scratch_shapes=scratch, compiler_params=_params("arbitrary", "arbitrary"),
        )(q, k, v, do)
    return pl.pallas_call(
        body, name="attn_bwd_exchange", grid=(nb, ncol), in_specs=[spec] * 4 + [ANY] * len(exchange),
        out_specs=[d_spec] + [ANY] * len(exchange),
        out_shape=[d_shape] + [jax.ShapeDtypeStruct(x.shape, x.dtype) for x in exchange],
        scratch_shapes=scratch + COMM_SEMS * len(exchange), compiler_params=_params("arbitrary", "arbitrary"),
    )(q, k, v, do, *exchange)


def _mixer_bwd(ua, dy, wconv, wbd, pscale, nb):
    t = ua.shape[0]
    lp = t // nb
    cw, pw = wconv.shape[1], wbd.shape[0]

    def body(ua_ref, dy_ref, wc_ref, wbd_ref, ps_ref, du_ref, dwc_ref, dps_ref, dwb_ref):
        @pl.when(pl.program_id(0) == 0)
        def _():
            dwc_ref[...] = jnp.zeros_like(dwc_ref)
            dps_ref[...] = jnp.zeros_like(dps_ref)
            dwb_ref[...] = jnp.zeros_like(dwb_ref)

        up = lambda x, n: pltpu.roll(x, lp - n, 0)
        cb, cc, cx = ua_ref[:, 0:cw], ua_ref[:, cw:2 * cw], ua_ref[:, 2 * cw:3 * cw]
        uu = cc * cx
        dyc = dy_ref[:, 0:cw]
        du_ref[:, 0:cw] = (dyc * _conv(uu, wc_ref)).astype(BF16)
        dcv = dyc * cb
        duu = wc_ref[2:3, :] * dcv + wc_ref[1:2, :] * up(dcv, 1) + wc_ref[0:1, :] * up(dcv, 2)
        du_ref[:, cw:2 * cw] = (duu * cx).astype(BF16)
        du_ref[:, 2 * cw:3 * cw] = (duu * cc).astype(BF16)
        dwc_ref[0:1, :] += jnp.sum(dcv * pltpu.roll(uu, 2, 0), axis=0, keepdims=True)
        dwc_ref[1:2, :] += jnp.sum(dcv * pltpu.roll(uu, 1, 0), axis=0, keepdims=True)
        dwc_ref[2:3, :] += jnp.sum(dcv * uu, axis=0, keepdims=True)
        grp, cnt, real = _pool_geometry(lp, pw)
        p = ua_ref[:, 3 * cw:3 * cw + pw]
        pooled = _pooled(p, grp, cnt, real).astype(BF16)
        dyp = dy_ref[:, cw:cw + pw]
        dps_ref[...] += jnp.sum(dyp * _dot(pooled, wbd_ref[...]), axis=0, keepdims=True)
        dpre = (dyp * ps_ref[...]).astype(BF16)
        dwb_ref[...] += _dot_tn(pooled, dpre)
        dpooled = jnp.where(real, _dot_nt(dpre, wbd_ref[...]), 0.0)
        xm = dpooled / cnt
        l2 = xm + up(xm, 1)
        l4 = l2 + up(l2, 2)
        l8 = l4 + up(l4, 4)
        l16 = l8 + up(l8, 8)
        du_ref[:, 3 * cw:3 * cw + pw] = jnp.where(real, _by_group(grp, (l2, l4, l8, l16)) - dpooled, 0.0).astype(BF16)

    seq = lambda w: pl.BlockSpec((lp, w), lambda s: (s, 0))
    return pl.pallas_call(
        body, name="mixer_bwd", grid=(nb,),
        in_specs=[seq(ua.shape[1]), seq(cw + pw), _full(wconv), _full(wbd), _full(pscale)],
        out_specs=[seq(ua.shape[1]), pl.BlockSpec((3, cw), lambda s: (0, 0)), pl.BlockSpec((1, pw), lambda s: (0, 0)),
                   pl.BlockSpec((pw, pw), lambda s: (0, 0))],
        out_shape=[jax.ShapeDtypeStruct(ua.shape, BF16), jax.ShapeDtypeStruct((3, cw), F32),
                   jax.ShapeDtypeStruct((1, pw), F32), jax.ShapeDtypeStruct((pw, pw), F32)],
        compiler_params=_params("arbitrary"),
    )(ua, dy, wconv, wbd, pscale)


def _inproj_bwd(dh, h, g, dua, dqkv, wa, wqkv, exchange=None):
    t, d = h.shape
    tm = _row_tile(t)
    nsteps = t // tm

    def body(dh_ref, h_ref, g_ref, dua_ref, dqkv_ref, wa_ref, wqkv_ref, *rest):
        if exchange is None:
            o_ref, dg_ref = rest
        else:
            x_ref, o_ref, dg_ref, e_ref, send_sems, recv_sems, local_sem = rest
            start, finish = _exchange_phases(x_ref, e_ref, send_sems, recv_sems, local_sem)
            pl.when(pl.program_id(0) == 0)(start)

        @pl.when(pl.program_id(0) == 0)
        def _():
            dg_ref[...] = jnp.zeros_like(dg_ref)

        dhn = _dot_nt(dua_ref[...], wa_ref[...]) + _dot_nt(dqkv_ref[...], wqkv_ref[...])
        dx, dg = _rms_bwd(dhn, h_ref[...], g_ref[...])
        o_ref[...] = dh_ref[...] + dx
        dg_ref[...] += dg
        if exchange is not None:
            pl.when(pl.program_id(0) == nsteps - 1)(finish)

    row = lambda w: pl.BlockSpec((tm, w), lambda i: (i, 0))
    in_specs = [row(d), row(d), _full(g), row(dua.shape[1]), row(dqkv.shape[1]), _full(wa), _full(wqkv)]
    out_specs = [row(d), pl.BlockSpec((1, d), lambda i: (0, 0))]
    out_shape = [jax.ShapeDtypeStruct((t, d), F32), jax.ShapeDtypeStruct((1, d), F32)]
    if exchange is None:
        return pl.pallas_call(
            body, name="inproj_bwd", grid=(nsteps,), in_specs=in_specs, out_specs=out_specs, out_shape=out_shape,
            compiler_params=_params("arbitrary"),
        )(dh, h, g, dua, dqkv, wa, wqkv)
    return pl.pallas_call(
        body, name="inproj_bwd_exchange", grid=(nsteps,), in_specs=in_specs + [ANY], out_specs=out_specs + [ANY],
        out_shape=out_shape + [jax.ShapeDtypeStruct(exchange.shape, exchange.dtype)], scratch_shapes=COMM_SEMS,
        compiler_params=_params("arbitrary"),
    )(dh, h, g, dua, dqkv, wa, wqkv, exchange)


def _part_tile(r):
    if r <= 512:
        return r
    for tr in (256, 128, 64, 32, 16, 8):
        if r % tr == 0:
            return tr
    return r


def _sum_parts(x, name):
    n, r, c = x.shape
    tr = _part_tile(r)

    def body(x_ref, o_ref):
        acc = x_ref[0]
        for p in range(1, n):
            acc = acc + x_ref[p]
        o_ref[...] = acc

    return pl.pallas_call(
        body, name=name, grid=(r // tr,),
        in_specs=[pl.BlockSpec((n, tr, c), lambda i: (0, i, 0))], out_specs=pl.BlockSpec((tr, c), lambda i: (i, 0)),
        out_shape=jax.ShapeDtypeStruct((r, c), F32), compiler_params=_params("arbitrary"),
    )(x)


def _adamw(w, gparts, m, v, name):
    n, r, c = gparts.shape
    tr = _part_tile(r)

    def body(w_ref, g_ref, m_ref, v_ref, go_ref, d_ref, mo_ref, vo_ref):
        g = g_ref[0].astype(F32)
        for p in range(1, n):
            g = g + g_ref[p].astype(F32)
        go_ref[...] = g
        mm = ADAM_B1 * m_ref[...] + (1.0 - ADAM_B1) * g
        vv = ADAM_B2 * v_ref[...] + (1.0 - ADAM_B2) * jnp.square(g)
        mo_ref[...] = mm
        vo_ref[...] = vv
        m_hat = mm / (1.0 - ADAM_B1 ** ADAM_STEP)
        v_hat = vv / (1.0 - ADAM_B2 ** ADAM_STEP)
        d_ref[...] = -ADAM_LR * (m_hat / (jnp.sqrt(v_hat) + ADAM_EPS) + ADAM_WD * w_ref[...])

    row = pl.BlockSpec((tr, c), lambda i: (i, 0))
    return pl.pallas_call(
        body, name=name, grid=(r // tr,),
        in_specs=[row, pl.BlockSpec((n, tr, c), lambda i: (0, i, 0)), row, row], out_specs=[row] * 4,
        out_shape=[jax.ShapeDtypeStruct((r, c), F32)] * 4, compiler_params=_params("arbitrary"),
    )(w, gparts, m, v)


MESH = pl.DeviceIdType.MESH
ANY = pl.BlockSpec(memory_space=pl.ANY)


def _all_gather(xs, name):
    n = len(xs)

    def body(*refs):
        x_refs, out_refs, sems = refs[:n], refs[n:2 * n], refs[2 * n:]
        start, forward, finish = _all_of([_gather_phases(x_refs[j], out_refs[j], *sems[3 * j:3 * j + 3]) for j in range(n)])
        start()
        forward()
        finish()

    return pl.pallas_call(
        body, name=name, in_specs=[ANY] * n, out_specs=[ANY] * n,
        out_shape=[jax.ShapeDtypeStruct((N_DEV,) + x.shape, x.dtype) for x in xs], scratch_shapes=COMM_SEMS * n,
    )(*xs)


COMM_SEMS = [pltpu.SemaphoreType.DMA((7,)), pltpu.SemaphoreType.DMA((7,)), pltpu.SemaphoreType.DMA(())]


def _gather_phases(x_ref, out_ref, send_sems, recv_sems, local_sem):
    mx, my, mc = lax.axis_index("x"), lax.axis_index("y"), lax.axis_index("c")
    me, sibling = (mx, my, mc), (mx, my, 1 - mc)
    chips = [(1 - mx, my), (mx, 1 - my), (1 - mx, 1 - my)]

    def slot(px, py, pc):
        return out_ref.at[4 * px + 2 * py + pc]

    def copy(kk, block, to, src=None):
        return pltpu.make_async_remote_copy(
            src_ref=slot(*block) if src is None else src, dst_ref=slot(*block),
            send_sem=send_sems.at[kk], recv_sem=recv_sems.at[kk], device_id=to, device_id_type=MESH)

    mine = pltpu.make_async_copy(x_ref, slot(*me), local_sem)
    first = [copy(0, me, sibling, src=x_ref)] + [copy(1 + j, me, (*chip, mc), src=x_ref) for j, chip in enumerate(chips)]
    passed = [copy(4 + j, (*chip, mc), sibling) for j, chip in enumerate(chips)]

    def start():
        mine.start()
        for cp in first:
            cp.start()

    def forward():
        for j, chip in enumerate(chips):
            copy(1 + j, (*chip, mc), me).wait_recv()
            passed[j].start()

    def finish():
        copy(0, sibling, me).wait_recv()
        for j, chip in enumerate(chips):
            copy(4 + j, (*chip, 1 - mc), me).wait_recv()
        for cp in first + passed:
            cp.wait_send()
        mine.wait()

    return start, forward, finish


def _all_of(phase_lists):
    def run(fns):
        def every():
            for fn in fns:
                fn()
        return every
    return tuple(run(fns) for fns in zip(*phase_lists))


def _exchange_phases(x_ref, out_ref, send_sems, recv_sems, local_sem):
    mx, my, mc = lax.axis_index("x"), lax.axis_index("y"), lax.axis_index("c")
    me = 4 * mx + 2 * my + mc
    mine = pltpu.make_async_copy(x_ref.at[me], out_ref.at[me], local_sem)
    copies = []
    for kk in range(1, N_DEV):
        px, py, pc = mx ^ (kk >> 2), my ^ ((kk >> 1) & 1), mc ^ (kk & 1)
        peer = 4 * px + 2 * py + pc
        copies.append((peer, pltpu.make_async_remote_copy(
            src_ref=x_ref.at[peer], dst_ref=out_ref.at[me], send_sem=send_sems.at[kk - 1],
            recv_sem=recv_sems.at[kk - 1], device_id=(px, py, pc), device_id_type=MESH)))

    def start():
        mine.start()
        for _, cp in copies:
            cp.start()

    def finish():
        for kk, (peer, _) in enumerate(copies):
            pltpu.make_async_remote_copy(
                src_ref=x_ref.at[peer], dst_ref=out_ref.at[peer], send_sem=send_sems.at[kk],
                recv_sem=recv_sems.at[kk], device_id=(mx, my, mc), device_id_type=MESH).wait_recv()
        for _, cp in copies:
            cp.wait_send()
        mine.wait()

    return start, finish


def _shards_last(a):
    lead = a.shape[:-1]
    return jnp.moveaxis(a.reshape(*lead, N_DEV, a.shape[-1] // N_DEV), -2, 0)


def _shards_rows(a):
    l, n, c = a.shape
    return jnp.moveaxis(a.reshape(l, N_DEV, n // N_DEV, c), 1, 0)


def _flat_pad(parts, mult):
    flat = jnp.concatenate([p.reshape(-1) for p in parts])
    return jnp.pad(flat, (0, (-flat.shape[0]) % mult))


def _unflatten(flat, shapes):
    out, o = [], 0
    for shp in shapes:
        n = 1
        for s in shp:
            n *= s
        out.append(flat[o:o + n].reshape(shp))
        o += n
    return out


def kernel(x, meta_tokens, g_mix, w_in, w_conv, w_pool, pool_scale, w_out, g_mlp, w_up, w_down, g_final, loss_target, m_meta_tokens, m_g_mix, m_w_in, m_w_conv, m_w_pool, m_pool_scale, m_w_out, m_g_mlp, m_w_up, m_w_down, m_g_final, v_meta_tokens, v_g_mix, v_w_in, v_w_conv, v_w_pool, v_pool_scale, v_w_out, v_g_mlp, v_w_up, v_w_down, v_g_final):
    nb, seq, d = x.shape
    depth = g_mix.shape[0]
    lp = BLK + seq
    t = nb * lp
    cw = w_conv.shape[2] * N_DEV
    pw = pool_scale.shape[1]
    pg = pw // N_POOL_GROUPS
    aw = (w_in.shape[2] * N_DEV - 3 * cw - pw) // 3
    ua_w = 3 * cw + pw
    d_ff = w_up.shape[2] * N_DEV
    me = 4 * lax.axis_index("x") + 2 * lax.axis_index("y") + lax.axis_index("c")

    def rows(a):
        return a.reshape(-1, a.shape[-1])

    def gathered_w_in(g, layers):
        return jnp.moveaxis(g.reshape(N_DEV, layers, d, -1), 0, 2).reshape(layers, d, -1)

    g_in0, small_in = _all_gather([rows(w_in[:1]).astype(BF16), _flat_pad([meta_tokens, w_conv], 8 * BLK).reshape(-1, BLK)],
                                  "gather_first_weights")
    w_in0_f = gathered_w_in(g_in0, 1)
    small_in = small_in.reshape(N_DEV, -1)
    meta_f = jnp.moveaxis(small_in[:, :meta_tokens.size].reshape(N_DEV, N_META, d // N_DEV), 0, 1).reshape(N_META, d)
    w_conv_f = small_in[:, meta_tokens.size:meta_tokens.size + w_conv.size].reshape((N_DEV,) + w_conv.shape)
    w_conv_f = jnp.moveaxis(w_conv_f, 0, 2).reshape(depth, w_conv.shape[1], cw)

    def split_w_in(w):
        return w[:, :ua_w], w[:, ua_w:ua_w + aw], w[:, ua_w + aw:ua_w + 2 * aw], w[:, ua_w + 2 * aw:]

    wbd = jnp.zeros((depth, pw, pw), F32)
    for gi in range(N_POOL_GROUPS):
        wbd = wbd.at[:, gi * pg:(gi + 1) * pg, gi * pg:(gi + 1) * pg].set(w_pool[:, gi])
    wbd = wbd.astype(BF16)

    h = jnp.concatenate([jnp.zeros((nb, PAD, d), F32), jnp.broadcast_to(meta_f[None], (nb, N_META, d)), x], axis=1).reshape(t, d)
    saved, w_in_parts, w_in_full = [], [None] * depth, [None] * depth
    w_in_full[0] = w_in0_f[0]
    w_in_parts[0] = split_w_in(w_in0_f[0])
    for l in range(depth):
        wa, wq, wk, wv = w_in_parts[l]
        ua, q, k, v, hn1 = _inproj_fwd(h, g_mix[l][None], wa, wq, wk, wv)
        ycp = _mixer_fwd(ua, w_conv_f[l], wbd[l], pool_scale[l][None], nb)
        if l == 0:
            ycat, g_in, g_out, g_up, g_down = _attn_fwd(
                q, k, v, ycp, nb, gather=[rows(a).astype(BF16) for a in (w_in[1:], w_out, w_up, w_down)])
            w_in_rest = gathered_w_in(g_in, depth - 1)
            for ll in range(1, depth):
                w_in_full[ll] = w_in_rest[ll - 1]
                w_in_parts[ll] = split_w_in(w_in_rest[ll - 1])
            w_out_f = jnp.moveaxis(g_out.reshape((N_DEV,) + w_out.shape), 0, 1).reshape(depth, -1, d)
            wo_a, wo_b = w_out_f[:, :cw + pw], w_out_f[:, cw + pw:]
            w_up_f = g_up.reshape((N_DEV,) + w_up.shape)
            w_down_f = g_down.reshape((N_DEV,) + w_down.shape)
        else:
            ycat = _attn_fwd(q, k, v, ycp, nb)
        h_mid = _outproj(h, ycat, w_out_f[l])
        h_next, hn2 = _mlp_fwd(h_mid, g_mlp[l][None], w_up_f, w_down_f, l)
        saved.append((h, ua, q, k, v, hn1, ycat, h_mid, hn2))
        h = h_next

    loss_part, dh, dg_final = _loss_head(h, g_final[None], loss_target.reshape(nb * seq, d), nb)

    g_gmix, g_gmlp, g_wconv, g_pscale, g_wpool = [None] * depth, [None] * depth, [None] * depth, [None] * depth, [None] * depth
    g_win, g_wout, g_wup, g_wdown = [None] * depth, [None] * depth, [None] * depth, [None] * depth
    for l in reversed(range(depth)):
        h_in, ua, q, k, v, hn1, ycat, h_mid, hn2 = saved[l]
        wa, wqkv = w_in_parts[l][0], w_in_full[l][:, ua_w:]
        dh_mid, act, dm, g_gmlp[l] = _mlp_bwd(dh, h_mid, hn2, g_mlp[l][None], w_up_f, w_down_f, l)
        g_wup[l] = _matmul_tn(hn2, dm, "grad_w_up")
        g_wdown[l] = _matmul_tn(act, dh, "grad_w_down")
        dycp, do = _outproj_bwd(dh_mid, wo_a[l], wo_b[l])
        g_wout[l] = _matmul_tn(ycat, dh_mid, "grad_w_out")
        if l == 0:
            send = [_shards_last(jnp.stack(g_win[1:])), _shards_rows(jnp.stack(g_wout)),
                    _shards_last(jnp.stack(g_wup)), _shards_rows(jnp.stack(g_wdown))]
            dqkv, p_in, p_out, p_up, p_down = _attn_bwd(
                q, k, v, do, nb, exchange=[s.reshape(N_DEV, -1, s.shape[-1]) for s in send])
        else:
            dqkv = _attn_bwd(q, k, v, do, nb)
        dua, g_wconv[l], g_pscale[l], dwb = _mixer_bwd(ua, dycp, w_conv_f[l], wbd[l], pool_scale[l][None], nb)
        g_wpool[l] = jnp.stack([dwb[gi * pg:(gi + 1) * pg, gi * pg:(gi + 1) * pg] for gi in range(N_POOL_GROUPS)])
        g_win[l] = jnp.concatenate([_matmul_tn(hn1, dua, "grad_w_in_a"), _matmul_tn(hn1, dqkv, "grad_w_in_qkv")], axis=1)
        if l == 0:
            dh, g_gmix[l], p_in0 = _inproj_bwd(dh_mid, h_in, g_mix[l][None], dua, dqkv, wa, wqkv,
                                               exchange=_shards_last(g_win[0][None]).reshape(N_DEV, -1, w_in.shape[-1]))
        else:
            dh, g_gmix[l] = _inproj_bwd(dh_mid, h_in, g_mix[l][None], dua, dqkv, wa, wqkv)
    dh3 = dh.reshape(nb, lp, d)
    grad_x = dh3[:, BLK:]
    g_meta = _sum_parts(dh3[:, PAD:BLK], "sum_meta_grad")

    o_out = _adamw(rows(w_out), p_out, rows(m_w_out), rows(v_w_out), "adamw_w_out")
    o_up = _adamw(rows(w_up), p_up, rows(m_w_up), rows(v_w_up), "adamw_w_up")
    o_down = _adamw(rows(w_down), p_down, rows(m_w_down), rows(v_w_down), "adamw_w_down")
    o_in1 = _adamw(rows(w_in[1:]), p_in, rows(m_w_in[1:]), rows(v_w_in[1:]), "adamw_w_in_rest")
    o_in0 = _adamw(rows(w_in[:1]), p_in0, rows(m_w_in[:1]), rows(v_w_in[:1]), "adamw_w_in0")
    (gr_win, gr_wout, gr_wup, gr_wdown), (de_win, de_wout, de_wup, de_wdown), (nm_win, nm_wout, nm_wup, nm_wdown), \
        (nv_win, nv_wout, nv_wup, nv_wdown) = [
            [jnp.concatenate([i0, i1], axis=0).reshape(w_in.shape), o.reshape(w_out.shape), u.reshape(w_up.shape), dn.reshape(w_down.shape)]
            for i0, i1, o, u, dn in zip(o_in0, o_in1, o_out, o_up, o_down)]

    small_full = [g_meta, jnp.concatenate(g_gmix), jnp.stack(g_wconv), jnp.stack(g_wpool), jnp.concatenate(g_pscale),
                  jnp.concatenate(g_gmlp), dg_final.reshape(-1)]
    small_shapes = [a.shape for a in small_full]
    small_sum = _sum_parts(_all_gather([_flat_pad(small_full, 8 * BLK).reshape(-1, BLK)], "gather_small_grads")[0], "sum_small_grads")
    gr_meta, gr_gmix, gr_wconv, gr_wpool, gr_pscale, gr_gmlp, gr_gfinal = _unflatten(small_sum.reshape(-1), small_shapes)
    gr_meta = lax.dynamic_slice_in_dim(gr_meta, me * (d // N_DEV), d // N_DEV, axis=1)
    gr_wconv = lax.dynamic_slice_in_dim(gr_wconv, me * (cw // N_DEV), cw // N_DEV, axis=2)
    small_g = [gr_meta, gr_gmix, gr_wconv, gr_wpool, gr_pscale, gr_gmlp, gr_gfinal]
    local_shapes = [a.shape for a in small_g]
    pack_small = lambda parts_: _flat_pad(parts_, 8 * BLK).reshape(-1, BLK)
    small = _adamw(pack_small([meta_tokens, g_mix, w_conv, w_pool, pool_scale, g_mlp, g_final]), pack_small(small_g)[None],
                   pack_small([m_meta_tokens, m_g_mix, m_w_conv, m_w_pool, m_pool_scale, m_g_mlp, m_g_final]),
                   pack_small([v_meta_tokens, v_g_mix, v_w_conv, v_w_pool, v_pool_scale, v_g_mlp, v_g_final]), "adamw_small")
    _, de_s, nm_s, nv_s = [_unflatten(b.reshape(-1), local_shapes) for b in small]

    loss = lax.psum(loss_part[0, 0], ("x", "y", "c"))

    def ordered(meta, gmix, wconv, wpool, pscale, gmlp, gfinal, win, wout, wup, wdown):
        return [meta, gmix, win, wconv, wpool, pscale, wout, gmlp, wup, wdown, gfinal]

    grads = ordered(*small_g, gr_win, gr_wout, gr_wup, gr_wdown)
    deltas = ordered(*de_s, de_win, de_wout, de_wup, de_wdown)
    new_m = ordered(*nm_s, nm_win, nm_wout, nm_wup, nm_wdown)
    new_v = ordered(*nv_s, nv_win, nv_wout, nv_wup, nv_wdown)
    return (loss, grad_x, *grads, *deltas, *new_m, *new_v)
```

```python
import jax
import jax.numpy as jnp
from jax import lax
from jax.experimental import pallas as pl
from jax.experimental.pallas import tpu as pltpu

F32 = jnp.float32
BF16 = jnp.bfloat16

N_DEV = 8
N_META = 16
HEAD_DIM = 64
BLK = 128
PAD = BLK - N_META
N_POOL_GROUPS = 4
EPS = 1e-6

ADAM_LR = 0.001
ADAM_B1 = 0.9
ADAM_B2 = 0.999
ADAM_EPS = 1e-08
ADAM_WD = 0.01
ADAM_STEP = 10

V7X_VMEM_LIMIT = 56 * 1024 * 1024


def _params(*sem):
    return pltpu.CompilerParams(dimension_semantics=sem, vmem_limit_bytes=V7X_VMEM_LIMIT)


SUBLANES = 8
ROW_TILE = 512
ROW_TILE_MLP = 1152
ROW_TILE_WIDE = 1152
TN_ACC_BYTES = 16 * 1024 * 1024


def _row_tile(t, cap=ROW_TILE):
    for tm in range(min(cap, t) // SUBLANES * SUBLANES, 0, -SUBLANES):
        if t % tm == 0:
            return tm
    raise ValueError(f"row count {t} is not a multiple of 8")


def _full(a):
    nd = a.ndim
    return pl.BlockSpec(a.shape, lambda *_: (0,) * nd)


def _dot(a, b):
    return jnp.dot(a, b, preferred_element_type=F32)


def _dot_nt(a, b):
    return lax.dot_general(a, b, (((1,), (1,)), ((), ())), preferred_element_type=F32)


def _dot_tn(a, b):
    return lax.dot_general(a, b, (((0,), (0,)), ((), ())), preferred_element_type=F32)


def _rms_fwd(x, g):
    rstd = lax.rsqrt(jnp.mean(x * x, axis=-1, keepdims=True) + EPS)
    return x * rstd * g


def _rms_bwd(dy, x, g):
    rstd = lax.rsqrt(jnp.mean(x * x, axis=-1, keepdims=True) + EPS)
    xhat = x * rstd
    dxhat = dy * g
    dx = rstd * (dxhat - xhat * jnp.mean(dxhat * xhat, axis=-1, keepdims=True))
    return dx, jnp.sum(dy * xhat, axis=0, keepdims=True)


def _inproj_fwd(h, g, wa, wq, wk, wv):
    t, d = h.shape
    tm = _row_tile(t, ROW_TILE_WIDE)
    ua_w, aw = wa.shape[1], wq.shape[1]

    def body(h_ref, g_ref, wa_ref, wq_ref, wk_ref, wv_ref, ua_ref, q_ref, k_ref, v_ref, hn_ref):
        hn = _rms_fwd(h_ref[...], g_ref[...]).astype(BF16)
        hn_ref[...] = hn
        ua_ref[...] = _dot(hn, wa_ref[...])
        q_ref[...] = _dot(hn, wq_ref[...]).astype(BF16)
        k_ref[...] = _dot(hn, wk_ref[...]).astype(BF16)
        v_ref[...] = _dot(hn, wv_ref[...]).astype(BF16)

    row = lambda w: pl.BlockSpec((tm, w), lambda i: (i, 0))
    return pl.pallas_call(
        body, name="inproj_fwd", grid=(t // tm,),
        in_specs=[row(d), _full(g), _full(wa), _full(wq), _full(wk), _full(wv)],
        out_specs=[row(ua_w), row(aw), row(aw), row(aw), row(d)],
        out_shape=[jax.ShapeDtypeStruct((t, ua_w), F32)] + [jax.ShapeDtypeStruct((t, aw), BF16)] * 3
        + [jax.ShapeDtypeStruct((t, d), BF16)],
        compiler_params=_params("arbitrary"),
    )(h, g, wa, wq, wk, wv)


def _pool_geometry(lp, pw):
    pg = pw // N_POOL_GROUPS
    row = lax.broadcasted_iota(jnp.int32, (lp, pw), 0)
    lane = lax.broadcasted_iota(jnp.int32, (lp, pw), 1)
    grp = [(lane >= g * pg) & (lane < (g + 1) * pg) for g in range(N_POOL_GROUPS)]
    wlen = jnp.where(grp[0], 2, jnp.where(grp[1], 4, jnp.where(grp[2], 8, 16)))
    cnt = jnp.clip(row - (PAD - 1), 1, wlen).astype(F32)
    return grp, cnt, row >= PAD


def _by_group(grp, vals):
    return jnp.where(grp[0], vals[0], jnp.where(grp[1], vals[1], jnp.where(grp[2], vals[2], vals[3])))


def _pooled(p, grp, cnt, real):
    s2 = p + pltpu.roll(p, 1, 0)
    s4 = s2 + pltpu.roll(s2, 2, 0)
    s8 = s4 + pltpu.roll(s4, 4, 0)
    s16 = s8 + pltpu.roll(s8, 8, 0)
    return jnp.where(real, _by_group(grp, (s2, s4, s8, s16)) / cnt - p, 0.0)


def _conv(uu, wc_ref):
    return wc_ref[2:3, :] * uu + wc_ref[1:2, :] * pltpu.roll(uu, 1, 0) + wc_ref[0:1, :] * pltpu.roll(uu, 2, 0)


def _mixer_fwd(ua, wconv, wbd, pscale, nb):
    t = ua.shape[0]
    lp = t // nb
    cw, pw = wconv.shape[1], wbd.shape[0]

    def body(ua_ref, wc_ref, wbd_ref, ps_ref, y_ref):
        cb = ua_ref[:, 0:cw]
        uu = ua_ref[:, cw:2 * cw] * ua_ref[:, 2 * cw:3 * cw]
        y_ref[:, 0:cw] = (cb * _conv(uu, wc_ref)).astype(BF16)
        grp, cnt, real = _pool_geometry(lp, pw)
        pooled = _pooled(ua_ref[:, 3 * cw:3 * cw + pw], grp, cnt, real)
        y_ref[:, cw:cw + pw] = (_dot(pooled.astype(BF16), wbd_ref[...]) * ps_ref[...]).astype(BF16)

    return pl.pallas_call(
        body, name="mixer_fwd", grid=(nb,),
        in_specs=[pl.BlockSpec((lp, ua.shape[1]), lambda s: (s, 0)), _full(wconv), _full(wbd), _full(pscale)],
        out_specs=pl.BlockSpec((lp, cw + pw), lambda s: (s, 0)),
        out_shape=jax.ShapeDtypeStruct((t, cw + pw), BF16),
        compiler_params=_params("arbitrary"),
    )(ua, wconv, wbd, pscale)


ATT_NW = 2
ATT_W = ATT_NW * BLK
ATT_PAIRS_FWD = 4
ATT_PAIRS_BWD = 4
HW = 2 * HEAD_DIM
ATT_SCALE = HEAD_DIM ** -0.5


def _attn_lk(z, valid):
    nz = -z
    return jnp.where(valid, jnp.minimum(nz, 0.0) - jnp.log(1.0 + jnp.exp(jnp.minimum(z, nz))), 0.0)


def _tri_sum_bf16(x, tri):
    return _dot(x.astype(BF16), tri)


def _tri_sum(x, tri):
    hi = x.astype(BF16)
    lo = (x - hi.astype(F32)).astype(BF16)
    return _dot(hi, tri) + _dot(lo, tri)


def _attn_weights(z, suffix, carry, valid):
    return jnp.where(valid, jnp.exp(z + suffix + carry), 0.0)


def _attn_consts():
    r = lax.broadcasted_iota(jnp.int32, (2 * BLK, ATT_W), 0)
    c = lax.broadcasted_iota(jnp.int32, (2 * BLK, ATT_W), 1)
    cmr = c - (r & (BLK - 1))
    kr = lax.broadcasted_iota(jnp.int32, (ATT_W, ATT_W), 0)
    kc = lax.broadcasted_iota(jnp.int32, (ATT_W, ATT_W), 1)
    lane = lax.broadcasted_iota(jnp.int32, (1, HW), 1)
    heads = (lane < HEAD_DIM, lane >= HEAD_DIM)
    return c, cmr, kr, kc, heads


def _tri(cond):
    return jnp.where(cond, 1.0, 0.0).astype(BF16)


def _rows(i):
    return pl.ds(pl.multiple_of(i * BLK, BLK), BLK)


def _stack_heads(x, heads):
    zero = jnp.zeros_like(x)
    return jnp.concatenate([jnp.where(heads[0], x, zero), jnp.where(heads[1], x, zero)], axis=0)


def _unstack_heads(x, heads):
    return jnp.where(heads[0], x[:BLK], x[BLK:])


def _window(i, s):
    start = jnp.maximum(i + 1 - ATT_NW * (s + 1), 0) * BLK
    return start, (i + 1 - ATT_NW * s) * BLK


def _attn_fwd(q, k, v, ycp, nb, gather=None):
    t, aw = q.shape
    cpw = ycp.shape[1]
    lp = t // nb
    nblk = lp // BLK
    pp = min(ATT_PAIRS_FWD, aw // HW)
    cols = [slice(p * HW, (p + 1) * HW) for p in range(pp)]
    ncol = aw // (pp * HW)
    assert ncol == 1, "the [y_conv | y_pool | y_attn] output block needs all head pairs in one grid step"
    nsteps = nb * ncol

    def body(q_ref, k_ref, v_ref, ycp_ref, *rest):
        if gather is None:
            (o_ref,) = rest
        else:
            ng = len(gather)
            x_refs, o_ref, g_refs, sems = rest[:ng], rest[ng], rest[ng + 1:2 * ng + 1], rest[2 * ng + 1:]
            step = pl.program_id(0) * ncol + pl.program_id(1)
            start, forward, finish = _all_of([_gather_phases(x_refs[j], g_refs[j], *sems[3 * j:3 * j + 3]) for j in range(ng)])
            pl.when(step == 0)(start)
            pl.when(step == nsteps - 1)(forward)
        c, cmr, kr, kc, heads = _attn_consts()
        m_from = _tri(kr >= kc)
        o_ref[:, 0:cpw] = ycp_ref[...]

        def qblock(i, _):
            qs = [_stack_heads(q_ref[_rows(i), cs] * ATT_SCALE, heads) for cs in cols]

            def window(s, st):
                accs, carries = st
                start, end = _window(i, s)
                keys = pl.ds(pl.multiple_of(start, BLK), ATT_W)
                valid = (cmr < i * BLK - start) & (c < end - start)
                zs = [_dot_nt(qs[p], k_ref[keys, cs]) for p, cs in enumerate(cols)]
                suffix = [_tri_sum_bf16(_attn_lk(zs[p], valid), m_from) for p in range(pp)]
                new_accs, new_carries = [], []
                for p, cs in enumerate(cols):
                    a = _attn_weights(zs[p], suffix[p], carries[p], valid)
                    new_accs.append(accs[p] + _dot(a.astype(BF16), v_ref[keys, cs]))
                    new_carries.append(carries[p] + suffix[p][:, 0:1])
                return tuple(new_accs), tuple(new_carries)

            init = (tuple(jnp.zeros((2 * BLK, HW), F32) for _ in cols), tuple(jnp.zeros((2 * BLK, 1), F32) for _ in cols))
            accs, _ = lax.fori_loop(0, (i + ATT_NW) // ATT_NW, window, init)
            for p in range(pp):
                o_ref[_rows(i), cpw + p * HW:cpw + (p + 1) * HW] = _unstack_heads(accs[p], heads).astype(BF16)
            return 0

        lax.fori_loop(0, nblk, qblock, 0)
        if gather is not None:
            pl.when(step == nsteps - 1)(finish)

    spec = pl.BlockSpec((lp, pp * HW), lambda s, p: (s, p))
    in_specs = [spec, spec, spec, pl.BlockSpec((lp, cpw), lambda s, p: (s, 0))]
    y_spec = pl.BlockSpec((lp, cpw + aw), lambda s, p: (s, 0))
    y_shape = jax.ShapeDtypeStruct((t, cpw + aw), BF16)
    if gather is None:
        return pl.pallas_call(
            body, name="attn_fwd", grid=(nb, ncol), in_specs=in_specs, out_specs=y_spec, out_shape=y_shape,
            compiler_params=_params("arbitrary", "arbitrary"),
        )(q, k, v, ycp)
    return pl.pallas_call(
        body, name="attn_fwd_gather", grid=(nb, ncol), in_specs=in_specs + [ANY] * len(gather),
        out_specs=[y_spec] + [ANY] * len(gather),
        out_shape=[y_shape] + [jax.ShapeDtypeStruct((N_DEV,) + x.shape, x.dtype) for x in gather],
        scratch_shapes=COMM_SEMS * len(gather), compiler_params=_params("arbitrary", "arbitrary"),
    )(q, k, v, ycp, *gather)


def _outproj(h, y, w):
    t, d = h.shape
    tm = _row_tile(t, ROW_TILE_WIDE)

    def body(h_ref, y_ref, w_ref, o_ref):
        o_ref[...] = h_ref[...] + _dot(y_ref[...], w_ref[...])

    row = lambda w_: pl.BlockSpec((tm, w_), lambda i: (i, 0))
    return pl.pallas_call(
        body, name="outproj", grid=(t // tm,),
        in_specs=[row(d), row(y.shape[1]), _full(w)],
        out_specs=row(d), out_shape=jax.ShapeDtypeStruct((t, d), F32),
        compiler_params=_params("arbitrary"),
    )(h, y, w)


def _mlp_weight_specs(wup, wdown, layer):
    nff, _, d, fc = wup.shape
    return nff, fc, [pl.BlockSpec((None, None, d, fc), lambda i, kk: (kk, layer, 0, 0)),
                     pl.BlockSpec((None, None, fc, d), lambda i, kk: (kk, layer, 0, 0))]


def _mlp_fwd(h, g, wup, wdown, layer):
    t, d = h.shape
    tm = _row_tile(t, ROW_TILE_MLP)
    nff, fc, wspecs = _mlp_weight_specs(wup, wdown, layer)

    def body(h_ref, g_ref, wu_ref, wd_ref, o_ref, hn_ref, acc):
        kk = pl.program_id(1)

        @pl.when(kk == 0)
        def _():
            x = h_ref[...]
            hn_ref[...] = _rms_fwd(x, g_ref[...]).astype(BF16)
            acc[...] = x

        m = _dot(hn_ref[...], wu_ref[...])
        a = jnp.square(jnp.maximum(m, 0.0)).astype(BF16)
        acc[...] += _dot(a, wd_ref[...])

        @pl.when(kk == nff - 1)
        def _():
            o_ref[...] = acc[...]

    row = pl.BlockSpec((tm, d), lambda i, kk: (i, 0))
    return pl.pallas_call(
        body, name="mlp_fwd", grid=(t // tm, nff),
        in_specs=[row, _full(g)] + wspecs,
        out_specs=[row, row],
        out_shape=[jax.ShapeDtypeStruct((t, d), F32), jax.ShapeDtypeStruct((t, d), BF16)],
        scratch_shapes=[pltpu.VMEM((tm, d), F32)],
        compiler_params=_params("arbitrary", "arbitrary"),
    )(h, g, wup, wdown)


def _loss_head(h, g, target, nb):
    t, d = h.shape
    nblk = t // nb // BLK
    nx = nblk - 1

    def body(h_ref, g_ref, t_ref, loss_ref, dh_ref, dg_ref):
        s, i = pl.program_id(0), pl.program_id(1)

        @pl.when((s == 0) & (i == 0))
        def _():
            loss_ref[...] = jnp.zeros_like(loss_ref)
            dg_ref[...] = jnp.zeros_like(dg_ref)

        @pl.when(i == 0)
        def _():
            dh_ref[...] = jnp.zeros_like(dh_ref)

        @pl.when(i > 0)
        def _():
            x, gg = h_ref[...], g_ref[...]
            err = _rms_fwd(x, gg) - t_ref[...]
            loss_ref[...] += jnp.sum(err * err) * (0.5 / d)
            dx, dg = _rms_bwd(err * (1.0 / d), x, gg)
            dh_ref[...] = dx
            dg_ref[...] += dg

    hspec = pl.BlockSpec((BLK, d), lambda s, i: (s * nblk + i, 0))
    return pl.pallas_call(
        body, name="loss_head", grid=(nb, nblk),
        in_specs=[hspec, _full(g), pl.BlockSpec((BLK, d), lambda s, i: (s * nx + jnp.maximum(i - 1, 0), 0))],
        out_specs=[pl.BlockSpec((1, BLK), lambda s, i: (0, 0)), hspec, pl.BlockSpec((1, d), lambda s, i: (0, 0))],
        out_shape=[jax.ShapeDtypeStruct((1, BLK), F32), jax.ShapeDtypeStruct((t, d), F32), jax.ShapeDtypeStruct((1, d), F32)],
        compiler_params=_params("arbitrary", "arbitrary"),
    )(h, g, target)


def _mlp_bwd(dh, h, hn, g, wup, wdown, layer):
    t, d = h.shape
    tm = _row_tile(t, ROW_TILE_MLP)
    nff, fc, wspecs = _mlp_weight_specs(wup, wdown, layer)

    def body(dh_ref, h_ref, hn_ref, g_ref, wu_ref, wd_ref, dhm_ref, a_ref, dm_ref, dg_ref, dhn, dhb):
        i, kk = pl.program_id(0), pl.program_id(1)

        @pl.when((i == 0) & (kk == 0))
        def _():
            dg_ref[...] = jnp.zeros_like(dg_ref)

        @pl.when(kk == 0)
        def _():
            dhb[...] = dh_ref[...].astype(BF16)
            dhn[...] = jnp.zeros_like(dhn)

        r = jnp.maximum(_dot(hn_ref[...], wu_ref[...]), 0.0)
        a_ref[...] = (r * r).astype(BF16)
        dm = (_dot_nt(dhb[...], wd_ref[...]) * (2.0 * r)).astype(BF16)
        dm_ref[...] = dm
        dhn[...] += _dot_nt(dm, wu_ref[...])

        @pl.when(kk == nff - 1)
        def _():
            dx, dg = _rms_bwd(dhn[...], h_ref[...], g_ref[...])
            dhm_ref[...] = dh_ref[...] + dx
            dg_ref[...] += dg

    row = pl.BlockSpec((tm, d), lambda i, kk: (i, 0))
    ff = pl.BlockSpec((tm, fc), lambda i, kk: (i, kk))
    return pl.pallas_call(
        body, name="mlp_bwd", grid=(t // tm, nff),
        in_specs=[row, row, row, _full(g)] + wspecs,
        out_specs=[row, ff, ff, pl.BlockSpec((1, d), lambda i, kk: (0, 0))],
        out_shape=[jax.ShapeDtypeStruct((t, d), F32), jax.ShapeDtypeStruct((t, nff * fc), BF16),
                   jax.ShapeDtypeStruct((t, nff * fc), BF16), jax.ShapeDtypeStruct((1, d), F32)],
        scratch_shapes=[pltpu.VMEM((tm, d), F32), pltpu.VMEM((tm, d), BF16)],
        compiler_params=_params("arbitrary", "arbitrary"),
    )(dh, h, hn, g, wup, wdown)


def _matmul_tn(x, y, name):
    t, k1 = x.shape
    n = y.shape[1]
    tt = _row_tile(t)
    tn = min(n, max(512, TN_ACC_BYTES // (4 * k1)))
    steps = t // tt

    def body(x_ref, y_ref, o_ref, acc):
        @pl.when(pl.program_id(1) == 0)
        def _():
            acc[...] = jnp.zeros_like(acc)

        acc[...] += _dot_tn(x_ref[...].astype(BF16), y_ref[...].astype(BF16))

        @pl.when(pl.program_id(1) == steps - 1)
        def _():
            o_ref[...] = acc[...].astype(BF16)

    return pl.pallas_call(
        body, name=name, grid=(n // tn, steps),
        in_specs=[pl.BlockSpec((tt, k1), lambda j, r: (r, 0)), pl.BlockSpec((tt, tn), lambda j, r: (r, j))],
        out_specs=pl.BlockSpec((k1, tn), lambda j, r: (0, j)),
        out_shape=jax.ShapeDtypeStruct((k1, n), BF16),
        scratch_shapes=[pltpu.VMEM((k1, tn), F32)],
        compiler_params=_params("arbitrary", "arbitrary"),
    )(x, y)


def _outproj_bwd(dh, wa, wb):
    t, d = dh.shape
    tm = _row_tile(t, ROW_TILE_WIDE)
    na, nbw = wa.shape[0], wb.shape[0]

    def body(dh_ref, wa_ref, wb_ref, da_ref, db_ref):
        x = dh_ref[...].astype(BF16)
        da_ref[...] = _dot_nt(x, wa_ref[...])
        db_ref[...] = _dot_nt(x, wb_ref[...]).astype(BF16)

    row = lambda w: pl.BlockSpec((tm, w), lambda i: (i, 0))
    return pl.pallas_call(
        body, name="outproj_bwd", grid=(t // tm,),
        in_specs=[row(d), _full(wa), _full(wb)], out_specs=[row(na), row(nbw)],
        out_shape=[jax.ShapeDtypeStruct((t, na), F32), jax.ShapeDtypeStruct((t, nbw), BF16)],
        compiler_params=_params("arbitrary"),
    )(dh, wa, wb)


def _attn_bwd(q, k, v, do, nb, exchange=None):
    t, aw = q.shape
    lp = t // nb
    nblk = lp // BLK
    pp = min(ATT_PAIRS_BWD, aw // HW)
    cols = [slice(p * HW, (p + 1) * HW) for p in range(pp)]
    ncol = aw // (pp * HW)
    assert ncol == 1, "the [dq | dk | dv] output block needs all head pairs in one grid step"
    nsteps = nb * ncol
    max_windows = (nblk + ATT_NW - 1) // ATT_NW

    def body(q_ref, k_ref, v_ref, do_ref, *rest):
        if exchange is None:
            dqkv_ref, dk_acc, dv_acc, g_s, sig_s = rest
        else:
            ne = len(exchange)
            x_refs, dqkv_ref, e_refs = rest[:ne], rest[ne], rest[ne + 1:2 * ne + 1]
            dk_acc, dv_acc, g_s, sig_s = rest[2 * ne + 1:2 * ne + 5]
            sems = rest[2 * ne + 5:]
            step = pl.program_id(0) * ncol + pl.program_id(1)
            start, finish = _all_of([_exchange_phases(x_refs[j], e_refs[j], *sems[3 * j:3 * j + 3]) for j in range(ne)])
            pl.when(step == 0)(start)
        c, cmr, kr, kc, heads = _attn_consts()
        m_from = _tri(kr >= kc)
        m_before = _tri(kr < kc)
        dk_acc[...] = jnp.zeros_like(dk_acc)
        dv_acc[...] = jnp.zeros_like(dv_acc)

        def qblock(i, _):
            qs = [_stack_heads(q_ref[_rows(i), cs] * ATT_SCALE, heads) for cs in cols]
            dos = [_stack_heads(do_ref[_rows(i), cs], heads) for cs in cols]
            nwin = (i + ATT_NW) // ATT_NW

            def geometry(s):
                start, end = _window(i, s)
                return (cmr < i * BLK - start) & (c < end - start), pl.ds(pl.multiple_of(start, BLK), ATT_W)

            def down(s, carries):
                valid, keys = geometry(s)
                zs = [_dot_nt(qs[p], k_ref[keys, cs]) for p, cs in enumerate(cols)]
                das = [_dot_nt(dos[p], v_ref[keys, cs]) for p, cs in enumerate(cols)]
                suffix = []
                for p in range(pp):
                    lk = _attn_lk(zs[p], valid)
                    sig_s[p, s] = jnp.exp(zs[p] + lk)
                    suffix.append(_tri_sum_bf16(lk, m_from))
                new_carries = []
                for p, cs in enumerate(cols):
                    a = _attn_weights(zs[p], suffix[p], carries[p], valid)
                    g_s[p, s] = das[p] * a
                    dv_acc[keys, cs] += _dot_tn(a.astype(BF16), dos[p])
                    new_carries.append(carries[p] + suffix[p][:, 0:1])
                return tuple(new_carries)

            lax.fori_loop(0, nwin, down, tuple(jnp.zeros((2 * BLK, 1), F32) for _ in cols))

            def up(ss, st):
                accs, carries = st
                s = nwin - 1 - ss
                valid, keys = geometry(s)
                gs = [g_s[p, s] for p in range(pp)]
                before = [_tri_sum(gs[p], m_before) for p in range(pp)]
                new_accs, new_carries = [], []
                for p, cs in enumerate(cols):
                    g, sig = gs[p], sig_s[p, s]
                    dz = jnp.where(valid, g * (1.0 - sig) - (before[p] + carries[p]) * sig, 0.0).astype(BF16)
                    new_accs.append(accs[p] + _dot(dz, k_ref[keys, cs]))
                    dk_acc[keys, cs] += _dot_tn(dz, qs[p])
                    new_carries.append(carries[p] + jnp.sum(g, axis=1, keepdims=True))
                return tuple(new_accs), tuple(new_carries)

            init = (tuple(jnp.zeros((2 * BLK, HW), F32) for _ in cols), tuple(jnp.zeros((2 * BLK, 1), F32) for _ in cols))
            accs, _ = lax.fori_loop(0, nwin, up, init)
            for p, cs in enumerate(cols):
                dqkv_ref[_rows(i), cs] = (_unstack_heads(accs[p], heads) * ATT_SCALE).astype(BF16)
            return 0

        lax.fori_loop(0, nblk, qblock, 0)
        dqkv_ref[:, aw:2 * aw] = dk_acc[...].astype(BF16)
        dqkv_ref[:, 2 * aw:3 * aw] = dv_acc[...].astype(BF16)
        if exchange is not None:
            pl.when(step == nsteps - 1)(finish)

    spec = pl.BlockSpec((lp, pp * HW), lambda s, p: (s, p), pipeline_mode=pl.Buffered(1))
    d_spec = pl.BlockSpec((lp, 3 * aw), lambda s, p: (s, 0), pipeline_mode=pl.Buffered(1))
    d_shape = jax.ShapeDtypeStruct((t, 3 * aw), BF16)
    scratch = [pltpu.VMEM((lp, pp * HW), F32), pltpu.VMEM((lp, pp * HW), F32),
               pltpu.VMEM((pp, max_windows, 2 * BLK, ATT_W), F32), pltpu.VMEM((pp, max_windows, 2 * BLK, ATT_W), F32)]
    if exchange is None:
        return pl.pallas_call(
            body, name="attn_bwd", grid=(nb, ncol), in_specs=[spec] * 4, out_specs=d_spec, out_shape=d_shape,
            scratch_shapes=scratch, compiler_params=_params("arbitrary", "arbitrary"),
        )(q, k, v, do)
    return pl.pallas_call(
        body, name="attn_bwd_exchange", grid=(nb, ncol), in_specs=[spec] * 4 + [ANY] * len(exchange),
        out_specs=[d_spec] + [ANY] * len(exchange),
        out_shape=[d_shape] + [jax.ShapeDtypeStruct(x.shape, x.dtype) for x in exchange],
        scratch_shapes=scratch + COMM_SEMS * len(exchange), compiler_params=_params("arbitrary", "arbitrary"),
    )(q, k, v, do, *exchange)


def _mixer_bwd(ua, dy, wconv, wbd, pscale, nb):
    t = ua.shape[0]
    lp = t // nb
    cw, pw = wconv.shape[1], wbd.shape[0]

    def body(ua_ref, dy_ref, wc_ref, wbd_ref, ps_ref, du_ref, dwc_ref, dps_ref, dwb_ref):
        @pl.when(pl.program_id(0) == 0)
        def _():
            dwc_ref[...] = jnp.zeros_like(dwc_ref)
            dps_ref[...] = jnp.zeros_like(dps_ref)
            dwb_ref[...] = jnp.zeros_like(dwb_ref)

        up = lambda x, n: pltpu.roll(x, lp - n, 0)
        cb, cc, cx = ua_ref[:, 0:cw], ua_ref[:, cw:2 * cw], ua_ref[:, 2 * cw:3 * cw]
        uu = cc * cx
        dyc = dy_ref[:, 0:cw]
        du_ref[:, 0:cw] = (dyc * _conv(uu, wc_ref)).astype(BF16)
        dcv = dyc * cb
        duu = wc_ref[2:3, :] * dcv + wc_ref[1:2, :] * up(dcv, 1) + wc_ref[0:1, :] * up(dcv, 2)
        du_ref[:, cw:2 * cw] = (duu * cx).astype(BF16)
        du_ref[:, 2 * cw:3 * cw] = (duu * cc).astype(BF16)
        dwc_ref[0:1, :] += jnp.sum(dcv * pltpu.roll(uu, 2, 0), axis=0, keepdims=True)
        dwc_ref[1:2, :] += jnp.sum(dcv * pltpu.roll(uu, 1, 0), axis=0, keepdims=True)
        dwc_ref[2:3, :] += jnp.sum(dcv * uu, axis=0, keepdims=True)
        grp, cnt, real = _pool_geometry(lp, pw)
        p = ua_ref[:, 3 * cw:3 * cw + pw]
        pooled = _pooled(p, grp, cnt, real).astype(BF16)
        dyp = dy_ref[:, cw:cw + pw]
        dps_ref[...] += jnp.sum(dyp * _dot(pooled, wbd_ref[...]), axis=0, keepdims=True)
        dpre = (dyp * ps_ref[...]).astype(BF16)
        dwb_ref[...] += _dot_tn(pooled, dpre)
        dpooled = jnp.where(real, _dot_nt(dpre, wbd_ref[...]), 0.0)
        xm = dpooled / cnt
        l2 = xm + up(xm, 1)
        l4 = l2 + up(l2, 2)
        l8 = l4 + up(l4, 4)
        l16 = l8 + up(l8, 8)
        du_ref[:, 3 * cw:3 * cw + pw] = jnp.where(real, _by_group(grp, (l2, l4, l8, l16)) - dpooled, 0.0).astype(BF16)

    seq = lambda w: pl.BlockSpec((lp, w), lambda s: (s, 0))
    return pl.pallas_call(
        body, name="mixer_bwd", grid=(nb,),
        in_specs=[seq(ua.shape[1]), seq(cw + pw), _full(wconv), _full(wbd), _full(pscale)],
        out_specs=[seq(ua.shape[1]), pl.BlockSpec((3, cw), lambda s: (0, 0)), pl.BlockSpec((1, pw), lambda s: (0, 0)),
                   pl.BlockSpec((pw, pw), lambda s: (0, 0))],
        out_shape=[jax.ShapeDtypeStruct(ua.shape, BF16), jax.ShapeDtypeStruct((3, cw), F32),
                   jax.ShapeDtypeStruct((1, pw), F32), jax.ShapeDtypeStruct((pw, pw), F32)],
        compiler_params=_params("arbitrary"),
    )(ua, dy, wconv, wbd, pscale)


def _inproj_bwd(dh, h, g, dua, dqkv, wa, wqkv, exchange=None):
    t, d = h.shape
    tm = _row_tile(t)
    nsteps = t // tm

    def body(dh_ref, h_ref, g_ref, dua_ref, dqkv_ref, wa_ref, wqkv_ref, *rest):
        if exchange is None:
            o_ref, dg_ref = rest
        else:
            x_ref, o_ref, dg_ref, e_ref, send_sems, recv_sems, local_sem = rest
            start, finish = _exchange_phases(x_ref, e_ref, send_sems, recv_sems, local_sem)
            pl.when(pl.program_id(0) == 0)(start)

        @pl.when(pl.program_id(0) == 0)
        def _():
            dg_ref[...] = jnp.zeros_like(dg_ref)

        dhn = _dot_nt(dua_ref[...], wa_ref[...]) + _dot_nt(dqkv_ref[...], wqkv_ref[...])
        dx, dg = _rms_bwd(dhn, h_ref[...], g_ref[...])
        o_ref[...] = dh_ref[...] + dx
        dg_ref[...] += dg
        if exchange is not None:
            pl.when(pl.program_id(0) == nsteps - 1)(finish)

    row = lambda w: pl.BlockSpec((tm, w), lambda i: (i, 0))
    in_specs = [row(d), row(d), _full(g), row(dua.shape[1]), row(dqkv.shape[1]), _full(wa), _full(wqkv)]
    out_specs = [row(d), pl.BlockSpec((1, d), lambda i: (0, 0))]
    out_shape = [jax.ShapeDtypeStruct((t, d), F32), jax.ShapeDtypeStruct((1, d), F32)]
    if exchange is None:
        return pl.pallas_call(
            body, name="inproj_bwd", grid=(nsteps,), in_specs=in_specs, out_specs=out_specs, out_shape=out_shape,
            compiler_params=_params("arbitrary"),
        )(dh, h, g, dua, dqkv, wa, wqkv)
    return pl.pallas_call(
        body, name="inproj_bwd_exchange", grid=(nsteps,), in_specs=in_specs + [ANY], out_specs=out_specs + [ANY],
        out_shape=out_shape + [jax.ShapeDtypeStruct(exchange.shape, exchange.dtype)], scratch_shapes=COMM_SEMS,
        compiler_params=_params("arbitrary"),
    )(dh, h, g, dua, dqkv, wa, wqkv, exchange)


def _part_tile(r):
    if r <= 512:
        return r
    for tr in (256, 128, 64, 32, 16, 8):
        if r % tr == 0:
            return tr
    return r


def _sum_parts(x, name):
    n, r, c = x.shape
    tr = _part_tile(r)

    def body(x_ref, o_ref):
        acc = x_ref[0]
        for p in range(1, n):
            acc = acc + x_ref[p]
        o_ref[...] = acc

    return pl.pallas_call(
        body, name=name, grid=(r // tr,),
        in_specs=[pl.BlockSpec((n, tr, c), lambda i: (0, i, 0))], out_specs=pl.BlockSpec((tr, c), lambda i: (i, 0)),
        out_shape=jax.ShapeDtypeStruct((r, c), F32), compiler_params=_params("arbitrary"),
    )(x)


def _adamw(w, gparts, m, v, name):
    n, r, c = gparts.shape
    tr = _part_tile(r)

    def body(w_ref, g_ref, m_ref, v_ref, go_ref, d_ref, mo_ref, vo_ref):
        g = g_ref[0].astype(F32)
        for p in range(1, n):
            g = g + g_ref[p].astype(F32)
        go_ref[...] = g
        mm = ADAM_B1 * m_ref[...] + (1.0 - ADAM_B1) * g
        vv = ADAM_B2 * v_ref[...] + (1.0 - ADAM_B2) * jnp.square(g)
        mo_ref[...] = mm
        vo_ref[...] = vv
        m_hat = mm / (1.0 - ADAM_B1 ** ADAM_STEP)
        v_hat = vv / (1.0 - ADAM_B2 ** ADAM_STEP)
        d_ref[...] = -ADAM_LR * (m_hat / (jnp.sqrt(v_hat) + ADAM_EPS) + ADAM_WD * w_ref[...])

    row = pl.BlockSpec((tr, c), lambda i: (i, 0))
    return pl.pallas_call(
        body, name=name, grid=(r // tr,),
        in_specs=[row, pl.BlockSpec((n, tr, c), lambda i: (0, i, 0)), row, row], out_specs=[row] * 4,
        out_shape=[jax.ShapeDtypeStruct((r, c), F32)] * 4, compiler_params=_params("arbitrary"),
    )(w, gparts, m, v)


MESH = pl.DeviceIdType.MESH
ANY = pl.BlockSpec(memory_space=pl.ANY)


def _all_gather(xs, name):
    n = len(xs)

    def body(*refs):
        x_refs, out_refs, sems = refs[:n], refs[n:2 * n], refs[2 * n:]
        start, forward, finish = _all_of([_gather_phases(x_refs[j], out_refs[j], *sems[3 * j:3 * j + 3]) for j in range(n)])
        start()
        forward()
        finish()

    return pl.pallas_call(
        body, name=name, in_specs=[ANY] * n, out_specs=[ANY] * n,
        out_shape=[jax.ShapeDtypeStruct((N_DEV,) + x.shape, x.dtype) for x in xs], scratch_shapes=COMM_SEMS * n,
    )(*xs)


COMM_SEMS = [pltpu.SemaphoreType.DMA((7,)), pltpu.SemaphoreType.DMA((7,)), pltpu.SemaphoreType.DMA(())]


def _gather_phases(x_ref, out_ref, send_sems, recv_sems, local_sem):
    mx, my, mc = lax.axis_index("x"), lax.axis_index("y"), lax.axis_index("c")
    me, sibling = (mx, my, mc), (mx, my, 1 - mc)
    chips = [(1 - mx, my), (mx, 1 - my), (1 - mx, 1 - my)]

    def slot(px, py, pc):
        return out_ref.at[4 * px + 2 * py + pc]

    def copy(kk, block, to, src=None):
        return pltpu.make_async_remote_copy(
            src_ref=slot(*block) if src is None else src, dst_ref=slot(*block),
            send_sem=send_sems.at[kk], recv_sem=recv_sems.at[kk], device_id=to, device_id_type=MESH)

    mine = pltpu.make_async_copy(x_ref, slot(*me), local_sem)
    first = [copy(0, me, sibling, src=x_ref)] + [copy(1 + j, me, (*chip, mc), src=x_ref) for j, chip in enumerate(chips)]
    passed = [copy(4 + j, (*chip, mc), sibling) for j, chip in enumerate(chips)]

    def start():
        mine.start()
        for cp in first:
            cp.start()

    def forward():
        for j, chip in enumerate(chips):
            copy(1 + j, (*chip, mc), me).wait_recv()
            passed[j].start()

    def finish():
        copy(0, sibling, me).wait_recv()
        for j, chip in enumerate(chips):
            copy(4 + j, (*chip, 1 - mc), me).wait_recv()
        for cp in first + passed:
            cp.wait_send()
        mine.wait()

    return start, forward, finish


def _all_of(phase_lists):
    def run(fns):
        def every():
            for fn in fns:
                fn()
        return every
    return tuple(run(fns) for fns in zip(*phase_lists))


def _exchange_phases(x_ref, out_ref, send_sems, recv_sems, local_sem):
    mx, my, mc = lax.axis_index("x"), lax.axis_index("y"), lax.axis_index("c")
    me = 4 * mx + 2 * my + mc
    mine = pltpu.make_async_copy(x_ref.at[me], out_ref.at[me], local_sem)
    copies = []
    for kk in range(1, N_DEV):
        px, py, pc = mx ^ (kk >> 2), my ^ ((kk >> 1) & 1), mc ^ (kk & 1)
        peer = 4 * px + 2 * py + pc
        copies.append((peer, pltpu.make_async_remote_copy(
            src_ref=x_ref.at[peer], dst_ref=out_ref.at[me], send_sem=send_sems.at[kk - 1],
            recv_sem=recv_sems.at[kk - 1], device_id=(px, py, pc), device_id_type=MESH)))

    def start():
        mine.start()
        for _, cp in copies:
            cp.start()

    def finish():
        for kk, (peer, _) in enumerate(copies):
            pltpu.make_async_remote_copy(
                src_ref=x_ref.at[peer], dst_ref=out_ref.at[peer], send_sem=send_sems.at[kk],
                recv_sem=recv_sems.at[kk], device_id=(mx, my, mc), device_id_type=MESH).wait_recv()
        for _, cp in copies:
            cp.wait_send()
        mine.wait()

    return start, finish


def _shards_last(a):
    lead = a.shape[:-1]
    return jnp.moveaxis(a.reshape(*lead, N_DEV, a.shape[-1] // N_DEV), -2, 0)


def _shards_rows(a):
    l, n, c = a.shape
    return jnp.moveaxis(a.reshape(l, N_DEV, n // N_DEV, c), 1, 0)


def _flat_pad(parts, mult):
    flat = jnp.concatenate([p.reshape(-1) for p in parts])
    return jnp.pad(flat, (0, (-flat.shape[0]) % mult))


def _unflatten(flat, shapes):
    out, o = [], 0
    for shp in shapes:
        n = 1
        for s in shp:
            n *= s
        out.append(flat[o:o + n].reshape(shp))
        o += n
    return out


def kernel(x, meta_tokens, g_mix, w_in, w_conv, w_pool, pool_scale, w_out, g_mlp, w_up, w_down, g_final, loss_target, m_meta_tokens, m_g_mix, m_w_in, m_w_conv, m_w_pool, m_pool_scale, m_w_out, m_g_mlp, m_w_up, m_w_down, m_g_final, v_meta_tokens, v_g_mix, v_w_in, v_w_conv, v_w_pool, v_pool_scale, v_w_out, v_g_mlp, v_w_up, v_w_down, v_g_final):
    nb, seq, d = x.shape
    depth = g_mix.shape[0]
    lp = BLK + seq
    t = nb * lp
    cw = w_conv.shape[2] * N_DEV
    pw = pool_scale.shape[1]
    pg = pw // N_POOL_GROUPS
    aw = (w_in.shape[2] * N_DEV - 3 * cw - pw) // 3
    ua_w = 3 * cw + pw
    d_ff = w_up.shape[2] * N_DEV
    me = 4 * lax.axis_index("x") + 2 * lax.axis_index("y") + lax.axis_index("c")

    def rows(a):
        return a.reshape(-1, a.shape[-1])

    def gathered_w_in(g, layers):
        return jnp.moveaxis(g.reshape(N_DEV, layers, d, -1), 0, 2).reshape(layers, d, -1)

    g_in0, small_in = _all_gather([rows(w_in[:1]).astype(BF16), _flat_pad([meta_tokens, w_conv], 8 * BLK).reshape(-1, BLK)],
                                  "gather_first_weights")
    w_in0_f = gathered_w_in(g_in0, 1)
    small_in = small_in.reshape(N_DEV, -1)
    meta_f = jnp.moveaxis(small_in[:, :meta_tokens.size].reshape(N_DEV, N_META, d // N_DEV), 0, 1).reshape(N_META, d)
    w_conv_f = small_in[:, meta_tokens.size:meta_tokens.size + w_conv.size].reshape((N_DEV,) + w_conv.shape)
    w_conv_f = jnp.moveaxis(w_conv_f, 0, 2).reshape(depth, w_conv.shape[1], cw)

    def split_w_in(w):
        return w[:, :ua_w], w[:, ua_w:ua_w + aw], w[:, ua_w + aw:ua_w + 2 * aw], w[:, ua_w + 2 * aw:]

    wbd = jnp.zeros((depth, pw, pw), F32)
    for gi in range(N_POOL_GROUPS):
        wbd = wbd.at[:, gi * pg:(gi + 1) * pg, gi * pg:(gi + 1) * pg].set(w_pool[:, gi])
    wbd = wbd.astype(BF16)

    h = jnp.concatenate([jnp.zeros((nb, PAD, d), F32), jnp.broadcast_to(meta_f[None], (nb, N_META, d)), x], axis=1).reshape(t, d)
    saved, w_in_parts, w_in_full = [], [None] * depth, [None] * depth
    w_in_full[0] = w_in0_f[0]
    w_in_parts[0] = split_w_in(w_in0_f[0])
    for l in range(depth):
        wa, wq, wk, wv = w_in_parts[l]
        ua, q, k, v, hn1 = _inproj_fwd(h, g_mix[l][None], wa, wq, wk, wv)
        ycp = _mixer_fwd(ua, w_conv_f[l], wbd[l], pool_scale[l][None], nb)
        if l == 0:
            ycat, g_in, g_out, g_up, g_down = _attn_fwd(
                q, k, v, ycp, nb, gather=[rows(a).astype(BF16) for a in (w_in[1:], w_out, w_up, w_down)])
            w_in_rest = gathered_w_in(g_in, depth - 1)
            for ll in range(1, depth):
                w_in_full[ll] = w_in_rest[ll - 1]
                w_in_parts[ll] = split_w_in(w_in_rest[ll - 1])
            w_out_f = jnp.moveaxis(g_out.reshape((N_DEV,) + w_out.shape), 0, 1).reshape(depth, -1, d)
            wo_a, wo_b = w_out_f[:, :cw + pw], w_out_f[:, cw + pw:]
            w_up_f = g_up.reshape((N_DEV,) + w_up.shape)
            w_down_f = g_down.reshape((N_DEV,) + w_down.shape)
        else:
            ycat = _attn_fwd(q, k, v, ycp, nb)
        h_mid = _outproj(h, ycat, w_out_f[l])
        h_next, hn2 = _mlp_fwd(h_mid, g_mlp[l][None], w_up_f, w_down_f, l)
        saved.append((h, ua, q, k, v, hn1, ycat, h_mid, hn2))
        h = h_next

    loss_part, dh, dg_final = _loss_head(h, g_final[None], loss_target.reshape(nb * seq, d), nb)

    g_gmix, g_gmlp, g_wconv, g_pscale, g_wpool = [None] * depth, [None] * depth, [None] * depth, [None] * depth, [None] * depth
    g_win, g_wout, g_wup, g_wdown = [None] * depth, [None] * depth, [None] * depth, [None] * depth
    for l in reversed(range(depth)):
        h_in, ua, q, k, v, hn1, ycat, h_mid, hn2 = saved[l]
        wa, wqkv = w_in_parts[l][0], w_in_full[l][:, ua_w:]
        dh_mid, act, dm, g_gmlp[l] = _mlp_bwd(dh, h_mid, hn2, g_mlp[l][None], w_up_f, w_down_f, l)
        g_wup[l] = _matmul_tn(hn2, dm, "grad_w_up")
        g_wdown[l] = _matmul_tn(act, dh, "grad_w_down")
        dycp, do = _outproj_bwd(dh_mid, wo_a[l], wo_b[l])
        g_wout[l] = _matmul_tn(ycat, dh_mid, "grad_w_out")
        if l == 0:
            send = [_shards_last(jnp.stack(g_win[1:])), _shards_rows(jnp.stack(g_wout)),
                    _shards_last(jnp.stack(g_wup)), _shards_rows(jnp.stack(g_wdown))]
            dqkv, p_in, p_out, p_up, p_down = _attn_bwd(
                q, k, v, do, nb, exchange=[s.reshape(N_DEV, -1, s.shape[-1]) for s in send])
        else:
            dqkv = _attn_bwd(q, k, v, do, nb)
        dua, g_wconv[l], g_pscale[l], dwb = _mixer_bwd(ua, dycp, w_conv_f[l], wbd[l], pool_scale[l][None], nb)
        g_wpool[l] = jnp.stack([dwb[gi * pg:(gi + 1) * pg, gi * pg:(gi + 1) * pg] for gi in range(N_POOL_GROUPS)])
        g_win[l] = jnp.concatenate([_matmul_tn(hn1, dua, "grad_w_in_a"), _matmul_tn(hn1, dqkv, "grad_w_in_qkv")], axis=1)
        if l == 0:
            dh, g_gmix[l], p_in0 = _inproj_bwd(dh_mid, h_in, g_mix[l][None], dua, dqkv, wa, wqkv,
                                               exchange=_shards_last(g_win[0][None]).reshape(N_DEV, -1, w_in.shape[-1]))
        else:
            dh, g_gmix[l] = _inproj_bwd(dh_mid, h_in, g_mix[l][None], dua, dqkv, wa, wqkv)
    dh3 = dh.reshape(nb, lp, d)
    grad_x = dh3[:, BLK:]
    g_meta = _sum_parts(dh3[:, PAD:BLK], "sum_meta_grad")

    o_out = _adamw(rows(w_out), p_out, rows(m_w_out), rows(v_w_out), "adamw_w_out")
    o_up = _adamw(rows(w_up), p_up, rows(m_w_up), rows(v_w_up), "adamw_w_up")
    o_down = _adamw(rows(w_down), p_down, rows(m_w_down), rows(v_w_down), "adamw_w_down")
    o_in1 = _adamw(rows(w_in[1:]), p_in, rows(m_w_in[1:]), rows(v_w_in[1:]), "adamw_w_in_rest")
    o_in0 = _adamw(rows(w_in[:1]), p_in0, rows(m_w_in[:1]), rows(v_w_in[:1]), "adamw_w_in0")
    (gr_win, gr_wout, gr_wup, gr_wdown), (de_win, de_wout, de_wup, de_wdown), (nm_win, nm_wout, nm_wup, nm_wdown), \
        (nv_win, nv_wout, nv_wup, nv_wdown) = [
            [jnp.concatenate([i0, i1], axis=0).reshape(w_in.shape), o.reshape(w_out.shape), u.reshape(w_up.shape), dn.reshape(w_down.shape)]
            for i0, i1, o, u, dn in zip(o_in0, o_in1, o_out, o_up, o_down)]

    small_full = [g_meta, jnp.concatenate(g_gmix), jnp.stack(g_wconv), jnp.stack(g_wpool), jnp.concatenate(g_pscale),
                  jnp.concatenate(g_gmlp), dg_final.reshape(-1)]
    small_shapes = [a.shape for a in small_full]
    small_sum = _sum_parts(_all_gather([_flat_pad(small_full, 8 * BLK).reshape(-1, BLK)], "gather_small_grads")[0], "sum_small_grads")
    gr_meta, gr_gmix, gr_wconv, gr_wpool, gr_pscale, gr_gmlp, gr_gfinal = _unflatten(small_sum.reshape(-1), small_shapes)
    gr_meta = lax.dynamic_slice_in_dim(gr_meta, me * (d // N_DEV), d // N_DEV, axis=1)
    gr_wconv = lax.dynamic_slice_in_dim(gr_wconv, me * (cw // N_DEV), cw // N_DEV, axis=2)
    small_g = [gr_meta, gr_gmix, gr_wconv, gr_wpool, gr_pscale, gr_gmlp, gr_gfinal]
    local_shapes = [a.shape for a in small_g]
    pack_small = lambda parts_: _flat_pad(parts_, 8 * BLK).reshape(-1, BLK)
    small = _adamw(pack_small([meta_tokens, g_mix, w_conv, w_pool, pool_scale, g_mlp, g_final]), pack_small(small_g)[None],
                   pack_small([m_meta_tokens, m_g_mix, m_w_conv, m_w_pool, m_pool_scale, m_g_mlp, m_g_final]),
                   pack_small([v_meta_tokens, v_g_mix, v_w_conv, v_w_pool, v_pool_scale, v_g_mlp, v_g_final]), "adamw_small")
    _, de_s, nm_s, nv_s = [_unflatten(b.reshape(-1), local_shapes) for b in small]

    loss = lax.psum(loss_part[0, 0], ("x", "y", "c"))

    def ordered(meta, gmix, wconv, wpool, pscale, gmlp, gfinal, win, wout, wup, wdown):
        return [meta, gmix, win, wconv, wpool, pscale, wout, gmlp, wup, wdown, gfinal]

    grads = ordered(*small_g, gr_win, gr_wout, gr_wup, gr_wdown)
    deltas = ordered(*de_s, de_win, de_wout, de_wup, de_wdown)
    new_m = ordered(*nm_s, nm_win, nm_wout, nm_wup, nm_wdown)
    new_v = ordered(*nv_s, nv_win, nv_wout, nv_wup, nv_wdown)
    return (loss, grad_x, *grads, *deltas, *new_m, *new_v)
```

```python
import jax
import jax.numpy as jnp
from jax import lax
from jax.experimental import pallas as pl
from jax.experimental.pallas import tpu as pltpu

F32 = jnp.float32
BF16 = jnp.bfloat16

N_DEV = 8
N_META = 16
HEAD_DIM = 64
BLK = 128
PAD = BLK - N_META
N_POOL_GROUPS = 4
EPS = 1e-6

ADAM_LR = 0.001
ADAM_B1 = 0.9
ADAM_B2 = 0.999
ADAM_EPS = 1e-08
ADAM_WD = 0.01
ADAM_STEP = 10

V7X_VMEM_LIMIT = 56 * 1024 * 1024


def _params(*sem):
    return pltpu.CompilerParams(dimension_semantics=sem, vmem_limit_bytes=V7X_VMEM_LIMIT)


SUBLANES = 8
ROW_TILE = 512
ROW_TILE_MLP = 1152
ROW_TILE_WIDE = 1152
TN_ACC_BYTES = 16 * 1024 * 1024


def _row_tile(t, cap=ROW_TILE):
    for tm in range(min(cap, t) // SUBLANES * SUBLANES, 0, -SUBLANES):
        if t % tm == 0:
            return tm
    raise ValueError(f"row count {t} is not a multiple of 8")


def _full(a):
    nd = a.ndim
    return pl.BlockSpec(a.shape, lambda *_: (0,) * nd)


def _dot(a, b):
    return jnp.dot(a, b, preferred_element_type=F32)


def _dot_nt(a, b):
    return lax.dot_general(a, b, (((1,), (1,)), ((), ())), preferred_element_type=F32)


def _dot_tn(a, b):
    return lax.dot_general(a, b, (((0,), (0,)), ((), ())), preferred_element_type=F32)


def _rms_fwd(x, g):
    rstd = lax.rsqrt(jnp.mean(x * x, axis=-1, keepdims=True) + EPS)
    return x * rstd * g


def _rms_bwd(dy, x, g):
    rstd = lax.rsqrt(jnp.mean(x * x, axis=-1, keepdims=True) + EPS)
    xhat = x * rstd
    dxhat = dy * g
    dx = rstd * (dxhat - xhat * jnp.mean(dxhat * xhat, axis=-1, keepdims=True))
    return dx, jnp.sum(dy * xhat, axis=0, keepdims=True)


def _inproj_fwd(h, g, wa, wq, wk, wv):
    t, d = h.shape
    tm = _row_tile(t, ROW_TILE_WIDE)
    ua_w, aw = wa.shape[1], wq.shape[1]

    def body(h_ref, g_ref, wa_ref, wq_ref, wk_ref, wv_ref, ua_ref, q_ref, k_ref, v_ref, hn_ref):
        hn = _rms_fwd(h_ref[...], g_ref[...]).astype(BF16)
        hn_ref[...] = hn
        ua_ref[...] = _dot(hn, wa_ref[...])
        q_ref[...] = _dot(hn, wq_ref[...]).astype(BF16)
        k_ref[...] = _dot(hn, wk_ref[...]).astype(BF16)
        v_ref[...] = _dot(hn, wv_ref[...]).astype(BF16)

    row = lambda w: pl.BlockSpec((tm, w), lambda i: (i, 0))
    return pl.pallas_call(
        body, name="inproj_fwd", grid=(t // tm,),
        in_specs=[row(d), _full(g), _full(wa), _full(wq), _full(wk), _full(wv)],
        out_specs=[row(ua_w), row(aw), row(aw), row(aw), row(d)],
        out_shape=[jax.ShapeDtypeStruct((t, ua_w), F32)] + [jax.ShapeDtypeStruct((t, aw), BF16)] * 3
        + [jax.ShapeDtypeStruct((t, d), BF16)],
        compiler_params=_params("arbitrary"),
    )(h, g, wa, wq, wk, wv)


def _pool_geometry(lp, pw):
    pg = pw // N_POOL_GROUPS
    row = lax.broadcasted_iota(jnp.int32, (lp, pw), 0)
    lane = lax.broadcasted_iota(jnp.int32, (lp, pw), 1)
    grp = [(lane >= g * pg) & (lane < (g + 1) * pg) for g in range(N_POOL_GROUPS)]
    wlen = jnp.where(grp[0], 2, jnp.where(grp[1], 4, jnp.where(grp[2], 8, 16)))
    cnt = jnp.clip(row - (PAD - 1), 1, wlen).astype(F32)
    return grp, cnt, row >= PAD


def _by_group(grp, vals):
    return jnp.where(grp[0], vals[0], jnp.where(grp[1], vals[1], jnp.where(grp[2], vals[2], vals[3])))


def _pooled(p, grp, cnt, real):
    s2 = p + pltpu.roll(p, 1, 0)
    s4 = s2 + pltpu.roll(s2, 2, 0)
    s8 = s4 + pltpu.roll(s4, 4, 0)
    s16 = s8 + pltpu.roll(s8, 8, 0)
    return jnp.where(real, _by_group(grp, (s2, s4, s8, s16)) / cnt - p, 0.0)


def _conv(uu, wc_ref):
    return wc_ref[2:3, :] * uu + wc_ref[1:2, :] * pltpu.roll(uu, 1, 0) + wc_ref[0:1, :] * pltpu.roll(uu, 2, 0)


def _mixer_fwd(ua, wconv, wbd, pscale, nb):
    t = ua.shape[0]
    lp = t // nb
    cw, pw = wconv.shape[1], wbd.shape[0]

    def body(ua_ref, wc_ref, wbd_ref, ps_ref, y_ref):
        cb = ua_ref[:, 0:cw]
        uu = ua_ref[:, cw:2 * cw] * ua_ref[:, 2 * cw:3 * cw]
        y_ref[:, 0:cw] = (cb * _conv(uu, wc_ref)).astype(BF16)
        grp, cnt, real = _pool_geometry(lp, pw)
        pooled = _pooled(ua_ref[:, 3 * cw:3 * cw + pw], grp, cnt, real)
        y_ref[:, cw:cw + pw] = (_dot(pooled.astype(BF16), wbd_ref[...]) * ps_ref[...]).astype(BF16)

    return pl.pallas_call(
        body, name="mixer_fwd", grid=(nb,),
        in_specs=[pl.BlockSpec((lp, ua.shape[1]), lambda s: (s, 0)), _full(wconv), _full(wbd), _full(pscale)],
        out_specs=pl.BlockSpec((lp, cw + pw), lambda s: (s, 0)),
        out_shape=jax.ShapeDtypeStruct((t, cw + pw), BF16),
        compiler_params=_params("arbitrary"),
    )(ua, wconv, wbd, pscale)


ATT_NW = 2
ATT_W = ATT_NW * BLK
ATT_PAIRS_FWD = 4
ATT_PAIRS_BWD = 4
HW = 2 * HEAD_DIM
ATT_SCALE = HEAD_DIM ** -0.5


def _attn_lk(z, valid):
    nz = -z
    return jnp.where(valid, jnp.minimum(nz, 0.0) - jnp.log(1.0 + jnp.exp(jnp.minimum(z, nz))), 0.0)


def _tri_sum_bf16(x, tri):
    return _dot(x.astype(BF16), tri)


def _tri_sum(x, tri):
    hi = x.astype(BF16)
    lo = (x - hi.astype(F32)).astype(BF16)
    return _dot(hi, tri) + _dot(lo, tri)


def _attn_weights(z, suffix, carry, valid):
    return jnp.where(valid, jnp.exp(z + suffix + carry), 0.0)


def _attn_consts():
    r = lax.broadcasted_iota(jnp.int32, (2 * BLK, ATT_W), 0)
    c = lax.broadcasted_iota(jnp.int32, (2 * BLK, ATT_W), 1)
    cmr = c - (r & (BLK - 1))
    kr = lax.broadcasted_iota(jnp.int32, (ATT_W, ATT_W), 0)
    kc = lax.broadcasted_iota(jnp.int32, (ATT_W, ATT_W), 1)
    lane = lax.broadcasted_iota(jnp.int32, (1, HW), 1)
    heads = (lane < HEAD_DIM, lane >= HEAD_DIM)
    return c, cmr, kr, kc, heads


def _tri(cond):
    return jnp.where(cond, 1.0, 0.0).astype(BF16)


def _rows(i):
    return pl.ds(pl.multiple_of(i * BLK, BLK), BLK)


def _stack_heads(x, heads):
    zero = jnp.zeros_like(x)
    return jnp.concatenate([jnp.where(heads[0], x, zero), jnp.where(heads[1], x, zero)], axis=0)


def _unstack_heads(x, heads):
    return jnp.where(heads[0], x[:BLK], x[BLK:])


def _window(i, s):
    start = jnp.maximum(i + 1 - ATT_NW * (s + 1), 0) * BLK
    return start, (i + 1 - ATT_NW * s) * BLK


def _attn_fwd(q, k, v, ycp, nb, gather=None):
    t, aw = q.shape
    cpw = ycp.shape[1]
    lp = t // nb
    nblk = lp // BLK
    pp = min(ATT_PAIRS_FWD, aw // HW)
    cols = [slice(p * HW, (p + 1) * HW) for p in range(pp)]
    ncol = aw // (pp * HW)
    assert ncol == 1, "the [y_conv | y_pool | y_attn] output block needs all head pairs in one grid step"
    nsteps = nb * ncol

    def body(q_ref, k_ref, v_ref, ycp_ref, *rest):
        if gather is None:
            (o_ref,) = rest
        else:
            ng = len(gather)
            x_refs, o_ref, g_refs, sems = rest[:ng], rest[ng], rest[ng + 1:2 * ng + 1], rest[2 * ng + 1:]
            step = pl.program_id(0) * ncol + pl.program_id(1)
            start, forward, finish = _all_of([_gather_phases(x_refs[j], g_refs[j], *sems[3 * j:3 * j + 3]) for j in range(ng)])
            pl.when(step == 0)(start)
            pl.when(step == nsteps - 1)(forward)
        c, cmr, kr, kc, heads = _attn_consts()
        m_from = _tri(kr >= kc)
        o_ref[:, 0:cpw] = ycp_ref[...]

        def qblock(i, _):
            qs = [_stack_heads(q_ref[_rows(i), cs] * ATT_SCALE, heads) for cs in cols]

            def window(s, st):
                accs, carries = st
                start, end = _window(i, s)
                keys = pl.ds(pl.multiple_of(start, BLK), ATT_W)
                valid = (cmr < i * BLK - start) & (c < end - start)
                zs = [_dot_nt(qs[p], k_ref[keys, cs]) for p, cs in enumerate(cols)]
                suffix = [_tri_sum_bf16(_attn_lk(zs[p], valid), m_from) for p in range(pp)]
                new_accs, new_carries = [], []
                for p, cs in enumerate(cols):
                    a = _attn_weights(zs[p], suffix[p], carries[p], valid)
                    new_accs.append(accs[p] + _dot(a.astype(BF16), v_ref[keys, cs]))
                    new_carries.append(carries[p] + suffix[p][:, 0:1])
                return tuple(new_accs), tuple(new_carries)

            init = (tuple(jnp.zeros((2 * BLK, HW), F32) for _ in cols), tuple(jnp.zeros((2 * BLK, 1), F32) for _ in cols))
            accs, _ = lax.fori_loop(0, (i + ATT_NW) // ATT_NW, window, init)
            for p in range(pp):
                o_ref[_rows(i), cpw + p * HW:cpw + (p + 1) * HW] = _unstack_heads(accs[p], heads).astype(BF16)
            return 0

        lax.fori_loop(0, nblk, qblock, 0)
        if gather is not None:
            pl.when(step == nsteps - 1)(finish)

    spec = pl.BlockSpec((lp, pp * HW), lambda s, p: (s, p))
    in_specs = [spec, spec, spec, pl.BlockSpec((lp, cpw), lambda s, p: (s, 0))]
    y_spec = pl.BlockSpec((lp, cpw + aw), lambda s, p: (s, 0))
    y_shape = jax.ShapeDtypeStruct((t, cpw + aw), BF16)
    if gather is None:
        return pl.pallas_call(
            body, name="attn_fwd", grid=(nb, ncol), in_specs=in_specs, out_specs=y_spec, out_shape=y_shape,
            compiler_params=_params("arbitrary", "arbitrary"),
        )(q, k, v, ycp)
    return pl.pallas_call(
        body, name="attn_fwd_gather", grid=(nb, ncol), in_specs=in_specs + [ANY] * len(gather),
        out_specs=[y_spec] + [ANY] * len(gather),
        out_shape=[y_shape] + [jax.ShapeDtypeStruct((N_DEV,) + x.shape, x.dtype) for x in gather],
        scratch_shapes=COMM_SEMS * len(gather), compiler_params=_params("arbitrary", "arbitrary"),
    )(q, k, v, ycp, *gather)


def _mlp_weight_specs(wup, wdown, layer):
    nff, _, d, fc = wup.shape
    return nff, fc, [pl.BlockSpec((None, None, d, fc), lambda i, kk: (kk, layer, 0, 0)),
                     pl.BlockSpec((None, None, fc, d), lambda i, kk: (kk, layer, 0, 0))]


def _mlp_fwd(h, y, wout, g, wup, wdown, layer):
    t, d = h.shape
    tm = _row_tile(t, ROW_TILE_MLP)
    nff, fc, wspecs = _mlp_weight_specs(wup, wdown, layer)

    def body(h_ref, y_ref, wo_ref, g_ref, wu_ref, wd_ref, o_ref, hn_ref, hm_ref, acc):
        kk = pl.program_id(1)

        @pl.when(kk == 0)
        def _():
            x = h_ref[...] + _dot(y_ref[...], wo_ref[...])
            hm_ref[...] = x
            hn_ref[...] = _rms_fwd(x, g_ref[...]).astype(BF16)
            acc[...] = x

        m = _dot(hn_ref[...], wu_ref[...])
        a = jnp.square(jnp.maximum(m, 0.0)).astype(BF16)
        acc[...] += _dot(a, wd_ref[...])

        @pl.when(kk == nff - 1)
        def _():
            o_ref[...] = acc[...]

    row = pl.BlockSpec((tm, d), lambda i, kk: (i, 0))
    return pl.pallas_call(
        body, name="outproj_mlp_fwd", grid=(t // tm, nff),
        in_specs=[row, pl.BlockSpec((tm, y.shape[1]), lambda i, kk: (i, 0)), _full(wout), _full(g)] + wspecs,
        out_specs=[row, row, row],
        out_shape=[jax.ShapeDtypeStruct((t, d), F32), jax.ShapeDtypeStruct((t, d), BF16), jax.ShapeDtypeStruct((t, d), F32)],
        scratch_shapes=[pltpu.VMEM((tm, d), F32)],
        compiler_params=_params("arbitrary", "arbitrary"),
    )(h, y, wout, g, wup, wdown)


def _loss_head(h, g, target, nb):
    t, d = h.shape
    nblk = t // nb // BLK
    nx = nblk - 1

    def body(h_ref, g_ref, t_ref, loss_ref, dh_ref, dg_ref):
        s, i = pl.program_id(0), pl.program_id(1)

        @pl.when((s == 0) & (i == 0))
        def _():
            loss_ref[...] = jnp.zeros_like(loss_ref)
            dg_ref[...] = jnp.zeros_like(dg_ref)

        @pl.when(i == 0)
        def _():
            dh_ref[...] = jnp.zeros_like(dh_ref)

        @pl.when(i > 0)
        def _():
            x, gg = h_ref[...], g_ref[...]
            err = _rms_fwd(x, gg) - t_ref[...]
            loss_ref[...] += jnp.sum(err * err) * (0.5 / d)
            dx, dg = _rms_bwd(err * (1.0 / d), x, gg)
            dh_ref[...] = dx
            dg_ref[...] += dg

    hspec = pl.BlockSpec((BLK, d), lambda s, i: (s * nblk + i, 0))
    return pl.pallas_call(
        body, name="loss_head", grid=(nb, nblk),
        in_specs=[hspec, _full(g), pl.BlockSpec((BLK, d), lambda s, i: (s * nx + jnp.maximum(i - 1, 0), 0))],
        out_specs=[pl.BlockSpec((1, BLK), lambda s, i: (0, 0)), hspec, pl.BlockSpec((1, d), lambda s, i: (0, 0))],
        out_shape=[jax.ShapeDtypeStruct((1, BLK), F32), jax.ShapeDtypeStruct((t, d), F32), jax.ShapeDtypeStruct((1, d), F32)],
        compiler_params=_params("arbitrary", "arbitrary"),
    )(h, g, target)


def _mlp_bwd(dh, h, hn, g, wup, wdown, layer):
    t, d = h.shape
    tm = _row_tile(t, ROW_TILE_MLP)
    nff, fc, wspecs = _mlp_weight_specs(wup, wdown, layer)

    def body(dh_ref, h_ref, hn_ref, g_ref, wu_ref, wd_ref, dhm_ref, a_ref, dm_ref, dg_ref, dhn, dhb):
        i, kk = pl.program_id(0), pl.program_id(1)

        @pl.when((i == 0) & (kk == 0))
        def _():
            dg_ref[...] = jnp.zeros_like(dg_ref)

        @pl.when(kk == 0)
        def _():
            dhb[...] = dh_ref[...].astype(BF16)
            dhn[...] = jnp.zeros_like(dhn)

        r = jnp.maximum(_dot(hn_ref[...], wu_ref[...]), 0.0)
        a_ref[...] = (r * r).astype(BF16)
        dm = (_dot_nt(dhb[...], wd_ref[...]) * (2.0 * r)).astype(BF16)
        dm_ref[...] = dm
        dhn[...] += _dot_nt(dm, wu_ref[...])

        @pl.when(kk == nff - 1)
        def _():
            dx, dg = _rms_bwd(dhn[...], h_ref[...], g_ref[...])
            dhm_ref[...] = dh_ref[...] + dx
            dg_ref[...] += dg

    row = pl.BlockSpec((tm, d), lambda i, kk: (i, 0))
    ff = pl.BlockSpec((tm, fc), lambda i, kk: (i, kk))
    return pl.pallas_call(
        body, name="mlp_bwd", grid=(t // tm, nff),
        in_specs=[row, row, row, _full(g)] + wspecs,
        out_specs=[row, ff, ff, pl.BlockSpec((1, d), lambda i, kk: (0, 0))],
        out_shape=[jax.ShapeDtypeStruct((t, d), F32), jax.ShapeDtypeStruct((t, nff * fc), BF16),
                   jax.ShapeDtypeStruct((t, nff * fc), BF16), jax.ShapeDtypeStruct((1, d), F32)],
        scratch_shapes=[pltpu.VMEM((tm, d), F32), pltpu.VMEM((tm, d), BF16)],
        compiler_params=_params("arbitrary", "arbitrary"),
    )(dh, h, hn, g, wup, wdown)


def _matmul_tn(x, y, name):
    t, k1 = x.shape
    n = y.shape[1]
    tt = _row_tile(t)
    tn = min(n, max(512, TN_ACC_BYTES // (4 * k1)))
    steps = t // tt

    def body(x_ref, y_ref, o_ref, acc):
        @pl.when(pl.program_id(1) == 0)
        def _():
            acc[...] = jnp.zeros_like(acc)

        acc[...] += _dot_tn(x_ref[...].astype(BF16), y_ref[...].astype(BF16))

        @pl.when(pl.program_id(1) == steps - 1)
        def _():
            o_ref[...] = acc[...].astype(BF16)

    return pl.pallas_call(
        body, name=name, grid=(n // tn, steps),
        in_specs=[pl.BlockSpec((tt, k1), lambda j, r: (r, 0)), pl.BlockSpec((tt, tn), lambda j, r: (r, j))],
        out_specs=pl.BlockSpec((k1, tn), lambda j, r: (0, j)),
        out_shape=jax.ShapeDtypeStruct((k1, n), BF16),
        scratch_shapes=[pltpu.VMEM((k1, tn), F32)],
        compiler_params=_params("arbitrary", "arbitrary"),
    )(x, y)


def _outproj_bwd(dh, wa, wb):
    t, d = dh.shape
    tm = _row_tile(t, ROW_TILE_WIDE)
    na, nbw = wa.shape[0], wb.shape[0]

    def body(dh_ref, wa_ref, wb_ref, da_ref, db_ref):
        x = dh_ref[...].astype(BF16)
        da_ref[...] = _dot_nt(x, wa_ref[...])
        db_ref[...] = _dot_nt(x, wb_ref[...]).astype(BF16)

    row = lambda w: pl.BlockSpec((tm, w), lambda i: (i, 0))
    return pl.pallas_call(
        body, name="outproj_bwd", grid=(t // tm,),
        in_specs=[row(d), _full(wa), _full(wb)], out_specs=[row(na), row(nbw)],
        out_shape=[jax.ShapeDtypeStruct((t, na), F32), jax.ShapeDtypeStruct((t, nbw), BF16)],
        compiler_params=_params("arbitrary"),
    )(dh, wa, wb)


def _attn_bwd(q, k, v, do, nb, exchange=None):
    t, aw = q.shape
    lp = t // nb
    nblk = lp // BLK
    pp = min(ATT_PAIRS_BWD, aw // HW)
    cols = [slice(p * HW, (p + 1) * HW) for p in range(pp)]
    ncol = aw // (pp * HW)
    assert ncol == 1, "the [dq | dk | dv] output block needs all head pairs in one grid step"
    nsteps = nb * ncol
    max_windows = (nblk + ATT_NW - 1) // ATT_NW

    def body(q_ref, k_ref, v_ref, do_ref, *rest):
        if exchange is None:
            dqkv_ref, dk_acc, dv_acc, g_s, sig_s = rest
        else:
            ne = len(exchange)
            x_refs, dqkv_ref, e_refs = rest[:ne], rest[ne], rest[ne + 1:2 * ne + 1]
            dk_acc, dv_acc, g_s, sig_s = rest[2 * ne + 1:2 * ne + 5]
            sems = rest[2 * ne + 5:]
            step = pl.program_id(0) * ncol + pl.program_id(1)
            start, finish = _all_of([_exchange_phases(x_refs[j], e_refs[j], *sems[3 * j:3 * j + 3]) for j in range(ne)])
            pl.when(step == 0)(start)
        c, cmr, kr, kc, heads = _attn_consts()
        m_from = _tri(kr >= kc)
        m_before = _tri(kr < kc)
        dk_acc[...] = jnp.zeros_like(dk_acc)
        dv_acc[...] = jnp.zeros_like(dv_acc)

        def qblock(i, _):
            qs = [_stack_heads(q_ref[_rows(i), cs] * ATT_SCALE, heads) for cs in cols]
            dos = [_stack_heads(do_ref[_rows(i), cs], heads) for cs in cols]
            nwin = (i + ATT_NW) // ATT_NW

            def geometry(s):
                start, end = _window(i, s)
                return (cmr < i * BLK - start) & (c < end - start), pl.ds(pl.multiple_of(start, BLK), ATT_W)

            def down(s, carries):
                valid, keys = geometry(s)
                zs = [_dot_nt(qs[p], k_ref[keys, cs]) for p, cs in enumerate(cols)]
                das = [_dot_nt(dos[p], v_ref[keys, cs]) for p, cs in enumerate(cols)]
                suffix = []
                for p in range(pp):
                    lk = _attn_lk(zs[p], valid)
                    sig_s[p, s] = jnp.exp(zs[p] + lk)
                    suffix.append(_tri_sum_bf16(lk, m_from))
                new_carries = []
                for p, cs in enumerate(cols):
                    a = _attn_weights(zs[p], suffix[p], carries[p], valid)
                    g_s[p, s] = das[p] * a
                    dv_acc[keys, cs] += _dot_tn(a.astype(BF16), dos[p])
                    new_carries.append(carries[p] + suffix[p][:, 0:1])
                return tuple(new_carries)

            lax.fori_loop(0, nwin, down, tuple(jnp.zeros((2 * BLK, 1), F32) for _ in cols))

            def up(ss, st):
                accs, carries = st
                s = nwin - 1 - ss
                valid, keys = geometry(s)
                gs = [g_s[p, s] for p in range(pp)]
                before = [_tri_sum(gs[p], m_before) for p in range(pp)]
                new_accs, new_carries = [], []
                for p, cs in enumerate(cols):
                    g, sig = gs[p], sig_s[p, s]
                    dz = jnp.where(valid, g * (1.0 - sig) - (before[p] + carries[p]) * sig, 0.0).astype(BF16)
                    new_accs.append(accs[p] + _dot(dz, k_ref[keys, cs]))
                    dk_acc[keys, cs] += _dot_tn(dz, qs[p])
                    new_carries.append(carries[p] + jnp.sum(g, axis=1, keepdims=True))
                return tuple(new_accs), tuple(new_carries)

            init = (tuple(jnp.zeros((2 * BLK, HW), F32) for _ in cols), tuple(jnp.zeros((2 * BLK, 1), F32) for _ in cols))
            accs, _ = lax.fori_loop(0, nwin, up, init)
            for p, cs in enumerate(cols):
                dqkv_ref[_rows(i), cs] = (_unstack_heads(accs[p], heads) * ATT_SCALE).astype(BF16)
            return 0

        lax.fori_loop(0, nblk, qblock, 0)
        dqkv_ref[:, aw:2 * aw] = dk_acc[...].astype(BF16)
        dqkv_ref[:, 2 * aw:3 * aw] = dv_acc[...].astype(BF16)
        if exchange is not None:
            pl.when(step == nsteps - 1)(finish)

    spec = pl.BlockSpec((lp, pp * HW), lambda s, p: (s, p), pipeline_mode=pl.Buffered(1))
    d_spec = pl.BlockSpec((lp, 3 * aw), lambda s, p: (s, 0), pipeline_mode=pl.Buffered(1))
    d_shape = jax.ShapeDtypeStruct((t, 3 * aw), BF16)
    scratch = [pltpu.VMEM((lp, pp * HW), F32), pltpu.VMEM((lp, pp * HW), F32),
               pltpu.VMEM((pp, max_windows, 2 * BLK, ATT_W), F32), pltpu.VMEM((pp, max_windows, 2 * BLK, ATT_W), F32)]
    if exchange is None:
        return pl.pallas_call(
            body, name="attn_bwd", grid=(nb, ncol), in_specs=[spec] * 4, out_specs=d_spec, out_shape=d_shape,
            scratch_shapes=scratch, compiler_params=_params("arbitrary", "arbitrary"),
        )(q, k, v, do)
    return pl.pallas_call(
        body, name="attn_bwd_exchange", grid=(nb, ncol), in_specs=[spec] * 4 + [ANY] * len(exchange),
        out_specs=[d_spec] + [ANY] * len(exchange),
        out_shape=[d_shape] + [jax.ShapeDtypeStruct(x.shape, x.dtype) for x in exchange],
        scratch_shapes=scratch + COMM_SEMS * len(exchange), compiler_params=_params("arbitrary", "arbitrary"),
    )(q, k, v, do, *exchange)


def _mixer_bwd(ua, dy, wconv, wbd, pscale, nb):
    t = ua.shape[0]
    lp = t // nb
    cw, pw = wconv.shape[1], wbd.shape[0]

    def body(ua_ref, dy_ref, wc_ref, wbd_ref, ps_ref, du_ref, dwc_ref, dps_ref, dwb_ref):
        @pl.when(pl.program_id(0) == 0)
        def _():
            dwc_ref[...] = jnp.zeros_like(dwc_ref)
            dps_ref[...] = jnp.zeros_like(dps_ref)
            dwb_ref[...] = jnp.zeros_like(dwb_ref)

        up = lambda x, n: pltpu.roll(x, lp - n, 0)
        cb, cc, cx = ua_ref[:, 0:cw], ua_ref[:, cw:2 * cw], ua_ref[:, 2 * cw:3 * cw]
        uu = cc * cx
        dyc = dy_ref[:, 0:cw]
        du_ref[:, 0:cw] = (dyc * _conv(uu, wc_ref)).astype(BF16)
        dcv = dyc * cb
        duu = wc_ref[2:3, :] * dcv + wc_ref[1:2, :] * up(dcv, 1) + wc_ref[0:1, :] * up(dcv, 2)
        du_ref[:, cw:2 * cw] = (duu * cx).astype(BF16)
        du_ref[:, 2 * cw:3 * cw] = (duu * cc).astype(BF16)
        dwc_ref[0:1, :] += jnp.sum(dcv * pltpu.roll(uu, 2, 0), axis=0, keepdims=True)
        dwc_ref[1:2, :] += jnp.sum(dcv * pltpu.roll(uu, 1, 0), axis=0, keepdims=True)
        dwc_ref[2:3, :] += jnp.sum(dcv * uu, axis=0, keepdims=True)
        grp, cnt, real = _pool_geometry(lp, pw)
        p = ua_ref[:, 3 * cw:3 * cw + pw]
        pooled = _pooled(p, grp, cnt, real).astype(BF16)
        dyp = dy_ref[:, cw:cw + pw]
        dps_ref[...] += jnp.sum(dyp * _dot(pooled, wbd_ref[...]), axis=0, keepdims=True)
        dpre = (dyp * ps_ref[...]).astype(BF16)
        dwb_ref[...] += _dot_tn(pooled, dpre)
        dpooled = jnp.where(real, _dot_nt(dpre, wbd_ref[...]), 0.0)
        xm = dpooled / cnt
        l2 = xm + up(xm, 1)
        l4 = l2 + up(l2, 2)
        l8 = l4 + up(l4, 4)
        l16 = l8 + up(l8, 8)
        du_ref[:, 3 * cw:3 * cw + pw] = jnp.where(real, _by_group(grp, (l2, l4, l8, l16)) - dpooled, 0.0).astype(BF16)

    seq = lambda w: pl.BlockSpec((lp, w), lambda s: (s, 0))
    return pl.pallas_call(
        body, name="mixer_bwd", grid=(nb,),
        in_specs=[seq(ua.shape[1]), seq(cw + pw), _full(wconv), _full(wbd), _full(pscale)],
        out_specs=[seq(ua.shape[1]), pl.BlockSpec((3, cw), lambda s: (0, 0)), pl.BlockSpec((1, pw), lambda s: (0, 0)),
                   pl.BlockSpec((pw, pw), lambda s: (0, 0))],
        out_shape=[jax.ShapeDtypeStruct(ua.shape, BF16), jax.ShapeDtypeStruct((3, cw), F32),
                   jax.ShapeDtypeStruct((1, pw), F32), jax.ShapeDtypeStruct((pw, pw), F32)],
        compiler_params=_params("arbitrary"),
    )(ua, dy, wconv, wbd, pscale)


def _inproj_bwd(dh, h, g, dua, dqkv, wa, wqkv, exchange=None):
    t, d = h.shape
    tm = _row_tile(t)
    nsteps = t // tm

    def body(dh_ref, h_ref, g_ref, dua_ref, dqkv_ref, wa_ref, wqkv_ref, *rest):
        if exchange is None:
            o_ref, dg_ref = rest
        else:
            x_ref, o_ref, dg_ref, e_ref, send_sems, recv_sems, local_sem = rest
            start, finish = _exchange_phases(x_ref, e_ref, send_sems, recv_sems, local_sem)
            pl.when(pl.program_id(0) == 0)(start)

        @pl.when(pl.program_id(0) == 0)
        def _():
            dg_ref[...] = jnp.zeros_like(dg_ref)

        dhn = _dot_nt(dua_ref[...], wa_ref[...]) + _dot_nt(dqkv_ref[...], wqkv_ref[...])
        dx, dg = _rms_bwd(dhn, h_ref[...], g_ref[...])
        o_ref[...] = dh_ref[...] + dx
        dg_ref[...] += dg
        if exchange is not None:
            pl.when(pl.program_id(0) == nsteps - 1)(finish)

    row = lambda w: pl.BlockSpec((tm, w), lambda i: (i, 0))
    in_specs = [row(d), row(d), _full(g), row(dua.shape[1]), row(dqkv.shape[1]), _full(wa), _full(wqkv)]
    out_specs = [row(d), pl.BlockSpec((1, d), lambda i: (0, 0))]
    out_shape = [jax.ShapeDtypeStruct((t, d), F32), jax.ShapeDtypeStruct((1, d), F32)]
    if exchange is None:
        return pl.pallas_call(
            body, name="inproj_bwd", grid=(nsteps,), in_specs=in_specs, out_specs=out_specs, out_shape=out_shape,
            compiler_params=_params("arbitrary"),
        )(dh, h, g, dua, dqkv, wa, wqkv)
    return pl.pallas_call(
        body, name="inproj_bwd_exchange", grid=(nsteps,), in_specs=in_specs + [ANY], out_specs=out_specs + [ANY],
        out_shape=out_shape + [jax.ShapeDtypeStruct(exchange.shape, exchange.dtype)], scratch_shapes=COMM_SEMS,
        compiler_params=_params("arbitrary"),
    )(dh, h, g, dua, dqkv, wa, wqkv, exchange)


def _part_tile(r):
    if r <= 512:
        return r
    for tr in (256, 128, 64, 32, 16, 8):
        if r % tr == 0:
            return tr
    return r


def _sum_parts(x, name):
    n, r, c = x.shape
    tr = _part_tile(r)

    def body(x_ref, o_ref):
        acc = x_ref[0]
        for p in range(1, n):
            acc = acc + x_ref[p]
        o_ref[...] = acc

    return pl.pallas_call(
        body, name=name, grid=(r // tr,),
        in_specs=[pl.BlockSpec((n, tr, c), lambda i: (0, i, 0))], out_specs=pl.BlockSpec((tr, c), lambda i: (i, 0)),
        out_shape=jax.ShapeDtypeStruct((r, c), F32), compiler_params=_params("arbitrary"),
    )(x)


def _adamw(w, gparts, m, v, name):
    n, r, c = gparts.shape
    tr = _part_tile(r)

    def body(w_ref, g_ref, m_ref, v_ref, go_ref, d_ref, mo_ref, vo_ref):
        g = g_ref[0].astype(F32)
        for p in range(1, n):
            g = g + g_ref[p].astype(F32)
        go_ref[...] = g
        mm = ADAM_B1 * m_ref[...] + (1.0 - ADAM_B1) * g
        vv = ADAM_B2 * v_ref[...] + (1.0 - ADAM_B2) * jnp.square(g)
        mo_ref[...] = mm
        vo_ref[...] = vv
        m_hat = mm / (1.0 - ADAM_B1 ** ADAM_STEP)
        v_hat = vv / (1.0 - ADAM_B2 ** ADAM_STEP)
        d_ref[...] = -ADAM_LR * (m_hat / (jnp.sqrt(v_hat) + ADAM_EPS) + ADAM_WD * w_ref[...])

    row = pl.BlockSpec((tr, c), lambda i: (i, 0))
    return pl.pallas_call(
        body, name=name, grid=(r // tr,),
        in_specs=[row, pl.BlockSpec((n, tr, c), lambda i: (0, i, 0)), row, row], out_specs=[row] * 4,
        out_shape=[jax.ShapeDtypeStruct((r, c), F32)] * 4, compiler_params=_params("arbitrary"),
    )(w, gparts, m, v)


MESH = pl.DeviceIdType.MESH
ANY = pl.BlockSpec(memory_space=pl.ANY)


def _all_gather(xs, name):
    n = len(xs)

    def body(*refs):
        x_refs, out_refs, sems = refs[:n], refs[n:2 * n], refs[2 * n:]
        start, forward, finish = _all_of([_gather_phases(x_refs[j], out_refs[j], *sems[3 * j:3 * j + 3]) for j in range(n)])
        start()
        forward()
        finish()

    return pl.pallas_call(
        body, name=name, in_specs=[ANY] * n, out_specs=[ANY] * n,
        out_shape=[jax.ShapeDtypeStruct((N_DEV,) + x.shape, x.dtype) for x in xs], scratch_shapes=COMM_SEMS * n,
    )(*xs)


COMM_SEMS = [pltpu.SemaphoreType.DMA((7,)), pltpu.SemaphoreType.DMA((7,)), pltpu.SemaphoreType.DMA(())]


def _gather_phases(x_ref, out_ref, send_sems, recv_sems, local_sem):
    mx, my, mc = lax.axis_index("x"), lax.axis_index("y"), lax.axis_index("c")
    me, sibling = (mx, my, mc), (mx, my, 1 - mc)
    chips = [(1 - mx, my), (mx, 1 - my), (1 - mx, 1 - my)]

    def slot(px, py, pc):
        return out_ref.at[4 * px + 2 * py + pc]

    def copy(kk, block, to, src=None):
        return pltpu.make_async_remote_copy(
            src_ref=slot(*block) if src is None else src, dst_ref=slot(*block),
            send_sem=send_sems.at[kk], recv_sem=recv_sems.at[kk], device_id=to, device_id_type=MESH)

    mine = pltpu.make_async_copy(x_ref, slot(*me), local_sem)
    first = [copy(0, me, sibling, src=x_ref)] + [copy(1 + j, me, (*chip, mc), src=x_ref) for j, chip in enumerate(chips)]
    passed = [copy(4 + j, (*chip, mc), sibling) for j, chip in enumerate(chips)]

    def start():
        mine.start()
        for cp in first:
            cp.start()

    def forward():
        for j, chip in enumerate(chips):
            copy(1 + j, (*chip, mc), me).wait_recv()
            passed[j].start()

    def finish():
        copy(0, sibling, me).wait_recv()
        for j, chip in enumerate(chips):
            copy(4 + j, (*chip, 1 - mc), me).wait_recv()
        for cp in first + passed:
            cp.wait_send()
        mine.wait()

    return start, forward, finish


def _all_of(phase_lists):
    def run(fns):
        def every():
            for fn in fns:
                fn()
        return every
    return tuple(run(fns) for fns in zip(*phase_lists))


def _exchange_phases(x_ref, out_ref, send_sems, recv_sems, local_sem):
    mx, my, mc = lax.axis_index("x"), lax.axis_index("y"), lax.axis_index("c")
    me = 4 * mx + 2 * my + mc
    mine = pltpu.make_async_copy(x_ref.at[me], out_ref.at[me], local_sem)
    copies = []
    for kk in range(1, N_DEV):
        px, py, pc = mx ^ (kk >> 2), my ^ ((kk >> 1) & 1), mc ^ (kk & 1)
        peer = 4 * px + 2 * py + pc
        copies.append((peer, pltpu.make_async_remote_copy(
            src_ref=x_ref.at[peer], dst_ref=out_ref.at[me], send_sem=send_sems.at[kk - 1],
            recv_sem=recv_sems.at[kk - 1], device_id=(px, py, pc), device_id_type=MESH)))

    def start():
        mine.start()
        for _, cp in copies:
            cp.start()

    def finish():
        for kk, (peer, _) in enumerate(copies):
            pltpu.make_async_remote_copy(
                src_ref=x_ref.at[peer], dst_ref=out_ref.at[peer], send_sem=send_sems.at[kk],
                recv_sem=recv_sems.at[kk], device_id=(mx, my, mc), device_id_type=MESH).wait_recv()
        for _, cp in copies:
            cp.wait_send()
        mine.wait()

    return start, finish


def _shards_last(a):
    lead = a.shape[:-1]
    return jnp.moveaxis(a.reshape(*lead, N_DEV, a.shape[-1] // N_DEV), -2, 0)


def _shards_rows(a):
    l, n, c = a.shape
    return jnp.moveaxis(a.reshape(l, N_DEV, n // N_DEV, c), 1, 0)


def _flat_pad(parts, mult):
    flat = jnp.concatenate([p.reshape(-1) for p in parts])
    return jnp.pad(flat, (0, (-flat.shape[0]) % mult))


def _unflatten(flat, shapes):
    out, o = [], 0
    for shp in shapes:
        n = 1
        for s in shp:
            n *= s
        out.append(flat[o:o + n].reshape(shp))
        o += n
    return out


def kernel(x, meta_tokens, g_mix, w_in, w_conv, w_pool, pool_scale, w_out, g_mlp, w_up, w_down, g_final, loss_target, m_meta_tokens, m_g_mix, m_w_in, m_w_conv, m_w_pool, m_pool_scale, m_w_out, m_g_mlp, m_w_up, m_w_down, m_g_final, v_meta_tokens, v_g_mix, v_w_in, v_w_conv, v_w_pool, v_pool_scale, v_w_out, v_g_mlp, v_w_up, v_w_down, v_g_final):
    nb, seq, d = x.shape
    depth = g_mix.shape[0]
    lp = BLK + seq
    t = nb * lp
    cw = w_conv.shape[2] * N_DEV
    pw = pool_scale.shape[1]
    pg = pw // N_POOL_GROUPS
    aw = (w_in.shape[2] * N_DEV - 3 * cw - pw) // 3
    ua_w = 3 * cw + pw
    d_ff = w_up.shape[2] * N_DEV
    me = 4 * lax.axis_index("x") + 2 * lax.axis_index("y") + lax.axis_index("c")

    def rows(a):
        return a.reshape(-1, a.shape[-1])

    def gathered_w_in(g, layers):
        return jnp.moveaxis(g.reshape(N_DEV, layers, d, -1), 0, 2).reshape(layers, d, -1)

    g_in0, small_in = _all_gather([rows(w_in[:1]).astype(BF16), _flat_pad([meta_tokens, w_conv], 8 * BLK).reshape(-1, BLK)],
                                  "gather_first_weights")
    w_in0_f = gathered_w_in(g_in0, 1)
    small_in = small_in.reshape(N_DEV, -1)
    meta_f = jnp.moveaxis(small_in[:, :meta_tokens.size].reshape(N_DEV, N_META, d // N_DEV), 0, 1).reshape(N_META, d)
    w_conv_f = small_in[:, meta_tokens.size:meta_tokens.size + w_conv.size].reshape((N_DEV,) + w_conv.shape)
    w_conv_f = jnp.moveaxis(w_conv_f, 0, 2).reshape(depth, w_conv.shape[1], cw)

    def split_w_in(w):
        return w[:, :ua_w], w[:, ua_w:ua_w + aw], w[:, ua_w + aw:ua_w + 2 * aw], w[:, ua_w + 2 * aw:]

    wbd = jnp.zeros((depth, pw, pw), F32)
    for gi in range(N_POOL_GROUPS):
        wbd = wbd.at[:, gi * pg:(gi + 1) * pg, gi * pg:(gi + 1) * pg].set(w_pool[:, gi])
    wbd = wbd.astype(BF16)

    h = jnp.concatenate([jnp.zeros((nb, PAD, d), F32), jnp.broadcast_to(meta_f[None], (nb, N_META, d)), x], axis=1).reshape(t, d)
    saved, w_in_parts, w_in_full = [], [None] * depth, [None] * depth
    w_in_full[0] = w_in0_f[0]
    w_in_parts[0] = split_w_in(w_in0_f[0])
    for l in range(depth):
        wa, wq, wk, wv = w_in_parts[l]
        ua, q, k, v, hn1 = _inproj_fwd(h, g_mix[l][None], wa, wq, wk, wv)
        ycp = _mixer_fwd(ua, w_conv_f[l], wbd[l], pool_scale[l][None], nb)
        if l == 0:
            ycat, g_in, g_out, g_up, g_down = _attn_fwd(
                q, k, v, ycp, nb, gather=[rows(a).astype(BF16) for a in (w_in[1:], w_out, w_up, w_down)])
            w_in_rest = gathered_w_in(g_in, depth - 1)
            for ll in range(1, depth):
                w_in_full[ll] = w_in_rest[ll - 1]
                w_in_parts[ll] = split_w_in(w_in_rest[ll - 1])
            w_out_f = jnp.moveaxis(g_out.reshape((N_DEV,) + w_out.shape), 0, 1).reshape(depth, -1, d)
            wo_a, wo_b = w_out_f[:, :cw + pw], w_out_f[:, cw + pw:]
            w_up_f = g_up.reshape((N_DEV,) + w_up.shape)
            w_down_f = g_down.reshape((N_DEV,) + w_down.shape)
        else:
            ycat = _attn_fwd(q, k, v, ycp, nb)
        h_next, hn2, h_mid = _mlp_fwd(h, ycat, w_out_f[l], g_mlp[l][None], w_up_f, w_down_f, l)
        saved.append((h, ua, q, k, v, hn1, ycat, h_mid, hn2))
        h = h_next

    loss_part, dh, dg_final = _loss_head(h, g_final[None], loss_target.reshape(nb * seq, d), nb)

    g_gmix, g_gmlp, g_wconv, g_pscale, g_wpool = [None] * depth, [None] * depth, [None] * depth, [None] * depth, [None] * depth
    g_win, g_wout, g_wup, g_wdown = [None] * depth, [None] * depth, [None] * depth, [None] * depth
    for l in reversed(range(depth)):
        h_in, ua, q, k, v, hn1, ycat, h_mid, hn2 = saved[l]
        wa, wqkv = w_in_parts[l][0], w_in_full[l][:, ua_w:]
        dh_mid, act, dm, g_gmlp[l] = _mlp_bwd(dh, h_mid, hn2, g_mlp[l][None], w_up_f, w_down_f, l)
        g_wup[l] = _matmul_tn(hn2, dm, "grad_w_up")
        g_wdown[l] = _matmul_tn(act, dh, "grad_w_down")
        dycp, do = _outproj_bwd(dh_mid, wo_a[l], wo_b[l])
        g_wout[l] = _matmul_tn(ycat, dh_mid, "grad_w_out")
        if l == 0:
            send = [_shards_last(jnp.stack(g_win[1:])), _shards_rows(jnp.stack(g_wout)),
                    _shards_last(jnp.stack(g_wup)), _shards_rows(jnp.stack(g_wdown))]
            dqkv, p_in, p_out, p_up, p_down = _attn_bwd(
                q, k, v, do, nb, exchange=[s.reshape(N_DEV, -1, s.shape[-1]) for s in send])
        else:
            dqkv = _attn_bwd(q, k, v, do, nb)
        dua, g_wconv[l], g_pscale[l], dwb = _mixer_bwd(ua, dycp, w_conv_f[l], wbd[l], pool_scale[l][None], nb)
        g_wpool[l] = jnp.stack([dwb[gi * pg:(gi + 1) * pg, gi * pg:(gi + 1) * pg] for gi in range(N_POOL_GROUPS)])
        g_win[l] = jnp.concatenate([_matmul_tn(hn1, dua, "grad_w_in_a"), _matmul_tn(hn1, dqkv, "grad_w_in_qkv")], axis=1)
        if l == 0:
            dh, g_gmix[l], p_in0 = _inproj_bwd(dh_mid, h_in, g_mix[l][None], dua, dqkv, wa, wqkv,
                                               exchange=_shards_last(g_win[0][None]).reshape(N_DEV, -1, w_in.shape[-1]))
        else:
            dh, g_gmix[l] = _inproj_bwd(dh_mid, h_in, g_mix[l][None], dua, dqkv, wa, wqkv)
    dh3 = dh.reshape(nb, lp, d)
    grad_x = dh3[:, BLK:]
    g_meta = _sum_parts(dh3[:, PAD:BLK], "sum_meta_grad")

    o_out = _adamw(rows(w_out), p_out, rows(m_w_out), rows(v_w_out), "adamw_w_out")
    o_up = _adamw(rows(w_up), p_up, rows(m_w_up), rows(v_w_up), "adamw_w_up")
    o_down = _adamw(rows(w_down), p_down, rows(m_w_down), rows(v_w_down), "adamw_w_down")
    o_in1 = _adamw(rows(w_in[1:]), p_in, rows(m_w_in[1:]), rows(v_w_in[1:]), "adamw_w_in_rest")
    o_in0 = _adamw(rows(w_in[:1]), p_in0, rows(m_w_in[:1]), rows(v_w_in[:1]), "adamw_w_in0")
    (gr_win, gr_wout, gr_wup, gr_wdown), (de_win, de_wout, de_wup, de_wdown), (nm_win, nm_wout, nm_wup, nm_wdown), \
        (nv_win, nv_wout, nv_wup, nv_wdown) = [
            [jnp.concatenate([i0, i1], axis=0).reshape(w_in.shape), o.reshape(w_out.shape), u.reshape(w_up.shape), dn.reshape(w_down.shape)]
            for i0, i1, o, u, dn in zip(o_in0, o_in1, o_out, o_up, o_down)]

    small_full = [g_meta, jnp.concatenate(g_gmix), jnp.stack(g_wconv), jnp.stack(g_wpool), jnp.concatenate(g_pscale),
                  jnp.concatenate(g_gmlp), dg_final.reshape(-1)]
    small_shapes = [a.shape for a in small_full]
    small_sum = _sum_parts(_all_gather([_flat_pad(small_full, 8 * BLK).reshape(-1, BLK)], "gather_small_grads")[0], "sum_small_grads")
    gr_meta, gr_gmix, gr_wconv, gr_wpool, gr_pscale, gr_gmlp, gr_gfinal = _unflatten(small_sum.reshape(-1), small_shapes)
    gr_meta = lax.dynamic_slice_in_dim(gr_meta, me * (d // N_DEV), d // N_DEV, axis=1)
    gr_wconv = lax.dynamic_slice_in_dim(gr_wconv, me * (cw // N_DEV), cw // N_DEV, axis=2)
    small_g = [gr_meta, gr_gmix, gr_wconv, gr_wpool, gr_pscale, gr_gmlp, gr_gfinal]
    local_shapes = [a.shape for a in small_g]
    pack_small = lambda parts_: _flat_pad(parts_, 8 * BLK).reshape(-1, BLK)
    small = _adamw(pack_small([meta_tokens, g_mix, w_conv, w_pool, pool_scale, g_mlp, g_final]), pack_small(small_g)[None],
                   pack_small([m_meta_tokens, m_g_mix, m_w_conv, m_w_pool, m_pool_scale, m_g_mlp, m_g_final]),
                   pack_small([v_meta_tokens, v_g_mix, v_w_conv, v_w_pool, v_pool_scale, v_g_mlp, v_g_final]), "adamw_small")
    _, de_s, nm_s, nv_s = [_unflatten(b.reshape(-1), local_shapes) for b in small]

    loss = lax.psum(loss_part[0, 0], ("x", "y", "c"))

    def ordered(meta, gmix, wconv, wpool, pscale, gmlp, gfinal, win, wout, wup, wdown):
        return [meta, gmix, win, wconv, wpool, pscale, wout, gmlp, wup, wdown, gfinal]

    grads = ordered(*small_g, gr_win, gr_wout, gr_wup, gr_wdown)
    deltas = ordered(*de_s, de_win, de_wout, de_wup, de_wdown)
    new_m = ordered(*nm_s, nm_win, nm_wout, nm_wup, nm_wdown)
    new_v = ordered(*nv_s, nv_win, nv_wout, nv_wup, nv_wdown)
    return (loss, grad_x, *grads, *deltas, *new_m, *new_v)
```

```python
import jax
import jax.numpy as jnp
from jax import lax
from jax.experimental import pallas as pl
from jax.experimental.pallas import tpu as pltpu

F32 = jnp.float32
BF16 = jnp.bfloat16

N_DEV = 8
N_META = 16
HEAD_DIM = 64
BLK = 128
PAD = BLK - N_META
N_POOL_GROUPS = 4
EPS = 1e-6

ADAM_LR = 0.001
ADAM_B1 = 0.9
ADAM_B2 = 0.999
ADAM_EPS = 1e-08
ADAM_WD = 0.01
ADAM_STEP = 10

V7X_VMEM_LIMIT = 56 * 1024 * 1024


def _params(*sem):
    return pltpu.CompilerParams(dimension_semantics=sem, vmem_limit_bytes=V7X_VMEM_LIMIT)


SUBLANES = 8
ROW_TILE = 512
ROW_TILE_MLP = 1152
ROW_TILE_WIDE = 1152
TN_ACC_BYTES = 16 * 1024 * 1024


def _row_tile(t, cap=ROW_TILE):
    for tm in range(min(cap, t) // SUBLANES * SUBLANES, 0, -SUBLANES):
        if t % tm == 0:
            return tm
    raise ValueError(f"row count {t} is not a multiple of 8")


def _full(a):
    nd = a.ndim
    return pl.BlockSpec(a.shape, lambda *_: (0,) * nd)


def _dot(a, b):
    return jnp.dot(a, b, preferred_element_type=F32)


def _dot_nt(a, b):
    return lax.dot_general(a, b, (((1,), (1,)), ((), ())), preferred_element_type=F32)


def _dot_tn(a, b):
    return lax.dot_general(a, b, (((0,), (0,)), ((), ())), preferred_element_type=F32)


def _rms_fwd(x, g):
    rstd = lax.rsqrt(jnp.mean(x * x, axis=-1, keepdims=True) + EPS)
    return x * rstd * g


def _rms_bwd(dy, x, g):
    rstd = lax.rsqrt(jnp.mean(x * x, axis=-1, keepdims=True) + EPS)
    xhat = x * rstd
    dxhat = dy * g
    dx = rstd * (dxhat - xhat * jnp.mean(dxhat * xhat, axis=-1, keepdims=True))
    return dx, jnp.sum(dy * xhat, axis=0, keepdims=True)


def _inproj_fwd(h, g, wa, wq, wk, wv):
    t, d = h.shape
    tm = _row_tile(t, ROW_TILE_WIDE)
    ua_w, aw = wa.shape[1], wq.shape[1]

    def body(h_ref, g_ref, wa_ref, wq_ref, wk_ref, wv_ref, ua_ref, q_ref, k_ref, v_ref, hn_ref):
        hn = _rms_fwd(h_ref[...], g_ref[...]).astype(BF16)
        hn_ref[...] = hn
        ua_ref[...] = _dot(hn, wa_ref[...])
        q_ref[...] = _dot(hn, wq_ref[...]).astype(BF16)
        k_ref[...] = _dot(hn, wk_ref[...]).astype(BF16)
        v_ref[...] = _dot(hn, wv_ref[...]).astype(BF16)

    row = lambda w: pl.BlockSpec((tm, w), lambda i: (i, 0))
    return pl.pallas_call(
        body, name="inproj_fwd", grid=(t // tm,),
        in_specs=[row(d), _full(g), _full(wa), _full(wq), _full(wk), _full(wv)],
        out_specs=[row(ua_w), row(aw), row(aw), row(aw), row(d)],
        out_shape=[jax.ShapeDtypeStruct((t, ua_w), F32)] + [jax.ShapeDtypeStruct((t, aw), BF16)] * 3
        + [jax.ShapeDtypeStruct((t, d), BF16)],
        compiler_params=_params("arbitrary"),
    )(h, g, wa, wq, wk, wv)


def _pool_geometry(lp, pw):
    pg = pw // N_POOL_GROUPS
    row = lax.broadcasted_iota(jnp.int32, (lp, pw), 0)
    lane = lax.broadcasted_iota(jnp.int32, (lp, pw), 1)
    grp = [(lane >= g * pg) & (lane < (g + 1) * pg) for g in range(N_POOL_GROUPS)]
    wlen = jnp.where(grp[0], 2, jnp.where(grp[1], 4, jnp.where(grp[2], 8, 16)))
    cnt = jnp.clip(row - (PAD - 1), 1, wlen).astype(F32)
    return grp, cnt, row >= PAD


def _by_group(grp, vals):
    return jnp.where(grp[0], vals[0], jnp.where(grp[1], vals[1], jnp.where(grp[2], vals[2], vals[3])))


def _pooled(p, grp, cnt, real):
    s2 = p + pltpu.roll(p, 1, 0)
    s4 = s2 + pltpu.roll(s2, 2, 0)
    s8 = s4 + pltpu.roll(s4, 4, 0)
    s16 = s8 + pltpu.roll(s8, 8, 0)
    return jnp.where(real, _by_group(grp, (s2, s4, s8, s16)) / cnt - p, 0.0)


def _conv(uu, wc_ref):
    return wc_ref[2:3, :] * uu + wc_ref[1:2, :] * pltpu.roll(uu, 1, 0) + wc_ref[0:1, :] * pltpu.roll(uu, 2, 0)


def _mixer_fwd(ua, wconv, wbd, pscale, nb):
    t = ua.shape[0]
    lp = t // nb
    cw, pw = wconv.shape[1], wbd.shape[0]

    def body(ua_ref, wc_ref, wbd_ref, ps_ref, y_ref):
        cb = ua_ref[:, 0:cw]
        uu = ua_ref[:, cw:2 * cw] * ua_ref[:, 2 * cw:3 * cw]
        y_ref[:, 0:cw] = (cb * _conv(uu, wc_ref)).astype(BF16)
        grp, cnt, real = _pool_geometry(lp, pw)
        pooled = _pooled(ua_ref[:, 3 * cw:3 * cw + pw], grp, cnt, real)
        y_ref[:, cw:cw + pw] = (_dot(pooled.astype(BF16), wbd_ref[...]) * ps_ref[...]).astype(BF16)

    return pl.pallas_call(
        body, name="mixer_fwd", grid=(nb,),
        in_specs=[pl.BlockSpec((lp, ua.shape[1]), lambda s: (s, 0)), _full(wconv), _full(wbd), _full(pscale)],
        out_specs=pl.BlockSpec((lp, cw + pw), lambda s: (s, 0)),
        out_shape=jax.ShapeDtypeStruct((t, cw + pw), BF16),
        compiler_params=_params("arbitrary"),
    )(ua, wconv, wbd, pscale)


ATT_NW = 2
ATT_W = ATT_NW * BLK
ATT_PAIRS_FWD = 4
ATT_PAIRS_BWD = 4
HW = 2 * HEAD_DIM
ATT_SCALE = HEAD_DIM ** -0.5


def _attn_lk(z, valid):
    nz = -z
    return jnp.where(valid, jnp.minimum(nz, 0.0) - jnp.log(1.0 + jnp.exp(jnp.minimum(z, nz))), 0.0)


def _tri_sum_bf16(x, tri):
    return _dot(x.astype(BF16), tri)


def _tri_sum(x, tri):
    hi = x.astype(BF16)
    lo = (x - hi.astype(F32)).astype(BF16)
    return _dot(hi, tri) + _dot(lo, tri)


def _attn_weights(z, suffix, carry, valid):
    return jnp.where(valid, jnp.exp(z + suffix + carry), 0.0)


def _attn_consts():
    r = lax.broadcasted_iota(jnp.int32, (2 * BLK, ATT_W), 0)
    c = lax.broadcasted_iota(jnp.int32, (2 * BLK, ATT_W), 1)
    cmr = c - (r & (BLK - 1))
    kr = lax.broadcasted_iota(jnp.int32, (ATT_W, ATT_W), 0)
    kc = lax.broadcasted_iota(jnp.int32, (ATT_W, ATT_W), 1)
    lane = lax.broadcasted_iota(jnp.int32, (1, HW), 1)
    heads = (lane < HEAD_DIM, lane >= HEAD_DIM)
    return c, cmr, kr, kc, heads


def _tri(cond):
    return jnp.where(cond, 1.0, 0.0).astype(BF16)


def _rows(i):
    return pl.ds(pl.multiple_of(i * BLK, BLK), BLK)


def _stack_heads(x, heads):
    zero = jnp.zeros_like(x)
    return jnp.concatenate([jnp.where(heads[0], x, zero), jnp.where(heads[1], x, zero)], axis=0)


def _unstack_heads(x, heads):
    return jnp.where(heads[0], x[:BLK], x[BLK:])


def _window(i, s):
    start = jnp.maximum(i + 1 - ATT_NW * (s + 1), 0) * BLK
    return start, (i + 1 - ATT_NW * s) * BLK


def _attn_fwd(q, k, v, ycp, nb, gather=None):
    t, aw = q.shape
    cpw = ycp.shape[1]
    lp = t // nb
    nblk = lp // BLK
    pp = min(ATT_PAIRS_FWD, aw // HW)
    cols = [slice(p * HW, (p + 1) * HW) for p in range(pp)]
    ncol = aw // (pp * HW)
    assert ncol == 1, "the [y_conv | y_pool | y_attn] output block needs all head pairs in one grid step"
    nsteps = nb * ncol

    def body(q_ref, k_ref, v_ref, ycp_ref, *rest):
        if gather is None:
            (o_ref,) = rest
        else:
            ng = len(gather)
            x_refs, o_ref, g_refs, sems = rest[:ng], rest[ng], rest[ng + 1:2 * ng + 1], rest[2 * ng + 1:]
            step = pl.program_id(0) * ncol + pl.program_id(1)
            start, forward, finish = _all_of([_gather_phases(x_refs[j], g_refs[j], *sems[3 * j:3 * j + 3]) for j in range(ng)])
            pl.when(step == 0)(start)
            pl.when(step == nsteps - 1)(forward)
        c, cmr, kr, kc, heads = _attn_consts()
        m_from = _tri(kr >= kc)
        o_ref[:, 0:cpw] = ycp_ref[...]

        def qblock(i, _):
            qs = [_stack_heads(q_ref[_rows(i), cs] * ATT_SCALE, heads) for cs in cols]

            def window(s, st):
                accs, carries = st
                start, end = _window(i, s)
                keys = pl.ds(pl.multiple_of(start, BLK), ATT_W)
                valid = (cmr < i * BLK - start) & (c < end - start)
                zs = [_dot_nt(qs[p], k_ref[keys, cs]) for p, cs in enumerate(cols)]
                suffix = [_tri_sum_bf16(_attn_lk(zs[p], valid), m_from) for p in range(pp)]
                new_accs, new_carries = [], []
                for p, cs in enumerate(cols):
                    a = _attn_weights(zs[p], suffix[p], carries[p], valid)
                    new_accs.append(accs[p] + _dot(a.astype(BF16), v_ref[keys, cs]))
                    new_carries.append(carries[p] + suffix[p][:, 0:1])
                return tuple(new_accs), tuple(new_carries)

            init = (tuple(jnp.zeros((2 * BLK, HW), F32) for _ in cols), tuple(jnp.zeros((2 * BLK, 1), F32) for _ in cols))
            accs, _ = lax.fori_loop(0, (i + ATT_NW) // ATT_NW, window, init)
            for p in range(pp):
                o_ref[_rows(i), cpw + p * HW:cpw + (p + 1) * HW] = _unstack_heads(accs[p], heads).astype(BF16)
            return 0

        lax.fori_loop(0, nblk, qblock, 0)
        if gather is not None:
            pl.when(step == nsteps - 1)(finish)

    spec = pl.BlockSpec((lp, pp * HW), lambda s, p: (s, p))
    in_specs = [spec, spec, spec, pl.BlockSpec((lp, cpw), lambda s, p: (s, 0))]
    y_spec = pl.BlockSpec((lp, cpw + aw), lambda s, p: (s, 0))
    y_shape = jax.ShapeDtypeStruct((t, cpw + aw), BF16)
    if gather is None:
        return pl.pallas_call(
            body, name="attn_fwd", grid=(nb, ncol), in_specs=in_specs, out_specs=y_spec, out_shape=y_shape,
            compiler_params=_params("arbitrary", "arbitrary"),
        )(q, k, v, ycp)
    return pl.pallas_call(
        body, name="attn_fwd_gather", grid=(nb, ncol), in_specs=in_specs + [ANY] * len(gather),
        out_specs=[y_spec] + [ANY] * len(gather),
        out_shape=[y_shape] + [jax.ShapeDtypeStruct((N_DEV,) + x.shape, x.dtype) for x in gather],
        scratch_shapes=COMM_SEMS * len(gather), compiler_params=_params("arbitrary", "arbitrary"),
    )(q, k, v, ycp, *gather)


def _mlp_weight_specs(wup, wdown, layer):
    nff, _, d, fc = wup.shape
    return nff, fc, [pl.BlockSpec((None, None, d, fc), lambda i, kk: (kk, layer, 0, 0)),
                     pl.BlockSpec((None, None, fc, d), lambda i, kk: (kk, layer, 0, 0))]


def _mlp_fwd(h, y, wout, g, wup, wdown, layer):
    t, d = h.shape
    tm = _row_tile(t, ROW_TILE_MLP)
    nff, fc, wspecs = _mlp_weight_specs(wup, wdown, layer)

    def body(h_ref, y_ref, wo_ref, g_ref, wu_ref, wd_ref, o_ref, hn_ref, hm_ref, acc):
        kk = pl.program_id(1)

        @pl.when(kk == 0)
        def _():
            x = h_ref[...] + _dot(y_ref[...], wo_ref[...])
            hm_ref[...] = x
            hn_ref[...] = _rms_fwd(x, g_ref[...]).astype(BF16)
            acc[...] = x

        m = _dot(hn_ref[...], wu_ref[...])
        a = jnp.square(jnp.maximum(m, 0.0)).astype(BF16)
        acc[...] += _dot(a, wd_ref[...])

        @pl.when(kk == nff - 1)
        def _():
            o_ref[...] = acc[...]

    row = pl.BlockSpec((tm, d), lambda i, kk: (i, 0))
    return pl.pallas_call(
        body, name="outproj_mlp_fwd", grid=(t // tm, nff),
        in_specs=[row, pl.BlockSpec((tm, y.shape[1]), lambda i, kk: (i, 0)), _full(wout), _full(g)] + wspecs,
        out_specs=[row, row, row],
        out_shape=[jax.ShapeDtypeStruct((t, d), F32), jax.ShapeDtypeStruct((t, d), BF16), jax.ShapeDtypeStruct((t, d), F32)],
        scratch_shapes=[pltpu.VMEM((tm, d), F32)],
        compiler_params=_params("arbitrary", "arbitrary"),
    )(h, y, wout, g, wup, wdown)


def _loss_head(h, g, target, nb):
    t, d = h.shape
    nblk = t // nb // BLK
    nx = nblk - 1

    def body(h_ref, g_ref, t_ref, loss_ref, dh_ref, dg_ref):
        s, i = pl.program_id(0), pl.program_id(1)

        @pl.when((s == 0) & (i == 0))
        def _():
            loss_ref[...] = jnp.zeros_like(loss_ref)
            dg_ref[...] = jnp.zeros_like(dg_ref)

        @pl.when(i == 0)
        def _():
            dh_ref[...] = jnp.zeros_like(dh_ref)

        @pl.when(i > 0)
        def _():
            x, gg = h_ref[...], g_ref[...]
            err = _rms_fwd(x, gg) - t_ref[...]
            loss_ref[...] += jnp.sum(err * err) * (0.5 / d)
            dx, dg = _rms_bwd(err * (1.0 / d), x, gg)
            dh_ref[...] = dx
            dg_ref[...] += dg

    hspec = pl.BlockSpec((BLK, d), lambda s, i: (s * nblk + i, 0))
    return pl.pallas_call(
        body, name="loss_head", grid=(nb, nblk),
        in_specs=[hspec, _full(g), pl.BlockSpec((BLK, d), lambda s, i: (s * nx + jnp.maximum(i - 1, 0), 0))],
        out_specs=[pl.BlockSpec((1, BLK), lambda s, i: (0, 0)), hspec, pl.BlockSpec((1, d), lambda s, i: (0, 0))],
        out_shape=[jax.ShapeDtypeStruct((1, BLK), F32), jax.ShapeDtypeStruct((t, d), F32), jax.ShapeDtypeStruct((1, d), F32)],
        compiler_params=_params("arbitrary", "arbitrary"),
    )(h, g, target)


def _mlp_bwd(dh, h, hn, g, wup, wdown, layer):
    t, d = h.shape
    tm = _row_tile(t, ROW_TILE_MLP)
    nff, fc, wspecs = _mlp_weight_specs(wup, wdown, layer)

    def body(dh_ref, h_ref, hn_ref, g_ref, wu_ref, wd_ref, dhm_ref, a_ref, dm_ref, dg_ref, dhn, dhb):
        i, kk = pl.program_id(0), pl.program_id(1)

        @pl.when((i == 0) & (kk == 0))
        def _():
            dg_ref[...] = jnp.zeros_like(dg_ref)

        @pl.when(kk == 0)
        def _():
            dhb[...] = dh_ref[...].astype(BF16)
            dhn[...] = jnp.zeros_like(dhn)

        r = jnp.maximum(_dot(hn_ref[...], wu_ref[...]), 0.0)
        a_ref[...] = (r * r).astype(BF16)
        dm = (_dot_nt(dhb[...], wd_ref[...]) * (2.0 * r)).astype(BF16)
        dm_ref[...] = dm
        dhn[...] += _dot_nt(dm, wu_ref[...])

        @pl.when(kk == nff - 1)
        def _():
            dx, dg = _rms_bwd(dhn[...], h_ref[...], g_ref[...])
            dhm_ref[...] = dh_ref[...] + dx
            dg_ref[...] += dg

    row = pl.BlockSpec((tm, d), lambda i, kk: (i, 0))
    ff = pl.BlockSpec((tm, fc), lambda i, kk: (i, kk))
    return pl.pallas_call(
        body, name="mlp_bwd", grid=(t // tm, nff),
        in_specs=[row, row, row, _full(g)] + wspecs,
        out_specs=[row, ff, ff, pl.BlockSpec((1, d), lambda i, kk: (0, 0))],
        out_shape=[jax.ShapeDtypeStruct((t, d), F32), jax.ShapeDtypeStruct((t, nff * fc), BF16),
                   jax.ShapeDtypeStruct((t, nff * fc), BF16), jax.ShapeDtypeStruct((1, d), F32)],
        scratch_shapes=[pltpu.VMEM((tm, d), F32), pltpu.VMEM((tm, d), BF16)],
        compiler_params=_params("arbitrary", "arbitrary"),
    )(dh, h, hn, g, wup, wdown)


def _matmul_tn(x, y, name):
    t, k1 = x.shape
    n = y.shape[1]
    tt = _row_tile(t, ROW_TILE_WIDE if k1 * n <= 2048 * 1024 else ROW_TILE)
    tn = min(n, max(512, TN_ACC_BYTES // (4 * k1)))
    steps = t // tt

    def body(x_ref, y_ref, o_ref, acc):
        @pl.when(pl.program_id(1) == 0)
        def _():
            acc[...] = jnp.zeros_like(acc)

        acc[...] += _dot_tn(x_ref[...].astype(BF16), y_ref[...].astype(BF16))

        @pl.when(pl.program_id(1) == steps - 1)
        def _():
            o_ref[...] = acc[...].astype(BF16)

    return pl.pallas_call(
        body, name=name, grid=(n // tn, steps),
        in_specs=[pl.BlockSpec((tt, k1), lambda j, r: (r, 0)), pl.BlockSpec((tt, tn), lambda j, r: (r, j))],
        out_specs=pl.BlockSpec((k1, tn), lambda j, r: (0, j)),
        out_shape=jax.ShapeDtypeStruct((k1, n), BF16),
        scratch_shapes=[pltpu.VMEM((k1, tn), F32)],
        compiler_params=_params("arbitrary", "arbitrary"),
    )(x, y)


def _outproj_bwd(dh, wa, wb):
    t, d = dh.shape
    tm = _row_tile(t, ROW_TILE_WIDE)
    na, nbw = wa.shape[0], wb.shape[0]

    def body(dh_ref, wa_ref, wb_ref, da_ref, db_ref):
        x = dh_ref[...].astype(BF16)
        da_ref[...] = _dot_nt(x, wa_ref[...])
        db_ref[...] = _dot_nt(x, wb_ref[...]).astype(BF16)

    row = lambda w: pl.BlockSpec((tm, w), lambda i: (i, 0))
    return pl.pallas_call(
        body, name="outproj_bwd", grid=(t // tm,),
        in_specs=[row(d), _full(wa), _full(wb)], out_specs=[row(na), row(nbw)],
        out_shape=[jax.ShapeDtypeStruct((t, na), F32), jax.ShapeDtypeStruct((t, nbw), BF16)],
        compiler_params=_params("arbitrary"),
    )(dh, wa, wb)


def _attn_bwd(q, k, v, do, nb, exchange=None):
    t, aw = q.shape
    lp = t // nb
    nblk = lp // BLK
    pp = min(ATT_PAIRS_BWD, aw // HW)
    cols = [slice(p * HW, (p + 1) * HW) for p in range(pp)]
    ncol = aw // (pp * HW)
    assert ncol == 1, "the [dq | dk | dv] output block needs all head pairs in one grid step"
    nsteps = nb * ncol
    max_windows = (nblk + ATT_NW - 1) // ATT_NW

    def body(q_ref, k_ref, v_ref, do_ref, *rest):
        if exchange is None:
            dqkv_ref, dk_acc, dv_acc, g_s, sig_s = rest
        else:
            ne = len(exchange)
            x_refs, dqkv_ref, e_refs = rest[:ne], rest[ne], rest[ne + 1:2 * ne + 1]
            dk_acc, dv_acc, g_s, sig_s = rest[2 * ne + 1:2 * ne + 5]
            sems = rest[2 * ne + 5:]
            step = pl.program_id(0) * ncol + pl.program_id(1)
            start, finish = _all_of([_exchange_phases(x_refs[j], e_refs[j], *sems[3 * j:3 * j + 3]) for j in range(ne)])
            pl.when(step == 0)(start)
        c, cmr, kr, kc, heads = _attn_consts()
        m_from = _tri(kr >= kc)
        m_before = _tri(kr < kc)
        dk_acc[...] = jnp.zeros_like(dk_acc)
        dv_acc[...] = jnp.zeros_like(dv_acc)

        def qblock(i, _):
            qs = [_stack_heads(q_ref[_rows(i), cs] * ATT_SCALE, heads) for cs in cols]
            dos = [_stack_heads(do_ref[_rows(i), cs], heads) for cs in cols]
            nwin = (i + ATT_NW) // ATT_NW

            def geometry(s):
                start, end = _window(i, s)
                return (cmr < i * BLK - start) & (c < end - start), pl.ds(pl.multiple_of(start, BLK), ATT_W)

            def down(s, carries):
                valid, keys = geometry(s)
                zs = [_dot_nt(qs[p], k_ref[keys, cs]) for p, cs in enumerate(cols)]
                das = [_dot_nt(dos[p], v_ref[keys, cs]) for p, cs in enumerate(cols)]
                suffix = []
                for p in range(pp):
                    lk = _attn_lk(zs[p], valid)
                    sig_s[p, s] = jnp.exp(zs[p] + lk)
                    suffix.append(_tri_sum_bf16(lk, m_from))
                new_carries = []
                for p, cs in enumerate(cols):
                    a = _attn_weights(zs[p], suffix[p], carries[p], valid)
                    g_s[p, s] = das[p] * a
                    dv_acc[keys, cs] += _dot_tn(a.astype(BF16), dos[p])
                    new_carries.append(carries[p] + suffix[p][:, 0:1])
                return tuple(new_carries)

            lax.fori_loop(0, nwin, down, tuple(jnp.zeros((2 * BLK, 1), F32) for _ in cols))

            def up(ss, st):
                accs, carries = st
                s = nwin - 1 - ss
                valid, keys = geometry(s)
                gs = [g_s[p, s] for p in range(pp)]
                before = [_tri_sum(gs[p], m_before) for p in range(pp)]
                new_accs, new_carries = [], []
                for p, cs in enumerate(cols):
                    g, sig = gs[p], sig_s[p, s]
                    dz = jnp.where(valid, g * (1.0 - sig) - (before[p] + carries[p]) * sig, 0.0).astype(BF16)
                    new_accs.append(accs[p] + _dot(dz, k_ref[keys, cs]))
                    dk_acc[keys, cs] += _dot_tn(dz, qs[p])
                    new_carries.append(carries[p] + jnp.sum(g, axis=1, keepdims=True))
                return tuple(new_accs), tuple(new_carries)

            init = (tuple(jnp.zeros((2 * BLK, HW), F32) for _ in cols), tuple(jnp.zeros((2 * BLK, 1), F32) for _ in cols))
            accs, _ = lax.fori_loop(0, nwin, up, init)
            for p, cs in enumerate(cols):
                dqkv_ref[_rows(i), cs] = (_unstack_heads(accs[p], heads) * ATT_SCALE).astype(BF16)
            return 0

        lax.fori_loop(0, nblk, qblock, 0)
        dqkv_ref[:, aw:2 * aw] = dk_acc[...].astype(BF16)
        dqkv_ref[:, 2 * aw:3 * aw] = dv_acc[...].astype(BF16)
        if exchange is not None:
            pl.when(step == nsteps - 1)(finish)

    spec = pl.BlockSpec((lp, pp * HW), lambda s, p: (s, p), pipeline_mode=pl.Buffered(1))
    d_spec = pl.BlockSpec((lp, 3 * aw), lambda s, p: (s, 0), pipeline_mode=pl.Buffered(1))
    d_shape = jax.ShapeDtypeStruct((t, 3 * aw), BF16)
    scratch = [pltpu.VMEM((lp, pp * HW), F32), pltpu.VMEM((lp, pp * HW), F32),
               pltpu.VMEM((pp, max_windows, 2 * BLK, ATT_W), F32), pltpu.VMEM((pp, max_windows, 2 * BLK, ATT_W), F32)]
    if exchange is None:
        return pl.pallas_call(
            body, name="attn_bwd", grid=(nb, ncol), in_specs=[spec] * 4, out_specs=d_spec, out_shape=d_shape,
            scratch_shapes=scratch, compiler_params=_params("arbitrary", "arbitrary"),
        )(q, k, v, do)
    return pl.pallas_call(
        body, name="attn_bwd_exchange", grid=(nb, ncol), in_specs=[spec] * 4 + [ANY] * len(exchange),
        out_specs=[d_spec] + [ANY] * len(exchange),
        out_shape=[d_shape] + [jax.ShapeDtypeStruct(x.shape, x.dtype) for x in exchange],
        scratch_shapes=scratch + COMM_SEMS * len(exchange), compiler_params=_params("arbitrary", "arbitrary"),
    )(q, k, v, do, *exchange)


def _mixer_bwd(ua, dy, wconv, wbd, pscale, nb):
    t = ua.shape[0]
    lp = t // nb
    cw, pw = wconv.shape[1], wbd.shape[0]

    def body(ua_ref, dy_ref, wc_ref, wbd_ref, ps_ref, du_ref, dwc_ref, dps_ref, dwb_ref):
        @pl.when(pl.program_id(0) == 0)
        def _():
            dwc_ref[...] = jnp.zeros_like(dwc_ref)
            dps_ref[...] = jnp.zeros_like(dps_ref)
            dwb_ref[...] = jnp.zeros_like(dwb_ref)

        up = lambda x, n: pltpu.roll(x, lp - n, 0)
        cb, cc, cx = ua_ref[:, 0:cw], ua_ref[:, cw:2 * cw], ua_ref[:, 2 * cw:3 * cw]
        uu = cc * cx
        dyc = dy_ref[:, 0:cw]
        du_ref[:, 0:cw] = (dyc * _conv(uu, wc_ref)).astype(BF16)
        dcv = dyc * cb
        duu = wc_ref[2:3, :] * dcv + wc_ref[1:2, :] * up(dcv, 1) + wc_ref[0:1, :] * up(dcv, 2)
        du_ref[:, cw:2 * cw] = (duu * cx).astype(BF16)
        du_ref[:, 2 * cw:3 * cw] = (duu * cc).astype(BF16)
        dwc_ref[0:1, :] += jnp.sum(dcv * pltpu.roll(uu, 2, 0), axis=0, keepdims=True)
        dwc_ref[1:2, :] += jnp.sum(dcv * pltpu.roll(uu, 1, 0), axis=0, keepdims=True)
        dwc_ref[2:3, :] += jnp.sum(dcv * uu, axis=0, keepdims=True)
        grp, cnt, real = _pool_geometry(lp, pw)
        p = ua_ref[:, 3 * cw:3 * cw + pw]
        pooled = _pooled(p, grp, cnt, real).astype(BF16)
        dyp = dy_ref[:, cw:cw + pw]
        dps_ref[...] += jnp.sum(dyp * _dot(pooled, wbd_ref[...]), axis=0, keepdims=True)
        dpre = (dyp * ps_ref[...]).astype(BF16)
        dwb_ref[...] += _dot_tn(pooled, dpre)
        dpooled = jnp.where(real, _dot_nt(dpre, wbd_ref[...]), 0.0)
        xm = dpooled / cnt
        l2 = xm + up(xm, 1)
        l4 = l2 + up(l2, 2)
        l8 = l4 + up(l4, 4)
        l16 = l8 + up(l8, 8)
        du_ref[:, 3 * cw:3 * cw + pw] = jnp.where(real, _by_group(grp, (l2, l4, l8, l16)) - dpooled, 0.0).astype(BF16)

    seq = lambda w: pl.BlockSpec((lp, w), lambda s: (s, 0))
    return pl.pallas_call(
        body, name="mixer_bwd", grid=(nb,),
        in_specs=[seq(ua.shape[1]), seq(cw + pw), _full(wconv), _full(wbd), _full(pscale)],
        out_specs=[seq(ua.shape[1]), pl.BlockSpec((3, cw), lambda s: (0, 0)), pl.BlockSpec((1, pw), lambda s: (0, 0)),
                   pl.BlockSpec((pw, pw), lambda s: (0, 0))],
        out_shape=[jax.ShapeDtypeStruct(ua.shape, BF16), jax.ShapeDtypeStruct((3, cw), F32),
                   jax.ShapeDtypeStruct((1, pw), F32), jax.ShapeDtypeStruct((pw, pw), F32)],
        compiler_params=_params("arbitrary"),
    )(ua, dy, wconv, wbd, pscale)


def _inproj_bwd(dh, h, g, dua, dqkv, wa, wqkv, exchange=None):
    t, d = h.shape
    tm = _row_tile(t)
    nsteps = t // tm

    def body(dh_ref, h_ref, g_ref, dua_ref, dqkv_ref, wa_ref, wqkv_ref, *rest):
        if exchange is None:
            o_ref, dg_ref = rest
        else:
            x_ref, o_ref, dg_ref, e_ref, send_sems, recv_sems, local_sem = rest
            start, finish = _exchange_phases(x_ref, e_ref, send_sems, recv_sems, local_sem)
            pl.when(pl.program_id(0) == 0)(start)

        @pl.when(pl.program_id(0) == 0)
        def _():
            dg_ref[...] = jnp.zeros_like(dg_ref)

        dhn = _dot_nt(dua_ref[...], wa_ref[...]) + _dot_nt(dqkv_ref[...], wqkv_ref[...])
        dx, dg = _rms_bwd(dhn, h_ref[...], g_ref[...])
        o_ref[...] = dh_ref[...] + dx
        dg_ref[...] += dg
        if exchange is not None:
            pl.when(pl.program_id(0) == nsteps - 1)(finish)

    row = lambda w: pl.BlockSpec((tm, w), lambda i: (i, 0))
    in_specs = [row(d), row(d), _full(g), row(dua.shape[1]), row(dqkv.shape[1]), _full(wa), _full(wqkv)]
    out_specs = [row(d), pl.BlockSpec((1, d), lambda i: (0, 0))]
    out_shape = [jax.ShapeDtypeStruct((t, d), F32), jax.ShapeDtypeStruct((1, d), F32)]
    if exchange is None:
        return pl.pallas_call(
            body, name="inproj_bwd", grid=(nsteps,), in_specs=in_specs, out_specs=out_specs, out_shape=out_shape,
            compiler_params=_params("arbitrary"),
        )(dh, h, g, dua, dqkv, wa, wqkv)
    return pl.pallas_call(
        body, name="inproj_bwd_exchange", grid=(nsteps,), in_specs=in_specs + [ANY], out_specs=out_specs + [ANY],
        out_shape=out_shape + [jax.ShapeDtypeStruct(exchange.shape, exchange.dtype)], scratch_shapes=COMM_SEMS,
        compiler_params=_params("arbitrary"),
    )(dh, h, g, dua, dqkv, wa, wqkv, exchange)


def _part_tile(r):
    if r <= 512:
        return r
    for tr in (256, 128, 64, 32, 16, 8):
        if r % tr == 0:
            return tr
    return r


def _sum_parts(x, name):
    n, r, c = x.shape
    tr = _part_tile(r)

    def body(x_ref, o_ref):
        acc = x_ref[0]
        for p in range(1, n):
            acc = acc + x_ref[p]
        o_ref[...] = acc

    return pl.pallas_call(
        body, name=name, grid=(r // tr,),
        in_specs=[pl.BlockSpec((n, tr, c), lambda i: (0, i, 0))], out_specs=pl.BlockSpec((tr, c), lambda i: (i, 0)),
        out_shape=jax.ShapeDtypeStruct((r, c), F32), compiler_params=_params("arbitrary"),
    )(x)


def _adamw(w, gparts, m, v, name):
    n, r, c = gparts.shape
    tr = _part_tile(r)

    def body(w_ref, g_ref, m_ref, v_ref, go_ref, d_ref, mo_ref, vo_ref):
        g = g_ref[0].astype(F32)
        for p in range(1, n):
            g = g + g_ref[p].astype(F32)
        go_ref[...] = g
        mm = ADAM_B1 * m_ref[...] + (1.0 - ADAM_B1) * g
        vv = ADAM_B2 * v_ref[...] + (1.0 - ADAM_B2) * jnp.square(g)
        mo_ref[...] = mm
        vo_ref[...] = vv
        m_hat = mm / (1.0 - ADAM_B1 ** ADAM_STEP)
        v_hat = vv / (1.0 - ADAM_B2 ** ADAM_STEP)
        d_ref[...] = -ADAM_LR * (m_hat / (jnp.sqrt(v_hat) + ADAM_EPS) + ADAM_WD * w_ref[...])

    row = pl.BlockSpec((tr, c), lambda i: (i, 0))
    return pl.pallas_call(
        body, name=name, grid=(r // tr,),
        in_specs=[row, pl.BlockSpec((n, tr, c), lambda i: (0, i, 0)), row, row], out_specs=[row] * 4,
        out_shape=[jax.ShapeDtypeStruct((r, c), F32)] * 4, compiler_params=_params("arbitrary"),
    )(w, gparts, m, v)


MESH = pl.DeviceIdType.MESH
ANY = pl.BlockSpec(memory_space=pl.ANY)


def _all_gather(xs, name):
    n = len(xs)

    def body(*refs):
        x_refs, out_refs, sems = refs[:n], refs[n:2 * n], refs[2 * n:]
        start, forward, finish = _all_of([_gather_phases(x_refs[j], out_refs[j], *sems[3 * j:3 * j + 3]) for j in range(n)])
        start()
        forward()
        finish()

    return pl.pallas_call(
        body, name=name, in_specs=[ANY] * n, out_specs=[ANY] * n,
        out_shape=[jax.ShapeDtypeStruct((N_DEV,) + x.shape, x.dtype) for x in xs], scratch_shapes=COMM_SEMS * n,
    )(*xs)


COMM_SEMS = [pltpu.SemaphoreType.DMA((7,)), pltpu.SemaphoreType.DMA((7,)), pltpu.SemaphoreType.DMA(())]


def _gather_phases(x_ref, out_ref, send_sems, recv_sems, local_sem):
    mx, my, mc = lax.axis_index("x"), lax.axis_index("y"), lax.axis_index("c")
    me, sibling = (mx, my, mc), (mx, my, 1 - mc)
    chips = [(1 - mx, my), (mx, 1 - my), (1 - mx, 1 - my)]

    def slot(px, py, pc):
        return out_ref.at[4 * px + 2 * py + pc]

    def copy(kk, block, to, src=None):
        return pltpu.make_async_remote_copy(
            src_ref=slot(*block) if src is None else src, dst_ref=slot(*block),
            send_sem=send_sems.at[kk], recv_sem=recv_sems.at[kk], device_id=to, device_id_type=MESH)

    mine = pltpu.make_async_copy(x_ref, slot(*me), local_sem)
    first = [copy(0, me, sibling, src=x_ref)] + [copy(1 + j, me, (*chip, mc), src=x_ref) for j, chip in enumerate(chips)]
    passed = [copy(4 + j, (*chip, mc), sibling) for j, chip in enumerate(chips)]

    def start():
        mine.start()
        for cp in first:
            cp.start()

    def forward():
        for j, chip in enumerate(chips):
            copy(1 + j, (*chip, mc), me).wait_recv()
            passed[j].start()

    def finish():
        copy(0, sibling, me).wait_recv()
        for j, chip in enumerate(chips):
            copy(4 + j, (*chip, 1 - mc), me).wait_recv()
        for cp in first + passed:
            cp.wait_send()
        mine.wait()

    return start, forward, finish


def _all_of(phase_lists):
    def run(fns):
        def every():
            for fn in fns:
                fn()
        return every
    return tuple(run(fns) for fns in zip(*phase_lists))


def _exchange_phases(x_ref, out_ref, send_sems, recv_sems, local_sem):
    mx, my, mc = lax.axis_index("x"), lax.axis_index("y"), lax.axis_index("c")
    me = 4 * mx + 2 * my + mc
    mine = pltpu.make_async_copy(x_ref.at[me], out_ref.at[me], local_sem)
    copies = []
    for kk in range(1, N_DEV):
        px, py, pc = mx ^ (kk >> 2), my ^ ((kk >> 1) & 1), mc ^ (kk & 1)
        peer = 4 * px + 2 * py + pc
        copies.append((peer, pltpu.make_async_remote_copy(
            src_ref=x_ref.at[peer], dst_ref=out_ref.at[me], send_sem=send_sems.at[kk - 1],
            recv_sem=recv_sems.at[kk - 1], device_id=(px, py, pc), device_id_type=MESH)))

    def start():
        mine.start()
        for _, cp in copies:
            cp.start()

    def finish():
        for kk, (peer, _) in enumerate(copies):
            pltpu.make_async_remote_copy(
                src_ref=x_ref.at[peer], dst_ref=out_ref.at[peer], send_sem=send_sems.at[kk],
                recv_sem=recv_sems.at[kk], device_id=(mx, my, mc), device_id_type=MESH).wait_recv()
        for _, cp in copies:
            cp.wait_send()
        mine.wait()

    return start, finish


def _shards_last(a):
    lead = a.shape[:-1]
    return jnp.moveaxis(a.reshape(*lead, N_DEV, a.shape[-1] // N_DEV), -2, 0)


def _shards_rows(a):
    l, n, c = a.shape
    return jnp.moveaxis(a.reshape(l, N_DEV, n // N_DEV, c), 1, 0)


def _flat_pad(parts, mult):
    flat = jnp.concatenate([p.reshape(-1) for p in parts])
    return jnp.pad(flat, (0, (-flat.shape[0]) % mult))


def _unflatten(flat, shapes):
    out, o = [], 0
    for shp in shapes:
        n = 1
        for s in shp:
            n *= s
        out.append(flat[o:o + n].reshape(shp))
        o += n
    return out


def kernel(x, meta_tokens, g_mix, w_in, w_conv, w_pool, pool_scale, w_out, g_mlp, w_up, w_down, g_final, loss_target, m_meta_tokens, m_g_mix, m_w_in, m_w_conv, m_w_pool, m_pool_scale, m_w_out, m_g_mlp, m_w_up, m_w_down, m_g_final, v_meta_tokens, v_g_mix, v_w_in, v_w_conv, v_w_pool, v_pool_scale, v_w_out, v_g_mlp, v_w_up, v_w_down, v_g_final):
    nb, seq, d = x.shape
    depth = g_mix.shape[0]
    lp = BLK + seq
    t = nb * lp
    cw = w_conv.shape[2] * N_DEV
    pw = pool_scale.shape[1]
    pg = pw // N_POOL_GROUPS
    aw = (w_in.shape[2] * N_DEV - 3 * cw - pw) // 3
    ua_w = 3 * cw + pw
    d_ff = w_up.shape[2] * N_DEV
    me = 4 * lax.axis_index("x") + 2 * lax.axis_index("y") + lax.axis_index("c")

    def rows(a):
        return a.reshape(-1, a.shape[-1])

    def gathered_w_in(g, layers):
        return jnp.moveaxis(g.reshape(N_DEV, layers, d, -1), 0, 2).reshape(layers, d, -1)

    g_in0, small_in = _all_gather([rows(w_in[:1]).astype(BF16), _flat_pad([meta_tokens, w_conv], 8 * BLK).reshape(-1, BLK)],
                                  "gather_first_weights")
    w_in0_f = gathered_w_in(g_in0, 1)
    small_in = small_in.reshape(N_DEV, -1)
    meta_f = jnp.moveaxis(small_in[:, :meta_tokens.size].reshape(N_DEV, N_META, d // N_DEV), 0, 1).reshape(N_META, d)
    w_conv_f = small_in[:, meta_tokens.size:meta_tokens.size + w_conv.size].reshape((N_DEV,) + w_conv.shape)
    w_conv_f = jnp.moveaxis(w_conv_f, 0, 2).reshape(depth, w_conv.shape[1], cw)

    def split_w_in(w):
        return w[:, :ua_w], w[:, ua_w:ua_w + aw], w[:, ua_w + aw:ua_w + 2 * aw], w[:, ua_w + 2 * aw:]

    wbd = jnp.zeros((depth, pw, pw), F32)
    for gi in range(N_POOL_GROUPS):
        wbd = wbd.at[:, gi * pg:(gi + 1) * pg, gi * pg:(gi + 1) * pg].set(w_pool[:, gi])
    wbd = wbd.astype(BF16)

    h = jnp.concatenate([jnp.zeros((nb, PAD, d), F32), jnp.broadcast_to(meta_f[None], (nb, N_META, d)), x], axis=1).reshape(t, d)
    saved, w_in_parts, w_in_full = [], [None] * depth, [None] * depth
    w_in_full[0] = w_in0_f[0]
    w_in_parts[0] = split_w_in(w_in0_f[0])
    for l in range(depth):
        wa, wq, wk, wv = w_in_parts[l]
        ua, q, k, v, hn1 = _inproj_fwd(h, g_mix[l][None], wa, wq, wk, wv)
        ycp = _mixer_fwd(ua, w_conv_f[l], wbd[l], pool_scale[l][None], nb)
        if l == 0:
            ycat, g_in, g_out, g_up, g_down = _attn_fwd(
                q, k, v, ycp, nb, gather=[rows(a).astype(BF16) for a in (w_in[1:], w_out, w_up, w_down)])
            w_in_rest = gathered_w_in(g_in, depth - 1)
            for ll in range(1, depth):
                w_in_full[ll] = w_in_rest[ll - 1]
                w_in_parts[ll] = split_w_in(w_in_rest[ll - 1])
            w_out_f = jnp.moveaxis(g_out.reshape((N_DEV,) + w_out.shape), 0, 1).reshape(depth, -1, d)
            wo_a, wo_b = w_out_f[:, :cw + pw], w_out_f[:, cw + pw:]
            w_up_f = g_up.reshape((N_DEV,) + w_up.shape)
            w_down_f = g_down.reshape((N_DEV,) + w_down.shape)
        else:
            ycat = _attn_fwd(q, k, v, ycp, nb)
        h_next, hn2, h_mid = _mlp_fwd(h, ycat, w_out_f[l], g_mlp[l][None], w_up_f, w_down_f, l)
        saved.append((h, ua, q, k, v, hn1, ycat, h_mid, hn2))
        h = h_next

    loss_part, dh, dg_final = _loss_head(h, g_final[None], loss_target.reshape(nb * seq, d), nb)

    g_gmix, g_gmlp, g_wconv, g_pscale, g_wpool = [None] * depth, [None] * depth, [None] * depth, [None] * depth, [None] * depth
    g_win, g_wout, g_wup, g_wdown = [None] * depth, [None] * depth, [None] * depth, [None] * depth
    for l in reversed(range(depth)):
        h_in, ua, q, k, v, hn1, ycat, h_mid, hn2 = saved[l]
        wa, wqkv = w_in_parts[l][0], w_in_full[l][:, ua_w:]
        dh_mid, act, dm, g_gmlp[l] = _mlp_bwd(dh, h_mid, hn2, g_mlp[l][None], w_up_f, w_down_f, l)
        g_wup[l] = _matmul_tn(hn2, dm, "grad_w_up")
        g_wdown[l] = _matmul_tn(act, dh, "grad_w_down")
        dycp, do = _outproj_bwd(dh_mid, wo_a[l], wo_b[l])
        g_wout[l] = _matmul_tn(ycat, dh_mid, "grad_w_out")
        if l == 0:
            send = [_shards_last(jnp.stack(g_win[1:])), _shards_rows(jnp.stack(g_wout)),
                    _shards_last(jnp.stack(g_wup)), _shards_rows(jnp.stack(g_wdown))]
            dqkv, p_in, p_out, p_up, p_down = _attn_bwd(
                q, k, v, do, nb, exchange=[s.reshape(N_DEV, -1, s.shape[-1]) for s in send])
        else:
            dqkv = _attn_bwd(q, k, v, do, nb)
        dua, g_wconv[l], g_pscale[l], dwb = _mixer_bwd(ua, dycp, w_conv_f[l], wbd[l], pool_scale[l][None], nb)
        g_wpool[l] = jnp.stack([dwb[gi * pg:(gi + 1) * pg, gi * pg:(gi + 1) * pg] for gi in range(N_POOL_GROUPS)])
        g_win[l] = jnp.concatenate([_matmul_tn(hn1, dua, "grad_w_in_a"), _matmul_tn(hn1, dqkv, "grad_w_in_qkv")], axis=1)
        if l == 0:
            dh, g_gmix[l], p_in0 = _inproj_bwd(dh_mid, h_in, g_mix[l][None], dua, dqkv, wa, wqkv,
                                               exchange=_shards_last(g_win[0][None]).reshape(N_DEV, -1, w_in.shape[-1]))
        else:
            dh, g_gmix[l] = _inproj_bwd(dh_mid, h_in, g_mix[l][None], dua, dqkv, wa, wqkv)
    dh3 = dh.reshape(nb, lp, d)
    grad_x = dh3[:, BLK:]
    g_meta = _sum_parts(dh3[:, PAD:BLK], "sum_meta_grad")

    o_out = _adamw(rows(w_out), p_out, rows(m_w_out), rows(v_w_out), "adamw_w_out")
    o_up = _adamw(rows(w_up), p_up, rows(m_w_up), rows(v_w_up), "adamw_w_up")
    o_down = _adamw(rows(w_down), p_down, rows(m_w_down), rows(v_w_down), "adamw_w_down")
    o_in1 = _adamw(rows(w_in[1:]), p_in, rows(m_w_in[1:]), rows(v_w_in[1:]), "adamw_w_in_rest")
    o_in0 = _adamw(rows(w_in[:1]), p_in0, rows(m_w_in[:1]), rows(v_w_in[:1]), "adamw_w_in0")
    (gr_win, gr_wout, gr_wup, gr_wdown), (de_win, de_wout, de_wup, de_wdown), (nm_win, nm_wout, nm_wup, nm_wdown), \
        (nv_win, nv_wout, nv_wup, nv_wdown) = [
            [jnp.concatenate([i0, i1], axis=0).reshape(w_in.shape), o.reshape(w_out.shape), u.reshape(w_up.shape), dn.reshape(w_down.shape)]
            for i0, i1, o, u, dn in zip(o_in0, o_in1, o_out, o_up, o_down)]

    small_full = [g_meta, jnp.concatenate(g_gmix), jnp.stack(g_wconv), jnp.stack(g_wpool), jnp.concatenate(g_pscale),
                  jnp.concatenate(g_gmlp), dg_final.reshape(-1)]
    small_shapes = [a.shape for a in small_full]
    small_sum = _sum_parts(_all_gather([_flat_pad(small_full, 8 * BLK).reshape(-1, BLK)], "gather_small_grads")[0], "sum_small_grads")
    gr_meta, gr_gmix, gr_wconv, gr_wpool, gr_pscale, gr_gmlp, gr_gfinal = _unflatten(small_sum.reshape(-1), small_shapes)
    gr_meta = lax.dynamic_slice_in_dim(gr_meta, me * (d // N_DEV), d // N_DEV, axis=1)
    gr_wconv = lax.dynamic_slice_in_dim(gr_wconv, me * (cw // N_DEV), cw // N_DEV, axis=2)
    small_g = [gr_meta, gr_gmix, gr_wconv, gr_wpool, gr_pscale, gr_gmlp, gr_gfinal]
    local_shapes = [a.shape for a in small_g]
    pack_small = lambda parts_: _flat_pad(parts_, 8 * BLK).reshape(-1, BLK)
    small = _adamw(pack_small([meta_tokens, g_mix, w_conv, w_pool, pool_scale, g_mlp, g_final]), pack_small(small_g)[None],
                   pack_small([m_meta_tokens, m_g_mix, m_w_conv, m_w_pool, m_pool_scale, m_g_mlp, m_g_final]),
                   pack_small([v_meta_tokens, v_g_mix, v_w_conv, v_w_pool, v_pool_scale, v_g_mlp, v_g_final]), "adamw_small")
    _, de_s, nm_s, nv_s = [_unflatten(b.reshape(-1), local_shapes) for b in small]

    loss = lax.psum(loss_part[0, 0], ("x", "y", "c"))

    def ordered(meta, gmix, wconv, wpool, pscale, gmlp, gfinal, win, wout, wup, wdown):
        return [meta, gmix, win, wconv, wpool, pscale, wout, gmlp, wup, wdown, gfinal]

    grads = ordered(*small_g, gr_win, gr_wout, gr_wup, gr_wdown)
    deltas = ordered(*de_s, de_win, de_wout, de_wup, de_wdown)
    new_m = ordered(*nm_s, nm_win, nm_wout, nm_wup, nm_wdown)
    new_v = ordered(*nv_s, nv_win, nv_wout, nv_wup, nv_wdown)
    return (loss, grad_x, *grads, *deltas, *new_m, *new_v)
```
